```python
import math
import jax, jax.numpy as jnp
from jax import lax
import numpy as np

D_MODEL = 1024
BATCH = 2
SEQ = 8192
DEPTH = 1

EPS = 1e-6
D_SSM = D_MODEL // 2
SSM_GROUP = 16
N_SSM_GROUPS = D_SSM // SSM_GROUP
SSM_STATE = 64
DT_MIN = 1e-3
DT_MAX = 1e-1
HEAD_DIM = 64
N_HEADS = (D_MODEL // 2) // HEAD_DIM
N_KV_HEADS = 2
Q_PER_KV = N_HEADS // N_KV_HEADS
D_ATTN = N_HEADS * HEAD_DIM
WINDOW = 128
BLOCK = 128
N_BRANCHES = 2
Q_COLS = D_ATTN
KV_COLS = N_KV_HEADS * HEAD_DIM
GATE_COLS = N_BRANCHES * D_MODEL
IN_COLS = Q_COLS + 2 * KV_COLS + D_SSM + GATE_COLS
N_EXPERT_GROUPS = 4
EXPERTS_PER_GROUP = 8
N_EXPERTS = N_EXPERT_GROUPS * EXPERTS_PER_GROUP
TOP_K = 2
D_FF_EXPERT = D_MODEL // 2

kernel_name = "hybrid_s5_swa_sink_hmoe_block"


def rmsnorm(x, g):
    xf = x.astype(jnp.float32)
    inv = lax.rsqrt(jnp.mean(xf * xf, axis=-1, keepdims=True) + EPS)
    return (xf * inv * g.astype(jnp.float32)).astype(x.dtype)


def sliding_window_attention(q, k, v, sinks):
    b, l = q.shape[0], q.shape[1]
    nb = l // BLOCK
    qb = q.reshape(b, nb, BLOCK, N_KV_HEADS, Q_PER_KV, HEAD_DIM)

    def band(t):
        tb = t.reshape(b, nb, BLOCK, N_KV_HEADS, HEAD_DIM)
        prev = jnp.pad(tb[:, :-1], ((0, 0), (1, 0), (0, 0), (0, 0), (0, 0)))
        return jnp.concatenate([prev, tb], axis=2)

    kb, vb = band(k), band(v)
    scale = 1.0 / math.sqrt(HEAD_DIM)
    s = jnp.einsum('bnqkgd,bnskd->bnkgqs', qb, kb,
                   preferred_element_type=jnp.float32) * scale
    blk = jnp.arange(nb)[:, None, None] * BLOCK
    qpos = blk + jnp.arange(BLOCK)[None, :, None]
    kpos = blk - BLOCK + jnp.arange(2 * BLOCK)[None, None, :]
    rel = qpos - kpos
    mask = (rel >= 0) & (rel < WINDOW) & (kpos >= 0)
    s = jnp.where(mask[None, :, None, None], s, jnp.finfo(jnp.float32).min)
    sink = sinks.astype(jnp.float32).reshape(1, 1, N_KV_HEADS, Q_PER_KV, 1, 1)
    m = jnp.maximum(jnp.max(s, axis=-1, keepdims=True), sink)
    p = jnp.exp(s - m)
    denom = jnp.sum(p, axis=-1, keepdims=True) + jnp.exp(sink - m)
    p = (p / denom).astype(v.dtype)
    o = jnp.einsum('bnkgqs,bnskd->bnqkgd', p, vb)
    return o.reshape(b, l, D_ATTN)


def s5_ssm(u, a_re, a_im, b_re, b_im, c_re, c_im, d_skip, log_dt):
    b, l = u.shape[0], u.shape[1]
    f32 = jnp.float32
    uf = u.astype(f32).reshape(b, l, N_SSM_GROUPS, SSM_GROUP)
    lam = lax.complex(a_re.astype(f32), a_im.astype(f32))
    dt = jnp.exp(log_dt.astype(f32))[:, None]
    lam_bar = jnp.exp(lam * dt)
    b_mat = lax.complex(b_re.astype(f32), b_im.astype(f32))
    b_bar = ((lam_bar - 1.0) / lam)[:, :, None] * b_mat
    bu = jnp.einsum('blgc,gpc->blgp', uf.astype(jnp.complex64), b_bar)
    a = jnp.broadcast_to(lam_bar, bu.shape)

    def combine(left, right):
        a_l, b_l = left
        a_r, b_r = right
        return a_l * a_r, a_r * b_l + b_r

    _, states = lax.associative_scan(combine, (a, bu), axis=1)
    c_mat = lax.complex(c_re.astype(f32), c_im.astype(f32))
    y = jnp.real(jnp.einsum('blgp,gcp->blgc', states, c_mat))
    y = y + d_skip.astype(f32).reshape(N_SSM_GROUPS, SSM_GROUP) * uf
    return y.reshape(b, l, D_SSM)


def hierarchical_moe(h, w_rg, b_rg, w_re, b_re, w_gate, w_up, w_down):
    b, l, d = h.shape
    n_tok = b * l
    t = h.reshape(n_tok, d)
    group_logits = (t @ w_rg).astype(jnp.float32) + b_rg.astype(jnp.float32)
    group_probs = jax.nn.softmax(group_logits, axis=-1)
    group_p, group_idx = lax.top_k(group_probs, 1)
    expert_logits = ((t @ w_re).astype(jnp.float32) + b_re.astype(jnp.float32)
                     ).reshape(n_tok, N_EXPERT_GROUPS, EXPERTS_PER_GROUP)
    sel_logits = jnp.take_along_axis(expert_logits, group_idx[:, :, None], axis=1)[:, 0]
    within = jax.nn.softmax(sel_logits, axis=-1)
    w_top, e_local = lax.top_k(within, TOP_K)
    w_top = w_top / jnp.sum(w_top, axis=-1, keepdims=True) * group_p
    expert_ids = group_idx * EXPERTS_PER_GROUP + e_local
    flat_e = expert_ids.reshape(-1)
    order = jnp.argsort(flat_e)
    tok = order // TOP_K
    xs = t[tok]
    sizes = jnp.bincount(flat_e, length=N_EXPERTS).astype(jnp.int32)
    hg = lax.ragged_dot(xs, w_gate, sizes)
    hu = lax.ragged_dot(xs, w_up, sizes)
    ys = lax.ragged_dot(jax.nn.silu(hg) * hu, w_down, sizes)
    ws = w_top.reshape(-1)[order].astype(ys.dtype)
    out = jax.ops.segment_sum(ys * ws[:, None], tok, num_segments=n_tok)
    return out.reshape(b, l, d).astype(h.dtype)


def setup_inputs(seed: int = 0) -> dict:
    key = jax.random.key(seed)
    ks = jax.random.split(key, 32)
    f32 = jnp.float32
    L, D = DEPTH, D_MODEL
    G, P, C = N_SSM_GROUPS, SSM_STATE, SSM_GROUP

    def nrm(k, shape, scale):
        return jax.random.normal(k, shape, f32) * scale

    n_idx = jnp.arange(P, dtype=f32)
    a_re = -0.5 + 0.01 * jax.random.normal(ks[4], (L, G, P), f32)
    a_im = math.pi * n_idx[None, None, :] + 0.01 * jax.random.normal(ks[5], (L, G, P), f32)
    log_dt = jax.random.uniform(ks[12], (L, G), f32, math.log(DT_MIN), math.log(DT_MAX))
    return {
        "x": nrm(ks[0], (BATCH, SEQ, D), 1.0),
        "norm_mix": 1.0 + nrm(ks[1], (L, D), 0.02),
        "w_in": nrm(ks[2], (L, D, IN_COLS), D ** -0.5),
        "b_gate": nrm(ks[3], (L, GATE_COLS), 0.01),
        "attn_sinks": nrm(ks[13], (L, N_HEADS), 1.0),
        "ssm_a_re": a_re,
        "ssm_a_im": a_im,
        "ssm_b_re": nrm(ks[6], (L, G, P, C), (2 * C) ** -0.5),
        "ssm_b_im": nrm(ks[7], (L, G, P, C), (2 * C) ** -0.5),
        "ssm_c_re": nrm(ks[8], (L, G, C, P), (2 * P) ** -0.5),
        "ssm_c_im": nrm(ks[9], (L, G, C, P), (2 * P) ** -0.5),
        "ssm_d": nrm(ks[10], (L, D_SSM), 1.0),
        "ssm_log_dt": log_dt,
        "w_glu": nrm(ks[11], (L, D_SSM, D_SSM), D_SSM ** -0.5),
        "b_glu": nrm(ks[14], (L, D_SSM), 0.01),
        "w_attn_branch": nrm(ks[15], (L, D_ATTN, D), D_ATTN ** -0.5),
        "w_ssm_branch": nrm(ks[16], (L, D_SSM, D), D_SSM ** -0.5),
        "w_out": nrm(ks[17], (L, D, D), D ** -0.5),
        "norm_moe": 1.0 + nrm(ks[18], (L, D), 0.02),
        "w_router_group": nrm(ks[19], (L, D, N_EXPERT_GROUPS), D ** -0.5),
        "b_router_group": nrm(ks[20], (L, N_EXPERT_GROUPS), 0.01),
        "w_router_expert": nrm(ks[21], (L, D, N_EXPERTS), D ** -0.5),
        "b_router_expert": nrm(ks[22], (L, N_EXPERTS), 0.01),
        "w_expert_gate": nrm(ks[23], (L, N_EXPERTS, D, D_FF_EXPERT), D ** -0.5),
        "w_expert_up": nrm(ks[24], (L, N_EXPERTS, D, D_FF_EXPERT), D ** -0.5),
        "w_expert_down": nrm(ks[25], (L, N_EXPERTS, D_FF_EXPERT, D), D_FF_EXPERT ** -0.5),
        "norm_final": 1.0 + nrm(ks[26], (D,), 0.02),
    }


def reference(x, norm_mix, w_in, b_gate, attn_sinks, ssm_a_re, ssm_a_im, ssm_b_re, ssm_b_im,
              ssm_c_re, ssm_c_im, ssm_d, ssm_log_dt, w_glu, b_glu, w_attn_branch, w_ssm_branch,
              w_out, norm_moe, w_router_group, b_router_group, w_router_expert, b_router_expert,
              w_expert_gate, w_expert_up, w_expert_down, norm_final):
    b, l, _ = x.shape
    splits = [Q_COLS, Q_COLS + KV_COLS, Q_COLS + 2 * KV_COLS, Q_COLS + 2 * KV_COLS + D_SSM]
    for i in range(DEPTH):
        h = rmsnorm(x, norm_mix[i])
        proj = h @ w_in[i]
        q, k, v, u, gate_logits = jnp.split(proj, splits, axis=-1)
        q = q.reshape(b, l, N_HEADS, HEAD_DIM)
        k = k.reshape(b, l, N_KV_HEADS, HEAD_DIM)
        v = v.reshape(b, l, N_KV_HEADS, HEAD_DIM)
        attn = sliding_window_attention(q, k, v, attn_sinks[i])
        y = s5_ssm(u, ssm_a_re[i], ssm_a_im[i], ssm_b_re[i], ssm_b_im[i],
                   ssm_c_re[i], ssm_c_im[i], ssm_d[i], ssm_log_dt[i]).astype(x.dtype)
        z = jax.nn.gelu(y)
        z = z * jax.nn.sigmoid(z @ w_glu[i] + b_glu[i])
        gates = jax.nn.sigmoid(gate_logits.astype(jnp.float32) + b_gate[i].astype(jnp.float32))
        g_attn, g_ssm = jnp.split(gates.astype(x.dtype), 2, axis=-1)
        merged = g_attn * (attn @ w_attn_branch[i]) + g_ssm * (z @ w_ssm_branch[i])
        x = x + merged @ w_out[i]
        h2 = rmsnorm(x, norm_moe[i])
        x = x + hierarchical_moe(h2, w_router_group[i], b_router_group[i], w_router_expert[i],
                                 b_router_expert[i], w_expert_gate[i], w_expert_up[i],
                                 w_expert_down[i])
    return rmsnorm(x, norm_final)
```

```python
import functools
import math

import jax
import jax.numpy as jnp
from jax import lax
from jax.experimental import pallas as pl
from jax.experimental.pallas import tpu as pltpu

EPS = 1e-6
HEAD_DIM = 64
N_HEADS = 8
N_KV_HEADS = 2
Q_PER_KV = N_HEADS // N_KV_HEADS
ATTN_BLOCK = 128
SSM_GROUP = 16
SSM_STATE = 64
N_EXPERT_GROUPS = 4
EXPERTS_PER_GROUP = 8
N_EXPERTS = N_EXPERT_GROUPS * EXPERTS_PER_GROUP
TOP_K = 2

LANES = 128
SUBLANES = 8
SSM_CHUNK_GROUPS = LANES // SSM_GROUP
SSM_CHUNK_STATES = SSM_CHUNK_GROUPS * SSM_STATE

TM_PROJ = 512
TM_SSM = 512
SSM_SEG = TM_SSM // SUBLANES
TM_MERGE = 512
TM_RANK = 512
TM_MOVE = 256
TM_EXPERT = 256
VMEM_LIMIT = 56 * 1024 * 1024


def _cparams(*sem):
    return pltpu.CompilerParams(dimension_semantics=sem, vmem_limit_bytes=VMEM_LIMIT)


def _proj_kernel(x_ref, g_ref, w_ref, bg_ref, q_ref, k_ref, v_ref, u_ref, gate_ref, *, cols):
    q_c, kv_c, d_ssm = cols
    xf = x_ref[...]
    inv = lax.rsqrt(jnp.mean(xf * xf, axis=-1, keepdims=True) + EPS)
    h = (xf * inv * g_ref[...]).astype(jnp.bfloat16)
    o = 0
    q_ref[...] = (jnp.dot(h, w_ref[:, o:o + q_c], preferred_element_type=jnp.float32)
                  * (1.0 / math.sqrt(HEAD_DIM))).astype(q_ref.dtype)
    o += q_c
    k_ref[...] = jnp.dot(h, w_ref[:, o:o + kv_c], preferred_element_type=jnp.float32).astype(k_ref.dtype)
    o += kv_c
    v_ref[...] = jnp.dot(h, w_ref[:, o:o + kv_c], preferred_element_type=jnp.float32).astype(v_ref.dtype)
    o += kv_c
    for s in range(d_ssm // LANES):
        u_ref[s] = jnp.dot(h, w_ref[:, o:o + LANES], preferred_element_type=jnp.float32)
        o += LANES
    gl = jnp.dot(h, w_ref[:, o:], preferred_element_type=jnp.float32) + bg_ref[...]
    gate_ref[...] = jax.nn.sigmoid(gl).astype(gate_ref.dtype)


def _in_proj(x2, norm_w, w_in, b_gate, d_attn, kv_cols, d_ssm):
    t, d = x2.shape
    gate_cols = b_gate.shape[-1]
    n_slab = d_ssm // LANES
    tm = TM_PROJ
    kern = functools.partial(_proj_kernel, cols=(d_attn, kv_cols, d_ssm))
    return pl.pallas_call(
        kern,
        grid=(t // tm,),
        in_specs=[
            pl.BlockSpec((tm, d), lambda i: (i, 0)),
            pl.BlockSpec((1, d), lambda i: (0, 0)),
            pl.BlockSpec(w_in.shape, lambda i: (0, 0)),
            pl.BlockSpec((1, gate_cols), lambda i: (0, 0)),
        ],
        out_specs=[
            pl.BlockSpec((tm, d_attn), lambda i: (i, 0)),
            pl.BlockSpec((tm, kv_cols), lambda i: (i, 0)),
            pl.BlockSpec((tm, kv_cols), lambda i: (i, 0)),
            pl.BlockSpec((n_slab, tm, LANES), lambda i: (0, i, 0)),
            pl.BlockSpec((tm, gate_cols), lambda i: (i, 0)),
        ],
        out_shape=[
            jax.ShapeDtypeStruct((t, d_attn), jnp.bfloat16),
            jax.ShapeDtypeStruct((t, kv_cols), jnp.bfloat16),
            jax.ShapeDtypeStruct((t, kv_cols), jnp.bfloat16),
            jax.ShapeDtypeStruct((n_slab, t, LANES), jnp.float32),
            jax.ShapeDtypeStruct((t, gate_cols), jnp.bfloat16),
        ],
        compiler_params=_cparams("arbitrary"),
        name="in_proj",
    )(x2, norm_w, w_in, b_gate)


def _attn_kernel(sink_ref, q_ref, kp_ref, kc_ref, vp_ref, vc_ref, rep_ref, o_ref):
    i = pl.program_id(1)
    blk = ATTN_BLOCK
    hw = Q_PER_KV * HEAD_DIM
    k2 = jnp.concatenate([kp_ref[...], kc_ref[...]], axis=0)
    v2 = jnp.concatenate([vp_ref[...], vc_ref[...]], axis=0)
    rows = Q_PER_KV * blk
    row = lax.broadcasted_iota(jnp.int32, (rows, 2 * blk), 0)
    col = lax.broadcasted_iota(jnp.int32, (rows, 2 * blk), 1)
    r = row & (blk - 1)
    valid = (col > r) & (col <= r + blk) & ((col >= blk) | (i > 0))
    head_of_row = row[:, :1] >> 7
    lane_head_q = lax.broadcasted_iota(jnp.int32, (blk, hw), 1) >> 6
    lane_head_v = lax.broadcasted_iota(jnp.int32, (2 * blk, hw), 1) >> 6
    neg = jnp.finfo(jnp.float32).min
    for kh in range(N_KV_HEADS):
        rep = rep_ref[kh]
        k4 = jnp.dot(k2, rep, preferred_element_type=jnp.float32).astype(jnp.bfloat16)
        v4 = jnp.dot(v2, rep, preferred_element_type=jnp.float32).astype(jnp.bfloat16)
        qh = q_ref[:, kh * hw:(kh + 1) * hw]
        qm = jnp.concatenate(
            [jnp.where(lane_head_q == g, qh, jnp.zeros_like(qh)) for g in range(Q_PER_KV)], axis=0)
        s = lax.dot_general(qm, k4, (((1,), (1,)), ((), ())), preferred_element_type=jnp.float32)
        s = jnp.where(valid, s, neg)
        sink = jnp.zeros((rows, 1), jnp.float32)
        for g in range(Q_PER_KV):
            sink = jnp.where(head_of_row == g, sink_ref[kh * Q_PER_KV + g], sink)
        m = jnp.maximum(jnp.max(s, axis=-1, keepdims=True), sink)
        p = jnp.exp(s - m)
        denom = jnp.sum(p, axis=-1, keepdims=True) + jnp.exp(sink - m)
        p = (p / denom).astype(jnp.bfloat16)
        p_cat = jnp.concatenate([p[g * blk:(g + 1) * blk, :] for g in range(Q_PER_KV)], axis=1)
        vm = jnp.concatenate(
            [jnp.where(lane_head_v == g, v4, jnp.zeros_like(v4)) for g in range(Q_PER_KV)], axis=0)
        o = jnp.dot(p_cat, vm, preferred_element_type=jnp.float32)
        o_ref[:, kh * hw:(kh + 1) * hw] = o.astype(o_ref.dtype)


def _attention(q, k, v, sinks, b, l):
    d_attn = q.shape[-1]
    kv_cols = k.shape[-1]
    blk = ATTN_BLOCK
    hw = Q_PER_KV * HEAD_DIM
    lane = jnp.arange(hw)[None, :]
    src = jnp.arange(kv_cols)[:, None]
    rep = jnp.stack([(src == kh * HEAD_DIM + (lane % HEAD_DIM)) for kh in range(N_KV_HEADS)]
                    ).astype(jnp.bfloat16)
    q3 = q.reshape(b, l, d_attn)
    k3 = k.reshape(b, l, kv_cols)
    v3 = v.reshape(b, l, kv_cols)
    cur = lambda bi, i: (bi, i, 0)
    prev = lambda bi, i: (bi, jnp.maximum(i - 1, 0), 0)
    out = pl.pallas_call(
        _attn_kernel,
        grid=(b, l // blk),
        in_specs=[
            pl.BlockSpec(memory_space=pltpu.SMEM),
            pl.BlockSpec((None, blk, d_attn), cur),
            pl.BlockSpec((None, blk, kv_cols), prev),
            pl.BlockSpec((None, blk, kv_cols), cur),
            pl.BlockSpec((None, blk, kv_cols), prev),
            pl.BlockSpec((None, blk, kv_cols), cur),
            pl.BlockSpec(rep.shape, lambda bi, i: (0, 0, 0)),
        ],
        out_specs=pl.BlockSpec((None, blk, d_attn), cur),
        out_shape=jax.ShapeDtypeStruct((b, l, d_attn), jnp.bfloat16),
        compiler_params=_cparams("arbitrary", "arbitrary"),
        name="swa",
    )(sinks, q3, k3, k3, v3, v3, rep)
    return out.reshape(b * l, d_attn)


def _ssm_kernel(u_ref, bmat_ref, cmat_ref, lam_ref, pw_ref, lamseg_ref, d_ref, y_ref,
                up_ref, bu_ref, yp_ref, carry_ref):
    i = pl.program_id(1)
    n_slab = u_ref.shape[0]
    tm = u_ref.shape[1]
    seg = tm // SUBLANES
    ns = SSM_CHUNK_STATES

    @pl.when(i == 0)
    def _():
        carry_ref[...] = jnp.zeros_like(carry_ref)

    sub = lax.broadcasted_iota(jnp.int32, (SUBLANES, ns), 0)
    for s in range(n_slab):
        u_slab = u_ref.at[s]
        for r in range(seg):
            up_ref[r * SUBLANES:(r + 1) * SUBLANES, :] = u_slab[pl.ds(r, SUBLANES, stride=seg), :]
        up = up_ref[...]
        bu_ref[...] = jnp.dot(up.astype(jnp.bfloat16), bmat_ref[s], preferred_element_type=jnp.float32)
        lr = jnp.broadcast_to(lam_ref[s, 0:1, :], (SUBLANES, ns))
        li = jnp.broadcast_to(lam_ref[s, 1:2, :], (SUBLANES, ns))

        def step(r, st):
            sr, si = st
            rows = pl.ds(pl.multiple_of(r * SUBLANES, SUBLANES), SUBLANES)
            nr = lr * sr - li * si + bu_ref[rows, 0:ns]
            ni = lr * si + li * sr + bu_ref[rows, ns:2 * ns]
            bu_ref[rows, 0:ns] = nr
            bu_ref[rows, ns:2 * ns] = ni
            return nr, ni

        zero = jnp.zeros((SUBLANES, ns), jnp.float32)
        er, ei = lax.fori_loop(0, seg, step, (zero, zero), unroll=4)

        ar = lamseg_ref[s, 0:1, :]
        ai = lamseg_ref[s, 1:2, :]
        cr = carry_ref[s, 0:1, :]
        ci = carry_ref[s, 1:2, :]
        car = jnp.zeros((SUBLANES, ns), jnp.float32)
        cai = jnp.zeros((SUBLANES, ns), jnp.float32)
        for j in range(SUBLANES):
            car = jnp.where(sub == j, jnp.broadcast_to(cr, (SUBLANES, ns)), car)
            cai = jnp.where(sub == j, jnp.broadcast_to(ci, (SUBLANES, ns)), cai)
            ejr = jnp.sum(jnp.where(sub == j, er, 0.0), axis=0, keepdims=True)
            eji = jnp.sum(jnp.where(sub == j, ei, 0.0), axis=0, keepdims=True)
            cr, ci = ar * cr - ai * ci + ejr, ar * ci + ai * cr + eji
        carry_ref[s, 0:1, :] = cr
        carry_ref[s, 1:2, :] = ci

        ctr = jnp.broadcast_to(car[None], (seg, SUBLANES, ns)).reshape(tm, ns)
        cti = jnp.broadcast_to(cai[None], (seg, SUBLANES, ns)).reshape(tm, ns)
        pr = pw_ref[s, 0]
        pi = pw_ref[s, 1]
        st_r = bu_ref[:, 0:ns] + pr * ctr - pi * cti
        st_i = bu_ref[:, ns:2 * ns] + pr * cti + pi * ctr
        st = jnp.concatenate([st_r, st_i], axis=1).astype(jnp.bfloat16)
        yp_ref[...] = jnp.dot(st, cmat_ref[s], preferred_element_type=jnp.float32) + d_ref[s] * up
        y_slab = y_ref.at[s]
        for r in range(seg):
            y_slab[pl.ds(r, SUBLANES, stride=seg), :] = yp_ref[r * SUBLANES:(r + 1) * SUBLANES, :]


def _ssm_tables(a_re, a_im, b_re, b_im, c_re, c_im, d_skip, log_dt, seg):
    f32 = jnp.float32
    g, p = a_re.shape
    c = b_re.shape[-1]
    ng = SSM_CHUNK_GROUPS
    n_slab = g // ng
    lam = lax.complex(a_re.astype(f32), a_im.astype(f32))
    dt = jnp.exp(log_dt.astype(f32))[:, None]
    lam_bar = jnp.exp(lam * dt)
    b_bar = ((lam_bar - 1.0) / lam)[:, :, None] * lax.complex(b_re.astype(f32), b_im.astype(f32))
    c_mat = lax.complex(c_re.astype(f32), c_im.astype(f32))
    eye = jnp.eye(ng, dtype=f32)

    def bdiag_b(m):
        m = m.reshape(n_slab, ng, p, c)
        return jnp.einsum('ab,kbpc->kacbp', eye, m).reshape(n_slab, ng * c, ng * p)

    def bdiag_c(m):
        m = m.reshape(n_slab, ng, c, p)
        return jnp.einsum('ab,kbcp->kbpac', eye, m).reshape(n_slab, ng * p, ng * c)

    bmat = jnp.concatenate([bdiag_b(jnp.real(b_bar)), bdiag_b(jnp.imag(b_bar))], axis=2).astype(jnp.bfloat16)
    cmat = jnp.concatenate([bdiag_c(jnp.real(c_mat)), -bdiag_c(jnp.imag(c_mat))], axis=1).astype(jnp.bfloat16)

    def slab_rows(z):
        z = z.reshape(n_slab, 1, ng * p)
        return jnp.concatenate([jnp.real(z), jnp.imag(z)], axis=1)

    lam_t = slab_rows(lam_bar)
    steps = jnp.arange(1, seg + 1, dtype=f32)
    pw = jnp.exp((lam * dt)[None] * steps[:, None, None])
    pw = pw.reshape(seg, n_slab, ng * p).transpose(1, 0, 2)
    pw = jnp.repeat(pw, SUBLANES, axis=1)
    pw_t = jnp.stack([jnp.real(pw), jnp.imag(pw)], axis=1)
    lamseg_t = slab_rows(jnp.exp(lam * dt * float(seg)))
    d_t = d_skip.astype(f32).reshape(n_slab, 1, ng * c)
    return bmat, cmat, lam_t, pw_t, lamseg_t, d_t


def _ssm(u4, tables, b, l):
    bmat, cmat, lam_t, pw_t, lamseg_t, d_t = tables
    n_slab, t, _ = u4.shape
    tm = TM_SSM
    nt = l // tm
    ns = SSM_CHUNK_STATES
    const = lambda nd: (lambda bi, i: (0,) * nd)
    return pl.pallas_call(
        _ssm_kernel,
        grid=(b, nt),
        in_specs=[
            pl.BlockSpec((n_slab, tm, LANES), lambda bi, i: (0, bi * nt + i, 0)),
            pl.BlockSpec(bmat.shape, const(3)),
            pl.BlockSpec(cmat.shape, const(3)),
            pl.BlockSpec(lam_t.shape, const(3)),
            pl.BlockSpec(pw_t.shape, const(4)),
            pl.BlockSpec(lamseg_t.shape, const(3)),
            pl.BlockSpec(d_t.shape, const(3)),
        ],
        out_specs=pl.BlockSpec((n_slab, tm, LANES), lambda bi, i: (0, bi * nt + i, 0)),
        out_shape=jax.ShapeDtypeStruct((n_slab, t, LANES), jnp.float32),
        scratch_shapes=[
            pltpu.VMEM((tm, LANES), jnp.float32),
            pltpu.VMEM((tm, 2 * ns), jnp.float32),
            pltpu.VMEM((tm, LANES), jnp.float32),
            pltpu.VMEM((n_slab, 2, ns), jnp.float32),
        ],
        compiler_params=_cparams("arbitrary", "arbitrary"),
        name="s5_scan",
    )(u4, bmat, cmat, lam_t, pw_t, lamseg_t, d_t)


def _merge_kernel(x_ref, attn_ref, y_ref, gate_ref, wglu_ref, bglu_ref, wa_ref, ws_ref, wo_ref,
                  nm_ref, wr_ref, br_ref, x1_ref, h2_ref, ids_ref, rw_ref):
    d = x_ref.shape[1]
    tm = x_ref.shape[0]
    y = jnp.concatenate([y_ref[s] for s in range(y_ref.shape[0])], axis=1)
    z = jax.nn.gelu(y)
    zg = jnp.dot(z.astype(jnp.bfloat16), wglu_ref[...], preferred_element_type=jnp.float32) + bglu_ref[...]
    z = z * jax.nn.sigmoid(zg)
    a = jnp.dot(attn_ref[...], wa_ref[...], preferred_element_type=jnp.float32)
    sb = jnp.dot(z.astype(jnp.bfloat16), ws_ref[...], preferred_element_type=jnp.float32)
    merged = gate_ref[:, 0:d].astype(jnp.float32) * a + gate_ref[:, d:2 * d].astype(jnp.float32) * sb
    x1 = x_ref[...] + jnp.dot(merged.astype(jnp.bfloat16), wo_ref[...], preferred_element_type=jnp.float32)
    x1_ref[...] = x1
    inv = lax.rsqrt(jnp.mean(x1 * x1, axis=-1, keepdims=True) + EPS)
    h2 = x1 * inv * nm_ref[...]
    h2_ref[...] = h2

    logits = jnp.dot(h2, wr_ref[...], preferred_element_type=jnp.float32,
                     precision=lax.Precision.HIGHEST) + br_ref[...]
    lane = lax.broadcasted_iota(jnp.int32, (tm, LANES), 1)
    ninf = -jnp.inf
    gl = jnp.where((lane >= N_EXPERTS) & (lane < N_EXPERTS + N_EXPERT_GROUPS), logits, ninf)
    gmax = jnp.max(gl, axis=-1, keepdims=True)
    gidx = jnp.min(jnp.where(gl == gmax, lane - N_EXPERTS, LANES), axis=-1, keepdims=True)
    group_p = 1.0 / jnp.sum(jnp.exp(gl - gmax), axis=-1, keepdims=True)
    el = jnp.where((lane < N_EXPERTS) & ((lane >> 3) == gidx), logits, ninf)
    m1 = jnp.max(el, axis=-1, keepdims=True)
    i1 = jnp.min(jnp.where(el == m1, lane, LANES), axis=-1, keepdims=True)
    el2 = jnp.where(lane == i1, ninf, el)
    m2 = jnp.max(el2, axis=-1, keepdims=True)
    i2 = jnp.min(jnp.where(el2 == m2, lane, LANES), axis=-1, keepdims=True)
    e2 = jnp.exp(m2 - m1)
    w1 = group_p / (1.0 + e2)
    w2 = group_p * e2 / (1.0 + e2)
    ids_ref[...] = jnp.where(lane == 0, i1, jnp.where(lane == 1, i2, 0))
    rw_ref[...] = jnp.where(lane == 0, w1, jnp.where(lane == 1, w2, 0.0))


def _merge(x2, attn, y4, gates, w_glu, b_glu, w_a, w_s, w_o, norm_moe, w_router, b_router):
    t, d = x2.shape
    tm = TM_MERGE
    n_slab = y4.shape[0]
    full = lambda a: pl.BlockSpec(a.shape, lambda i: (0,) * a.ndim)
    row = lambda c: pl.BlockSpec((tm, c), lambda i: (i, 0))
    return pl.pallas_call(
        _merge_kernel,
        grid=(t // tm,),
        in_specs=[
            row(d), row(attn.shape[1]),
            pl.BlockSpec((n_slab, tm, LANES), lambda i: (0, i, 0)),
            row(gates.shape[1]),
            full(w_glu), full(b_glu), full(w_a), full(w_s), full(w_o), full(norm_moe),
            full(w_router), full(b_router),
        ],
        out_specs=[row(d), row(d), row(LANES), row(LANES)],
        out_shape=[
            jax.ShapeDtypeStruct((t, d), jnp.float32),
            jax.ShapeDtypeStruct((t, d), jnp.float32),
            jax.ShapeDtypeStruct((t, LANES), jnp.int32),
            jax.ShapeDtypeStruct((t, LANES), jnp.float32),
        ],
        compiler_params=_cparams("arbitrary"),
        name="merge_router",
    )(x2, attn, y4, gates, w_glu, b_glu, w_a, w_s, w_o, norm_moe, w_router, b_router)


def _rank_kernel(ids_ref, rank_ref, cnt_ref, carry_ref):
    i = pl.program_id(0)
    tm = ids_ref.shape[0]

    @pl.when(i == 0)
    def _():
        carry_ref[...] = jnp.zeros_like(carry_ref)

    lane = lax.broadcasted_iota(jnp.int32, (tm, LANES), 1)
    ids = ids_ref[...]
    e0 = ids[:, 0:1]
    e1 = ids[:, 1:2]
    oh0 = lane == e0
    oh1 = lane == e1
    oh = oh0.astype(jnp.float32) + oh1.astype(jnp.float32)
    r_i = lax.broadcasted_iota(jnp.int32, (tm, tm), 0)
    c_i = lax.broadcasted_iota(jnp.int32, (tm, tm), 1)
    tri = (c_i < r_i).astype(jnp.bfloat16)
    cum = jnp.dot(tri, oh.astype(jnp.bfloat16), preferred_element_type=jnp.float32) + carry_ref[...]
    r0 = jnp.sum(jnp.where(oh0, cum, 0.0), axis=-1, keepdims=True)
    r1 = jnp.sum(jnp.where(oh1, cum, 0.0), axis=-1, keepdims=True)
    rank_ref[...] = jnp.where(lane == 0, r0, jnp.where(lane == 1, r1, 0.0)).astype(jnp.int32)
    carry_ref[...] = carry_ref[...] + jnp.sum(oh, axis=0, keepdims=True)
    cnt_ref[...] = carry_ref[...].astype(jnp.int32)


def _ranks(ids):
    t = ids.shape[0]
    tm = TM_RANK
    return pl.pallas_call(
        _rank_kernel,
        grid=(t // tm,),
        in_specs=[pl.BlockSpec((tm, LANES), lambda i: (i, 0))],
        out_specs=[pl.BlockSpec((tm, LANES), lambda i: (i, 0)),
                   pl.BlockSpec((1, LANES), lambda i: (0, 0))],
        out_shape=[jax.ShapeDtypeStruct((t, LANES), jnp.int32),
                   jax.ShapeDtypeStruct((1, LANES), jnp.int32)],
        scratch_shapes=[pltpu.VMEM((1, LANES), jnp.float32)],
        compiler_params=_cparams("arbitrary"),
        name="route_rank",
    )(ids)


def _dispatch_kernel(offs_ref, ids_ref, rank_ref, h_ref, xs_in_ref, xs_ref, pos_ref, sem):
    del xs_in_ref
    tm = h_ref.shape[0]

    def copy(tok, k, pos):
        return pltpu.make_async_copy(h_ref.at[pl.ds(tok, 1)], xs_ref.at[pl.ds(pos, 1)], sem)

    def issue(tok, c):
        for k in range(TOP_K):
            pos = offs_ref[ids_ref[k, tok]] + rank_ref[k, tok]
            pos_ref[k, tok] = pos
            copy(tok, k, pos).start()
        return c

    lax.fori_loop(0, tm, issue, 0)

    def drain(tok, c):
        for k in range(TOP_K):
            copy(tok, k, pos_ref[k, tok]).wait()
        return c

    lax.fori_loop(0, tm, drain, 0)


def _dispatch(offs, ids2, rank2, h2, n_pad):
    t, d = h2.shape
    tm = TM_MOVE
    xs0 = jnp.zeros((n_pad, d), h2.dtype)
    smem_blk = pl.BlockSpec((TOP_K, tm), lambda i: (0, i), memory_space=pltpu.SMEM)
    return pl.pallas_call(
        _dispatch_kernel,
        grid=(t // tm,),
        in_specs=[
            pl.BlockSpec(memory_space=pltpu.SMEM),
            smem_blk, smem_blk,
            pl.BlockSpec((tm, d), lambda i: (i, 0)),
            pl.BlockSpec(memory_space=pl.ANY),
        ],
        out_specs=[pl.BlockSpec(memory_space=pl.ANY), smem_blk],
        out_shape=[jax.ShapeDtypeStruct((n_pad, d), h2.dtype),
                   jax.ShapeDtypeStruct((TOP_K, t), jnp.int32)],
        scratch_shapes=[pltpu.SemaphoreType.DMA(())],
        input_output_aliases={4: 0},
        compiler_params=_cparams("arbitrary"),
        name="moe_dispatch",
    )(offs, ids2, rank2, h2, xs0)


def _expert_kernel(te_ref, nu_ref, xs_ref, wg_ref, wu_ref, wd_ref, ys_ref):
    j = pl.program_id(0)

    @pl.when(j < nu_ref[0])
    def _():
        x = xs_ref[...].astype(jnp.bfloat16)
        hg = jnp.dot(x, wg_ref[...].astype(jnp.bfloat16), preferred_element_type=jnp.float32)
        hu = jnp.dot(x, wu_ref[...].astype(jnp.bfloat16), preferred_element_type=jnp.float32)
        a = (jax.nn.silu(hg) * hu).astype(jnp.bfloat16)
        ys_ref[...] = jnp.dot(a, wd_ref[...].astype(jnp.bfloat16), preferred_element_type=jnp.float32)

    @pl.when(j >= nu_ref[0])
    def _():
        ys_ref[...] = jnp.zeros_like(ys_ref)


def _experts(tile_expert, n_used, xs, w_gate, w_up, w_down):
    n_pad, d = xs.shape
    ne, _, dff = w_gate.shape
    tm = TM_EXPERT
    rows = lambda j, te, nu: (jnp.minimum(j, nu[0] - 1), 0)
    grid_spec = pltpu.PrefetchScalarGridSpec(
        num_scalar_prefetch=2,
        grid=(n_pad // tm,),
        in_specs=[
            pl.BlockSpec((tm, d), rows),
            pl.BlockSpec((None, d, dff), lambda j, te, nu: (te[j], 0, 0)),
            pl.BlockSpec((None, d, dff), lambda j, te, nu: (te[j], 0, 0)),
            pl.BlockSpec((None, dff, d), lambda j, te, nu: (te[j], 0, 0)),
        ],
        out_specs=pl.BlockSpec((tm, d), lambda j, te, nu: (j, 0)),
    )
    return pl.pallas_call(
        _expert_kernel,
        grid_spec=grid_spec,
        out_shape=jax.ShapeDtypeStruct((n_pad, d), jnp.float32),
        compiler_params=_cparams("arbitrary"),
        name="moe_experts",
    )(tile_expert, n_used, xs, w_gate, w_up, w_down)


def _combine_kernel(pos_ref, x1_ref, rw_ref, nf_ref, ys_ref, o_ref, buf_ref, sem):
    tm = x1_ref.shape[0]

    def copy(tok, k):
        return pltpu.make_async_copy(ys_ref.at[pl.ds(pos_ref[k, tok], 1)],
                                     buf_ref.at[k, pl.ds(tok, 1)], sem)

    def issue(tok, c):
        for k in range(TOP_K):
            copy(tok, k).start()
        return c

    lax.fori_loop(0, tm, issue, 0)

    def drain(tok, c):
        for k in range(TOP_K):
            copy(tok, k).wait()
        return c

    lax.fori_loop(0, tm, drain, 0)
    rw = rw_ref[...]
    x = x1_ref[...] + rw[:, 0:1] * buf_ref[0] + rw[:, 1:2] * buf_ref[1]
    inv = lax.rsqrt(jnp.mean(x * x, axis=-1, keepdims=True) + EPS)
    o_ref[...] = x * inv * nf_ref[...]


def _combine(pos, x1, rw, norm_final, ys):
    t, d = x1.shape
    tm = TM_MOVE
    return pl.pallas_call(
        _combine_kernel,
        grid=(t // tm,),
        in_specs=[
            pl.BlockSpec((TOP_K, tm), lambda i: (0, i), memory_space=pltpu.SMEM),
            pl.BlockSpec((tm, d), lambda i: (i, 0)),
            pl.BlockSpec((tm, LANES), lambda i: (i, 0)),
            pl.BlockSpec((1, d), lambda i: (0, 0)),
            pl.BlockSpec(memory_space=pl.ANY),
        ],
        out_specs=pl.BlockSpec((tm, d), lambda i: (i, 0)),
        out_shape=jax.ShapeDtypeStruct((t, d), jnp.float32),
        scratch_shapes=[pltpu.VMEM((TOP_K, tm, d), jnp.float32), pltpu.SemaphoreType.DMA(())],
        compiler_params=_cparams("arbitrary"),
        name="moe_combine",
    )(pos, x1, rw, norm_final, ys)


def _moe(x1, h2, ids, rw, w_gate, w_up, w_down, norm_final):
    t, d = x1.shape
    rank, cnt = _ranks(ids)
    counts = cnt[0, :N_EXPERTS]
    padded = ((counts + TM_EXPERT - 1) // TM_EXPERT) * TM_EXPERT
    ends = jnp.cumsum(padded)
    offs = (ends - padded).astype(jnp.int32)
    n_pad = t * TOP_K + N_EXPERTS * TM_EXPERT
    n_tiles = n_pad // TM_EXPERT
    n_used = (ends[-1] // TM_EXPERT).astype(jnp.int32).reshape(1)
    tile_start = jnp.minimum(jnp.arange(n_tiles, dtype=jnp.int32), n_used[0] - 1) * TM_EXPERT
    tile_expert = jnp.minimum(jnp.searchsorted(ends, tile_start, side='right'), N_EXPERTS - 1).astype(jnp.int32)
    ids2 = ids[:, :TOP_K].T
    rank2 = rank[:, :TOP_K].T
    xs, pos = _dispatch(offs, ids2, rank2, h2, n_pad)
    ys = _experts(tile_expert, n_used, xs, w_gate, w_up, w_down)
    return _combine(pos, x1, rw, norm_final, ys)


def kernel(x, norm_mix, w_in, b_gate, attn_sinks, ssm_a_re, ssm_a_im, ssm_b_re, ssm_b_im, ssm_c_re, ssm_c_im, ssm_d, ssm_log_dt, w_glu, b_glu, w_attn_branch, w_ssm_branch, w_out, norm_moe, w_router_group, b_router_group, w_router_expert, b_router_expert, w_expert_gate, w_expert_up, w_expert_down, norm_final):
    b, l, d = x.shape
    depth = w_in.shape[0]
    assert depth == 1, "the final norm is fused into the last layer's combine kernel"
    d_attn = N_HEADS * HEAD_DIM
    kv_cols = N_KV_HEADS * HEAD_DIM
    d_ssm = ssm_d.shape[-1]
    bf16 = jnp.bfloat16
    x2 = x.reshape(b * l, d)
    i = 0
    q, k, v, u4, gates = _in_proj(x2, norm_mix[i][None], w_in[i].astype(bf16), b_gate[i][None],
                                  d_attn, kv_cols, d_ssm)
    attn = _attention(q, k, v, attn_sinks[i], b, l)
    tables = _ssm_tables(ssm_a_re[i], ssm_a_im[i], ssm_b_re[i], ssm_b_im[i], ssm_c_re[i], ssm_c_im[i],
                         ssm_d[i], ssm_log_dt[i], SSM_SEG)
    y4 = _ssm(u4, tables, b, l)
    pad = LANES - N_EXPERTS - N_EXPERT_GROUPS
    w_router = jnp.concatenate([w_router_expert[i], w_router_group[i], jnp.zeros((d, pad), jnp.float32)], axis=1)
    b_router = jnp.concatenate([b_router_expert[i], b_router_group[i], jnp.zeros((pad,), jnp.float32)])[None]
    x1, h2, ids, rw = _merge(x2, attn, y4, gates, w_glu[i].astype(bf16), b_glu[i][None],
                             w_attn_branch[i].astype(bf16), w_ssm_branch[i].astype(bf16),
                             w_out[i].astype(bf16), norm_moe[i][None], w_router, b_router)
    out = _moe(x1, h2, ids, rw, w_expert_gate[i], w_expert_up[i], w_expert_down[i], norm_final[None])
    return out.reshape(b, l, d)
```

```python
import functools
import math

import jax
import jax.numpy as jnp
from jax import lax
from jax.experimental import pallas as pl
from jax.experimental.pallas import tpu as pltpu

EPS = 1e-6
HEAD_DIM = 64
N_HEADS = 8
N_KV_HEADS = 2
Q_PER_KV = N_HEADS // N_KV_HEADS
ATTN_BLOCK = 128
SSM_GROUP = 16
SSM_STATE = 64
N_EXPERT_GROUPS = 4
EXPERTS_PER_GROUP = 8
N_EXPERTS = N_EXPERT_GROUPS * EXPERTS_PER_GROUP
TOP_K = 2

LANES = 128
SUBLANES = 8
SSM_CHUNK_GROUPS = LANES // SSM_GROUP
SSM_CHUNK_STATES = SSM_CHUNK_GROUPS * SSM_STATE

TM_PROJ = 512
TM_SSM = 512
SSM_SEG = TM_SSM // SUBLANES
TM_MERGE = 512
TM_RANK = 512
TM_MOVE = 256
TM_EXPERT = 256
VMEM_LIMIT = 56 * 1024 * 1024


def _cparams(*sem):
    return pltpu.CompilerParams(dimension_semantics=sem, vmem_limit_bytes=VMEM_LIMIT)


def _pack_bf16_pairs(x):
    half = x.shape[1] // 2
    bits = lax.bitcast_convert_type(x, jnp.uint32)
    return (bits[:, :half] & jnp.uint32(0xFFFF0000)) | (bits[:, half:] >> 16)


def _unpack_bf16_pairs(p):
    hi = lax.bitcast_convert_type(p & jnp.uint32(0xFFFF0000), jnp.float32)
    lo = lax.bitcast_convert_type(p << 16, jnp.float32)
    return jnp.concatenate([hi, lo], axis=1)


def _proj_kernel(x_ref, g_ref, w_ref, bg_ref, q_ref, k_ref, v_ref, u_ref, gate_ref, *, cols):
    q_c, kv_c, d_ssm = cols
    xf = x_ref[...]
    inv = lax.rsqrt(jnp.mean(xf * xf, axis=-1, keepdims=True) + EPS)
    h = (xf * inv * g_ref[...]).astype(jnp.bfloat16)
    o = 0
    q_ref[...] = (jnp.dot(h, w_ref[:, o:o + q_c], preferred_element_type=jnp.float32)
                  * (1.0 / math.sqrt(HEAD_DIM))).astype(q_ref.dtype)
    o += q_c
    k_ref[...] = jnp.dot(h, w_ref[:, o:o + kv_c], preferred_element_type=jnp.float32).astype(k_ref.dtype)
    o += kv_c
    v_ref[...] = jnp.dot(h, w_ref[:, o:o + kv_c], preferred_element_type=jnp.float32).astype(v_ref.dtype)
    o += kv_c
    for s in range(d_ssm // LANES):
        u_ref[s] = jnp.dot(h, w_ref[:, o:o + LANES], preferred_element_type=jnp.float32)
        o += LANES
    gl = jnp.dot(h, w_ref[:, o:], preferred_element_type=jnp.float32) + bg_ref[...]
    gate_ref[...] = jax.nn.sigmoid(gl).astype(gate_ref.dtype)


def _in_proj(x2, norm_w, w_in, b_gate, d_attn, kv_cols, d_ssm):
    t, d = x2.shape
    gate_cols = b_gate.shape[-1]
    n_slab = d_ssm // LANES
    tm = TM_PROJ
    kern = functools.partial(_proj_kernel, cols=(d_attn, kv_cols, d_ssm))
    return pl.pallas_call(
        kern,
        grid=(t // tm,),
        in_specs=[
            pl.BlockSpec((tm, d), lambda i: (i, 0)),
            pl.BlockSpec((1, d), lambda i: (0, 0)),
            pl.BlockSpec(w_in.shape, lambda i: (0, 0)),
            pl.BlockSpec((1, gate_cols), lambda i: (0, 0)),
        ],
        out_specs=[
            pl.BlockSpec((tm, d_attn), lambda i: (i, 0)),
            pl.BlockSpec((tm, kv_cols), lambda i: (i, 0)),
            pl.BlockSpec((tm, kv_cols), lambda i: (i, 0)),
            pl.BlockSpec((n_slab, tm, LANES), lambda i: (0, i, 0)),
            pl.BlockSpec((tm, gate_cols), lambda i: (i, 0)),
        ],
        out_shape=[
            jax.ShapeDtypeStruct((t, d_attn), jnp.bfloat16),
            jax.ShapeDtypeStruct((t, kv_cols), jnp.bfloat16),
            jax.ShapeDtypeStruct((t, kv_cols), jnp.bfloat16),
            jax.ShapeDtypeStruct((n_slab, t, LANES), jnp.float32),
            jax.ShapeDtypeStruct((t, gate_cols), jnp.bfloat16),
        ],
        compiler_params=_cparams("arbitrary"),
        name="in_proj",
    )(x2, norm_w, w_in, b_gate)


def _attn_kernel(sink_ref, q_ref, kp_ref, kc_ref, vp_ref, vc_ref, rep_ref, o_ref):
    i = pl.program_id(1)
    blk = ATTN_BLOCK
    hw = Q_PER_KV * HEAD_DIM
    k2 = jnp.concatenate([kp_ref[...], kc_ref[...]], axis=0)
    v2 = jnp.concatenate([vp_ref[...], vc_ref[...]], axis=0)
    rows = Q_PER_KV * blk
    row = lax.broadcasted_iota(jnp.int32, (rows, 2 * blk), 0)
    col = lax.broadcasted_iota(jnp.int32, (rows, 2 * blk), 1)
    r = row & (blk - 1)
    valid = (col > r) & (col <= r + blk) & ((col >= blk) | (i > 0))
    head_of_row = row[:, :1] >> 7
    lane_head_q = lax.broadcasted_iota(jnp.int32, (blk, hw), 1) >> 6
    lane_head_v = lax.broadcasted_iota(jnp.int32, (2 * blk, hw), 1) >> 6
    neg = jnp.finfo(jnp.float32).min
    for kh in range(N_KV_HEADS):
        rep = rep_ref[kh]
        k4 = jnp.dot(k2, rep, preferred_element_type=jnp.float32).astype(jnp.bfloat16)
        v4 = jnp.dot(v2, rep, preferred_element_type=jnp.float32).astype(jnp.bfloat16)
        qh = q_ref[:, kh * hw:(kh + 1) * hw]
        qm = jnp.concatenate(
            [jnp.where(lane_head_q == g, qh, jnp.zeros_like(qh)) for g in range(Q_PER_KV)], axis=0)
        s = lax.dot_general(qm, k4, (((1,), (1,)), ((), ())), preferred_element_type=jnp.float32)
        s = jnp.where(valid, s, neg)
        sink = jnp.zeros((rows, 1), jnp.float32)
        for g in range(Q_PER_KV):
            sink = jnp.where(head_of_row == g, sink_ref[kh * Q_PER_KV + g], sink)
        m = jnp.maximum(jnp.max(s, axis=-1, keepdims=True), sink)
        p = jnp.exp(s - m)
        denom = jnp.sum(p, axis=-1, keepdims=True) + jnp.exp(sink - m)
        p = (p / denom).astype(jnp.bfloat16)
        p_cat = jnp.concatenate([p[g * blk:(g + 1) * blk, :] for g in range(Q_PER_KV)], axis=1)
        vm = jnp.concatenate(
            [jnp.where(lane_head_v == g, v4, jnp.zeros_like(v4)) for g in range(Q_PER_KV)], axis=0)
        o = jnp.dot(p_cat, vm, preferred_element_type=jnp.float32)
        o_ref[:, kh * hw:(kh + 1) * hw] = o.astype(o_ref.dtype)


def _attention(q, k, v, sinks, b, l):
    d_attn = q.shape[-1]
    kv_cols = k.shape[-1]
    blk = ATTN_BLOCK
    hw = Q_PER_KV * HEAD_DIM
    lane = jnp.arange(hw)[None, :]
    src = jnp.arange(kv_cols)[:, None]
    rep = jnp.stack([(src == kh * HEAD_DIM + (lane % HEAD_DIM)) for kh in range(N_KV_HEADS)]
                    ).astype(jnp.bfloat16)
    q3 = q.reshape(b, l, d_attn)
    k3 = k.reshape(b, l, kv_cols)
    v3 = v.reshape(b, l, kv_cols)
    cur = lambda bi, i: (bi, i, 0)
    prev = lambda bi, i: (bi, jnp.maximum(i - 1, 0), 0)
    out = pl.pallas_call(
        _attn_kernel,
        grid=(b, l // blk),
        in_specs=[
            pl.BlockSpec(memory_space=pltpu.SMEM),
            pl.BlockSpec((None, blk, d_attn), cur),
            pl.BlockSpec((None, blk, kv_cols), prev),
            pl.BlockSpec((None, blk, kv_cols), cur),
            pl.BlockSpec((None, blk, kv_cols), prev),
            pl.BlockSpec((None, blk, kv_cols), cur),
            pl.BlockSpec(rep.shape, lambda bi, i: (0, 0, 0)),
        ],
        out_specs=pl.BlockSpec((None, blk, d_attn), cur),
        out_shape=jax.ShapeDtypeStruct((b, l, d_attn), jnp.bfloat16),
        compiler_params=_cparams("arbitrary", "arbitrary"),
        name="swa",
    )(sinks, q3, k3, k3, v3, v3, rep)
    return out.reshape(b * l, d_attn)


def _ssm_kernel(u_ref, bmat_ref, cmat_ref, lam_ref, pw_ref, lamseg_ref, d_ref, y_ref,
                up_ref, bu_ref, yp_ref, carry_ref):
    i = pl.program_id(1)
    n_slab = u_ref.shape[0]
    tm = u_ref.shape[1]
    seg = tm // SUBLANES
    ns = SSM_CHUNK_STATES

    @pl.when(i == 0)
    def _():
        carry_ref[...] = jnp.zeros_like(carry_ref)

    sub = lax.broadcasted_iota(jnp.int32, (SUBLANES, ns), 0)
    for s in range(n_slab):
        u_slab = u_ref.at[s]
        for r in range(seg):
            up_ref[r * SUBLANES:(r + 1) * SUBLANES, :] = u_slab[pl.ds(r, SUBLANES, stride=seg), :]
        up = up_ref[...]
        bu_ref[...] = jnp.dot(up.astype(jnp.bfloat16), bmat_ref[s], preferred_element_type=jnp.float32)
        lr = jnp.broadcast_to(lam_ref[s, 0:1, :], (SUBLANES, ns))
        li = jnp.broadcast_to(lam_ref[s, 1:2, :], (SUBLANES, ns))

        def step(r, st):
            sr, si = st
            rows = pl.ds(pl.multiple_of(r * SUBLANES, SUBLANES), SUBLANES)
            nr = lr * sr - li * si + bu_ref[rows, 0:ns]
            ni = lr * si + li * sr + bu_ref[rows, ns:2 * ns]
            bu_ref[rows, 0:ns] = nr
            bu_ref[rows, ns:2 * ns] = ni
            return nr, ni

        zero = jnp.zeros((SUBLANES, ns), jnp.float32)
        er, ei = lax.fori_loop(0, seg, step, (zero, zero), unroll=4)

        ar = lamseg_ref[s, 0:1, :]
        ai = lamseg_ref[s, 1:2, :]
        cr = carry_ref[s, 0:1, :]
        ci = carry_ref[s, 1:2, :]
        car = jnp.zeros((SUBLANES, ns), jnp.float32)
        cai = jnp.zeros((SUBLANES, ns), jnp.float32)
        for j in range(SUBLANES):
            car = jnp.where(sub == j, jnp.broadcast_to(cr, (SUBLANES, ns)), car)
            cai = jnp.where(sub == j, jnp.broadcast_to(ci, (SUBLANES, ns)), cai)
            ejr = jnp.sum(jnp.where(sub == j, er, 0.0), axis=0, keepdims=True)
            eji = jnp.sum(jnp.where(sub == j, ei, 0.0), axis=0, keepdims=True)
            cr, ci = ar * cr - ai * ci + ejr, ar * ci + ai * cr + eji
        carry_ref[s, 0:1, :] = cr
        carry_ref[s, 1:2, :] = ci

        ctr = jnp.broadcast_to(car[None], (seg, SUBLANES, ns)).reshape(tm, ns)
        cti = jnp.broadcast_to(cai[None], (seg, SUBLANES, ns)).reshape(tm, ns)
        pr = pw_ref[s, 0]
        pi = pw_ref[s, 1]
        st_r = bu_ref[:, 0:ns] + pr * ctr - pi * cti
        st_i = bu_ref[:, ns:2 * ns] + pr * cti + pi * ctr
        st = jnp.concatenate([st_r, st_i], axis=1).astype(jnp.bfloat16)
        yp_ref[...] = jnp.dot(st, cmat_ref[s], preferred_element_type=jnp.float32) + d_ref[s] * up
        y_slab = y_ref.at[s]
        for r in range(seg):
            y_slab[pl.ds(r, SUBLANES, stride=seg), :] = yp_ref[r * SUBLANES:(r + 1) * SUBLANES, :]


def _ssm_tables(a_re, a_im, b_re, b_im, c_re, c_im, d_skip, log_dt, seg):
    f32 = jnp.float32
    g, p = a_re.shape
    c = b_re.shape[-1]
    ng = SSM_CHUNK_GROUPS
    n_slab = g // ng
    lam = lax.complex(a_re.astype(f32), a_im.astype(f32))
    dt = jnp.exp(log_dt.astype(f32))[:, None]
    lam_bar = jnp.exp(lam * dt)
    b_bar = ((lam_bar - 1.0) / lam)[:, :, None] * lax.complex(b_re.astype(f32), b_im.astype(f32))
    c_mat = lax.complex(c_re.astype(f32), c_im.astype(f32))
    eye = jnp.eye(ng, dtype=f32)

    def bdiag_b(m):
        m = m.reshape(n_slab, ng, p, c)
        return jnp.einsum('ab,kbpc->kacbp', eye, m).reshape(n_slab, ng * c, ng * p)

    def bdiag_c(m):
        m = m.reshape(n_slab, ng, c, p)
        return jnp.einsum('ab,kbcp->kbpac', eye, m).reshape(n_slab, ng * p, ng * c)

    bmat = jnp.concatenate([bdiag_b(jnp.real(b_bar)), bdiag_b(jnp.imag(b_bar))], axis=2).astype(jnp.bfloat16)
    cmat = jnp.concatenate([bdiag_c(jnp.real(c_mat)), -bdiag_c(jnp.imag(c_mat))], axis=1).astype(jnp.bfloat16)

    def slab_rows(z):
        z = z.reshape(n_slab, 1, ng * p)
        return jnp.concatenate([jnp.real(z), jnp.imag(z)], axis=1)

    lam_t = slab_rows(lam_bar)
    steps = jnp.arange(1, seg + 1, dtype=f32)
    pw = jnp.exp((lam * dt)[None] * steps[:, None, None])
    pw = pw.reshape(seg, n_slab, ng * p).transpose(1, 0, 2)
    pw = jnp.repeat(pw, SUBLANES, axis=1)
    pw_t = jnp.stack([jnp.real(pw), jnp.imag(pw)], axis=1)
    lamseg_t = slab_rows(jnp.exp(lam * dt * float(seg)))
    d_t = d_skip.astype(f32).reshape(n_slab, 1, ng * c)
    return bmat, cmat, lam_t, pw_t, lamseg_t, d_t


def _ssm(u4, tables, b, l):
    bmat, cmat, lam_t, pw_t, lamseg_t, d_t = tables
    n_slab, t, _ = u4.shape
    tm = TM_SSM
    nt = l // tm
    ns = SSM_CHUNK_STATES
    const = lambda nd: (lambda bi, i: (0,) * nd)
    return pl.pallas_call(
        _ssm_kernel,
        grid=(b, nt),
        in_specs=[
            pl.BlockSpec((n_slab, tm, LANES), lambda bi, i: (0, bi * nt + i, 0)),
            pl.BlockSpec(bmat.shape, const(3)),
            pl.BlockSpec(cmat.shape, const(3)),
            pl.BlockSpec(lam_t.shape, const(3)),
            pl.BlockSpec(pw_t.shape, const(4)),
            pl.BlockSpec(lamseg_t.shape, const(3)),
            pl.BlockSpec(d_t.shape, const(3)),
        ],
        out_specs=pl.BlockSpec((n_slab, tm, LANES), lambda bi, i: (0, bi * nt + i, 0)),
        out_shape=jax.ShapeDtypeStruct((n_slab, t, LANES), jnp.float32),
        scratch_shapes=[
            pltpu.VMEM((tm, LANES), jnp.float32),
            pltpu.VMEM((tm, 2 * ns), jnp.float32),
            pltpu.VMEM((tm, LANES), jnp.float32),
            pltpu.VMEM((n_slab, 2, ns), jnp.float32),
        ],
        compiler_params=_cparams("arbitrary", "arbitrary"),
        name="s5_scan",
    )(u4, bmat, cmat, lam_t, pw_t, lamseg_t, d_t)


def _merge_kernel(x_ref, attn_ref, y_ref, gate_ref, wglu_ref, bglu_ref, wa_ref, ws_ref, wo_ref,
                  nm_ref, wr_ref, br_ref, x1_ref, h2_ref, ids_ref, rw_ref):
    d = x_ref.shape[1]
    tm = x_ref.shape[0]
    y = jnp.concatenate([y_ref[s] for s in range(y_ref.shape[0])], axis=1)
    z = jax.nn.gelu(y)
    zg = jnp.dot(z.astype(jnp.bfloat16), wglu_ref[...], preferred_element_type=jnp.float32) + bglu_ref[...]
    z = z * jax.nn.sigmoid(zg)
    a = jnp.dot(attn_ref[...], wa_ref[...], preferred_element_type=jnp.float32)
    sb = jnp.dot(z.astype(jnp.bfloat16), ws_ref[...], preferred_element_type=jnp.float32)
    merged = gate_ref[:, 0:d].astype(jnp.float32) * a + gate_ref[:, d:2 * d].astype(jnp.float32) * sb
    x1 = x_ref[...] + jnp.dot(merged.astype(jnp.bfloat16), wo_ref[...], preferred_element_type=jnp.float32)
    x1_ref[...] = x1
    inv = lax.rsqrt(jnp.mean(x1 * x1, axis=-1, keepdims=True) + EPS)
    h2 = x1 * inv * nm_ref[...]
    h_hi = h2.astype(jnp.bfloat16)
    h2_ref[...] = _pack_bf16_pairs(h_hi.astype(jnp.float32))

    h_lo = (h2 - h_hi.astype(jnp.float32)).astype(jnp.bfloat16)
    hh = jnp.dot(h_hi, wr_ref[...], preferred_element_type=jnp.float32)
    lh = jnp.dot(h_lo, wr_ref[:, 0:LANES], preferred_element_type=jnp.float32)
    logits = hh[:, 0:LANES] + hh[:, LANES:2 * LANES] + lh + br_ref[...]
    lane = lax.broadcasted_iota(jnp.int32, (tm, LANES), 1)
    ninf = -jnp.inf
    gl = jnp.where((lane >= N_EXPERTS) & (lane < N_EXPERTS + N_EXPERT_GROUPS), logits, ninf)
    gmax = jnp.max(gl, axis=-1, keepdims=True)
    gidx = jnp.min(jnp.where(gl == gmax, lane - N_EXPERTS, LANES), axis=-1, keepdims=True)
    group_p = 1.0 / jnp.sum(jnp.exp(gl - gmax), axis=-1, keepdims=True)
    el = jnp.where((lane < N_EXPERTS) & ((lane >> 3) == gidx), logits, ninf)
    m1 = jnp.max(el, axis=-1, keepdims=True)
    i1 = jnp.min(jnp.where(el == m1, lane, LANES), axis=-1, keepdims=True)
    el2 = jnp.where(lane == i1, ninf, el)
    m2 = jnp.max(el2, axis=-1, keepdims=True)
    i2 = jnp.min(jnp.where(el2 == m2, lane, LANES), axis=-1, keepdims=True)
    e2 = jnp.exp(m2 - m1)
    w1 = group_p / (1.0 + e2)
    w2 = group_p * e2 / (1.0 + e2)
    ids_ref[...] = jnp.where(lane == 0, i1, jnp.where(lane == 1, i2, 0))
    rw_ref[...] = jnp.where(lane == 0, w1, jnp.where(lane == 1, w2, 0.0))


def _merge(x2, attn, y4, gates, w_glu, b_glu, w_a, w_s, w_o, norm_moe, w_router, b_router):
    t, d = x2.shape
    tm = TM_MERGE
    n_slab = y4.shape[0]
    full = lambda a: pl.BlockSpec(a.shape, lambda i: (0,) * a.ndim)
    row = lambda c: pl.BlockSpec((tm, c), lambda i: (i, 0))
    return pl.pallas_call(
        _merge_kernel,
        grid=(t // tm,),
        in_specs=[
            row(d), row(attn.shape[1]),
            pl.BlockSpec((n_slab, tm, LANES), lambda i: (0, i, 0)),
            row(gates.shape[1]),
            full(w_glu), full(b_glu), full(w_a), full(w_s), full(w_o), full(norm_moe),
            full(w_router), full(b_router),
        ],
        out_specs=[row(d), row(d // 2), row(LANES), row(LANES)],
        out_shape=[
            jax.ShapeDtypeStruct((t, d), jnp.float32),
            jax.ShapeDtypeStruct((t, d // 2), jnp.uint32),
            jax.ShapeDtypeStruct((t, LANES), jnp.int32),
            jax.ShapeDtypeStruct((t, LANES), jnp.float32),
        ],
        compiler_params=_cparams("arbitrary"),
        name="merge_router",
    )(x2, attn, y4, gates, w_glu, b_glu, w_a, w_s, w_o, norm_moe, w_router, b_router)


def _rank_kernel(ids_ref, rank_ref, cnt_ref, carry_ref):
    i = pl.program_id(0)
    tm = ids_ref.shape[0]

    @pl.when(i == 0)
    def _():
        carry_ref[...] = jnp.zeros_like(carry_ref)

    lane = lax.broadcasted_iota(jnp.int32, (tm, LANES), 1)
    ids = ids_ref[...]
    e0 = ids[:, 0:1]
    e1 = ids[:, 1:2]
    oh0 = lane == e0
    oh1 = lane == e1
    oh = oh0.astype(jnp.float32) + oh1.astype(jnp.float32)
    r_i = lax.broadcasted_iota(jnp.int32, (tm, tm), 0)
    c_i = lax.broadcasted_iota(jnp.int32, (tm, tm), 1)
    tri = (c_i < r_i).astype(jnp.bfloat16)
    cum = jnp.dot(tri, oh.astype(jnp.bfloat16), preferred_element_type=jnp.float32) + carry_ref[...]
    r0 = jnp.sum(jnp.where(oh0, cum, 0.0), axis=-1, keepdims=True)
    r1 = jnp.sum(jnp.where(oh1, cum, 0.0), axis=-1, keepdims=True)
    rank_ref[...] = jnp.where(lane == 0, r0, jnp.where(lane == 1, r1, 0.0)).astype(jnp.int32)
    carry_ref[...] = carry_ref[...] + jnp.sum(oh, axis=0, keepdims=True)
    cnt_ref[...] = carry_ref[...].astype(jnp.int32)


def _ranks(ids):
    t = ids.shape[0]
    tm = TM_RANK
    return pl.pallas_call(
        _rank_kernel,
        grid=(t // tm,),
        in_specs=[pl.BlockSpec((tm, LANES), lambda i: (i, 0))],
        out_specs=[pl.BlockSpec((tm, LANES), lambda i: (i, 0)),
                   pl.BlockSpec((1, LANES), lambda i: (0, 0))],
        out_shape=[jax.ShapeDtypeStruct((t, LANES), jnp.int32),
                   jax.ShapeDtypeStruct((1, LANES), jnp.int32)],
        scratch_shapes=[pltpu.VMEM((1, LANES), jnp.float32)],
        compiler_params=_cparams("arbitrary"),
        name="route_rank",
    )(ids)


def _dispatch_kernel(pos_ref, h_ref, xs_in_ref, xs_ref, sem):
    del xs_in_ref
    tm = h_ref.shape[0]

    def issue(tok, c):
        for k in range(TOP_K):
            pltpu.make_async_copy(h_ref.at[pl.ds(tok, 1)], xs_ref.at[pl.ds(pos_ref[k, tok], 1)], sem).start()
        return c

    lax.fori_loop(0, tm, issue, 0, unroll=8)
    for k in range(TOP_K):
        pltpu.make_async_copy(h_ref, xs_ref.at[pl.ds(0, tm)], sem).wait()


def _dispatch(pos2, h2p, n_pad):
    t, dp = h2p.shape
    tm = TM_MOVE
    xs0 = jnp.zeros((n_pad, dp), h2p.dtype)
    return pl.pallas_call(
        _dispatch_kernel,
        grid=(t // tm,),
        in_specs=[
            pl.BlockSpec((TOP_K, tm), lambda i: (0, i), memory_space=pltpu.SMEM),
            pl.BlockSpec((tm, dp), lambda i: (i, 0)),
            pl.BlockSpec(memory_space=pl.ANY),
        ],
        out_specs=pl.BlockSpec(memory_space=pl.ANY),
        out_shape=jax.ShapeDtypeStruct((n_pad, dp), h2p.dtype),
        scratch_shapes=[pltpu.SemaphoreType.DMA(())],
        input_output_aliases={2: 0},
        compiler_params=_cparams("arbitrary"),
        name="moe_dispatch",
    )(pos2, h2p, xs0)


def _expert_kernel(te_ref, nu_ref, xs_ref, wg_ref, wu_ref, wd_ref, ys_ref):
    j = pl.program_id(0)

    @pl.when(j < nu_ref[0])
    def _():
        x = _unpack_bf16_pairs(xs_ref[...]).astype(jnp.bfloat16)
        hg = jnp.dot(x, wg_ref[...].astype(jnp.bfloat16), preferred_element_type=jnp.float32)
        hu = jnp.dot(x, wu_ref[...].astype(jnp.bfloat16), preferred_element_type=jnp.float32)
        a = (jax.nn.silu(hg) * hu).astype(jnp.bfloat16)
        y = jnp.dot(a, wd_ref[...].astype(jnp.bfloat16), preferred_element_type=jnp.float32)
        ys_ref[...] = _pack_bf16_pairs(y.astype(jnp.bfloat16).astype(jnp.float32))

    @pl.when(j >= nu_ref[0])
    def _():
        ys_ref[...] = jnp.zeros_like(ys_ref)


def _experts(tile_expert, n_used, xs, w_gate, w_up, w_down):
    n_pad, dp = xs.shape
    ne, d, dff = w_gate.shape
    tm = TM_EXPERT
    grid_spec = pltpu.PrefetchScalarGridSpec(
        num_scalar_prefetch=2,
        grid=(n_pad // tm,),
        in_specs=[
            pl.BlockSpec((tm, dp), lambda j, te, nu: (jnp.minimum(j, nu[0] - 1), 0)),
            pl.BlockSpec((None, d, dff), lambda j, te, nu: (te[j], 0, 0)),
            pl.BlockSpec((None, d, dff), lambda j, te, nu: (te[j], 0, 0)),
            pl.BlockSpec((None, dff, d), lambda j, te, nu: (te[j], 0, 0)),
        ],
        out_specs=pl.BlockSpec((tm, dp), lambda j, te, nu: (j, 0)),
    )
    return pl.pallas_call(
        _expert_kernel,
        grid_spec=grid_spec,
        out_shape=jax.ShapeDtypeStruct((n_pad, dp), jnp.uint32),
        compiler_params=_cparams("arbitrary"),
        name="moe_experts",
    )(tile_expert, n_used, xs, w_gate, w_up, w_down)


def _combine_kernel(pos_ref, x1_ref, rw_ref, nf_ref, ys_ref, o_ref, buf_ref, sem):
    i = pl.program_id(0)
    n = pl.num_programs(0)
    tm = x1_ref.shape[0]
    t = n * tm

    def issue(tile, slot):
        base = tile * tm

        def body(tok, c):
            for k in range(TOP_K):
                pltpu.make_async_copy(ys_ref.at[pl.ds(pos_ref[k * t + base + tok], 1)],
                                      buf_ref.at[slot, k, pl.ds(tok, 1)], sem.at[slot]).start()
            return c

        lax.fori_loop(0, tm, body, 0, unroll=8)

    @pl.when(i == 0)
    def _():
        issue(0, 0)

    @pl.when(i + 1 < n)
    def _():
        issue(i + 1, (i + 1) % 2)

    slot = i % 2
    for k in range(TOP_K):
        pltpu.make_async_copy(ys_ref.at[pl.ds(0, tm)], buf_ref.at[slot, k], sem.at[slot]).wait()
    rw = rw_ref[...]
    x = (x1_ref[...] + rw[:, 0:1] * _unpack_bf16_pairs(buf_ref[slot, 0])
         + rw[:, 1:2] * _unpack_bf16_pairs(buf_ref[slot, 1]))
    inv = lax.rsqrt(jnp.mean(x * x, axis=-1, keepdims=True) + EPS)
    o_ref[...] = x * inv * nf_ref[...]


def _combine(pos1d, x1, rw, norm_final, ys):
    t, d = x1.shape
    dp = ys.shape[1]
    tm = TM_MOVE
    return pl.pallas_call(
        _combine_kernel,
        grid=(t // tm,),
        in_specs=[
            pl.BlockSpec(memory_space=pltpu.SMEM),
            pl.BlockSpec((tm, d), lambda i: (i, 0)),
            pl.BlockSpec((tm, LANES), lambda i: (i, 0)),
            pl.BlockSpec((1, d), lambda i: (0, 0)),
            pl.BlockSpec(memory_space=pl.ANY),
        ],
        out_specs=pl.BlockSpec((tm, d), lambda i: (i, 0)),
        out_shape=jax.ShapeDtypeStruct((t, d), jnp.float32),
        scratch_shapes=[pltpu.VMEM((2, TOP_K, tm, dp), jnp.uint32), pltpu.SemaphoreType.DMA((2,))],
        compiler_params=_cparams("arbitrary"),
        name="moe_combine",
    )(pos1d, x1, rw, norm_final, ys)


def _moe(x1, h2p, ids, rw, w_gate, w_up, w_down, norm_final):
    t, d = x1.shape
    rank, cnt = _ranks(ids)
    counts = cnt[0, :N_EXPERTS]
    padded = ((counts + TM_EXPERT - 1) // TM_EXPERT) * TM_EXPERT
    ends = jnp.cumsum(padded)
    offs = (ends - padded).astype(jnp.int32)
    n_pad = t * TOP_K + N_EXPERTS * TM_EXPERT
    n_tiles = n_pad // TM_EXPERT
    n_used = (ends[-1] // TM_EXPERT).astype(jnp.int32).reshape(1)
    tile_start = jnp.minimum(jnp.arange(n_tiles, dtype=jnp.int32), n_used[0] - 1) * TM_EXPERT
    tile_expert = jnp.minimum(jnp.sum(tile_start[:, None] >= ends[None, :], axis=1),
                              N_EXPERTS - 1).astype(jnp.int32)
    ids2 = ids[:, :TOP_K].T
    pos2 = offs[ids2] + rank[:, :TOP_K].T
    xs = _dispatch(pos2, h2p, n_pad)
    ys = _experts(tile_expert, n_used, xs, w_gate, w_up, w_down)
    return _combine(pos2.reshape(-1), x1, rw, norm_final, ys)


def kernel(x, norm_mix, w_in, b_gate, attn_sinks, ssm_a_re, ssm_a_im, ssm_b_re, ssm_b_im, ssm_c_re, ssm_c_im, ssm_d, ssm_log_dt, w_glu, b_glu, w_attn_branch, w_ssm_branch, w_out, norm_moe, w_router_group, b_router_group, w_router_expert, b_router_expert, w_expert_gate, w_expert_up, w_expert_down, norm_final):
    b, l, d = x.shape
    depth = w_in.shape[0]
    assert depth == 1, "the final norm is fused into the last layer's combine kernel"
    d_attn = N_HEADS * HEAD_DIM
    kv_cols = N_KV_HEADS * HEAD_DIM
    d_ssm = ssm_d.shape[-1]
    bf16 = jnp.bfloat16
    x2 = x.reshape(b * l, d)
    i = 0
    q, k, v, u4, gates = _in_proj(x2, norm_mix[i][None], w_in[i].astype(bf16), b_gate[i][None],
                                  d_attn, kv_cols, d_ssm)
    attn = _attention(q, k, v, attn_sinks[i], b, l)
    tables = _ssm_tables(ssm_a_re[i], ssm_a_im[i], ssm_b_re[i], ssm_b_im[i], ssm_c_re[i], ssm_c_im[i],
                         ssm_d[i], ssm_log_dt[i], SSM_SEG)
    y4 = _ssm(u4, tables, b, l)
    pad = LANES - N_EXPERTS - N_EXPERT_GROUPS
    w_router = jnp.concatenate([w_router_expert[i], w_router_group[i], jnp.zeros((d, pad), jnp.float32)], axis=1)
    w_r_hi = w_router.astype(bf16)
    w_router = jnp.concatenate([w_r_hi, (w_router - w_r_hi.astype(jnp.float32)).astype(bf16)], axis=1)
    b_router = jnp.concatenate([b_router_expert[i], b_router_group[i], jnp.zeros((pad,), jnp.float32)])[None]
    x1, h2p, ids, rw = _merge(x2, attn, y4, gates, w_glu[i].astype(bf16), b_glu[i][None],
                             w_attn_branch[i].astype(bf16), w_ssm_branch[i].astype(bf16),
                             w_out[i].astype(bf16), norm_moe[i][None], w_router, b_router)
    out = _moe(x1, h2p, ids, rw, w_expert_gate[i], w_expert_up[i], w_expert_down[i], norm_final[None])
    return out.reshape(b, l, d)
```

```python
import functools
import math

import jax
import jax.numpy as jnp
from jax import lax
from jax.experimental import pallas as pl
from jax.experimental.pallas import tpu as pltpu

EPS = 1e-6
HEAD_DIM = 64
N_HEADS = 8
N_KV_HEADS = 2
Q_PER_KV = N_HEADS // N_KV_HEADS
ATTN_BLOCK = 128
SSM_GROUP = 16
SSM_STATE = 64
N_EXPERT_GROUPS = 4
EXPERTS_PER_GROUP = 8
N_EXPERTS = N_EXPERT_GROUPS * EXPERTS_PER_GROUP
TOP_K = 2

LANES = 128
SUBLANES = 8
SSM_CHUNK_GROUPS = LANES // SSM_GROUP
SSM_CHUNK_STATES = SSM_CHUNK_GROUPS * SSM_STATE

TM_PROJ = 512
TM_SSM = 512
SSM_SEG = TM_SSM // SUBLANES
TM_MERGE = 512
TM_RANK = 512
TM_MOVE = 256
TM_EXPERT = 256
VMEM_LIMIT = 56 * 1024 * 1024


def _cparams(*sem):
    return pltpu.CompilerParams(dimension_semantics=sem, vmem_limit_bytes=VMEM_LIMIT)


def _pack_bf16_pairs(x):
    half = x.shape[1] // 2
    bits = lax.bitcast_convert_type(x, jnp.uint32)
    return (bits[:, :half] & jnp.uint32(0xFFFF0000)) | (bits[:, half:] >> 16)


def _unpack_bf16_pairs(p):
    hi = lax.bitcast_convert_type(p & jnp.uint32(0xFFFF0000), jnp.float32)
    lo = lax.bitcast_convert_type(p << 16, jnp.float32)
    return jnp.concatenate([hi, lo], axis=1)


def _proj_kernel(x_ref, g_ref, w_ref, bg_ref, q_ref, k_ref, v_ref, u_ref, gate_ref, *, cols):
    q_c, kv_c, d_ssm = cols
    xf = x_ref[...]
    inv = lax.rsqrt(jnp.mean(xf * xf, axis=-1, keepdims=True) + EPS)
    h = (xf * inv * g_ref[...]).astype(jnp.bfloat16)
    o = 0
    q_ref[...] = (jnp.dot(h, w_ref[:, o:o + q_c], preferred_element_type=jnp.float32)
                  * (1.0 / math.sqrt(HEAD_DIM))).astype(q_ref.dtype)
    o += q_c
    k_ref[...] = jnp.dot(h, w_ref[:, o:o + kv_c], preferred_element_type=jnp.float32).astype(k_ref.dtype)
    o += kv_c
    v_ref[...] = jnp.dot(h, w_ref[:, o:o + kv_c], preferred_element_type=jnp.float32).astype(v_ref.dtype)
    o += kv_c
    for s in range(d_ssm // LANES):
        u_ref[s] = jnp.dot(h, w_ref[:, o:o + LANES], preferred_element_type=jnp.float32)
        o += LANES
    gl = jnp.dot(h, w_ref[:, o:], preferred_element_type=jnp.float32) + bg_ref[...]
    gate_ref[...] = jax.nn.sigmoid(gl).astype(gate_ref.dtype)


def _in_proj(x2, norm_w, w_in, b_gate, d_attn, kv_cols, d_ssm):
    t, d = x2.shape
    gate_cols = b_gate.shape[-1]
    n_slab = d_ssm // LANES
    tm = TM_PROJ
    kern = functools.partial(_proj_kernel, cols=(d_attn, kv_cols, d_ssm))
    return pl.pallas_call(
        kern,
        grid=(t // tm,),
        in_specs=[
            pl.BlockSpec((tm, d), lambda i: (i, 0)),
            pl.BlockSpec((1, d), lambda i: (0, 0)),
            pl.BlockSpec(w_in.shape, lambda i: (0, 0)),
            pl.BlockSpec((1, gate_cols), lambda i: (0, 0)),
        ],
        out_specs=[
            pl.BlockSpec((tm, d_attn), lambda i: (i, 0)),
            pl.BlockSpec((tm, kv_cols), lambda i: (i, 0)),
            pl.BlockSpec((tm, kv_cols), lambda i: (i, 0)),
            pl.BlockSpec((n_slab, tm, LANES), lambda i: (0, i, 0)),
            pl.BlockSpec((tm, gate_cols), lambda i: (i, 0)),
        ],
        out_shape=[
            jax.ShapeDtypeStruct((t, d_attn), jnp.bfloat16),
            jax.ShapeDtypeStruct((t, kv_cols), jnp.bfloat16),
            jax.ShapeDtypeStruct((t, kv_cols), jnp.bfloat16),
            jax.ShapeDtypeStruct((n_slab, t, LANES), jnp.float32),
            jax.ShapeDtypeStruct((t, gate_cols), jnp.bfloat16),
        ],
        compiler_params=_cparams("arbitrary"),
        name="in_proj",
    )(x2, norm_w, w_in, b_gate)


def _attn_kernel(sink_ref, q_ref, kp_ref, kc_ref, vp_ref, vc_ref, rep_ref, o_ref):
    i = pl.program_id(1)
    blk = ATTN_BLOCK
    hw = Q_PER_KV * HEAD_DIM
    k2 = jnp.concatenate([kp_ref[...], kc_ref[...]], axis=0)
    v2 = jnp.concatenate([vp_ref[...], vc_ref[...]], axis=0)
    rows = Q_PER_KV * blk
    row = lax.broadcasted_iota(jnp.int32, (rows, 2 * blk), 0)
    col = lax.broadcasted_iota(jnp.int32, (rows, 2 * blk), 1)
    r = row & (blk - 1)
    valid = (col > r) & (col <= r + blk) & ((col >= blk) | (i > 0))
    head_of_row = row[:, :1] >> 7
    lane_head_q = lax.broadcasted_iota(jnp.int32, (blk, hw), 1) >> 6
    lane_head_v = lax.broadcasted_iota(jnp.int32, (2 * blk, hw), 1) >> 6
    neg = jnp.finfo(jnp.float32).min
    for kh in range(N_KV_HEADS):
        rep = rep_ref[kh]
        k4 = jnp.dot(k2, rep, preferred_element_type=jnp.float32).astype(jnp.bfloat16)
        v4 = jnp.dot(v2, rep, preferred_element_type=jnp.float32).astype(jnp.bfloat16)
        qh = q_ref[:, kh * hw:(kh + 1) * hw]
        qm = jnp.concatenate(
            [jnp.where(lane_head_q == g, qh, jnp.zeros_like(qh)) for g in range(Q_PER_KV)], axis=0)
        s = lax.dot_general(qm, k4, (((1,), (1,)), ((), ())), preferred_element_type=jnp.float32)
        s = jnp.where(valid, s, neg)
        sink = jnp.zeros((rows, 1), jnp.float32)
        for g in range(Q_PER_KV):
            sink = jnp.where(head_of_row == g, sink_ref[kh * Q_PER_KV + g], sink)
        m = jnp.maximum(jnp.max(s, axis=-1, keepdims=True), sink)
        p = jnp.exp(s - m)
        denom = jnp.sum(p, axis=-1, keepdims=True) + jnp.exp(sink - m)
        p = (p / denom).astype(jnp.bfloat16)
        p_cat = jnp.concatenate([p[g * blk:(g + 1) * blk, :] for g in range(Q_PER_KV)], axis=1)
        vm = jnp.concatenate(
            [jnp.where(lane_head_v == g, v4, jnp.zeros_like(v4)) for g in range(Q_PER_KV)], axis=0)
        o = jnp.dot(p_cat, vm, preferred_element_type=jnp.float32)
        o_ref[:, kh * hw:(kh + 1) * hw] = o.astype(o_ref.dtype)


def _attention(q, k, v, sinks, b, l):
    d_attn = q.shape[-1]
    kv_cols = k.shape[-1]
    blk = ATTN_BLOCK
    hw = Q_PER_KV * HEAD_DIM
    lane = jnp.arange(hw)[None, :]
    src = jnp.arange(kv_cols)[:, None]
    rep = jnp.stack([(src == kh * HEAD_DIM + (lane % HEAD_DIM)) for kh in range(N_KV_HEADS)]
                    ).astype(jnp.bfloat16)
    q3 = q.reshape(b, l, d_attn)
    k3 = k.reshape(b, l, kv_cols)
    v3 = v.reshape(b, l, kv_cols)
    cur = lambda bi, i: (bi, i, 0)
    prev = lambda bi, i: (bi, jnp.maximum(i - 1, 0), 0)
    out = pl.pallas_call(
        _attn_kernel,
        grid=(b, l // blk),
        in_specs=[
            pl.BlockSpec(memory_space=pltpu.SMEM),
            pl.BlockSpec((None, blk, d_attn), cur),
            pl.BlockSpec((None, blk, kv_cols), prev),
            pl.BlockSpec((None, blk, kv_cols), cur),
            pl.BlockSpec((None, blk, kv_cols), prev),
            pl.BlockSpec((None, blk, kv_cols), cur),
            pl.BlockSpec(rep.shape, lambda bi, i: (0, 0, 0)),
        ],
        out_specs=pl.BlockSpec((None, blk, d_attn), cur),
        out_shape=jax.ShapeDtypeStruct((b, l, d_attn), jnp.bfloat16),
        compiler_params=_cparams("arbitrary", "arbitrary"),
        name="swa",
    )(sinks, q3, k3, k3, v3, v3, rep)
    return out.reshape(b * l, d_attn)


def _ssm_kernel(u_ref, bmat_ref, cmat_ref, lam_ref, pw_ref, lamseg_ref, d_ref, y_ref,
                up_ref, bu_ref, yp_ref, carry_ref):
    i = pl.program_id(1)
    n_slab = u_ref.shape[0]
    tm = u_ref.shape[1]
    seg = tm // SUBLANES
    ns = SSM_CHUNK_STATES

    @pl.when(i == 0)
    def _():
        carry_ref[...] = jnp.zeros_like(carry_ref)

    sub = lax.broadcasted_iota(jnp.int32, (SUBLANES, ns), 0)
    for s in range(n_slab):
        u_slab = u_ref.at[s]
        for r in range(seg):
            up_ref[r * SUBLANES:(r + 1) * SUBLANES, :] = u_slab[pl.ds(r, SUBLANES, stride=seg), :]
        up = up_ref[...]
        bu_ref[...] = jnp.dot(up.astype(jnp.bfloat16), bmat_ref[s], preferred_element_type=jnp.float32)
        lr = jnp.broadcast_to(lam_ref[s, 0:1, :], (SUBLANES, ns))
        li = jnp.broadcast_to(lam_ref[s, 1:2, :], (SUBLANES, ns))

        def step(r, st):
            sr, si = st
            rows = pl.ds(pl.multiple_of(r * SUBLANES, SUBLANES), SUBLANES)
            nr = lr * sr - li * si + bu_ref[rows, 0:ns]
            ni = lr * si + li * sr + bu_ref[rows, ns:2 * ns]
            bu_ref[rows, 0:ns] = nr
            bu_ref[rows, ns:2 * ns] = ni
            return nr, ni

        zero = jnp.zeros((SUBLANES, ns), jnp.float32)
        er, ei = lax.fori_loop(0, seg, step, (zero, zero), unroll=4)

        ar = lamseg_ref[s, 0:1, :]
        ai = lamseg_ref[s, 1:2, :]
        cr = carry_ref[s, 0:1, :]
        ci = carry_ref[s, 1:2, :]
        car = jnp.zeros((SUBLANES, ns), jnp.float32)
        cai = jnp.zeros((SUBLANES, ns), jnp.float32)
        for j in range(SUBLANES):
            car = jnp.where(sub == j, jnp.broadcast_to(cr, (SUBLANES, ns)), car)
            cai = jnp.where(sub == j, jnp.broadcast_to(ci, (SUBLANES, ns)), cai)
            ejr = jnp.sum(jnp.where(sub == j, er, 0.0), axis=0, keepdims=True)
            eji = jnp.sum(jnp.where(sub == j, ei, 0.0), axis=0, keepdims=True)
            cr, ci = ar * cr - ai * ci + ejr, ar * ci + ai * cr + eji
        carry_ref[s, 0:1, :] = cr
        carry_ref[s, 1:2, :] = ci

        ctr = jnp.broadcast_to(car[None], (seg, SUBLANES, ns)).reshape(tm, ns)
        cti = jnp.broadcast_to(cai[None], (seg, SUBLANES, ns)).reshape(tm, ns)
        pr = pw_ref[s, 0]
        pi = pw_ref[s, 1]
        st_r = bu_ref[:, 0:ns] + pr * ctr - pi * cti
        st_i = bu_ref[:, ns:2 * ns] + pr * cti + pi * ctr
        st = jnp.concatenate([st_r, st_i], axis=1).astype(jnp.bfloat16)
        yp_ref[...] = jnp.dot(st, cmat_ref[s], preferred_element_type=jnp.float32) + d_ref[s] * up
        y_slab = y_ref.at[s]
        for r in range(seg):
            y_slab[pl.ds(r, SUBLANES, stride=seg), :] = yp_ref[r * SUBLANES:(r + 1) * SUBLANES, :]


def _ssm_tables(a_re, a_im, b_re, b_im, c_re, c_im, d_skip, log_dt, seg):
    f32 = jnp.float32
    g, p = a_re.shape
    c = b_re.shape[-1]
    ng = SSM_CHUNK_GROUPS
    n_slab = g // ng
    lam = lax.complex(a_re.astype(f32), a_im.astype(f32))
    dt = jnp.exp(log_dt.astype(f32))[:, None]
    lam_bar = jnp.exp(lam * dt)
    b_bar = ((lam_bar - 1.0) / lam)[:, :, None] * lax.complex(b_re.astype(f32), b_im.astype(f32))
    c_mat = lax.complex(c_re.astype(f32), c_im.astype(f32))
    eye = jnp.eye(ng, dtype=f32)

    def bdiag_b(m):
        m = m.reshape(n_slab, ng, p, c)
        return jnp.einsum('ab,kbpc->kacbp', eye, m).reshape(n_slab, ng * c, ng * p)

    def bdiag_c(m):
        m = m.reshape(n_slab, ng, c, p)
        return jnp.einsum('ab,kbcp->kbpac', eye, m).reshape(n_slab, ng * p, ng * c)

    bmat = jnp.concatenate([bdiag_b(jnp.real(b_bar)), bdiag_b(jnp.imag(b_bar))], axis=2).astype(jnp.bfloat16)
    cmat = jnp.concatenate([bdiag_c(jnp.real(c_mat)), -bdiag_c(jnp.imag(c_mat))], axis=1).astype(jnp.bfloat16)

    def slab_rows(z):
        z = z.reshape(n_slab, 1, ng * p)
        return jnp.concatenate([jnp.real(z), jnp.imag(z)], axis=1)

    lam_t = slab_rows(lam_bar)
    steps = jnp.arange(1, seg + 1, dtype=f32)
    pw = jnp.exp((lam * dt)[None] * steps[:, None, None])
    pw = pw.reshape(seg, n_slab, ng * p).transpose(1, 0, 2)
    pw = jnp.repeat(pw, SUBLANES, axis=1)
    pw_t = jnp.stack([jnp.real(pw), jnp.imag(pw)], axis=1)
    lamseg_t = slab_rows(jnp.exp(lam * dt * float(seg)))
    d_t = d_skip.astype(f32).reshape(n_slab, 1, ng * c)
    return bmat, cmat, lam_t, pw_t, lamseg_t, d_t


def _ssm(u4, tables, b, l):
    bmat, cmat, lam_t, pw_t, lamseg_t, d_t = tables
    n_slab, t, _ = u4.shape
    tm = TM_SSM
    nt = l // tm
    ns = SSM_CHUNK_STATES
    const = lambda nd: (lambda bi, i: (0,) * nd)
    return pl.pallas_call(
        _ssm_kernel,
        grid=(b, nt),
        in_specs=[
            pl.BlockSpec((n_slab, tm, LANES), lambda bi, i: (0, bi * nt + i, 0)),
            pl.BlockSpec(bmat.shape, const(3)),
            pl.BlockSpec(cmat.shape, const(3)),
            pl.BlockSpec(lam_t.shape, const(3)),
            pl.BlockSpec(pw_t.shape, const(4)),
            pl.BlockSpec(lamseg_t.shape, const(3)),
            pl.BlockSpec(d_t.shape, const(3)),
        ],
        out_specs=pl.BlockSpec((n_slab, tm, LANES), lambda bi, i: (0, bi * nt + i, 0)),
        out_shape=jax.ShapeDtypeStruct((n_slab, t, LANES), jnp.float32),
        scratch_shapes=[
            pltpu.VMEM((tm, LANES), jnp.float32),
            pltpu.VMEM((tm, 2 * ns), jnp.float32),
            pltpu.VMEM((tm, LANES), jnp.float32),
            pltpu.VMEM((n_slab, 2, ns), jnp.float32),
        ],
        compiler_params=_cparams("arbitrary", "arbitrary"),
        name="s5_scan",
    )(u4, bmat, cmat, lam_t, pw_t, lamseg_t, d_t)


def _merge_kernel(x_ref, attn_ref, y_ref, gate_ref, wglu_ref, bglu_ref, wa_ref, ws_ref, wo_ref,
                  nm_ref, wr_ref, br_ref, x1_ref, h2_ref, ids_ref, rw_ref):
    d = x_ref.shape[1]
    tm = x_ref.shape[0]
    y = jnp.concatenate([y_ref[s] for s in range(y_ref.shape[0])], axis=1)
    z = jax.nn.gelu(y)
    zg = jnp.dot(z.astype(jnp.bfloat16), wglu_ref[...], preferred_element_type=jnp.float32) + bglu_ref[...]
    z = z * jax.nn.sigmoid(zg)
    a = jnp.dot(attn_ref[...], wa_ref[...], preferred_element_type=jnp.float32)
    sb = jnp.dot(z.astype(jnp.bfloat16), ws_ref[...], preferred_element_type=jnp.float32)
    merged = gate_ref[:, 0:d].astype(jnp.float32) * a + gate_ref[:, d:2 * d].astype(jnp.float32) * sb
    x1 = x_ref[...] + jnp.dot(merged.astype(jnp.bfloat16), wo_ref[...], preferred_element_type=jnp.float32)
    x1_ref[...] = x1
    inv = lax.rsqrt(jnp.mean(x1 * x1, axis=-1, keepdims=True) + EPS)
    h2 = x1 * inv * nm_ref[...]
    h_hi = h2.astype(jnp.bfloat16)
    h2_ref[...] = _pack_bf16_pairs(h_hi.astype(jnp.float32))

    h_lo = (h2 - h_hi.astype(jnp.float32)).astype(jnp.bfloat16)
    hh = jnp.dot(h_hi, wr_ref[...], preferred_element_type=jnp.float32)
    lh = jnp.dot(h_lo, wr_ref[:, 0:LANES], preferred_element_type=jnp.float32)
    logits = hh[:, 0:LANES] + hh[:, LANES:2 * LANES] + lh + br_ref[...]
    lane = lax.broadcasted_iota(jnp.int32, (tm, LANES), 1)
    ninf = -jnp.inf
    gl = jnp.where((lane >= N_EXPERTS) & (lane < N_EXPERTS + N_EXPERT_GROUPS), logits, ninf)
    gmax = jnp.max(gl, axis=-1, keepdims=True)
    gidx = jnp.min(jnp.where(gl == gmax, lane - N_EXPERTS, LANES), axis=-1, keepdims=True)
    group_p = 1.0 / jnp.sum(jnp.exp(gl - gmax), axis=-1, keepdims=True)
    el = jnp.where((lane < N_EXPERTS) & ((lane >> 3) == gidx), logits, ninf)
    m1 = jnp.max(el, axis=-1, keepdims=True)
    i1 = jnp.min(jnp.where(el == m1, lane, LANES), axis=-1, keepdims=True)
    el2 = jnp.where(lane == i1, ninf, el)
    m2 = jnp.max(el2, axis=-1, keepdims=True)
    i2 = jnp.min(jnp.where(el2 == m2, lane, LANES), axis=-1, keepdims=True)
    e2 = jnp.exp(m2 - m1)
    w1 = group_p / (1.0 + e2)
    w2 = group_p * e2 / (1.0 + e2)
    ids_ref[...] = jnp.where(lane == 0, i1, jnp.where(lane == 1, i2, 0))
    rw_ref[...] = jnp.where(lane == 0, w1, jnp.where(lane == 1, w2, 0.0))


def _merge(x2, attn, y4, gates, w_glu, b_glu, w_a, w_s, w_o, norm_moe, w_router, b_router):
    t, d = x2.shape
    tm = TM_MERGE
    n_slab = y4.shape[0]
    full = lambda a: pl.BlockSpec(a.shape, lambda i: (0,) * a.ndim)
    row = lambda c: pl.BlockSpec((tm, c), lambda i: (i, 0))
    return pl.pallas_call(
        _merge_kernel,
        grid=(t // tm,),
        in_specs=[
            row(d), row(attn.shape[1]),
            pl.BlockSpec((n_slab, tm, LANES), lambda i: (0, i, 0)),
            row(gates.shape[1]),
            full(w_glu), full(b_glu), full(w_a), full(w_s), full(w_o), full(norm_moe),
            full(w_router), full(b_router),
        ],
        out_specs=[row(d), row(d // 2), row(LANES), row(LANES)],
        out_shape=[
            jax.ShapeDtypeStruct((t, d), jnp.float32),
            jax.ShapeDtypeStruct((t, d // 2), jnp.uint32),
            jax.ShapeDtypeStruct((t, LANES), jnp.int32),
            jax.ShapeDtypeStruct((t, LANES), jnp.float32),
        ],
        compiler_params=_cparams("arbitrary"),
        name="merge_router",
    )(x2, attn, y4, gates, w_glu, b_glu, w_a, w_s, w_o, norm_moe, w_router, b_router)


def _rank_kernel(ids_ref, rank_ref, cnt_ref, carry_ref):
    i = pl.program_id(0)
    tm = ids_ref.shape[0]

    @pl.when(i == 0)
    def _():
        carry_ref[...] = jnp.zeros_like(carry_ref)

    lane = lax.broadcasted_iota(jnp.int32, (tm, LANES), 1)
    ids = ids_ref[...]
    e0 = ids[:, 0:1]
    e1 = ids[:, 1:2]
    oh0 = lane == e0
    oh1 = lane == e1
    oh = oh0.astype(jnp.float32) + oh1.astype(jnp.float32)
    r_i = lax.broadcasted_iota(jnp.int32, (tm, tm), 0)
    c_i = lax.broadcasted_iota(jnp.int32, (tm, tm), 1)
    tri = (c_i < r_i).astype(jnp.bfloat16)
    cum = jnp.dot(tri, oh.astype(jnp.bfloat16), preferred_element_type=jnp.float32) + carry_ref[...]
    r0 = jnp.sum(jnp.where(oh0, cum, 0.0), axis=-1, keepdims=True)
    r1 = jnp.sum(jnp.where(oh1, cum, 0.0), axis=-1, keepdims=True)
    rank_ref[...] = jnp.where(lane == 0, r0, jnp.where(lane == 1, r1, 0.0)).astype(jnp.int32)
    carry_ref[...] = carry_ref[...] + jnp.sum(oh, axis=0, keepdims=True)
    cnt_ref[...] = carry_ref[...].astype(jnp.int32)


def _ranks(ids):
    t = ids.shape[0]
    tm = TM_RANK
    return pl.pallas_call(
        _rank_kernel,
        grid=(t // tm,),
        in_specs=[pl.BlockSpec((tm, LANES), lambda i: (i, 0))],
        out_specs=[pl.BlockSpec((tm, LANES), lambda i: (i, 0)),
                   pl.BlockSpec((1, LANES), lambda i: (0, 0))],
        out_shape=[jax.ShapeDtypeStruct((t, LANES), jnp.int32),
                   jax.ShapeDtypeStruct((1, LANES), jnp.int32)],
        scratch_shapes=[pltpu.VMEM((1, LANES), jnp.float32)],
        compiler_params=_cparams("arbitrary"),
        name="route_rank",
    )(ids)


def _dispatch_kernel(pos_ref, h_ref, xs_in_ref, xs_ref, sem):
    del xs_in_ref
    tm = h_ref.shape[0]
    t = pl.num_programs(0) * tm
    base = pl.program_id(0) * tm
    for k in range(TOP_K):
        for tok in range(tm):
            pltpu.make_async_copy(h_ref.at[pl.ds(tok, 1)],
                                  xs_ref.at[pl.ds(pos_ref[k * t + base + tok], 1)], sem).start()
    for k in range(TOP_K):
        pltpu.make_async_copy(h_ref, xs_ref.at[pl.ds(0, tm)], sem).wait()


def _dispatch(pos1d, h2p, n_pad):
    t, dp = h2p.shape
    tm = TM_MOVE
    xs0 = jnp.zeros((n_pad, dp), h2p.dtype)
    return pl.pallas_call(
        _dispatch_kernel,
        grid=(t // tm,),
        in_specs=[
            pl.BlockSpec(memory_space=pltpu.SMEM),
            pl.BlockSpec((tm, dp), lambda i: (i, 0)),
            pl.BlockSpec(memory_space=pl.ANY),
        ],
        out_specs=pl.BlockSpec(memory_space=pl.ANY),
        out_shape=jax.ShapeDtypeStruct((n_pad, dp), h2p.dtype),
        scratch_shapes=[pltpu.SemaphoreType.DMA(())],
        input_output_aliases={2: 0},
        compiler_params=_cparams("arbitrary"),
        name="moe_dispatch",
    )(pos1d, h2p, xs0)


def _expert_kernel(te_ref, nu_ref, xs_ref, wg_ref, wu_ref, wd_ref, ys_ref):
    j = pl.program_id(0)

    @pl.when(j < nu_ref[0])
    def _():
        x = _unpack_bf16_pairs(xs_ref[...]).astype(jnp.bfloat16)
        hg = jnp.dot(x, wg_ref[...].astype(jnp.bfloat16), preferred_element_type=jnp.float32)
        hu = jnp.dot(x, wu_ref[...].astype(jnp.bfloat16), preferred_element_type=jnp.float32)
        a = (jax.nn.silu(hg) * hu).astype(jnp.bfloat16)
        y = jnp.dot(a, wd_ref[...].astype(jnp.bfloat16), preferred_element_type=jnp.float32)
        ys_ref[...] = _pack_bf16_pairs(y.astype(jnp.bfloat16).astype(jnp.float32))

    @pl.when(j >= nu_ref[0])
    def _():
        ys_ref[...] = jnp.zeros_like(ys_ref)


def _experts(tile_expert, n_used, xs, w_gate, w_up, w_down):
    n_pad, dp = xs.shape
    ne, d, dff = w_gate.shape
    tm = TM_EXPERT
    grid_spec = pltpu.PrefetchScalarGridSpec(
        num_scalar_prefetch=2,
        grid=(n_pad // tm,),
        in_specs=[
            pl.BlockSpec((tm, dp), lambda j, te, nu: (jnp.minimum(j, nu[0] - 1), 0)),
            pl.BlockSpec((None, d, dff), lambda j, te, nu: (te[j], 0, 0)),
            pl.BlockSpec((None, d, dff), lambda j, te, nu: (te[j], 0, 0)),
            pl.BlockSpec((None, dff, d), lambda j, te, nu: (te[j], 0, 0)),
        ],
        out_specs=pl.BlockSpec((tm, dp), lambda j, te, nu: (j, 0)),
    )
    return pl.pallas_call(
        _expert_kernel,
        grid_spec=grid_spec,
        out_shape=jax.ShapeDtypeStruct((n_pad, dp), jnp.uint32),
        compiler_params=_cparams("arbitrary"),
        name="moe_experts",
    )(tile_expert, n_used, xs, w_gate, w_up, w_down)


def _combine_kernel(pos_ref, x1_ref, rw_ref, nf_ref, ys_ref, o_ref, buf_ref, sem):
    i = pl.program_id(0)
    n = pl.num_programs(0)
    tm = x1_ref.shape[0]
    t = n * tm

    def issue(tile, slot):
        base = tile * tm
        for k in range(TOP_K):
            for tok in range(tm):
                pltpu.make_async_copy(ys_ref.at[pl.ds(pos_ref[k * t + base + tok], 1)],
                                      buf_ref.at[slot, k, pl.ds(tok, 1)], sem.at[slot]).start()

    @pl.when(i == 0)
    def _():
        issue(0, 0)

    for parity in range(2):
        @pl.when((i + 1 < n) & ((i + 1) % 2 == parity))
        def _():
            issue(i + 1, parity)

    slot = i % 2
    for k in range(TOP_K):
        pltpu.make_async_copy(ys_ref.at[pl.ds(0, tm)], buf_ref.at[slot, k], sem.at[slot]).wait()
    rw = rw_ref[...]
    x = (x1_ref[...] + rw[:, 0:1] * _unpack_bf16_pairs(buf_ref[slot, 0])
         + rw[:, 1:2] * _unpack_bf16_pairs(buf_ref[slot, 1]))
    inv = lax.rsqrt(jnp.mean(x * x, axis=-1, keepdims=True) + EPS)
    o_ref[...] = x * inv * nf_ref[...]


def _combine(pos1d, x1, rw, norm_final, ys):
    t, d = x1.shape
    dp = ys.shape[1]
    tm = TM_MOVE
    return pl.pallas_call(
        _combine_kernel,
        grid=(t // tm,),
        in_specs=[
            pl.BlockSpec(memory_space=pltpu.SMEM),
            pl.BlockSpec((tm, d), lambda i: (i, 0)),
            pl.BlockSpec((tm, LANES), lambda i: (i, 0)),
            pl.BlockSpec((1, d), lambda i: (0, 0)),
            pl.BlockSpec(memory_space=pl.ANY),
        ],
        out_specs=pl.BlockSpec((tm, d), lambda i: (i, 0)),
        out_shape=jax.ShapeDtypeStruct((t, d), jnp.float32),
        scratch_shapes=[pltpu.VMEM((2, TOP_K, tm, dp), jnp.uint32), pltpu.SemaphoreType.DMA((2,))],
        compiler_params=_cparams("arbitrary"),
        name="moe_combine",
    )(pos1d, x1, rw, norm_final, ys)


def _moe(x1, h2p, ids, rw, w_gate, w_up, w_down, norm_final):
    t, d = x1.shape
    rank, cnt = _ranks(ids)
    counts = cnt[0, :N_EXPERTS]
    padded = ((counts + TM_EXPERT - 1) // TM_EXPERT) * TM_EXPERT
    ends = jnp.cumsum(padded)
    offs = (ends - padded).astype(jnp.int32)
    n_pad = t * TOP_K + N_EXPERTS * TM_EXPERT
    n_tiles = n_pad // TM_EXPERT
    n_used = (ends[-1] // TM_EXPERT).astype(jnp.int32).reshape(1)
    tile_start = jnp.minimum(jnp.arange(n_tiles, dtype=jnp.int32), n_used[0] - 1) * TM_EXPERT
    tile_expert = jnp.minimum(jnp.sum(tile_start[:, None] >= ends[None, :], axis=1),
                              N_EXPERTS - 1).astype(jnp.int32)
    ids2 = ids[:, :TOP_K].T
    onehot = ids2[:, :, None] == jnp.arange(N_EXPERTS, dtype=jnp.int32)
    pos1d = (jnp.sum(jnp.where(onehot, offs, 0), axis=-1) + rank[:, :TOP_K].T).reshape(-1)
    xs = _dispatch(pos1d, h2p, n_pad)
    ys = _experts(tile_expert, n_used, xs, w_gate, w_up, w_down)
    return _combine(pos1d, x1, rw, norm_final, ys)


def kernel(x, norm_mix, w_in, b_gate, attn_sinks, ssm_a_re, ssm_a_im, ssm_b_re, ssm_b_im, ssm_c_re, ssm_c_im, ssm_d, ssm_log_dt, w_glu, b_glu, w_attn_branch, w_ssm_branch, w_out, norm_moe, w_router_group, b_router_group, w_router_expert, b_router_expert, w_expert_gate, w_expert_up, w_expert_down, norm_final):
    b, l, d = x.shape
    depth = w_in.shape[0]
    assert depth == 1, "the final norm is fused into the last layer's combine kernel"
    d_attn = N_HEADS * HEAD_DIM
    kv_cols = N_KV_HEADS * HEAD_DIM
    d_ssm = ssm_d.shape[-1]
    bf16 = jnp.bfloat16
    x2 = x.reshape(b * l, d)
    i = 0
    q, k, v, u4, gates = _in_proj(x2, norm_mix[i][None], w_in[i].astype(bf16), b_gate[i][None],
                                  d_attn, kv_cols, d_ssm)
    attn = _attention(q, k, v, attn_sinks[i], b, l)
    tables = _ssm_tables(ssm_a_re[i], ssm_a_im[i], ssm_b_re[i], ssm_b_im[i], ssm_c_re[i], ssm_c_im[i],
                         ssm_d[i], ssm_log_dt[i], SSM_SEG)
    y4 = _ssm(u4, tables, b, l)
    pad = LANES - N_EXPERTS - N_EXPERT_GROUPS
    w_router = jnp.concatenate([w_router_expert[i], w_router_group[i], jnp.zeros((d, pad), jnp.float32)], axis=1)
    w_r_hi = w_router.astype(bf16)
    w_router = jnp.concatenate([w_r_hi, (w_router - w_r_hi.astype(jnp.float32)).astype(bf16)], axis=1)
    b_router = jnp.concatenate([b_router_expert[i], b_router_group[i], jnp.zeros((pad,), jnp.float32)])[None]
    x1, h2p, ids, rw = _merge(x2, attn, y4, gates, w_glu[i].astype(bf16), b_glu[i][None],
                             w_attn_branch[i].astype(bf16), w_ssm_branch[i].astype(bf16),
                             w_out[i].astype(bf16), norm_moe[i][None], w_router, b_router)
    out = _moe(x1, h2p, ids, rw, w_expert_gate[i], w_expert_up[i], w_expert_down[i], norm_final[None])
    return out.reshape(b, l, d)
```

```python
import functools
import math

import jax
import jax.numpy as jnp
from jax import lax
from jax.experimental import pallas as pl
from jax.experimental.pallas import tpu as pltpu

EPS = 1e-6
HEAD_DIM = 64
N_HEADS = 8
N_KV_HEADS = 2
Q_PER_KV = N_HEADS // N_KV_HEADS
ATTN_BLOCK = 128
SSM_GROUP = 16
SSM_STATE = 64
N_EXPERT_GROUPS = 4
EXPERTS_PER_GROUP = 8
N_EXPERTS = N_EXPERT_GROUPS * EXPERTS_PER_GROUP
TOP_K = 2

LANES = 128
SUBLANES = 8
SSM_CHUNK_GROUPS = LANES // SSM_GROUP
SSM_CHUNK_STATES = SSM_CHUNK_GROUPS * SSM_STATE

TM_PROJ = 512
TM_SSM = 512
SSM_SEG = TM_SSM // SUBLANES
TM_MERGE = 512
TM_RANK = 512
TM_MOVE = 256
TM_EXPERT = 256
VMEM_LIMIT = 56 * 1024 * 1024


def _cparams(*sem):
    return pltpu.CompilerParams(dimension_semantics=sem, vmem_limit_bytes=VMEM_LIMIT)


def _pack_bf16_pairs(x):
    half = x.shape[1] // 2
    bits = lax.bitcast_convert_type(x, jnp.uint32)
    return (bits[:, :half] & jnp.uint32(0xFFFF0000)) | (bits[:, half:] >> 16)


def _unpack_bf16_pairs(p):
    hi = lax.bitcast_convert_type(p & jnp.uint32(0xFFFF0000), jnp.float32)
    lo = lax.bitcast_convert_type(p << 16, jnp.float32)
    return jnp.concatenate([hi, lo], axis=1)


def _proj_kernel(x_ref, g_ref, w_ref, bg_ref, q_ref, k_ref, v_ref, u_ref, gate_ref, *, cols):
    q_c, kv_c, d_ssm = cols
    xf = x_ref[...]
    inv = lax.rsqrt(jnp.mean(xf * xf, axis=-1, keepdims=True) + EPS)
    h = (xf * inv * g_ref[...]).astype(jnp.bfloat16)
    o = 0
    q_ref[...] = (jnp.dot(h, w_ref[:, o:o + q_c], preferred_element_type=jnp.float32)
                  * (1.0 / math.sqrt(HEAD_DIM))).astype(q_ref.dtype)
    o += q_c
    k_ref[...] = jnp.dot(h, w_ref[:, o:o + kv_c], preferred_element_type=jnp.float32).astype(k_ref.dtype)
    o += kv_c
    v_ref[...] = jnp.dot(h, w_ref[:, o:o + kv_c], preferred_element_type=jnp.float32).astype(v_ref.dtype)
    o += kv_c
    for s in range(d_ssm // LANES):
        u_ref[s] = jnp.dot(h, w_ref[:, o:o + LANES], preferred_element_type=jnp.float32)
        o += LANES
    gl = jnp.dot(h, w_ref[:, o:], preferred_element_type=jnp.float32) + bg_ref[...]
    gate_ref[...] = jax.nn.sigmoid(gl).astype(gate_ref.dtype)


def _in_proj(x2, norm_w, w_in, b_gate, d_attn, kv_cols, d_ssm):
    t, d = x2.shape
    gate_cols = b_gate.shape[-1]
    n_slab = d_ssm // LANES
    tm = TM_PROJ
    kern = functools.partial(_proj_kernel, cols=(d_attn, kv_cols, d_ssm))
    return pl.pallas_call(
        kern,
        grid=(t // tm,),
        in_specs=[
            pl.BlockSpec((tm, d), lambda i: (i, 0)),
            pl.BlockSpec((1, d), lambda i: (0, 0)),
            pl.BlockSpec(w_in.shape, lambda i: (0, 0)),
            pl.BlockSpec((1, gate_cols), lambda i: (0, 0)),
        ],
        out_specs=[
            pl.BlockSpec((tm, d_attn), lambda i: (i, 0)),
            pl.BlockSpec((tm, kv_cols), lambda i: (i, 0)),
            pl.BlockSpec((tm, kv_cols), lambda i: (i, 0)),
            pl.BlockSpec((n_slab, tm, LANES), lambda i: (0, i, 0)),
            pl.BlockSpec((tm, gate_cols), lambda i: (i, 0)),
        ],
        out_shape=[
            jax.ShapeDtypeStruct((t, d_attn), jnp.bfloat16),
            jax.ShapeDtypeStruct((t, kv_cols), jnp.bfloat16),
            jax.ShapeDtypeStruct((t, kv_cols), jnp.bfloat16),
            jax.ShapeDtypeStruct((n_slab, t, LANES), jnp.float32),
            jax.ShapeDtypeStruct((t, gate_cols), jnp.bfloat16),
        ],
        compiler_params=_cparams("arbitrary"),
        name="in_proj",
    )(x2, norm_w, w_in, b_gate)


def _attn_kernel(sink_ref, q_ref, kp_ref, kc_ref, vp_ref, vc_ref, rep_ref, o_ref):
    i = pl.program_id(1)
    blk = ATTN_BLOCK
    hw = Q_PER_KV * HEAD_DIM
    k2 = jnp.concatenate([kp_ref[...], kc_ref[...]], axis=0)
    v2 = jnp.concatenate([vp_ref[...], vc_ref[...]], axis=0)
    rows = Q_PER_KV * blk
    row = lax.broadcasted_iota(jnp.int32, (rows, 2 * blk), 0)
    col = lax.broadcasted_iota(jnp.int32, (rows, 2 * blk), 1)
    r = row & (blk - 1)
    valid = (col > r) & (col <= r + blk) & ((col >= blk) | (i > 0))
    head_of_row = row[:, :1] >> 7
    lane_head_q = lax.broadcasted_iota(jnp.int32, (blk, hw), 1) >> 6
    lane_head_v = lax.broadcasted_iota(jnp.int32, (2 * blk, hw), 1) >> 6
    neg = jnp.finfo(jnp.float32).min
    for kh in range(N_KV_HEADS):
        rep = rep_ref[kh]
        k4 = jnp.dot(k2, rep, preferred_element_type=jnp.float32).astype(jnp.bfloat16)
        v4 = jnp.dot(v2, rep, preferred_element_type=jnp.float32).astype(jnp.bfloat16)
        qh = q_ref[:, kh * hw:(kh + 1) * hw]
        qm = jnp.concatenate(
            [jnp.where(lane_head_q == g, qh, jnp.zeros_like(qh)) for g in range(Q_PER_KV)], axis=0)
        s = lax.dot_general(qm, k4, (((1,), (1,)), ((), ())), preferred_element_type=jnp.float32)
        s = jnp.where(valid, s, neg)
        sink = jnp.zeros((rows, 1), jnp.float32)
        for g in range(Q_PER_KV):
            sink = jnp.where(head_of_row == g, sink_ref[kh * Q_PER_KV + g], sink)
        m = jnp.maximum(jnp.max(s, axis=-1, keepdims=True), sink)
        p = jnp.exp(s - m)
        denom = jnp.sum(p, axis=-1, keepdims=True) + jnp.exp(sink - m)
        p = (p / denom).astype(jnp.bfloat16)
        p_cat = jnp.concatenate([p[g * blk:(g + 1) * blk, :] for g in range(Q_PER_KV)], axis=1)
        vm = jnp.concatenate(
            [jnp.where(lane_head_v == g, v4, jnp.zeros_like(v4)) for g in range(Q_PER_KV)], axis=0)
        o = jnp.dot(p_cat, vm, preferred_element_type=jnp.float32)
        o_ref[:, kh * hw:(kh + 1) * hw] = o.astype(o_ref.dtype)


def _attention(q, k, v, sinks, b, l):
    d_attn = q.shape[-1]
    kv_cols = k.shape[-1]
    blk = ATTN_BLOCK
    hw = Q_PER_KV * HEAD_DIM
    lane = jnp.arange(hw)[None, :]
    src = jnp.arange(kv_cols)[:, None]
    rep = jnp.stack([(src == kh * HEAD_DIM + (lane % HEAD_DIM)) for kh in range(N_KV_HEADS)]
                    ).astype(jnp.bfloat16)
    q3 = q.reshape(b, l, d_attn)
    k3 = k.reshape(b, l, kv_cols)
    v3 = v.reshape(b, l, kv_cols)
    cur = lambda bi, i: (bi, i, 0)
    prev = lambda bi, i: (bi, jnp.maximum(i - 1, 0), 0)
    out = pl.pallas_call(
        _attn_kernel,
        grid=(b, l // blk),
        in_specs=[
            pl.BlockSpec(memory_space=pltpu.SMEM),
            pl.BlockSpec((None, blk, d_attn), cur),
            pl.BlockSpec((None, blk, kv_cols), prev),
            pl.BlockSpec((None, blk, kv_cols), cur),
            pl.BlockSpec((None, blk, kv_cols), prev),
            pl.BlockSpec((None, blk, kv_cols), cur),
            pl.BlockSpec(rep.shape, lambda bi, i: (0, 0, 0)),
        ],
        out_specs=pl.BlockSpec((None, blk, d_attn), cur),
        out_shape=jax.ShapeDtypeStruct((b, l, d_attn), jnp.bfloat16),
        compiler_params=_cparams("arbitrary", "arbitrary"),
        name="swa",
    )(sinks, q3, k3, k3, v3, v3, rep)
    return out.reshape(b * l, d_attn)


def _ssm_kernel(u_ref, bmat_ref, cmat_ref, lam_ref, pw_ref, lamseg_ref, d_ref, y_ref,
                up_ref, bu_ref, yp_ref, carry_ref):
    i = pl.program_id(1)
    n_slab = u_ref.shape[0]
    tm = u_ref.shape[1]
    seg = tm // SUBLANES
    ns = SSM_CHUNK_STATES

    @pl.when(i == 0)
    def _():
        carry_ref[...] = jnp.zeros_like(carry_ref)

    sub = lax.broadcasted_iota(jnp.int32, (SUBLANES, ns), 0)
    for s in range(n_slab):
        u_slab = u_ref.at[s]
        for r in range(seg):
            up_ref[r * SUBLANES:(r + 1) * SUBLANES, :] = u_slab[pl.ds(r, SUBLANES, stride=seg), :]
        up = up_ref[...]
        bu_ref[...] = jnp.dot(up.astype(jnp.bfloat16), bmat_ref[s], preferred_element_type=jnp.float32)
        lr = jnp.broadcast_to(lam_ref[s, 0:1, :], (SUBLANES, ns))
        li = jnp.broadcast_to(lam_ref[s, 1:2, :], (SUBLANES, ns))

        def step(r, st):
            sr, si = st
            rows = pl.ds(pl.multiple_of(r * SUBLANES, SUBLANES), SUBLANES)
            nr = lr * sr - li * si + bu_ref[rows, 0:ns]
            ni = lr * si + li * sr + bu_ref[rows, ns:2 * ns]
            bu_ref[rows, 0:ns] = nr
            bu_ref[rows, ns:2 * ns] = ni
            return nr, ni

        zero = jnp.zeros((SUBLANES, ns), jnp.float32)
        er, ei = lax.fori_loop(0, seg, step, (zero, zero), unroll=4)

        ar = lamseg_ref[s, 0:1, :]
        ai = lamseg_ref[s, 1:2, :]
        cr = carry_ref[s, 0:1, :]
        ci = carry_ref[s, 1:2, :]
        car = jnp.zeros((SUBLANES, ns), jnp.float32)
        cai = jnp.zeros((SUBLANES, ns), jnp.float32)
        for j in range(SUBLANES):
            car = jnp.where(sub == j, jnp.broadcast_to(cr, (SUBLANES, ns)), car)
            cai = jnp.where(sub == j, jnp.broadcast_to(ci, (SUBLANES, ns)), cai)
            ejr = jnp.sum(jnp.where(sub == j, er, 0.0), axis=0, keepdims=True)
            eji = jnp.sum(jnp.where(sub == j, ei, 0.0), axis=0, keepdims=True)
            cr, ci = ar * cr - ai * ci + ejr, ar * ci + ai * cr + eji
        carry_ref[s, 0:1, :] = cr
        carry_ref[s, 1:2, :] = ci

        ctr = jnp.broadcast_to(car[None], (seg, SUBLANES, ns)).reshape(tm, ns)
        cti = jnp.broadcast_to(cai[None], (seg, SUBLANES, ns)).reshape(tm, ns)
        pr = pw_ref[s, 0]
        pi = pw_ref[s, 1]
        st_r = bu_ref[:, 0:ns] + pr * ctr - pi * cti
        st_i = bu_ref[:, ns:2 * ns] + pr * cti + pi * ctr
        st = jnp.concatenate([st_r, st_i], axis=1).astype(jnp.bfloat16)
        yp_ref[...] = jnp.dot(st, cmat_ref[s], preferred_element_type=jnp.float32) + d_ref[s] * up
        y_slab = y_ref.at[s]
        for r in range(seg):
            y_slab[pl.ds(r, SUBLANES, stride=seg), :] = yp_ref[r * SUBLANES:(r + 1) * SUBLANES, :]


def _ssm_tables(a_re, a_im, b_re, b_im, c_re, c_im, d_skip, log_dt, seg):
    f32 = jnp.float32
    g, p = a_re.shape
    c = b_re.shape[-1]
    ng = SSM_CHUNK_GROUPS
    n_slab = g // ng
    lam = lax.complex(a_re.astype(f32), a_im.astype(f32))
    dt = jnp.exp(log_dt.astype(f32))[:, None]
    lam_bar = jnp.exp(lam * dt)
    b_bar = ((lam_bar - 1.0) / lam)[:, :, None] * lax.complex(b_re.astype(f32), b_im.astype(f32))
    c_mat = lax.complex(c_re.astype(f32), c_im.astype(f32))
    eye = jnp.eye(ng, dtype=f32)

    def bdiag_b(m):
        m = m.reshape(n_slab, ng, p, c)
        return jnp.einsum('ab,kbpc->kacbp', eye, m).reshape(n_slab, ng * c, ng * p)

    def bdiag_c(m):
        m = m.reshape(n_slab, ng, c, p)
        return jnp.einsum('ab,kbcp->kbpac', eye, m).reshape(n_slab, ng * p, ng * c)

    bmat = jnp.concatenate([bdiag_b(jnp.real(b_bar)), bdiag_b(jnp.imag(b_bar))], axis=2).astype(jnp.bfloat16)
    cmat = jnp.concatenate([bdiag_c(jnp.real(c_mat)), -bdiag_c(jnp.imag(c_mat))], axis=1).astype(jnp.bfloat16)

    def slab_rows(z):
        z = z.reshape(n_slab, 1, ng * p)
        return jnp.concatenate([jnp.real(z), jnp.imag(z)], axis=1)

    lam_t = slab_rows(lam_bar)
    steps = jnp.arange(1, seg + 1, dtype=f32)
    pw = jnp.exp((lam * dt)[None] * steps[:, None, None])
    pw = pw.reshape(seg, n_slab, ng * p).transpose(1, 0, 2)
    pw = jnp.repeat(pw, SUBLANES, axis=1)
    pw_t = jnp.stack([jnp.real(pw), jnp.imag(pw)], axis=1)
    lamseg_t = slab_rows(jnp.exp(lam * dt * float(seg)))
    d_t = d_skip.astype(f32).reshape(n_slab, 1, ng * c)
    return bmat, cmat, lam_t, pw_t, lamseg_t, d_t


def _ssm(u4, tables, b, l):
    bmat, cmat, lam_t, pw_t, lamseg_t, d_t = tables
    n_slab, t, _ = u4.shape
    tm = TM_SSM
    nt = l // tm
    ns = SSM_CHUNK_STATES
    const = lambda nd: (lambda bi, i: (0,) * nd)
    return pl.pallas_call(
        _ssm_kernel,
        grid=(b, nt),
        in_specs=[
            pl.BlockSpec((n_slab, tm, LANES), lambda bi, i: (0, bi * nt + i, 0)),
            pl.BlockSpec(bmat.shape, const(3)),
            pl.BlockSpec(cmat.shape, const(3)),
            pl.BlockSpec(lam_t.shape, const(3)),
            pl.BlockSpec(pw_t.shape, const(4)),
            pl.BlockSpec(lamseg_t.shape, const(3)),
            pl.BlockSpec(d_t.shape, const(3)),
        ],
        out_specs=pl.BlockSpec((n_slab, tm, LANES), lambda bi, i: (0, bi * nt + i, 0)),
        out_shape=jax.ShapeDtypeStruct((n_slab, t, LANES), jnp.float32),
        scratch_shapes=[
            pltpu.VMEM((tm, LANES), jnp.float32),
            pltpu.VMEM((tm, 2 * ns), jnp.float32),
            pltpu.VMEM((tm, LANES), jnp.float32),
            pltpu.VMEM((n_slab, 2, ns), jnp.float32),
        ],
        compiler_params=_cparams("arbitrary", "arbitrary"),
        name="s5_scan",
    )(u4, bmat, cmat, lam_t, pw_t, lamseg_t, d_t)


def _merge_kernel(x_ref, attn_ref, y_ref, gate_ref, wglu_ref, bglu_ref, wa_ref, ws_ref, wo_ref,
                  nm_ref, wr_ref, br_ref, x1_ref, h2_ref, ids_ref, rw_ref):
    d = x_ref.shape[1]
    tm = x_ref.shape[0]
    y = jnp.concatenate([y_ref[s] for s in range(y_ref.shape[0])], axis=1)
    z = jax.nn.gelu(y)
    zg = jnp.dot(z.astype(jnp.bfloat16), wglu_ref[...], preferred_element_type=jnp.float32) + bglu_ref[...]
    z = z * jax.nn.sigmoid(zg)
    a = jnp.dot(attn_ref[...], wa_ref[...], preferred_element_type=jnp.float32)
    sb = jnp.dot(z.astype(jnp.bfloat16), ws_ref[...], preferred_element_type=jnp.float32)
    merged = gate_ref[:, 0:d].astype(jnp.float32) * a + gate_ref[:, d:2 * d].astype(jnp.float32) * sb
    x1 = x_ref[...] + jnp.dot(merged.astype(jnp.bfloat16), wo_ref[...], preferred_element_type=jnp.float32)
    x1_ref[...] = x1
    inv = lax.rsqrt(jnp.mean(x1 * x1, axis=-1, keepdims=True) + EPS)
    h2 = x1 * inv * nm_ref[...]
    h_hi = h2.astype(jnp.bfloat16)
    h_pk = _pack_bf16_pairs(h_hi.astype(jnp.float32))
    for c in range(h2_ref.shape[1]):
        h2_ref[:, c, :] = h_pk[:, c * LANES:(c + 1) * LANES]

    h_lo = (h2 - h_hi.astype(jnp.float32)).astype(jnp.bfloat16)
    hh = jnp.dot(h_hi, wr_ref[...], preferred_element_type=jnp.float32)
    lh = jnp.dot(h_lo, wr_ref[:, 0:LANES], preferred_element_type=jnp.float32)
    logits = hh[:, 0:LANES] + hh[:, LANES:2 * LANES] + lh + br_ref[...]
    lane = lax.broadcasted_iota(jnp.int32, (tm, LANES), 1)
    ninf = -jnp.inf
    gl = jnp.where((lane >= N_EXPERTS) & (lane < N_EXPERTS + N_EXPERT_GROUPS), logits, ninf)
    gmax = jnp.max(gl, axis=-1, keepdims=True)
    gidx = jnp.min(jnp.where(gl == gmax, lane - N_EXPERTS, LANES), axis=-1, keepdims=True)
    group_p = 1.0 / jnp.sum(jnp.exp(gl - gmax), axis=-1, keepdims=True)
    el = jnp.where((lane < N_EXPERTS) & ((lane >> 3) == gidx), logits, ninf)
    m1 = jnp.max(el, axis=-1, keepdims=True)
    i1 = jnp.min(jnp.where(el == m1, lane, LANES), axis=-1, keepdims=True)
    el2 = jnp.where(lane == i1, ninf, el)
    m2 = jnp.max(el2, axis=-1, keepdims=True)
    i2 = jnp.min(jnp.where(el2 == m2, lane, LANES), axis=-1, keepdims=True)
    e2 = jnp.exp(m2 - m1)
    w1 = group_p / (1.0 + e2)
    w2 = group_p * e2 / (1.0 + e2)
    ids_ref[...] = jnp.where(lane == 0, i1, jnp.where(lane == 1, i2, 0))
    rw_ref[...] = jnp.where(lane == 0, w1, jnp.where(lane == 1, w2, 0.0))


def _merge(x2, attn, y4, gates, w_glu, b_glu, w_a, w_s, w_o, norm_moe, w_router, b_router):
    t, d = x2.shape
    tm = TM_MERGE
    n_slab = y4.shape[0]
    full = lambda a: pl.BlockSpec(a.shape, lambda i: (0,) * a.ndim)
    row = lambda c: pl.BlockSpec((tm, c), lambda i: (i, 0))
    return pl.pallas_call(
        _merge_kernel,
        grid=(t // tm,),
        in_specs=[
            row(d), row(attn.shape[1]),
            pl.BlockSpec((n_slab, tm, LANES), lambda i: (0, i, 0)),
            row(gates.shape[1]),
            full(w_glu), full(b_glu), full(w_a), full(w_s), full(w_o), full(norm_moe),
            full(w_router), full(b_router),
        ],
        out_specs=[row(d), pl.BlockSpec((tm, d // 2 // LANES, LANES), lambda i: (i, 0, 0)), row(LANES), row(LANES)],
        out_shape=[
            jax.ShapeDtypeStruct((t, d), jnp.float32),
            jax.ShapeDtypeStruct((t, d // 2 // LANES, LANES), jnp.uint32),
            jax.ShapeDtypeStruct((t, LANES), jnp.int32),
            jax.ShapeDtypeStruct((t, LANES), jnp.float32),
        ],
        compiler_params=_cparams("arbitrary"),
        name="merge_router",
    )(x2, attn, y4, gates, w_glu, b_glu, w_a, w_s, w_o, norm_moe, w_router, b_router)


def _rank_kernel(ids_ref, rank_ref, cnt_ref, carry_ref):
    i = pl.program_id(0)
    tm = ids_ref.shape[0]

    @pl.when(i == 0)
    def _():
        carry_ref[...] = jnp.zeros_like(carry_ref)

    lane = lax.broadcasted_iota(jnp.int32, (tm, LANES), 1)
    ids = ids_ref[...]
    e0 = ids[:, 0:1]
    e1 = ids[:, 1:2]
    oh0 = lane == e0
    oh1 = lane == e1
    oh = oh0.astype(jnp.float32) + oh1.astype(jnp.float32)
    r_i = lax.broadcasted_iota(jnp.int32, (tm, tm), 0)
    c_i = lax.broadcasted_iota(jnp.int32, (tm, tm), 1)
    tri = (c_i < r_i).astype(jnp.bfloat16)
    cum = jnp.dot(tri, oh.astype(jnp.bfloat16), preferred_element_type=jnp.float32) + carry_ref[...]
    r0 = jnp.sum(jnp.where(oh0, cum, 0.0), axis=-1, keepdims=True)
    r1 = jnp.sum(jnp.where(oh1, cum, 0.0), axis=-1, keepdims=True)
    rank_ref[...] = jnp.where(lane == 0, r0, jnp.where(lane == 1, r1, 0.0)).astype(jnp.int32)
    carry_ref[...] = carry_ref[...] + jnp.sum(oh, axis=0, keepdims=True)
    cnt_ref[...] = carry_ref[...].astype(jnp.int32)


def _ranks(ids):
    t = ids.shape[0]
    tm = TM_RANK
    return pl.pallas_call(
        _rank_kernel,
        grid=(t // tm,),
        in_specs=[pl.BlockSpec((tm, LANES), lambda i: (i, 0))],
        out_specs=[pl.BlockSpec((tm, LANES), lambda i: (i, 0)),
                   pl.BlockSpec((1, LANES), lambda i: (0, 0))],
        out_shape=[jax.ShapeDtypeStruct((t, LANES), jnp.int32),
                   jax.ShapeDtypeStruct((1, LANES), jnp.int32)],
        scratch_shapes=[pltpu.VMEM((1, LANES), jnp.float32)],
        compiler_params=_cparams("arbitrary"),
        name="route_rank",
    )(ids)


def _inverse_kernel(pos_ref, src_ref, dst_ref):
    n_slots = pos_ref.shape[0]
    t = n_slots // TOP_K

    def body(tok, c):
        for k in range(TOP_K):
            p = pos_ref[k * t + tok]
            src_ref[p] = tok
            dst_ref[p] = k * t + tok
        return c

    lax.fori_loop(0, t, body, 0, unroll=8)


def _inverse(pos1d):
    n_slots = pos1d.shape[0]
    smem = pl.BlockSpec(memory_space=pltpu.SMEM)
    return pl.pallas_call(
        _inverse_kernel,
        in_specs=[smem],
        out_specs=[smem, smem],
        out_shape=[jax.ShapeDtypeStruct((n_slots,), jnp.int32)] * 2,
        name="route_inverse",
    )(pos1d)


def _expert_kernel(tile_ref, exp_ref, lo_ref, hi_ref, meta_ref, src_ref, dst_ref,
                   h_ref, wg_ref, wu_ref, wd_ref, y_ref, xbuf_ref, ybuf_ref, gsem, ssem):
    s = pl.program_id(0)
    tm = xbuf_ref.shape[1]
    n_tiles = src_ref.shape[0] // tm
    n_items = meta_ref[0]
    j = tile_ref[s]

    def gather(tile, slot, start):
        base = tile * tm
        if start:
            for r in range(tm):
                pltpu.make_async_copy(h_ref.at[src_ref[base + r]], xbuf_ref.at[slot, r], gsem.at[slot]).start()
        else:
            pltpu.make_async_copy(h_ref.at[pl.ds(0, tm)], xbuf_ref.at[slot], gsem.at[slot]).wait()

    def scatter(tile, slot, start):
        base = tile * tm
        if start:
            for r in range(tm):
                pltpu.make_async_copy(ybuf_ref.at[slot, r], y_ref.at[dst_ref[base + r]], ssem.at[slot]).start()
        else:
            pltpu.make_async_copy(ybuf_ref.at[slot], y_ref.at[pl.ds(0, tm)], ssem.at[slot]).wait()

    def on_parity(cond, tile, fn, start):
        for parity in range(2):
            @pl.when(cond & (tile % 2 == parity))
            def _():
                fn(tile, parity, start)

    live = s <= n_items
    first = live & (lo_ref[s] == 0)

    @pl.when(first & (j == 0))
    def _():
        gather(0, 0, True)

    on_parity(first & (j < n_tiles), j, gather, False)
    on_parity(first & (j + 1 < n_tiles), j + 1, gather, True)
    on_parity(first & (j >= 2), j, scatter, False)
    on_parity(first & (j >= 1), j - 1, scatter, True)

    @pl.when(s < n_items)
    def _():
        slot = j % 2
        xp = jnp.concatenate([xbuf_ref[slot, :, c, :] for c in range(xbuf_ref.shape[2])], axis=1)
        x = _unpack_bf16_pairs(xp).astype(jnp.bfloat16)
        hg = jnp.dot(x, wg_ref[...].astype(jnp.bfloat16), preferred_element_type=jnp.float32)
        hu = jnp.dot(x, wu_ref[...].astype(jnp.bfloat16), preferred_element_type=jnp.float32)
        a = (jax.nn.silu(hg) * hu).astype(jnp.bfloat16)
        y = jnp.dot(a, wd_ref[...].astype(jnp.bfloat16), preferred_element_type=jnp.float32)
        yp = _pack_bf16_pairs(y.astype(jnp.bfloat16).astype(jnp.float32))
        row = lax.broadcasted_iota(jnp.int32, (tm, LANES), 0)
        mine = (row >= lo_ref[s]) & (row < hi_ref[s])

        @pl.when(lo_ref[s] == 0)
        def _():
            for c in range(ybuf_ref.shape[2]):
                ybuf_ref[slot, :, c, :] = yp[:, c * LANES:(c + 1) * LANES]

        @pl.when(lo_ref[s] != 0)
        def _():
            for c in range(ybuf_ref.shape[2]):
                ybuf_ref[slot, :, c, :] = jnp.where(mine, yp[:, c * LANES:(c + 1) * LANES],
                                                    ybuf_ref[slot, :, c, :])

    @pl.when(s == n_items)
    def _():
        scatter(n_tiles - 1, (n_tiles - 1) % 2, False)


def _experts(items, src_tok, dst_row, h_fat, w_gate, w_up, w_down):
    it_tile, it_exp, it_lo, it_hi, meta = items
    t, n_sub, _ = h_fat.shape
    n_slots = src_tok.shape[0]
    ne, d, dff = w_gate.shape
    tm = TM_EXPERT
    wmap = lambda s, tile, exp, lo, hi, meta, src, dst: (exp[s], 0, 0)
    grid_spec = pltpu.PrefetchScalarGridSpec(
        num_scalar_prefetch=7,
        grid=(it_tile.shape[0],),
        in_specs=[
            pl.BlockSpec(memory_space=pl.ANY),
            pl.BlockSpec((None, d, dff), wmap),
            pl.BlockSpec((None, d, dff), wmap),
            pl.BlockSpec((None, dff, d), wmap),
        ],
        out_specs=pl.BlockSpec(memory_space=pl.ANY),
        scratch_shapes=[
            pltpu.VMEM((2, tm, n_sub, LANES), jnp.uint32),
            pltpu.VMEM((2, tm, n_sub, LANES), jnp.uint32),
            pltpu.SemaphoreType.DMA((2,)),
            pltpu.SemaphoreType.DMA((2,)),
        ],
    )
    return pl.pallas_call(
        _expert_kernel,
        grid_spec=grid_spec,
        out_shape=jax.ShapeDtypeStruct((n_slots, n_sub, LANES), jnp.uint32),
        compiler_params=_cparams("arbitrary"),
        name="moe_experts",
    )(it_tile, it_exp, it_lo, it_hi, meta, src_tok, dst_row, h_fat, w_gate, w_up, w_down)


def _combine_kernel(x1_ref, rw_ref, nf_ref, y0_ref, y1_ref, o_ref):
    rw = rw_ref[...]
    n_sub = y0_ref.shape[1]
    y0 = _unpack_bf16_pairs(jnp.concatenate([y0_ref[:, c, :] for c in range(n_sub)], axis=1))
    y1 = _unpack_bf16_pairs(jnp.concatenate([y1_ref[:, c, :] for c in range(n_sub)], axis=1))
    x = x1_ref[...] + rw[:, 0:1] * y0 + rw[:, 1:2] * y1
    inv = lax.rsqrt(jnp.mean(x * x, axis=-1, keepdims=True) + EPS)
    o_ref[...] = x * inv * nf_ref[...]


def _combine(x1, rw, norm_final, y_fat):
    t, d = x1.shape
    n_sub = y_fat.shape[1]
    tm = TM_MOVE
    nt = t // tm
    return pl.pallas_call(
        _combine_kernel,
        grid=(nt,),
        in_specs=[
            pl.BlockSpec((tm, d), lambda i: (i, 0)),
            pl.BlockSpec((tm, LANES), lambda i: (i, 0)),
            pl.BlockSpec((1, d), lambda i: (0, 0)),
            pl.BlockSpec((tm, n_sub, LANES), lambda i: (i, 0, 0)),
            pl.BlockSpec((tm, n_sub, LANES), lambda i: (i + nt, 0, 0)),
        ],
        out_specs=pl.BlockSpec((tm, d), lambda i: (i, 0)),
        out_shape=jax.ShapeDtypeStruct((t, d), jnp.float32),
        compiler_params=_cparams("arbitrary"),
        name="moe_combine",
    )(x1, rw, norm_final, y_fat, y_fat)


def _expert_items(counts, n_slots):
    tm = TM_EXPERT
    n_tiles = n_slots // tm
    n_max = n_tiles + N_EXPERTS
    ends = jnp.cumsum(counts)
    starts = ends - counts
    tile_lo = jnp.arange(n_tiles, dtype=jnp.int32)[:, None] * tm
    overlap = (starts[None, :] < tile_lo + tm) & (ends[None, :] > tile_lo)
    flat = overlap.reshape(-1)
    n_items = jnp.sum(flat).astype(jnp.int32)
    order = jnp.cumsum(flat) - 1
    pair = jnp.arange(flat.shape[0], dtype=jnp.int32)
    sel = (order[None, :] == jnp.arange(n_max, dtype=jnp.int32)[:, None]) & flat[None, :]
    idx = jnp.sum(jnp.where(sel, pair[None, :], 0), axis=1)
    used = jnp.arange(n_max) < n_items
    tile = jnp.where(used, idx // N_EXPERTS, n_tiles).astype(jnp.int32)
    last_exp = jnp.max(jnp.where(used, idx % N_EXPERTS, 0))
    exp = jnp.where(used, idx % N_EXPERTS, last_exp).astype(jnp.int32)
    lo = jnp.where(used, jnp.maximum(starts[exp] - tile * tm, 0), 0).astype(jnp.int32)
    hi = jnp.where(used, jnp.minimum(ends[exp] - tile * tm, tm), 0).astype(jnp.int32)
    return tile, exp, lo, hi, n_items.reshape(1)


def _moe(x1, h_fat, ids, rw, w_gate, w_up, w_down, norm_final):
    t, d = x1.shape
    rank, cnt = _ranks(ids)
    counts = cnt[0, :N_EXPERTS]
    offs = (jnp.cumsum(counts) - counts).astype(jnp.int32)
    ids2 = ids[:, :TOP_K].T
    onehot = ids2[:, :, None] == jnp.arange(N_EXPERTS, dtype=jnp.int32)
    pos1d = (jnp.sum(jnp.where(onehot, offs, 0), axis=-1) + rank[:, :TOP_K].T).reshape(-1)
    src_tok, dst_row = _inverse(pos1d)
    items = _expert_items(counts, t * TOP_K)
    y_fat = _experts(items, src_tok, dst_row, h_fat, w_gate, w_up, w_down)
    return _combine(x1, rw, norm_final, y_fat)


def kernel(x, norm_mix, w_in, b_gate, attn_sinks, ssm_a_re, ssm_a_im, ssm_b_re, ssm_b_im, ssm_c_re, ssm_c_im, ssm_d, ssm_log_dt, w_glu, b_glu, w_attn_branch, w_ssm_branch, w_out, norm_moe, w_router_group, b_router_group, w_router_expert, b_router_expert, w_expert_gate, w_expert_up, w_expert_down, norm_final):
    b, l, d = x.shape
    depth = w_in.shape[0]
    assert depth == 1, "the final norm is fused into the last layer's combine kernel"
    d_attn = N_HEADS * HEAD_DIM
    kv_cols = N_KV_HEADS * HEAD_DIM
    d_ssm = ssm_d.shape[-1]
    bf16 = jnp.bfloat16
    x2 = x.reshape(b * l, d)
    i = 0
    q, k, v, u4, gates = _in_proj(x2, norm_mix[i][None], w_in[i].astype(bf16), b_gate[i][None],
                                  d_attn, kv_cols, d_ssm)
    attn = _attention(q, k, v, attn_sinks[i], b, l)
    tables = _ssm_tables(ssm_a_re[i], ssm_a_im[i], ssm_b_re[i], ssm_b_im[i], ssm_c_re[i], ssm_c_im[i],
                         ssm_d[i], ssm_log_dt[i], SSM_SEG)
    y4 = _ssm(u4, tables, b, l)
    pad = LANES - N_EXPERTS - N_EXPERT_GROUPS
    w_router = jnp.concatenate([w_router_expert[i], w_router_group[i], jnp.zeros((d, pad), jnp.float32)], axis=1)
    w_r_hi = w_router.astype(bf16)
    w_router = jnp.concatenate([w_r_hi, (w_router - w_r_hi.astype(jnp.float32)).astype(bf16)], axis=1)
    b_router = jnp.concatenate([b_router_expert[i], b_router_group[i], jnp.zeros((pad,), jnp.float32)])[None]
    x1, h_fat, ids, rw = _merge(x2, attn, y4, gates, w_glu[i].astype(bf16), b_glu[i][None],
                                w_attn_branch[i].astype(bf16), w_ssm_branch[i].astype(bf16),
                                w_out[i].astype(bf16), norm_moe[i][None], w_router, b_router)
    out = _moe(x1, h_fat, ids, rw, w_expert_gate[i], w_expert_up[i], w_expert_down[i], norm_final[None])
    return out.reshape(b, l, d)
```

```python
import functools
import math

import jax
import jax.numpy as jnp
from jax import lax
from jax.experimental import pallas as pl
from jax.experimental.pallas import tpu as pltpu

EPS = 1e-6
HEAD_DIM = 64
N_HEADS = 8
N_KV_HEADS = 2
Q_PER_KV = N_HEADS // N_KV_HEADS
ATTN_BLOCK = 128
SSM_GROUP = 16
SSM_STATE = 64
N_EXPERT_GROUPS = 4
EXPERTS_PER_GROUP = 8
N_EXPERTS = N_EXPERT_GROUPS * EXPERTS_PER_GROUP
TOP_K = 2

LANES = 128
SUBLANES = 8
SSM_CHUNK_GROUPS = LANES // SSM_GROUP
SSM_CHUNK_STATES = SSM_CHUNK_GROUPS * SSM_STATE

TM_PROJ = 512
TM_SSM = 512
SSM_SEG = TM_SSM // SUBLANES
TM_MERGE = 512
TM_RANK = 512
TM_MOVE = 256
TM_EXPERT = 256
ROW_SUB = 4
VMEM_LIMIT = 56 * 1024 * 1024


def _cparams(*sem):
    return pltpu.CompilerParams(dimension_semantics=sem, vmem_limit_bytes=VMEM_LIMIT)


def _pack_bf16_pairs(x):
    half = x.shape[1] // 2
    bits = lax.bitcast_convert_type(x, jnp.uint32)
    return (bits[:, :half] & jnp.uint32(0xFFFF0000)) | (bits[:, half:] >> 16)


def _unpack_bf16_pairs(p):
    hi = lax.bitcast_convert_type(p & jnp.uint32(0xFFFF0000), jnp.float32)
    lo = lax.bitcast_convert_type(p << 16, jnp.float32)
    return jnp.concatenate([hi, lo], axis=1)


def _proj_kernel(x_ref, g_ref, w_ref, bg_ref, q_ref, k_ref, v_ref, u_ref, gate_ref, *, cols):
    q_c, kv_c, d_ssm = cols
    xf = x_ref[...]
    inv = lax.rsqrt(jnp.mean(xf * xf, axis=-1, keepdims=True) + EPS)
    h = (xf * inv * g_ref[...]).astype(jnp.bfloat16)
    o = 0
    q_ref[...] = (jnp.dot(h, w_ref[:, o:o + q_c], preferred_element_type=jnp.float32)
                  * (1.0 / math.sqrt(HEAD_DIM))).astype(q_ref.dtype)
    o += q_c
    k_ref[...] = jnp.dot(h, w_ref[:, o:o + kv_c], preferred_element_type=jnp.float32).astype(k_ref.dtype)
    o += kv_c
    v_ref[...] = jnp.dot(h, w_ref[:, o:o + kv_c], preferred_element_type=jnp.float32).astype(v_ref.dtype)
    o += kv_c
    for s in range(d_ssm // LANES):
        u_ref[s] = jnp.dot(h, w_ref[:, o:o + LANES], preferred_element_type=jnp.float32)
        o += LANES
    gl = jnp.dot(h, w_ref[:, o:], preferred_element_type=jnp.float32) + bg_ref[...]
    gate_ref[...] = jax.nn.sigmoid(gl).astype(gate_ref.dtype)


def _in_proj(x2, norm_w, w_in, b_gate, d_attn, kv_cols, d_ssm):
    t, d = x2.shape
    gate_cols = b_gate.shape[-1]
    n_slab = d_ssm // LANES
    tm = TM_PROJ
    kern = functools.partial(_proj_kernel, cols=(d_attn, kv_cols, d_ssm))
    return pl.pallas_call(
        kern,
        grid=(t // tm,),
        in_specs=[
            pl.BlockSpec((tm, d), lambda i: (i, 0)),
            pl.BlockSpec((1, d), lambda i: (0, 0)),
            pl.BlockSpec(w_in.shape, lambda i: (0, 0)),
            pl.BlockSpec((1, gate_cols), lambda i: (0, 0)),
        ],
        out_specs=[
            pl.BlockSpec((tm, d_attn), lambda i: (i, 0)),
            pl.BlockSpec((tm, kv_cols), lambda i: (i, 0)),
            pl.BlockSpec((tm, kv_cols), lambda i: (i, 0)),
            pl.BlockSpec((n_slab, tm, LANES), lambda i: (0, i, 0)),
            pl.BlockSpec((tm, gate_cols), lambda i: (i, 0)),
        ],
        out_shape=[
            jax.ShapeDtypeStruct((t, d_attn), jnp.bfloat16),
            jax.ShapeDtypeStruct((t, kv_cols), jnp.bfloat16),
            jax.ShapeDtypeStruct((t, kv_cols), jnp.bfloat16),
            jax.ShapeDtypeStruct((n_slab, t, LANES), jnp.float32),
            jax.ShapeDtypeStruct((t, gate_cols), jnp.bfloat16),
        ],
        compiler_params=_cparams("arbitrary"),
        name="in_proj",
    )(x2, norm_w, w_in, b_gate)


def _attn_kernel(sink_ref, q_ref, kp_ref, kc_ref, vp_ref, vc_ref, rep_ref, o_ref):
    i = pl.program_id(1)
    blk = ATTN_BLOCK
    hw = Q_PER_KV * HEAD_DIM
    k2 = jnp.concatenate([kp_ref[...], kc_ref[...]], axis=0)
    v2 = jnp.concatenate([vp_ref[...], vc_ref[...]], axis=0)
    rows = Q_PER_KV * blk
    row = lax.broadcasted_iota(jnp.int32, (rows, 2 * blk), 0)
    col = lax.broadcasted_iota(jnp.int32, (rows, 2 * blk), 1)
    r = row & (blk - 1)
    valid = (col > r) & (col <= r + blk) & ((col >= blk) | (i > 0))
    head_of_row = row[:, :1] >> 7
    lane_head_q = lax.broadcasted_iota(jnp.int32, (blk, hw), 1) >> 6
    lane_head_v = lax.broadcasted_iota(jnp.int32, (2 * blk, hw), 1) >> 6
    neg = jnp.finfo(jnp.float32).min
    for kh in range(N_KV_HEADS):
        rep = rep_ref[kh]
        k4 = jnp.dot(k2, rep, preferred_element_type=jnp.float32).astype(jnp.bfloat16)
        v4 = jnp.dot(v2, rep, preferred_element_type=jnp.float32).astype(jnp.bfloat16)
        qh = q_ref[:, kh * hw:(kh + 1) * hw]
        qm = jnp.concatenate(
            [jnp.where(lane_head_q == g, qh, jnp.zeros_like(qh)) for g in range(Q_PER_KV)], axis=0)
        s = lax.dot_general(qm, k4, (((1,), (1,)), ((), ())), preferred_element_type=jnp.float32)
        s = jnp.where(valid, s, neg)
        sink = jnp.zeros((rows, 1), jnp.float32)
        for g in range(Q_PER_KV):
            sink = jnp.where(head_of_row == g, sink_ref[kh * Q_PER_KV + g], sink)
        m = jnp.maximum(jnp.max(s, axis=-1, keepdims=True), sink)
        p = jnp.exp(s - m)
        denom = jnp.sum(p, axis=-1, keepdims=True) + jnp.exp(sink - m)
        p = (p / denom).astype(jnp.bfloat16)
        p_cat = jnp.concatenate([p[g * blk:(g + 1) * blk, :] for g in range(Q_PER_KV)], axis=1)
        vm = jnp.concatenate(
            [jnp.where(lane_head_v == g, v4, jnp.zeros_like(v4)) for g in range(Q_PER_KV)], axis=0)
        o = jnp.dot(p_cat, vm, preferred_element_type=jnp.float32)
        o_ref[:, kh * hw:(kh + 1) * hw] = o.astype(o_ref.dtype)


def _attention(q, k, v, sinks, b, l):
    d_attn = q.shape[-1]
    kv_cols = k.shape[-1]
    blk = ATTN_BLOCK
    hw = Q_PER_KV * HEAD_DIM
    lane = jnp.arange(hw)[None, :]
    src = jnp.arange(kv_cols)[:, None]
    rep = jnp.stack([(src == kh * HEAD_DIM + (lane % HEAD_DIM)) for kh in range(N_KV_HEADS)]
                    ).astype(jnp.bfloat16)
    q3 = q.reshape(b, l, d_attn)
    k3 = k.reshape(b, l, kv_cols)
    v3 = v.reshape(b, l, kv_cols)
    cur = lambda bi, i: (bi, i, 0)
    prev = lambda bi, i: (bi, jnp.maximum(i - 1, 0), 0)
    out = pl.pallas_call(
        _attn_kernel,
        grid=(b, l // blk),
        in_specs=[
            pl.BlockSpec(memory_space=pltpu.SMEM),
            pl.BlockSpec((None, blk, d_attn), cur),
            pl.BlockSpec((None, blk, kv_cols), prev),
            pl.BlockSpec((None, blk, kv_cols), cur),
            pl.BlockSpec((None, blk, kv_cols), prev),
            pl.BlockSpec((None, blk, kv_cols), cur),
            pl.BlockSpec(rep.shape, lambda bi, i: (0, 0, 0)),
        ],
        out_specs=pl.BlockSpec((None, blk, d_attn), cur),
        out_shape=jax.ShapeDtypeStruct((b, l, d_attn), jnp.bfloat16),
        compiler_params=_cparams("arbitrary", "arbitrary"),
        name="swa",
    )(sinks, q3, k3, k3, v3, v3, rep)
    return out.reshape(b * l, d_attn)


def _ssm_kernel(u_ref, bmat_ref, cmat_ref, lam_ref, pw_ref, lamseg_ref, d_ref, y_ref,
                up_ref, bu_ref, yp_ref, carry_ref):
    i = pl.program_id(1)
    n_slab = u_ref.shape[0]
    tm = u_ref.shape[1]
    seg = tm // SUBLANES
    ns = SSM_CHUNK_STATES

    @pl.when(i == 0)
    def _():
        carry_ref[...] = jnp.zeros_like(carry_ref)

    sub = lax.broadcasted_iota(jnp.int32, (SUBLANES, ns), 0)
    for s in range(n_slab):
        u_slab = u_ref.at[s]
        for r in range(seg):
            up_ref[r * SUBLANES:(r + 1) * SUBLANES, :] = u_slab[pl.ds(r, SUBLANES, stride=seg), :]
        up = up_ref[...]
        bu_ref[...] = jnp.dot(up.astype(jnp.bfloat16), bmat_ref[s], preferred_element_type=jnp.float32)
        lr = jnp.broadcast_to(lam_ref[s, 0:1, :], (SUBLANES, ns))
        li = jnp.broadcast_to(lam_ref[s, 1:2, :], (SUBLANES, ns))

        def step(r, st):
            sr, si = st
            rows = pl.ds(pl.multiple_of(r * SUBLANES, SUBLANES), SUBLANES)
            nr = lr * sr - li * si + bu_ref[rows, 0:ns]
            ni = lr * si + li * sr + bu_ref[rows, ns:2 * ns]
            bu_ref[rows, 0:ns] = nr
            bu_ref[rows, ns:2 * ns] = ni
            return nr, ni

        zero = jnp.zeros((SUBLANES, ns), jnp.float32)
        er, ei = lax.fori_loop(0, seg, step, (zero, zero), unroll=4)

        ar = lamseg_ref[s, 0:1, :]
        ai = lamseg_ref[s, 1:2, :]
        cr = carry_ref[s, 0:1, :]
        ci = carry_ref[s, 1:2, :]
        car = jnp.zeros((SUBLANES, ns), jnp.float32)
        cai = jnp.zeros((SUBLANES, ns), jnp.float32)
        for j in range(SUBLANES):
            car = jnp.where(sub == j, jnp.broadcast_to(cr, (SUBLANES, ns)), car)
            cai = jnp.where(sub == j, jnp.broadcast_to(ci, (SUBLANES, ns)), cai)
            ejr = jnp.sum(jnp.where(sub == j, er, 0.0), axis=0, keepdims=True)
            eji = jnp.sum(jnp.where(sub == j, ei, 0.0), axis=0, keepdims=True)
            cr, ci = ar * cr - ai * ci + ejr, ar * ci + ai * cr + eji
        carry_ref[s, 0:1, :] = cr
        carry_ref[s, 1:2, :] = ci

        ctr = jnp.broadcast_to(car[None], (seg, SUBLANES, ns)).reshape(tm, ns)
        cti = jnp.broadcast_to(cai[None], (seg, SUBLANES, ns)).reshape(tm, ns)
        pr = pw_ref[s, 0]
        pi = pw_ref[s, 1]
        st_r = bu_ref[:, 0:ns] + pr * ctr - pi * cti
        st_i = bu_ref[:, ns:2 * ns] + pr * cti + pi * ctr
        st = jnp.concatenate([st_r, st_i], axis=1).astype(jnp.bfloat16)
        yp_ref[...] = jnp.dot(st, cmat_ref[s], preferred_element_type=jnp.float32) + d_ref[s] * up
        y_slab = y_ref.at[s]
        for r in range(seg):
            y_slab[pl.ds(r, SUBLANES, stride=seg), :] = yp_ref[r * SUBLANES:(r + 1) * SUBLANES, :]


def _ssm_tables(a_re, a_im, b_re, b_im, c_re, c_im, d_skip, log_dt, seg):
    f32 = jnp.float32
    g, p = a_re.shape
    c = b_re.shape[-1]
    ng = SSM_CHUNK_GROUPS
    n_slab = g // ng
    lam = lax.complex(a_re.astype(f32), a_im.astype(f32))
    dt = jnp.exp(log_dt.astype(f32))[:, None]
    lam_bar = jnp.exp(lam * dt)
    b_bar = ((lam_bar - 1.0) / lam)[:, :, None] * lax.complex(b_re.astype(f32), b_im.astype(f32))
    c_mat = lax.complex(c_re.astype(f32), c_im.astype(f32))
    eye = jnp.eye(ng, dtype=f32)

    def bdiag_b(m):
        m = m.reshape(n_slab, ng, p, c)
        return jnp.einsum('ab,kbpc->kacbp', eye, m).reshape(n_slab, ng * c, ng * p)

    def bdiag_c(m):
        m = m.reshape(n_slab, ng, c, p)
        return jnp.einsum('ab,kbcp->kbpac', eye, m).reshape(n_slab, ng * p, ng * c)

    bmat = jnp.concatenate([bdiag_b(jnp.real(b_bar)), bdiag_b(jnp.imag(b_bar))], axis=2).astype(jnp.bfloat16)
    cmat = jnp.concatenate([bdiag_c(jnp.real(c_mat)), -bdiag_c(jnp.imag(c_mat))], axis=1).astype(jnp.bfloat16)

    def slab_rows(z):
        z = z.reshape(n_slab, 1, ng * p)
        return jnp.concatenate([jnp.real(z), jnp.imag(z)], axis=1)

    lam_t = slab_rows(lam_bar)
    steps = jnp.arange(1, seg + 1, dtype=f32)
    pw = jnp.exp((lam * dt)[None] * steps[:, None, None])
    pw = pw.reshape(seg, n_slab, ng * p).transpose(1, 0, 2)
    pw = jnp.repeat(pw, SUBLANES, axis=1)
    pw_t = jnp.stack([jnp.real(pw), jnp.imag(pw)], axis=1)
    lamseg_t = slab_rows(jnp.exp(lam * dt * float(seg)))
    d_t = d_skip.astype(f32).reshape(n_slab, 1, ng * c)
    return bmat, cmat, lam_t, pw_t, lamseg_t, d_t


def _ssm(u4, tables, b, l):
    bmat, cmat, lam_t, pw_t, lamseg_t, d_t = tables
    n_slab, t, _ = u4.shape
    tm = TM_SSM
    nt = l // tm
    ns = SSM_CHUNK_STATES
    const = lambda nd: (lambda bi, i: (0,) * nd)
    return pl.pallas_call(
        _ssm_kernel,
        grid=(b, nt),
        in_specs=[
            pl.BlockSpec((n_slab, tm, LANES), lambda bi, i: (0, bi * nt + i, 0)),
            pl.BlockSpec(bmat.shape, const(3)),
            pl.BlockSpec(cmat.shape, const(3)),
            pl.BlockSpec(lam_t.shape, const(3)),
            pl.BlockSpec(pw_t.shape, const(4)),
            pl.BlockSpec(lamseg_t.shape, const(3)),
            pl.BlockSpec(d_t.shape, const(3)),
        ],
        out_specs=pl.BlockSpec((n_slab, tm, LANES), lambda bi, i: (0, bi * nt + i, 0)),
        out_shape=jax.ShapeDtypeStruct((n_slab, t, LANES), jnp.float32),
        scratch_shapes=[
            pltpu.VMEM((tm, LANES), jnp.float32),
            pltpu.VMEM((tm, 2 * ns), jnp.float32),
            pltpu.VMEM((tm, LANES), jnp.float32),
            pltpu.VMEM((n_slab, 2, ns), jnp.float32),
        ],
        compiler_params=_cparams("arbitrary", "arbitrary"),
        name="s5_scan",
    )(u4, bmat, cmat, lam_t, pw_t, lamseg_t, d_t)


def _merge_kernel(x_ref, attn_ref, y_ref, gate_ref, wglu_ref, bglu_ref, wa_ref, ws_ref, wo_ref,
                  nm_ref, wr_ref, br_ref, x1_ref, h2_ref, ids_ref, rw_ref):
    d = x_ref.shape[1]
    tm = x_ref.shape[0]
    y = jnp.concatenate([y_ref[s] for s in range(y_ref.shape[0])], axis=1)
    z = jax.nn.gelu(y)
    zg = jnp.dot(z.astype(jnp.bfloat16), wglu_ref[...], preferred_element_type=jnp.float32) + bglu_ref[...]
    z = z * jax.nn.sigmoid(zg)
    a = jnp.dot(attn_ref[...], wa_ref[...], preferred_element_type=jnp.float32)
    sb = jnp.dot(z.astype(jnp.bfloat16), ws_ref[...], preferred_element_type=jnp.float32)
    merged = gate_ref[:, 0:d].astype(jnp.float32) * a + gate_ref[:, d:2 * d].astype(jnp.float32) * sb
    x1 = x_ref[...] + jnp.dot(merged.astype(jnp.bfloat16), wo_ref[...], preferred_element_type=jnp.float32)
    x1_ref[...] = x1
    inv = lax.rsqrt(jnp.mean(x1 * x1, axis=-1, keepdims=True) + EPS)
    h2 = x1 * inv * nm_ref[...]
    h_hi = h2.astype(jnp.bfloat16)
    h_pk = _pack_bf16_pairs(h_hi.astype(jnp.float32))
    for c in range(ROW_SUB):
        h2_ref[pl.ds(c, tm, stride=ROW_SUB), :] = h_pk[:, c * LANES:(c + 1) * LANES]

    h_lo = (h2 - h_hi.astype(jnp.float32)).astype(jnp.bfloat16)
    hh = jnp.dot(h_hi, wr_ref[...], preferred_element_type=jnp.float32)
    lh = jnp.dot(h_lo, wr_ref[:, 0:LANES], preferred_element_type=jnp.float32)
    logits = hh[:, 0:LANES] + hh[:, LANES:2 * LANES] + lh + br_ref[...]
    lane = lax.broadcasted_iota(jnp.int32, (tm, LANES), 1)
    ninf = -jnp.inf
    gl = jnp.where((lane >= N_EXPERTS) & (lane < N_EXPERTS + N_EXPERT_GROUPS), logits, ninf)
    gmax = jnp.max(gl, axis=-1, keepdims=True)
    gidx = jnp.min(jnp.where(gl == gmax, lane - N_EXPERTS, LANES), axis=-1, keepdims=True)
    group_p = 1.0 / jnp.sum(jnp.exp(gl - gmax), axis=-1, keepdims=True)
    el = jnp.where((lane < N_EXPERTS) & ((lane >> 3) == gidx), logits, ninf)
    m1 = jnp.max(el, axis=-1, keepdims=True)
    i1 = jnp.min(jnp.where(el == m1, lane, LANES), axis=-1, keepdims=True)
    el2 = jnp.where(lane == i1, ninf, el)
    m2 = jnp.max(el2, axis=-1, keepdims=True)
    i2 = jnp.min(jnp.where(el2 == m2, lane, LANES), axis=-1, keepdims=True)
    e2 = jnp.exp(m2 - m1)
    w1 = group_p / (1.0 + e2)
    w2 = group_p * e2 / (1.0 + e2)
    ids_ref[...] = jnp.where(lane == 0, i1, jnp.where(lane == 1, i2, 0))
    rw_ref[...] = jnp.where(lane == 0, w1, jnp.where(lane == 1, w2, 0.0))


def _merge(x2, attn, y4, gates, w_glu, b_glu, w_a, w_s, w_o, norm_moe, w_router, b_router):
    t, d = x2.shape
    tm = TM_MERGE
    n_slab = y4.shape[0]
    full = lambda a: pl.BlockSpec(a.shape, lambda i: (0,) * a.ndim)
    row = lambda c: pl.BlockSpec((tm, c), lambda i: (i, 0))
    return pl.pallas_call(
        _merge_kernel,
        grid=(t // tm,),
        in_specs=[
            row(d), row(attn.shape[1]),
            pl.BlockSpec((n_slab, tm, LANES), lambda i: (0, i, 0)),
            row(gates.shape[1]),
            full(w_glu), full(b_glu), full(w_a), full(w_s), full(w_o), full(norm_moe),
            full(w_router), full(b_router),
        ],
        out_specs=[row(d), pl.BlockSpec((tm * ROW_SUB, LANES), lambda i: (i, 0)), row(LANES), row(LANES)],
        out_shape=[
            jax.ShapeDtypeStruct((t, d), jnp.float32),
            jax.ShapeDtypeStruct((t * ROW_SUB, LANES), jnp.uint32),
            jax.ShapeDtypeStruct((t, LANES), jnp.int32),
            jax.ShapeDtypeStruct((t, LANES), jnp.float32),
        ],
        compiler_params=_cparams("arbitrary"),
        name="merge_router",
    )(x2, attn, y4, gates, w_glu, b_glu, w_a, w_s, w_o, norm_moe, w_router, b_router)


def _rank_kernel(ids_ref, rank_ref, cnt_ref, carry_ref):
    i = pl.program_id(0)
    tm = ids_ref.shape[0]

    @pl.when(i == 0)
    def _():
        carry_ref[...] = jnp.zeros_like(carry_ref)

    lane = lax.broadcasted_iota(jnp.int32, (tm, LANES), 1)
    ids = ids_ref[...]
    e0 = ids[:, 0:1]
    e1 = ids[:, 1:2]
    oh0 = lane == e0
    oh1 = lane == e1
    oh = oh0.astype(jnp.float32) + oh1.astype(jnp.float32)
    r_i = lax.broadcasted_iota(jnp.int32, (tm, tm), 0)
    c_i = lax.broadcasted_iota(jnp.int32, (tm, tm), 1)
    tri = (c_i < r_i).astype(jnp.bfloat16)
    cum = jnp.dot(tri, oh.astype(jnp.bfloat16), preferred_element_type=jnp.float32) + carry_ref[...]
    r0 = jnp.sum(jnp.where(oh0, cum, 0.0), axis=-1, keepdims=True)
    r1 = jnp.sum(jnp.where(oh1, cum, 0.0), axis=-1, keepdims=True)
    rank_ref[...] = jnp.where(lane == 0, r0, jnp.where(lane == 1, r1, 0.0)).astype(jnp.int32)
    carry_ref[...] = carry_ref[...] + jnp.sum(oh, axis=0, keepdims=True)
    cnt_ref[...] = carry_ref[...].astype(jnp.int32)


def _ranks(ids):
    t = ids.shape[0]
    tm = TM_RANK
    return pl.pallas_call(
        _rank_kernel,
        grid=(t // tm,),
        in_specs=[pl.BlockSpec((tm, LANES), lambda i: (i, 0))],
        out_specs=[pl.BlockSpec((tm, LANES), lambda i: (i, 0)),
                   pl.BlockSpec((1, LANES), lambda i: (0, 0))],
        out_shape=[jax.ShapeDtypeStruct((t, LANES), jnp.int32),
                   jax.ShapeDtypeStruct((1, LANES), jnp.int32)],
        scratch_shapes=[pltpu.VMEM((1, LANES), jnp.float32)],
        compiler_params=_cparams("arbitrary"),
        name="route_rank",
    )(ids)


def _inverse_kernel(pos_ref, src_ref, dst_ref):
    n_slots = pos_ref.shape[0]
    t = n_slots // TOP_K

    def body(tok, c):
        for k in range(TOP_K):
            p = pos_ref[k * t + tok]
            src_ref[p] = tok
            dst_ref[p] = k * t + tok
        return c

    lax.fori_loop(0, t, body, 0, unroll=8)


def _inverse(pos1d):
    n_slots = pos1d.shape[0]
    smem = pl.BlockSpec(memory_space=pltpu.SMEM)
    return pl.pallas_call(
        _inverse_kernel,
        in_specs=[smem],
        out_specs=[smem, smem],
        out_shape=[jax.ShapeDtypeStruct((n_slots,), jnp.int32)] * 2,
        name="route_inverse",
    )(pos1d)


def _expert_kernel(ta_ref, tn_ref, tnew_ref, sexp_ref, slo_ref, shi_ref, meta_ref, src_ref, dst_ref,
                   h_ref, wg_hbm, wu_hbm, wd_hbm, y_ref,
                   xbuf_ref, ybuf_ref, wgs_ref, wus_ref, wds_ref, wgb_ref, wub_ref, wdb_ref,
                   gsem, ssem, wsem):
    j = pl.program_id(0)
    n_tiles = pl.num_programs(0)
    tm = TM_EXPERT
    rs = ROW_SUB
    n_used = meta_ref[0]
    slot = j % 2
    a = ta_ref[j]

    def gather(tile, sl, start):
        if start:
            base = tile * tm
            for r in range(tm):
                pltpu.make_async_copy(h_ref.at[pl.ds(pl.multiple_of(src_ref[base + r] * rs, rs), rs)],
                                      xbuf_ref.at[pl.ds(pl.multiple_of((sl * tm + r) * rs, rs), rs)],
                                      gsem.at[sl]).start()
        else:
            pltpu.make_async_copy(h_ref.at[pl.ds(0, tm * rs)],
                                  xbuf_ref.at[pl.ds(pl.multiple_of(sl * tm * rs, tm * rs), tm * rs)],
                                  gsem.at[sl]).wait()

    def scatter(tile, sl, start):
        if start:
            base = tile * tm
            for r in range(tm):
                pltpu.make_async_copy(ybuf_ref.at[pl.ds(pl.multiple_of((sl * tm + r) * rs, rs), rs)],
                                      y_ref.at[pl.ds(pl.multiple_of(dst_ref[base + r] * rs, rs), rs)],
                                      ssem.at[sl]).start()
        else:
            pltpu.make_async_copy(ybuf_ref.at[pl.ds(pl.multiple_of(sl * tm * rs, tm * rs), tm * rs)],
                                  y_ref.at[pl.ds(0, tm * rs)], ssem.at[sl]).wait()

    def weights(m, start):
        sl = m % 2
        e = sexp_ref[m]
        for hbm, stage in ((wg_hbm, wgs_ref), (wu_hbm, wus_ref), (wd_hbm, wds_ref)):
            cp = pltpu.make_async_copy(hbm.at[e], stage.at[sl], wsem.at[sl])
            cp.start() if start else cp.wait()

    def new_expert(m):
        weights(m, False)
        sl = m % 2
        wgb_ref[...] = wgs_ref[sl].astype(jnp.bfloat16)
        wub_ref[...] = wus_ref[sl].astype(jnp.bfloat16)
        wdb_ref[...] = wds_ref[sl].astype(jnp.bfloat16)

        @pl.when(m + 1 < n_used)
        def _():
            weights(m + 1, True)

    def expert_out():
        xp = jnp.concatenate([xbuf_ref[pl.ds(slot * tm * rs + c, tm, stride=rs), :] for c in range(rs)], axis=1)
        x = _unpack_bf16_pairs(xp).astype(jnp.bfloat16)
        hg = jnp.dot(x, wgb_ref[...], preferred_element_type=jnp.float32)
        hu = jnp.dot(x, wub_ref[...], preferred_element_type=jnp.float32)
        act = (jax.nn.silu(hg) * hu).astype(jnp.bfloat16)
        y = jnp.dot(act, wdb_ref[...], preferred_element_type=jnp.float32)
        return _pack_bf16_pairs(y.astype(jnp.bfloat16).astype(jnp.float32))

    @pl.when(j == 0)
    def _():
        weights(0, True)
        gather(0, 0, True)

    @pl.when(tnew_ref[j] == 1)
    def _():
        new_expert(a)

    @pl.when(j >= 2)
    def _():
        scatter(j - 2, slot, False)

    gather(j, slot, False)

    def main(with_scatter):
        gather(jnp.minimum(j + 1, n_tiles - 1), 1 - slot, True)
        if with_scatter:
            scatter(j - 1, 1 - slot, True)
        yp = expert_out()
        for c in range(rs):
            ybuf_ref[pl.ds(slot * tm * rs + c, tm, stride=rs), :] = yp[:, c * LANES:(c + 1) * LANES]

    @pl.when(j == 0)
    def _():
        main(False)

    @pl.when(j > 0)
    def _():
        main(True)

    def extra(i, carry):
        m = a + i
        new_expert(m)
        yp = expert_out()
        row = lax.broadcasted_iota(jnp.int32, (tm, LANES), 0) + j * tm
        mine = (row >= slo_ref[m]) & (row < shi_ref[m])
        for c in range(rs):
            idx = pl.ds(slot * tm * rs + c, tm, stride=rs)
            ybuf_ref[idx, :] = jnp.where(mine, yp[:, c * LANES:(c + 1) * LANES], ybuf_ref[idx, :])
        return carry

    lax.fori_loop(1, tn_ref[j], extra, 0)

    @pl.when(j == n_tiles - 1)
    def _():
        scatter(j - 1, 1 - slot, False)
        scatter(j, slot, True)
        scatter(j, slot, False)
        gather(j, 1 - slot, False)


def _experts(meta, src_tok, dst_row, h_fat, w_gate, w_up, w_down):
    n_slots = src_tok.shape[0]
    ne, d, dff = w_gate.shape
    tm = TM_EXPERT
    rs = ROW_SUB
    any_spec = pl.BlockSpec(memory_space=pl.ANY)
    grid_spec = pltpu.PrefetchScalarGridSpec(
        num_scalar_prefetch=9,
        grid=(n_slots // tm,),
        in_specs=[any_spec, any_spec, any_spec, any_spec],
        out_specs=any_spec,
        scratch_shapes=[
            pltpu.VMEM((2 * tm * rs, LANES), jnp.uint32),
            pltpu.VMEM((2 * tm * rs, LANES), jnp.uint32),
            pltpu.VMEM((2, d, dff), jnp.float32),
            pltpu.VMEM((2, d, dff), jnp.float32),
            pltpu.VMEM((2, dff, d), jnp.float32),
            pltpu.VMEM((d, dff), jnp.bfloat16),
            pltpu.VMEM((d, dff), jnp.bfloat16),
            pltpu.VMEM((dff, d), jnp.bfloat16),
            pltpu.SemaphoreType.DMA((2,)),
            pltpu.SemaphoreType.DMA((2,)),
            pltpu.SemaphoreType.DMA((2,)),
        ],
    )
    return pl.pallas_call(
        _expert_kernel,
        grid_spec=grid_spec,
        out_shape=jax.ShapeDtypeStruct((n_slots * rs, LANES), jnp.uint32),
        compiler_params=_cparams("arbitrary"),
        name="moe_experts",
    )(*meta, src_tok, dst_row, h_fat, w_gate, w_up, w_down)


def _combine_kernel(x1_ref, rw_ref, nf_ref, y0_ref, y1_ref, o_ref):
    rw = rw_ref[...]
    tm = x1_ref.shape[0]
    y0 = _unpack_bf16_pairs(jnp.concatenate(
        [y0_ref[pl.ds(c, tm, stride=ROW_SUB), :] for c in range(ROW_SUB)], axis=1))
    y1 = _unpack_bf16_pairs(jnp.concatenate(
        [y1_ref[pl.ds(c, tm, stride=ROW_SUB), :] for c in range(ROW_SUB)], axis=1))
    x = x1_ref[...] + rw[:, 0:1] * y0 + rw[:, 1:2] * y1
    inv = lax.rsqrt(jnp.mean(x * x, axis=-1, keepdims=True) + EPS)
    o_ref[...] = x * inv * nf_ref[...]


def _combine(x1, rw, norm_final, y_fat):
    t, d = x1.shape
    tm = TM_MOVE
    nt = t // tm
    return pl.pallas_call(
        _combine_kernel,
        grid=(nt,),
        in_specs=[
            pl.BlockSpec((tm, d), lambda i: (i, 0)),
            pl.BlockSpec((tm, LANES), lambda i: (i, 0)),
            pl.BlockSpec((1, d), lambda i: (0, 0)),
            pl.BlockSpec((tm * ROW_SUB, LANES), lambda i: (i, 0)),
            pl.BlockSpec((tm * ROW_SUB, LANES), lambda i: (i + nt, 0)),
        ],
        out_specs=pl.BlockSpec((tm, d), lambda i: (i, 0)),
        out_shape=jax.ShapeDtypeStruct((t, d), jnp.float32),
        compiler_params=_cparams("arbitrary"),
        name="moe_combine",
    )(x1, rw, norm_final, y_fat, y_fat)


def _expert_meta(counts, n_slots):
    tm = TM_EXPERT
    n_tiles = n_slots // tm
    i32 = jnp.int32
    ends = jnp.cumsum(counts).astype(i32)
    starts = ends - counts
    used = counts > 0
    n_used = jnp.sum(used).astype(i32)
    seq_of = jnp.cumsum(used).astype(i32) - 1
    m_idx = jnp.arange(N_EXPERTS, dtype=i32)
    pick = used[None, :] & (seq_of[None, :] == m_idx[:, None])
    s_exp = jnp.sum(jnp.where(pick, m_idx[None, :], 0), axis=1).astype(i32)
    s_lo = jnp.sum(jnp.where(pick, starts[None, :], 0), axis=1).astype(i32)
    s_hi = jnp.sum(jnp.where(pick, ends[None, :], 0), axis=1).astype(i32)
    row0 = jnp.arange(n_tiles, dtype=i32)[:, None] * tm
    t_a = jnp.sum(used[None, :] & (ends[None, :] <= row0), axis=1).astype(i32)
    t_b = jnp.sum(used[None, :] & (starts[None, :] < row0 + tm), axis=1).astype(i32) - 1
    t_new = jnp.any(used[None, :] & (starts[None, :] == row0), axis=1).astype(i32)
    return t_a, (t_b - t_a + 1).astype(i32), t_new, s_exp, s_lo, s_hi, n_used.reshape(1)


def _moe(x1, h_fat, ids, rw, w_gate, w_up, w_down, norm_final):
    t, d = x1.shape
    rank, cnt = _ranks(ids)
    counts = cnt[0, :N_EXPERTS]
    offs = (jnp.cumsum(counts) - counts).astype(jnp.int32)
    ids2 = ids[:, :TOP_K].T
    onehot = ids2[:, :, None] == jnp.arange(N_EXPERTS, dtype=jnp.int32)
    pos1d = (jnp.sum(jnp.where(onehot, offs, 0), axis=-1) + rank[:, :TOP_K].T).reshape(-1)
    src_tok, dst_row = _inverse(pos1d)
    meta = _expert_meta(counts, t * TOP_K)
    y_fat = _experts(meta, src_tok, dst_row, h_fat, w_gate, w_up, w_down)
    return _combine(x1, rw, norm_final, y_fat)


def kernel(x, norm_mix, w_in, b_gate, attn_sinks, ssm_a_re, ssm_a_im, ssm_b_re, ssm_b_im, ssm_c_re, ssm_c_im, ssm_d, ssm_log_dt, w_glu, b_glu, w_attn_branch, w_ssm_branch, w_out, norm_moe, w_router_group, b_router_group, w_router_expert, b_router_expert, w_expert_gate, w_expert_up, w_expert_down, norm_final):
    b, l, d = x.shape
    depth = w_in.shape[0]
    assert depth == 1, "the final norm is fused into the last layer's combine kernel"
    d_attn = N_HEADS * HEAD_DIM
    kv_cols = N_KV_HEADS * HEAD_DIM
    d_ssm = ssm_d.shape[-1]
    bf16 = jnp.bfloat16
    x2 = x.reshape(b * l, d)
    i = 0
    q, k, v, u4, gates = _in_proj(x2, norm_mix[i][None], w_in[i].astype(bf16), b_gate[i][None],
                                  d_attn, kv_cols, d_ssm)
    attn = _attention(q, k, v, attn_sinks[i], b, l)
    tables = _ssm_tables(ssm_a_re[i], ssm_a_im[i], ssm_b_re[i], ssm_b_im[i], ssm_c_re[i], ssm_c_im[i],
                         ssm_d[i], ssm_log_dt[i], SSM_SEG)
    y4 = _ssm(u4, tables, b, l)
    pad = LANES - N_EXPERTS - N_EXPERT_GROUPS
    w_router = jnp.concatenate([w_router_expert[i], w_router_group[i], jnp.zeros((d, pad), jnp.float32)], axis=1)
    w_r_hi = w_router.astype(bf16)
    w_router = jnp.concatenate([w_r_hi, (w_router - w_r_hi.astype(jnp.float32)).astype(bf16)], axis=1)
    b_router = jnp.concatenate([b_router_expert[i], b_router_group[i], jnp.zeros((pad,), jnp.float32)])[None]
    x1, h_fat, ids, rw = _merge(x2, attn, y4, gates, w_glu[i].astype(bf16), b_glu[i][None],
                                w_attn_branch[i].astype(bf16), w_ssm_branch[i].astype(bf16),
                                w_out[i].astype(bf16), norm_moe[i][None], w_router, b_router)
    out = _moe(x1, h_fat, ids, rw, w_expert_gate[i], w_expert_up[i], w_expert_down[i], norm_final[None])
    return out.reshape(b, l, d)
```

```python
import functools
import math

import jax
import jax.numpy as jnp
from jax import lax
from jax.experimental import pallas as pl
from jax.experimental.pallas import tpu as pltpu

EPS = 1e-6
HEAD_DIM = 64
N_HEADS = 8
N_KV_HEADS = 2
Q_PER_KV = N_HEADS // N_KV_HEADS
ATTN_BLOCK = 128
ATTN_QB = 2
SSM_GROUP = 16
SSM_STATE = 64
N_EXPERT_GROUPS = 4
EXPERTS_PER_GROUP = 8
N_EXPERTS = N_EXPERT_GROUPS * EXPERTS_PER_GROUP
TOP_K = 2

LANES = 128
SUBLANES = 8
SSM_CHUNK_GROUPS = LANES // SSM_GROUP
SSM_CHUNK_STATES = SSM_CHUNK_GROUPS * SSM_STATE

TM_PROJ = 512
TM_SSM = 512
SSM_SEG = TM_SSM // SUBLANES
TM_MERGE = 512
TM_RANK = 512
TM_MOVE = 512
TM_EXPERT = 256
ROW_SUB = 4
VMEM_LIMIT = 56 * 1024 * 1024


def _cparams(*sem):
    return pltpu.CompilerParams(dimension_semantics=sem, vmem_limit_bytes=VMEM_LIMIT)


def _pack_bf16_pairs(x):
    half = x.shape[1] // 2
    bits = lax.bitcast_convert_type(x, jnp.uint32)
    return (bits[:, :half] & jnp.uint32(0xFFFF0000)) | (bits[:, half:] >> 16)


def _unpack_bf16_pairs(p):
    hi = lax.bitcast_convert_type(p & jnp.uint32(0xFFFF0000), jnp.float32)
    lo = lax.bitcast_convert_type(p << 16, jnp.float32)
    return jnp.concatenate([hi, lo], axis=1)


def _proj_kernel(x_ref, g_ref, w_ref, bg_ref, q_ref, k_ref, v_ref, u_ref, gate_ref, *, cols):
    q_c, kv_c, d_ssm = cols
    xf = x_ref[...]
    inv = lax.rsqrt(jnp.mean(xf * xf, axis=-1, keepdims=True) + EPS)
    h = (xf * inv * g_ref[...]).astype(jnp.bfloat16)
    o = 0
    q_ref[...] = (jnp.dot(h, w_ref[:, o:o + q_c], preferred_element_type=jnp.float32)
                  * (1.0 / math.sqrt(HEAD_DIM))).astype(q_ref.dtype)
    o += q_c
    k_ref[...] = jnp.dot(h, w_ref[:, o:o + kv_c], preferred_element_type=jnp.float32).astype(k_ref.dtype)
    o += kv_c
    v_ref[...] = jnp.dot(h, w_ref[:, o:o + kv_c], preferred_element_type=jnp.float32).astype(v_ref.dtype)
    o += kv_c
    for s in range(0, d_ssm // LANES, 2):
        uu = jnp.dot(h, w_ref[:, o:o + 2 * LANES], preferred_element_type=jnp.float32)
        u_ref[s] = uu[:, :LANES]
        u_ref[s + 1] = uu[:, LANES:]
        o += 2 * LANES
    gl = jnp.dot(h, w_ref[:, o:], preferred_element_type=jnp.float32) + bg_ref[...]
    gate_ref[...] = jax.nn.sigmoid(gl).astype(gate_ref.dtype)


def _in_proj(x2, norm_w, w_in, b_gate, d_attn, kv_cols, d_ssm):
    t, d = x2.shape
    gate_cols = b_gate.shape[-1]
    n_slab = d_ssm // LANES
    tm = TM_PROJ
    kern = functools.partial(_proj_kernel, cols=(d_attn, kv_cols, d_ssm))
    return pl.pallas_call(
        kern,
        grid=(t // tm,),
        in_specs=[
            pl.BlockSpec((tm, d), lambda i: (i, 0)),
            pl.BlockSpec((1, d), lambda i: (0, 0)),
            pl.BlockSpec(w_in.shape, lambda i: (0, 0)),
            pl.BlockSpec((1, gate_cols), lambda i: (0, 0)),
        ],
        out_specs=[
            pl.BlockSpec((tm, d_attn), lambda i: (i, 0)),
            pl.BlockSpec((tm, kv_cols), lambda i: (i, 0)),
            pl.BlockSpec((tm, kv_cols), lambda i: (i, 0)),
            pl.BlockSpec((n_slab, tm, LANES), lambda i: (0, i, 0)),
            pl.BlockSpec((tm, gate_cols), lambda i: (i, 0)),
        ],
        out_shape=[
            jax.ShapeDtypeStruct((t, d_attn), jnp.bfloat16),
            jax.ShapeDtypeStruct((t, kv_cols), jnp.bfloat16),
            jax.ShapeDtypeStruct((t, kv_cols), jnp.bfloat16),
            jax.ShapeDtypeStruct((n_slab, t, LANES), jnp.float32),
            jax.ShapeDtypeStruct((t, gate_cols), jnp.bfloat16),
        ],
        compiler_params=_cparams("arbitrary"),
        name="in_proj",
    )(x2, norm_w, w_in, b_gate)


def _attn_kernel(sink_ref, q_ref, kp_ref, kc_ref, vp_ref, vc_ref, rep_ref, mask_ref, o_ref):
    i = pl.program_id(1)
    blk = ATTN_BLOCK
    hw = Q_PER_KV * HEAD_DIM
    rows = Q_PER_KV * blk
    head_of_row = lax.broadcasted_iota(jnp.int32, (rows, 1), 0) >> 7
    lane_head_q = lax.broadcasted_iota(jnp.int32, (blk, hw), 1) >> 6
    lane_head_v = lax.broadcasted_iota(jnp.int32, (2 * blk, hw), 1) >> 6
    neg = jnp.finfo(jnp.float32).min
    for qb in range(ATTN_QB):
        if qb == 0:
            k2 = jnp.concatenate([kp_ref[...], kc_ref[0:blk, :]], axis=0)
            v2 = jnp.concatenate([vp_ref[...], vc_ref[0:blk, :]], axis=0)
            valid = mask_ref[jnp.where(i == 0, 1, 0)] > 0.0
        else:
            k2 = kc_ref[(qb - 1) * blk:(qb + 1) * blk, :]
            v2 = vc_ref[(qb - 1) * blk:(qb + 1) * blk, :]
            valid = mask_ref[0] > 0.0
        for kh in range(N_KV_HEADS):
            rep = rep_ref[kh]
            k4 = jnp.dot(k2, rep, preferred_element_type=jnp.float32).astype(jnp.bfloat16)
            v4 = jnp.dot(v2, rep, preferred_element_type=jnp.float32).astype(jnp.bfloat16)
            qh = q_ref[qb * blk:(qb + 1) * blk, kh * hw:(kh + 1) * hw]
            qm = jnp.concatenate(
                [jnp.where(lane_head_q == g, qh, jnp.zeros_like(qh)) for g in range(Q_PER_KV)], axis=0)
            s = lax.dot_general(qm, k4, (((1,), (1,)), ((), ())), preferred_element_type=jnp.float32)
            s = jnp.where(valid, s, neg)
            sink = jnp.zeros((rows, 1), jnp.float32)
            for g in range(Q_PER_KV):
                sink = jnp.where(head_of_row == g, sink_ref[kh * Q_PER_KV + g], sink)
            m = jnp.maximum(jnp.max(s, axis=-1, keepdims=True), sink)
            p = jnp.exp(s - m)
            rinv = 1.0 / (jnp.sum(p, axis=-1, keepdims=True) + jnp.exp(sink - m))
            p = p.astype(jnp.bfloat16)
            p_cat = jnp.concatenate([p[g * blk:(g + 1) * blk, :] for g in range(Q_PER_KV)], axis=1)
            vm = jnp.concatenate(
                [jnp.where(lane_head_v == g, v4, jnp.zeros_like(v4)) for g in range(Q_PER_KV)], axis=0)
            o = jnp.dot(p_cat, vm, preferred_element_type=jnp.float32)
            scale = jnp.zeros((blk, hw), jnp.float32)
            for g in range(Q_PER_KV):
                scale = jnp.where(lane_head_q == g, rinv[g * blk:(g + 1) * blk, :], scale)
            o_ref[qb * blk:(qb + 1) * blk, kh * hw:(kh + 1) * hw] = (o * scale).astype(o_ref.dtype)


def _attention(q, k, v, sinks, b, l):
    d_attn = q.shape[-1]
    kv_cols = k.shape[-1]
    blk = ATTN_BLOCK
    tq = ATTN_QB * blk
    hw = Q_PER_KV * HEAD_DIM
    lane = jnp.arange(hw)[None, :]
    src = jnp.arange(kv_cols)[:, None]
    rep = jnp.stack([(src == kh * HEAD_DIM + (lane % HEAD_DIM)) for kh in range(N_KV_HEADS)]
                    ).astype(jnp.bfloat16)
    r = (jnp.arange(Q_PER_KV * blk) % blk)[:, None]
    c = jnp.arange(2 * blk)[None, :]
    band = (c > r) & (c <= r + blk)
    mask = jnp.stack([band, band & (c >= blk)]).astype(jnp.float32)
    q3 = q.reshape(b, l, d_attn)
    k3 = k.reshape(b, l, kv_cols)
    v3 = v.reshape(b, l, kv_cols)
    cur = lambda bi, i: (bi, i, 0)
    prev = lambda bi, i: (bi, jnp.maximum(ATTN_QB * i - 1, 0), 0)
    out = pl.pallas_call(
        _attn_kernel,
        grid=(b, l // tq),
        in_specs=[
            pl.BlockSpec(memory_space=pltpu.SMEM),
            pl.BlockSpec((None, tq, d_attn), cur),
            pl.BlockSpec((None, blk, kv_cols), prev),
            pl.BlockSpec((None, tq, kv_cols), cur),
            pl.BlockSpec((None, blk, kv_cols), prev),
            pl.BlockSpec((None, tq, kv_cols), cur),
            pl.BlockSpec(rep.shape, lambda bi, i: (0, 0, 0)),
            pl.BlockSpec(mask.shape, lambda bi, i: (0, 0, 0)),
        ],
        out_specs=pl.BlockSpec((None, tq, d_attn), cur),
        out_shape=jax.ShapeDtypeStruct((b, l, d_attn), jnp.bfloat16),
        compiler_params=_cparams("arbitrary", "arbitrary"),
        name="swa",
    )(sinks, q3, k3, k3, v3, v3, rep, mask)
    return out.reshape(b * l, d_attn)


def _ssm_kernel(u_ref, bmat_ref, cmat_ref, lam_ref, pw_ref, lamseg_ref, d_ref, y_ref,
                up_ref, bu_ref, yp_ref, carry_ref):
    i = pl.program_id(1)
    n_slab = u_ref.shape[0]
    tm = u_ref.shape[1]
    seg = tm // SUBLANES
    ns = SSM_CHUNK_STATES

    @pl.when(i == 0)
    def _():
        carry_ref[...] = jnp.zeros_like(carry_ref)

    sub = lax.broadcasted_iota(jnp.int32, (SUBLANES, ns), 0)
    for s in range(n_slab):
        u_slab = u_ref.at[s]
        for r in range(seg):
            up_ref[r * SUBLANES:(r + 1) * SUBLANES, :] = u_slab[pl.ds(r, SUBLANES, stride=seg), :]
        up = up_ref[...]
        bu_ref[...] = jnp.dot(up.astype(jnp.bfloat16), bmat_ref[s], preferred_element_type=jnp.float32)
        lr = jnp.broadcast_to(lam_ref[s, 0:1, :], (SUBLANES, ns))
        li = jnp.broadcast_to(lam_ref[s, 1:2, :], (SUBLANES, ns))

        def step(r, st):
            sr, si = st
            rows = pl.ds(pl.multiple_of(r * SUBLANES, SUBLANES), SUBLANES)
            nr = lr * sr - li * si + bu_ref[rows, 0:ns]
            ni = lr * si + li * sr + bu_ref[rows, ns:2 * ns]
            bu_ref[rows, 0:ns] = nr
            bu_ref[rows, ns:2 * ns] = ni
            return nr, ni

        zero = jnp.zeros((SUBLANES, ns), jnp.float32)
        er, ei = lax.fori_loop(0, seg, step, (zero, zero), unroll=4)

        ar = lamseg_ref[s, 0:1, :]
        ai = lamseg_ref[s, 1:2, :]
        cr = carry_ref[s, 0:1, :]
        ci = carry_ref[s, 1:2, :]
        car = jnp.zeros((SUBLANES, ns), jnp.float32)
        cai = jnp.zeros((SUBLANES, ns), jnp.float32)
        for j in range(SUBLANES):
            car = jnp.where(sub == j, jnp.broadcast_to(cr, (SUBLANES, ns)), car)
            cai = jnp.where(sub == j, jnp.broadcast_to(ci, (SUBLANES, ns)), cai)
            ejr = jnp.sum(jnp.where(sub == j, er, 0.0), axis=0, keepdims=True)
            eji = jnp.sum(jnp.where(sub == j, ei, 0.0), axis=0, keepdims=True)
            cr, ci = ar * cr - ai * ci + ejr, ar * ci + ai * cr + eji
        carry_ref[s, 0:1, :] = cr
        carry_ref[s, 1:2, :] = ci

        ctr = jnp.broadcast_to(car[None], (seg, SUBLANES, ns)).reshape(tm, ns)
        cti = jnp.broadcast_to(cai[None], (seg, SUBLANES, ns)).reshape(tm, ns)
        pr = pw_ref[s, 0]
        pi = pw_ref[s, 1]
        st_r = bu_ref[:, 0:ns] + pr * ctr - pi * cti
        st_i = bu_ref[:, ns:2 * ns] + pr * cti + pi * ctr
        st = jnp.concatenate([st_r, st_i], axis=1).astype(jnp.bfloat16)
        yp_ref[...] = jnp.dot(st, cmat_ref[s], preferred_element_type=jnp.float32) + d_ref[s] * up
        y_slab = y_ref.at[s]
        for r in range(seg):
            y_slab[pl.ds(r, SUBLANES, stride=seg), :] = yp_ref[r * SUBLANES:(r + 1) * SUBLANES, :]


def _ssm_tables(a_re, a_im, b_re, b_im, c_re, c_im, d_skip, log_dt, seg):
    f32 = jnp.float32
    g, p = a_re.shape
    c = b_re.shape[-1]
    ng = SSM_CHUNK_GROUPS
    n_slab = g // ng
    lam = lax.complex(a_re.astype(f32), a_im.astype(f32))
    dt = jnp.exp(log_dt.astype(f32))[:, None]
    lam_bar = jnp.exp(lam * dt)
    b_bar = ((lam_bar - 1.0) / lam)[:, :, None] * lax.complex(b_re.astype(f32), b_im.astype(f32))
    c_mat = lax.complex(c_re.astype(f32), c_im.astype(f32))
    eye = jnp.eye(ng, dtype=f32)

    def bdiag_b(m):
        m = m.reshape(n_slab, ng, p, c)
        return jnp.einsum('ab,kbpc->kacbp', eye, m).reshape(n_slab, ng * c, ng * p)

    def bdiag_c(m):
        m = m.reshape(n_slab, ng, c, p)
        return jnp.einsum('ab,kbcp->kbpac', eye, m).reshape(n_slab, ng * p, ng * c)

    bmat = jnp.concatenate([bdiag_b(jnp.real(b_bar)), bdiag_b(jnp.imag(b_bar))], axis=2).astype(jnp.bfloat16)
    cmat = jnp.concatenate([bdiag_c(jnp.real(c_mat)), -bdiag_c(jnp.imag(c_mat))], axis=1).astype(jnp.bfloat16)

    def slab_rows(z):
        z = z.reshape(n_slab, 1, ng * p)
        return jnp.concatenate([jnp.real(z), jnp.imag(z)], axis=1)

    lam_t = slab_rows(lam_bar)
    steps = jnp.arange(1, seg + 1, dtype=f32)
    pw = jnp.exp((lam * dt)[None] * steps[:, None, None])
    pw = pw.reshape(seg, n_slab, ng * p).transpose(1, 0, 2)
    pw = jnp.repeat(pw, SUBLANES, axis=1)
    pw_t = jnp.stack([jnp.real(pw), jnp.imag(pw)], axis=1)
    lamseg_t = slab_rows(jnp.exp(lam * dt * float(seg)))
    d_t = d_skip.astype(f32).reshape(n_slab, 1, ng * c)
    return bmat, cmat, lam_t, pw_t, lamseg_t, d_t


def _ssm(u4, tables, b, l):
    bmat, cmat, lam_t, pw_t, lamseg_t, d_t = tables
    n_slab, t, _ = u4.shape
    tm = TM_SSM
    nt = l // tm
    ns = SSM_CHUNK_STATES
    const = lambda nd: (lambda bi, i: (0,) * nd)
    return pl.pallas_call(
        _ssm_kernel,
        grid=(b, nt),
        in_specs=[
            pl.BlockSpec((n_slab, tm, LANES), lambda bi, i: (0, bi * nt + i, 0)),
            pl.BlockSpec(bmat.shape, const(3)),
            pl.BlockSpec(cmat.shape, const(3)),
            pl.BlockSpec(lam_t.shape, const(3)),
            pl.BlockSpec(pw_t.shape, const(4)),
            pl.BlockSpec(lamseg_t.shape, const(3)),
            pl.BlockSpec(d_t.shape, const(3)),
        ],
        out_specs=pl.BlockSpec((n_slab, tm, LANES), lambda bi, i: (0, bi * nt + i, 0)),
        out_shape=jax.ShapeDtypeStruct((n_slab, t, LANES), jnp.float32),
        scratch_shapes=[
            pltpu.VMEM((tm, LANES), jnp.float32),
            pltpu.VMEM((tm, 2 * ns), jnp.float32),
            pltpu.VMEM((tm, LANES), jnp.float32),
            pltpu.VMEM((n_slab, 2, ns), jnp.float32),
        ],
        compiler_params=_cparams("arbitrary", "arbitrary"),
        name="s5_scan",
    )(u4, bmat, cmat, lam_t, pw_t, lamseg_t, d_t)


def _merge_kernel(x_ref, attn_ref, y_ref, gate_ref, wglu_ref, bglu_ref, wa_ref, ws_ref, wo_ref,
                  nm_ref, wr_ref, br_ref, x1_ref, h2_ref, ids_ref, rw_ref):
    d = x_ref.shape[1]
    tm = x_ref.shape[0]
    y = jnp.concatenate([y_ref[s] for s in range(y_ref.shape[0])], axis=1)
    z = jax.nn.gelu(y)
    zg = jnp.dot(z.astype(jnp.bfloat16), wglu_ref[...], preferred_element_type=jnp.float32) + bglu_ref[...]
    z = z * jax.nn.sigmoid(zg)
    a = jnp.dot(attn_ref[...], wa_ref[...], preferred_element_type=jnp.float32)
    sb = jnp.dot(z.astype(jnp.bfloat16), ws_ref[...], preferred_element_type=jnp.float32)
    merged = gate_ref[:, 0:d].astype(jnp.float32) * a + gate_ref[:, d:2 * d].astype(jnp.float32) * sb
    x1 = x_ref[...] + jnp.dot(merged.astype(jnp.bfloat16), wo_ref[...], preferred_element_type=jnp.float32)
    x1_ref[...] = x1
    inv = lax.rsqrt(jnp.mean(x1 * x1, axis=-1, keepdims=True) + EPS)
    h2 = x1 * inv * nm_ref[...]
    h_hi = h2.astype(jnp.bfloat16)
    h_pk = _pack_bf16_pairs(h_hi.astype(jnp.float32))
    for c in range(ROW_SUB):
        h2_ref[pl.ds(c, tm, stride=ROW_SUB), :] = h_pk[:, c * LANES:(c + 1) * LANES]

    h_lo = (h2 - h_hi.astype(jnp.float32)).astype(jnp.bfloat16)
    hh = jnp.dot(h_hi, wr_ref[...], preferred_element_type=jnp.float32)
    lh = jnp.dot(h_lo, wr_ref[:, 0:LANES], preferred_element_type=jnp.float32)
    logits = hh[:, 0:LANES] + hh[:, LANES:2 * LANES] + lh + br_ref[...]
    lane = lax.broadcasted_iota(jnp.int32, (tm, LANES), 1)
    ninf = -jnp.inf
    gl = jnp.where((lane >= N_EXPERTS) & (lane < N_EXPERTS + N_EXPERT_GROUPS), logits, ninf)
    gmax = jnp.max(gl, axis=-1, keepdims=True)
    gidx = jnp.min(jnp.where(gl == gmax, lane - N_EXPERTS, LANES), axis=-1, keepdims=True)
    group_p = 1.0 / jnp.sum(jnp.exp(gl - gmax), axis=-1, keepdims=True)
    el = jnp.where((lane < N_EXPERTS) & ((lane >> 3) == gidx), logits, ninf)
    m1 = jnp.max(el, axis=-1, keepdims=True)
    i1 = jnp.min(jnp.where(el == m1, lane, LANES), axis=-1, keepdims=True)
    el2 = jnp.where(lane == i1, ninf, el)
    m2 = jnp.max(el2, axis=-1, keepdims=True)
    i2 = jnp.min(jnp.where(el2 == m2, lane, LANES), axis=-1, keepdims=True)
    e2 = jnp.exp(m2 - m1)
    w1 = group_p / (1.0 + e2)
    w2 = group_p * e2 / (1.0 + e2)
    ids_ref[...] = jnp.where(lane == 0, i1, jnp.where(lane == 1, i2, 0))
    rw_ref[...] = jnp.where(lane == 0, w1, jnp.where(lane == 1, w2, 0.0))


def _merge(x2, attn, y4, gates, w_glu, b_glu, w_a, w_s, w_o, norm_moe, w_router, b_router):
    t, d = x2.shape
    tm = TM_MERGE
    n_slab = y4.shape[0]
    full = lambda a: pl.BlockSpec(a.shape, lambda i: (0,) * a.ndim)
    row = lambda c: pl.BlockSpec((tm, c), lambda i: (i, 0))
    return pl.pallas_call(
        _merge_kernel,
        grid=(t // tm,),
        in_specs=[
            row(d), row(attn.shape[1]),
            pl.BlockSpec((n_slab, tm, LANES), lambda i: (0, i, 0)),
            row(gates.shape[1]),
            full(w_glu), full(b_glu), full(w_a), full(w_s), full(w_o), full(norm_moe),
            full(w_router), full(b_router),
        ],
        out_specs=[row(d), pl.BlockSpec((tm * ROW_SUB, LANES), lambda i: (i, 0)), row(LANES), row(LANES)],
        out_shape=[
            jax.ShapeDtypeStruct((t, d), jnp.float32),
            jax.ShapeDtypeStruct((t * ROW_SUB, LANES), jnp.uint32),
            jax.ShapeDtypeStruct((t, LANES), jnp.int32),
            jax.ShapeDtypeStruct((t, LANES), jnp.float32),
        ],
        compiler_params=_cparams("arbitrary"),
        name="merge_router",
    )(x2, attn, y4, gates, w_glu, b_glu, w_a, w_s, w_o, norm_moe, w_router, b_router)


def _rank_kernel(ids_ref, rank_ref, cnt_ref, carry_ref):
    i = pl.program_id(0)
    tm = ids_ref.shape[0]

    @pl.when(i == 0)
    def _():
        carry_ref[...] = jnp.zeros_like(carry_ref)

    lane = lax.broadcasted_iota(jnp.int32, (tm, LANES), 1)
    ids = ids_ref[...]
    e0 = ids[:, 0:1]
    e1 = ids[:, 1:2]
    oh0 = lane == e0
    oh1 = lane == e1
    oh = oh0.astype(jnp.float32) + oh1.astype(jnp.float32)
    r_i = lax.broadcasted_iota(jnp.int32, (tm, tm), 0)
    c_i = lax.broadcasted_iota(jnp.int32, (tm, tm), 1)
    tri = (c_i < r_i).astype(jnp.bfloat16)
    cum = jnp.dot(tri, oh.astype(jnp.bfloat16), preferred_element_type=jnp.float32) + carry_ref[...]
    r0 = jnp.sum(jnp.where(oh0, cum, 0.0), axis=-1, keepdims=True)
    r1 = jnp.sum(jnp.where(oh1, cum, 0.0), axis=-1, keepdims=True)
    rank_ref[...] = jnp.where(lane == 0, r0, jnp.where(lane == 1, r1, 0.0)).astype(jnp.int32)
    carry_ref[...] = carry_ref[...] + jnp.sum(oh, axis=0, keepdims=True)
    cnt_ref[...] = carry_ref[...].astype(jnp.int32)


def _ranks(ids):
    t = ids.shape[0]
    tm = TM_RANK
    return pl.pallas_call(
        _rank_kernel,
        grid=(t // tm,),
        in_specs=[pl.BlockSpec((tm, LANES), lambda i: (i, 0))],
        out_specs=[pl.BlockSpec((tm, LANES), lambda i: (i, 0)),
                   pl.BlockSpec((1, LANES), lambda i: (0, 0))],
        out_shape=[jax.ShapeDtypeStruct((t, LANES), jnp.int32),
                   jax.ShapeDtypeStruct((1, LANES), jnp.int32)],
        scratch_shapes=[pltpu.VMEM((1, LANES), jnp.float32)],
        compiler_params=_cparams("arbitrary"),
        name="route_rank",
    )(ids)


def _inverse_kernel(pos_ref, src_ref, dst_ref):
    n_slots = pos_ref.shape[0]
    t = n_slots // TOP_K

    def body(tok, c):
        for k in range(TOP_K):
            p = pos_ref[k * t + tok]
            src_ref[p] = tok
            dst_ref[p] = k * t + tok
        return c

    lax.fori_loop(0, t, body, 0, unroll=8)


def _inverse(pos1d):
    n_slots = pos1d.shape[0]
    smem = pl.BlockSpec(memory_space=pltpu.SMEM)
    return pl.pallas_call(
        _inverse_kernel,
        in_specs=[smem],
        out_specs=[smem, smem],
        out_shape=[jax.ShapeDtypeStruct((n_slots,), jnp.int32)] * 2,
        name="route_inverse",
    )(pos1d)


def _expert_kernel(ta_ref, tn_ref, tnew_ref, sexp_ref, slo_ref, shi_ref, meta_ref, src_ref, dst_ref,
                   h_ref, wg_hbm, wu_hbm, wd_hbm, y_ref,
                   xbuf_ref, ybuf_ref, wgs_ref, wus_ref, wds_ref, wgb_ref, wub_ref, wdb_ref,
                   gsem, ssem, wsem):
    j = pl.program_id(0)
    n_tiles = pl.num_programs(0)
    tm = TM_EXPERT
    rs = ROW_SUB
    n_used = meta_ref[0]
    slot = j % 2
    a = ta_ref[j]

    def gather(tile, sl, start):
        if start:
            base = tile * tm
            for r in range(tm):
                pltpu.make_async_copy(h_ref.at[pl.ds(pl.multiple_of(src_ref[base + r] * rs, rs), rs)],
                                      xbuf_ref.at[pl.ds(pl.multiple_of((sl * tm + r) * rs, rs), rs)],
                                      gsem.at[sl]).start(priority=r % 2)
        else:
            pltpu.make_async_copy(h_ref.at[pl.ds(0, tm * rs)],
                                  xbuf_ref.at[pl.ds(pl.multiple_of(sl * tm * rs, tm * rs), tm * rs)],
                                  gsem.at[sl]).wait()

    def scatter(tile, sl, start):
        if start:
            base = tile * tm
            for r in range(tm):
                pltpu.make_async_copy(ybuf_ref.at[pl.ds(pl.multiple_of((sl * tm + r) * rs, rs), rs)],
                                      y_ref.at[pl.ds(pl.multiple_of(dst_ref[base + r] * rs, rs), rs)],
                                      ssem.at[sl]).start(priority=r % 2)
        else:
            pltpu.make_async_copy(ybuf_ref.at[pl.ds(pl.multiple_of(sl * tm * rs, tm * rs), tm * rs)],
                                  y_ref.at[pl.ds(0, tm * rs)], ssem.at[sl]).wait()

    def weights(m, start):
        sl = m % 2
        e = sexp_ref[m]
        for hbm, stage in ((wg_hbm, wgs_ref), (wu_hbm, wus_ref), (wd_hbm, wds_ref)):
            cp = pltpu.make_async_copy(hbm.at[e], stage.at[sl], wsem.at[sl])
            cp.start() if start else cp.wait()

    def new_expert(m):
        weights(m, False)
        sl = m % 2
        wgb_ref[...] = wgs_ref[sl].astype(jnp.bfloat16)
        wub_ref[...] = wus_ref[sl].astype(jnp.bfloat16)
        wdb_ref[...] = wds_ref[sl].astype(jnp.bfloat16)

        @pl.when(m + 1 < n_used)
        def _():
            weights(m + 1, True)

    def expert_out():
        xp = jnp.concatenate([xbuf_ref[pl.ds(slot * tm * rs + c, tm, stride=rs), :] for c in range(rs)], axis=1)
        x = _unpack_bf16_pairs(xp).astype(jnp.bfloat16)
        hg = jnp.dot(x, wgb_ref[...], preferred_element_type=jnp.float32)
        hu = jnp.dot(x, wub_ref[...], preferred_element_type=jnp.float32)
        act = (jax.nn.silu(hg) * hu).astype(jnp.bfloat16)
        y = jnp.dot(act, wdb_ref[...], preferred_element_type=jnp.float32)
        return _pack_bf16_pairs(y.astype(jnp.bfloat16).astype(jnp.float32))

    @pl.when(j == 0)
    def _():
        weights(0, True)
        gather(0, 0, True)

    @pl.when(tnew_ref[j] == 1)
    def _():
        new_expert(a)

    @pl.when(j >= 2)
    def _():
        scatter(j - 2, slot, False)

    gather(j, slot, False)

    def main(with_scatter):
        gather(jnp.minimum(j + 1, n_tiles - 1), 1 - slot, True)
        if with_scatter:
            scatter(j - 1, 1 - slot, True)
        yp = expert_out()
        for c in range(rs):
            ybuf_ref[pl.ds(slot * tm * rs + c, tm, stride=rs), :] = yp[:, c * LANES:(c + 1) * LANES]

    @pl.when(j == 0)
    def _():
        main(False)

    @pl.when(j > 0)
    def _():
        main(True)

    def extra(i, carry):
        m = a + i
        new_expert(m)
        yp = expert_out()
        row = lax.broadcasted_iota(jnp.int32, (tm, LANES), 0) + j * tm
        mine = (row >= slo_ref[m]) & (row < shi_ref[m])
        for c in range(rs):
            idx = pl.ds(slot * tm * rs + c, tm, stride=rs)
            ybuf_ref[idx, :] = jnp.where(mine, yp[:, c * LANES:(c + 1) * LANES], ybuf_ref[idx, :])
        return carry

    lax.fori_loop(1, tn_ref[j], extra, 0)

    @pl.when(j == n_tiles - 1)
    def _():
        scatter(j - 1, 1 - slot, False)
        scatter(j, slot, True)
        scatter(j, slot, False)
        gather(j, 1 - slot, False)


def _experts(meta, src_tok, dst_row, h_fat, w_gate, w_up, w_down):
    n_slots = src_tok.shape[0]
    ne, d, dff = w_gate.shape
    tm = TM_EXPERT
    rs = ROW_SUB
    any_spec = pl.BlockSpec(memory_space=pl.ANY)
    grid_spec = pltpu.PrefetchScalarGridSpec(
        num_scalar_prefetch=9,
        grid=(n_slots // tm,),
        in_specs=[any_spec, any_spec, any_spec, any_spec],
        out_specs=any_spec,
        scratch_shapes=[
            pltpu.VMEM((2 * tm * rs, LANES), jnp.uint32),
            pltpu.VMEM((2 * tm * rs, LANES), jnp.uint32),
            pltpu.VMEM((2, d, dff), jnp.float32),
            pltpu.VMEM((2, d, dff), jnp.float32),
            pltpu.VMEM((2, dff, d), jnp.float32),
            pltpu.VMEM((d, dff), jnp.bfloat16),
            pltpu.VMEM((d, dff), jnp.bfloat16),
            pltpu.VMEM((dff, d), jnp.bfloat16),
            pltpu.SemaphoreType.DMA((2,)),
            pltpu.SemaphoreType.DMA((2,)),
            pltpu.SemaphoreType.DMA((2,)),
        ],
    )
    return pl.pallas_call(
        _expert_kernel,
        grid_spec=grid_spec,
        out_shape=jax.ShapeDtypeStruct((n_slots * rs, LANES), jnp.uint32),
        compiler_params=_cparams("arbitrary"),
        name="moe_experts",
    )(*meta, src_tok, dst_row, h_fat, w_gate, w_up, w_down)


def _combine_kernel(x1_ref, rw_ref, nf_ref, y0_ref, y1_ref, o_ref):
    rw = rw_ref[...]
    tm = x1_ref.shape[0]
    y0 = _unpack_bf16_pairs(jnp.concatenate(
        [y0_ref[pl.ds(c, tm, stride=ROW_SUB), :] for c in range(ROW_SUB)], axis=1))
    y1 = _unpack_bf16_pairs(jnp.concatenate(
        [y1_ref[pl.ds(c, tm, stride=ROW_SUB), :] for c in range(ROW_SUB)], axis=1))
    x = x1_ref[...] + rw[:, 0:1] * y0 + rw[:, 1:2] * y1
    inv = lax.rsqrt(jnp.mean(x * x, axis=-1, keepdims=True) + EPS)
    o_ref[...] = x * inv * nf_ref[...]


def _combine(x1, rw, norm_final, y_fat):
    t, d = x1.shape
    tm = TM_MOVE
    nt = t // tm
    return pl.pallas_call(
        _combine_kernel,
        grid=(nt,),
        in_specs=[
            pl.BlockSpec((tm, d), lambda i: (i, 0)),
            pl.BlockSpec((tm, LANES), lambda i: (i, 0)),
            pl.BlockSpec((1, d), lambda i: (0, 0)),
            pl.BlockSpec((tm * ROW_SUB, LANES), lambda i: (i, 0)),
            pl.BlockSpec((tm * ROW_SUB, LANES), lambda i: (i + nt, 0)),
        ],
        out_specs=pl.BlockSpec((tm, d), lambda i: (i, 0)),
        out_shape=jax.ShapeDtypeStruct((t, d), jnp.float32),
        compiler_params=_cparams("arbitrary"),
        name="moe_combine",
    )(x1, rw, norm_final, y_fat, y_fat)


def _expert_meta(counts, n_slots):
    tm = TM_EXPERT
    n_tiles = n_slots // tm
    i32 = jnp.int32
    ends = jnp.cumsum(counts).astype(i32)
    starts = ends - counts
    used = counts > 0
    n_used = jnp.sum(used).astype(i32)
    seq_of = jnp.cumsum(used).astype(i32) - 1
    m_idx = jnp.arange(N_EXPERTS, dtype=i32)
    pick = used[None, :] & (seq_of[None, :] == m_idx[:, None])
    s_exp = jnp.sum(jnp.where(pick, m_idx[None, :], 0), axis=1).astype(i32)
    s_lo = jnp.sum(jnp.where(pick, starts[None, :], 0), axis=1).astype(i32)
    s_hi = jnp.sum(jnp.where(pick, ends[None, :], 0), axis=1).astype(i32)
    row0 = jnp.arange(n_tiles, dtype=i32)[:, None] * tm
    t_a = jnp.sum(used[None, :] & (ends[None, :] <= row0), axis=1).astype(i32)
    t_b = jnp.sum(used[None, :] & (starts[None, :] < row0 + tm), axis=1).astype(i32) - 1
    t_new = jnp.any(used[None, :] & (starts[None, :] == row0), axis=1).astype(i32)
    return t_a, (t_b - t_a + 1).astype(i32), t_new, s_exp, s_lo, s_hi, n_used.reshape(1)


def _moe(x1, h_fat, ids, rw, w_gate, w_up, w_down, norm_final):
    t, d = x1.shape
    rank, cnt = _ranks(ids)
    counts = cnt[0, :N_EXPERTS]
    offs = (jnp.cumsum(counts) - counts).astype(jnp.int32)
    ids2 = ids[:, :TOP_K].T
    onehot = ids2[:, :, None] == jnp.arange(N_EXPERTS, dtype=jnp.int32)
    pos1d = (jnp.sum(jnp.where(onehot, offs, 0), axis=-1) + rank[:, :TOP_K].T).reshape(-1)
    src_tok, dst_row = _inverse(pos1d)
    meta = _expert_meta(counts, t * TOP_K)
    y_fat = _experts(meta, src_tok, dst_row, h_fat, w_gate, w_up, w_down)
    return _combine(x1, rw, norm_final, y_fat)


def kernel(x, norm_mix, w_in, b_gate, attn_sinks, ssm_a_re, ssm_a_im, ssm_b_re, ssm_b_im, ssm_c_re, ssm_c_im, ssm_d, ssm_log_dt, w_glu, b_glu, w_attn_branch, w_ssm_branch, w_out, norm_moe, w_router_group, b_router_group, w_router_expert, b_router_expert, w_expert_gate, w_expert_up, w_expert_down, norm_final):
    b, l, d = x.shape
    depth = w_in.shape[0]
    assert depth == 1, "the final norm is fused into the last layer's combine kernel"
    d_attn = N_HEADS * HEAD_DIM
    kv_cols = N_KV_HEADS * HEAD_DIM
    d_ssm = ssm_d.shape[-1]
    bf16 = jnp.bfloat16
    x2 = x.reshape(b * l, d)
    i = 0
    q, k, v, u4, gates = _in_proj(x2, norm_mix[i][None], w_in[i].astype(bf16), b_gate[i][None],
                                  d_attn, kv_cols, d_ssm)
    attn = _attention(q, k, v, attn_sinks[i], b, l)
    tables = _ssm_tables(ssm_a_re[i], ssm_a_im[i], ssm_b_re[i], ssm_b_im[i], ssm_c_re[i], ssm_c_im[i],
                         ssm_d[i], ssm_log_dt[i], SSM_SEG)
    y4 = _ssm(u4, tables, b, l)
    pad = LANES - N_EXPERTS - N_EXPERT_GROUPS
    w_router = jnp.concatenate([w_router_expert[i], w_router_group[i], jnp.zeros((d, pad), jnp.float32)], axis=1)
    w_r_hi = w_router.astype(bf16)
    w_router = jnp.concatenate([w_r_hi, (w_router - w_r_hi.astype(jnp.float32)).astype(bf16)], axis=1)
    b_router = jnp.concatenate([b_router_expert[i], b_router_group[i], jnp.zeros((pad,), jnp.float32)])[None]
    x1, h_fat, ids, rw = _merge(x2, attn, y4, gates, w_glu[i].astype(bf16), b_glu[i][None],
                                w_attn_branch[i].astype(bf16), w_ssm_branch[i].astype(bf16),
                                w_out[i].astype(bf16), norm_moe[i][None], w_router, b_router)
    out = _moe(x1, h_fat, ids, rw, w_expert_gate[i], w_expert_up[i], w_expert_down[i], norm_final[None])
    return out.reshape(b, l, d)
```

```python
import functools
import math

import jax
import jax.numpy as jnp
from jax import lax
from jax.experimental import pallas as pl
from jax.experimental.pallas import tpu as pltpu

EPS = 1e-6
HEAD_DIM = 64
N_HEADS = 8
N_KV_HEADS = 2
Q_PER_KV = N_HEADS // N_KV_HEADS
ATTN_BLOCK = 128
ATTN_QB = 2
SSM_GROUP = 16
SSM_STATE = 64
N_EXPERT_GROUPS = 4
EXPERTS_PER_GROUP = 8
N_EXPERTS = N_EXPERT_GROUPS * EXPERTS_PER_GROUP
TOP_K = 2

LANES = 128
SUBLANES = 8
SSM_CHUNK_GROUPS = LANES // SSM_GROUP
SSM_CHUNK_STATES = SSM_CHUNK_GROUPS * SSM_STATE

TM_PROJ = 512
TM_SSM = 512
SSM_SEG = TM_SSM // SUBLANES
TM_MERGE = 512
TM_RANK = 512
TM_MOVE = 512
TM_EXPERT = 256
ROW_SUB = 4
EXPERT_BUFS = 3
VMEM_LIMIT = 56 * 1024 * 1024


def _cparams(*sem):
    return pltpu.CompilerParams(dimension_semantics=sem, vmem_limit_bytes=VMEM_LIMIT)


def _pack_bf16_pairs(x):
    half = x.shape[1] // 2
    bits = lax.bitcast_convert_type(x, jnp.uint32)
    return (bits[:, :half] & jnp.uint32(0xFFFF0000)) | (bits[:, half:] >> 16)


def _unpack_bf16_pairs(p):
    hi = lax.bitcast_convert_type(p & jnp.uint32(0xFFFF0000), jnp.float32)
    lo = lax.bitcast_convert_type(p << 16, jnp.float32)
    return jnp.concatenate([hi, lo], axis=1)


def _proj_kernel(x_ref, g_ref, w_ref, bg_ref, q_ref, k_ref, v_ref, u_ref, gate_ref, *, cols):
    q_c, kv_c, d_ssm = cols
    xf = x_ref[...]
    inv = lax.rsqrt(jnp.mean(xf * xf, axis=-1, keepdims=True) + EPS)
    h = (xf * inv * g_ref[...]).astype(jnp.bfloat16)
    o = 0
    q_ref[...] = (jnp.dot(h, w_ref[:, o:o + q_c], preferred_element_type=jnp.float32)
                  * (1.0 / math.sqrt(HEAD_DIM))).astype(q_ref.dtype)
    o += q_c
    k_ref[...] = jnp.dot(h, w_ref[:, o:o + kv_c], preferred_element_type=jnp.float32).astype(k_ref.dtype)
    o += kv_c
    v_ref[...] = jnp.dot(h, w_ref[:, o:o + kv_c], preferred_element_type=jnp.float32).astype(v_ref.dtype)
    o += kv_c
    for s in range(0, d_ssm // LANES, 2):
        uu = jnp.dot(h, w_ref[:, o:o + 2 * LANES], preferred_element_type=jnp.float32)
        u_ref[s] = uu[:, :LANES]
        u_ref[s + 1] = uu[:, LANES:]
        o += 2 * LANES
    gl = jnp.dot(h, w_ref[:, o:], preferred_element_type=jnp.float32) + bg_ref[...]
    gate_ref[...] = jax.nn.sigmoid(gl).astype(gate_ref.dtype)


def _in_proj(x2, norm_w, w_in, b_gate, d_attn, kv_cols, d_ssm):
    t, d = x2.shape
    gate_cols = b_gate.shape[-1]
    n_slab = d_ssm // LANES
    tm = TM_PROJ
    kern = functools.partial(_proj_kernel, cols=(d_attn, kv_cols, d_ssm))
    return pl.pallas_call(
        kern,
        grid=(t // tm,),
        in_specs=[
            pl.BlockSpec((tm, d), lambda i: (i, 0)),
            pl.BlockSpec((1, d), lambda i: (0, 0)),
            pl.BlockSpec(w_in.shape, lambda i: (0, 0)),
            pl.BlockSpec((1, gate_cols), lambda i: (0, 0)),
        ],
        out_specs=[
            pl.BlockSpec((tm, d_attn), lambda i: (i, 0)),
            pl.BlockSpec((tm, kv_cols), lambda i: (i, 0)),
            pl.BlockSpec((tm, kv_cols), lambda i: (i, 0)),
            pl.BlockSpec((n_slab, tm, LANES), lambda i: (0, i, 0)),
            pl.BlockSpec((tm, gate_cols), lambda i: (i, 0)),
        ],
        out_shape=[
            jax.ShapeDtypeStruct((t, d_attn), jnp.bfloat16),
            jax.ShapeDtypeStruct((t, kv_cols), jnp.bfloat16),
            jax.ShapeDtypeStruct((t, kv_cols), jnp.bfloat16),
            jax.ShapeDtypeStruct((n_slab, t, LANES), jnp.float32),
            jax.ShapeDtypeStruct((t, gate_cols), jnp.bfloat16),
        ],
        compiler_params=_cparams("arbitrary"),
        name="in_proj",
    )(x2, norm_w, w_in, b_gate)


def _attn_kernel(sink_ref, q_ref, kp_ref, kc_ref, vp_ref, vc_ref, rep_ref, mask_ref, o_ref):
    i = pl.program_id(1)
    blk = ATTN_BLOCK
    hw = Q_PER_KV * HEAD_DIM
    rows = Q_PER_KV * blk
    head_of_row = lax.broadcasted_iota(jnp.int32, (rows, 1), 0) >> 7
    lane_head_q = lax.broadcasted_iota(jnp.int32, (blk, hw), 1) >> 6
    lane_head_v = lax.broadcasted_iota(jnp.int32, (2 * blk, hw), 1) >> 6
    neg = jnp.finfo(jnp.float32).min
    for qb in range(ATTN_QB):
        if qb == 0:
            k2 = jnp.concatenate([kp_ref[...], kc_ref[0:blk, :]], axis=0)
            v2 = jnp.concatenate([vp_ref[...], vc_ref[0:blk, :]], axis=0)
            valid = mask_ref[jnp.where(i == 0, 1, 0)] > 0.0
        else:
            k2 = kc_ref[(qb - 1) * blk:(qb + 1) * blk, :]
            v2 = vc_ref[(qb - 1) * blk:(qb + 1) * blk, :]
            valid = mask_ref[0] > 0.0
        for kh in range(N_KV_HEADS):
            rep = rep_ref[kh]
            k4 = jnp.dot(k2, rep, preferred_element_type=jnp.float32).astype(jnp.bfloat16)
            v4 = jnp.dot(v2, rep, preferred_element_type=jnp.float32).astype(jnp.bfloat16)
            qh = q_ref[qb * blk:(qb + 1) * blk, kh * hw:(kh + 1) * hw]
            qm = jnp.concatenate(
                [jnp.where(lane_head_q == g, qh, jnp.zeros_like(qh)) for g in range(Q_PER_KV)], axis=0)
            s = lax.dot_general(qm, k4, (((1,), (1,)), ((), ())), preferred_element_type=jnp.float32)
            s = jnp.where(valid, s, neg)
            sink = jnp.zeros((rows, 1), jnp.float32)
            for g in range(Q_PER_KV):
                sink = jnp.where(head_of_row == g, sink_ref[kh * Q_PER_KV + g], sink)
            m = jnp.maximum(jnp.max(s, axis=-1, keepdims=True), sink)
            p = jnp.exp(s - m)
            rinv = 1.0 / (jnp.sum(p, axis=-1, keepdims=True) + jnp.exp(sink - m))
            p = p.astype(jnp.bfloat16)
            p_cat = jnp.concatenate([p[g * blk:(g + 1) * blk, :] for g in range(Q_PER_KV)], axis=1)
            vm = jnp.concatenate(
                [jnp.where(lane_head_v == g, v4, jnp.zeros_like(v4)) for g in range(Q_PER_KV)], axis=0)
            o = jnp.dot(p_cat, vm, preferred_element_type=jnp.float32)
            scale = jnp.zeros((blk, hw), jnp.float32)
            for g in range(Q_PER_KV):
                scale = jnp.where(lane_head_q == g, rinv[g * blk:(g + 1) * blk, :], scale)
            o_ref[qb * blk:(qb + 1) * blk, kh * hw:(kh + 1) * hw] = (o * scale).astype(o_ref.dtype)


def _attention(q, k, v, sinks, b, l):
    d_attn = q.shape[-1]
    kv_cols = k.shape[-1]
    blk = ATTN_BLOCK
    tq = ATTN_QB * blk
    hw = Q_PER_KV * HEAD_DIM
    lane = jnp.arange(hw)[None, :]
    src = jnp.arange(kv_cols)[:, None]
    rep = jnp.stack([(src == kh * HEAD_DIM + (lane % HEAD_DIM)) for kh in range(N_KV_HEADS)]
                    ).astype(jnp.bfloat16)
    r = (jnp.arange(Q_PER_KV * blk) % blk)[:, None]
    c = jnp.arange(2 * blk)[None, :]
    band = (c > r) & (c <= r + blk)
    mask = jnp.stack([band, band & (c >= blk)]).astype(jnp.float32)
    q3 = q.reshape(b, l, d_attn)
    k3 = k.reshape(b, l, kv_cols)
    v3 = v.reshape(b, l, kv_cols)
    cur = lambda bi, i: (bi, i, 0)
    prev = lambda bi, i: (bi, jnp.maximum(ATTN_QB * i - 1, 0), 0)
    out = pl.pallas_call(
        _attn_kernel,
        grid=(b, l // tq),
        in_specs=[
            pl.BlockSpec(memory_space=pltpu.SMEM),
            pl.BlockSpec((None, tq, d_attn), cur),
            pl.BlockSpec((None, blk, kv_cols), prev),
            pl.BlockSpec((None, tq, kv_cols), cur),
            pl.BlockSpec((None, blk, kv_cols), prev),
            pl.BlockSpec((None, tq, kv_cols), cur),
            pl.BlockSpec(rep.shape, lambda bi, i: (0, 0, 0)),
            pl.BlockSpec(mask.shape, lambda bi, i: (0, 0, 0)),
        ],
        out_specs=pl.BlockSpec((None, tq, d_attn), cur),
        out_shape=jax.ShapeDtypeStruct((b, l, d_attn), jnp.bfloat16),
        compiler_params=_cparams("arbitrary", "arbitrary"),
        name="swa",
    )(sinks, q3, k3, k3, v3, v3, rep, mask)
    return out.reshape(b * l, d_attn)


def _ssm_kernel(u_ref, bmat_ref, cmat_ref, lam_ref, pw_ref, lamseg_ref, d_ref, y_ref,
                up_ref, bu_ref, yp_ref, carry_ref):
    i = pl.program_id(1)
    n_slab = u_ref.shape[0]
    tm = u_ref.shape[1]
    seg = tm // SUBLANES
    ns = SSM_CHUNK_STATES

    @pl.when(i == 0)
    def _():
        carry_ref[...] = jnp.zeros_like(carry_ref)

    sub = lax.broadcasted_iota(jnp.int32, (SUBLANES, ns), 0)
    for s in range(n_slab):
        u_slab = u_ref.at[s]
        for r in range(seg):
            up_ref[r * SUBLANES:(r + 1) * SUBLANES, :] = u_slab[pl.ds(r, SUBLANES, stride=seg), :]
        up = up_ref[...]
        bu_ref[...] = jnp.dot(up.astype(jnp.bfloat16), bmat_ref[s], preferred_element_type=jnp.float32)
        lr = jnp.broadcast_to(lam_ref[s, 0:1, :], (SUBLANES, ns))
        li = jnp.broadcast_to(lam_ref[s, 1:2, :], (SUBLANES, ns))

        def step(r, st):
            sr, si = st
            rows = pl.ds(pl.multiple_of(r * SUBLANES, SUBLANES), SUBLANES)
            nr = lr * sr - li * si + bu_ref[rows, 0:ns]
            ni = lr * si + li * sr + bu_ref[rows, ns:2 * ns]
            bu_ref[rows, 0:ns] = nr
            bu_ref[rows, ns:2 * ns] = ni
            return nr, ni

        zero = jnp.zeros((SUBLANES, ns), jnp.float32)
        er, ei = lax.fori_loop(0, seg, step, (zero, zero), unroll=4)

        ar = lamseg_ref[s, 0:1, :]
        ai = lamseg_ref[s, 1:2, :]
        cr = carry_ref[s, 0:1, :]
        ci = carry_ref[s, 1:2, :]
        car = jnp.zeros((SUBLANES, ns), jnp.float32)
        cai = jnp.zeros((SUBLANES, ns), jnp.float32)
        for j in range(SUBLANES):
            car = jnp.where(sub == j, jnp.broadcast_to(cr, (SUBLANES, ns)), car)
            cai = jnp.where(sub == j, jnp.broadcast_to(ci, (SUBLANES, ns)), cai)
            ejr = jnp.sum(jnp.where(sub == j, er, 0.0), axis=0, keepdims=True)
            eji = jnp.sum(jnp.where(sub == j, ei, 0.0), axis=0, keepdims=True)
            cr, ci = ar * cr - ai * ci + ejr, ar * ci + ai * cr + eji
        carry_ref[s, 0:1, :] = cr
        carry_ref[s, 1:2, :] = ci

        ctr = jnp.broadcast_to(car[None], (seg, SUBLANES, ns)).reshape(tm, ns)
        cti = jnp.broadcast_to(cai[None], (seg, SUBLANES, ns)).reshape(tm, ns)
        pr = pw_ref[s, 0]
        pi = pw_ref[s, 1]
        st_r = bu_ref[:, 0:ns] + pr * ctr - pi * cti
        st_i = bu_ref[:, ns:2 * ns] + pr * cti + pi * ctr
        st = jnp.concatenate([st_r, st_i], axis=1).astype(jnp.bfloat16)
        yp_ref[...] = jnp.dot(st, cmat_ref[s], preferred_element_type=jnp.float32) + d_ref[s] * up
        y_slab = y_ref.at[s]
        for r in range(seg):
            y_slab[pl.ds(r, SUBLANES, stride=seg), :] = yp_ref[r * SUBLANES:(r + 1) * SUBLANES, :]


def _ssm_tables(a_re, a_im, b_re, b_im, c_re, c_im, d_skip, log_dt, seg):
    f32 = jnp.float32
    g, p = a_re.shape
    c = b_re.shape[-1]
    ng = SSM_CHUNK_GROUPS
    n_slab = g // ng
    lam = lax.complex(a_re.astype(f32), a_im.astype(f32))
    dt = jnp.exp(log_dt.astype(f32))[:, None]
    lam_bar = jnp.exp(lam * dt)
    b_bar = ((lam_bar - 1.0) / lam)[:, :, None] * lax.complex(b_re.astype(f32), b_im.astype(f32))
    c_mat = lax.complex(c_re.astype(f32), c_im.astype(f32))
    eye = jnp.eye(ng, dtype=f32)

    def bdiag_b(m):
        m = m.reshape(n_slab, ng, p, c)
        return jnp.einsum('ab,kbpc->kacbp', eye, m).reshape(n_slab, ng * c, ng * p)

    def bdiag_c(m):
        m = m.reshape(n_slab, ng, c, p)
        return jnp.einsum('ab,kbcp->kbpac', eye, m).reshape(n_slab, ng * p, ng * c)

    bmat = jnp.concatenate([bdiag_b(jnp.real(b_bar)), bdiag_b(jnp.imag(b_bar))], axis=2).astype(jnp.bfloat16)
    cmat = jnp.concatenate([bdiag_c(jnp.real(c_mat)), -bdiag_c(jnp.imag(c_mat))], axis=1).astype(jnp.bfloat16)

    def slab_rows(z):
        z = z.reshape(n_slab, 1, ng * p)
        return jnp.concatenate([jnp.real(z), jnp.imag(z)], axis=1)

    lam_t = slab_rows(lam_bar)
    steps = jnp.arange(1, seg + 1, dtype=f32)
    pw = jnp.exp((lam * dt)[None] * steps[:, None, None])
    pw = pw.reshape(seg, n_slab, ng * p).transpose(1, 0, 2)
    pw = jnp.repeat(pw, SUBLANES, axis=1)
    pw_t = jnp.stack([jnp.real(pw), jnp.imag(pw)], axis=1)
    lamseg_t = slab_rows(jnp.exp(lam * dt * float(seg)))
    d_t = d_skip.astype(f32).reshape(n_slab, 1, ng * c)
    return bmat, cmat, lam_t, pw_t, lamseg_t, d_t


def _ssm(u4, tables, b, l):
    bmat, cmat, lam_t, pw_t, lamseg_t, d_t = tables
    n_slab, t, _ = u4.shape
    tm = TM_SSM
    nt = l // tm
    ns = SSM_CHUNK_STATES
    const = lambda nd: (lambda bi, i: (0,) * nd)
    return pl.pallas_call(
        _ssm_kernel,
        grid=(b, nt),
        in_specs=[
            pl.BlockSpec((n_slab, tm, LANES), lambda bi, i: (0, bi * nt + i, 0)),
            pl.BlockSpec(bmat.shape, const(3)),
            pl.BlockSpec(cmat.shape, const(3)),
            pl.BlockSpec(lam_t.shape, const(3)),
            pl.BlockSpec(pw_t.shape, const(4)),
            pl.BlockSpec(lamseg_t.shape, const(3)),
            pl.BlockSpec(d_t.shape, const(3)),
        ],
        out_specs=pl.BlockSpec((n_slab, tm, LANES), lambda bi, i: (0, bi * nt + i, 0)),
        out_shape=jax.ShapeDtypeStruct((n_slab, t, LANES), jnp.float32),
        scratch_shapes=[
            pltpu.VMEM((tm, LANES), jnp.float32),
            pltpu.VMEM((tm, 2 * ns), jnp.float32),
            pltpu.VMEM((tm, LANES), jnp.float32),
            pltpu.VMEM((n_slab, 2, ns), jnp.float32),
        ],
        compiler_params=_cparams("arbitrary", "arbitrary"),
        name="s5_scan",
    )(u4, bmat, cmat, lam_t, pw_t, lamseg_t, d_t)


def _merge_kernel(x_ref, attn_ref, y_ref, gate_ref, wglu_ref, bglu_ref, wa_ref, ws_ref, wo_ref,
                  nm_ref, wr_ref, br_ref, x1_ref, h2_ref, ids_ref, rw_ref):
    d = x_ref.shape[1]
    tm = x_ref.shape[0]
    y = jnp.concatenate([y_ref[s] for s in range(y_ref.shape[0])], axis=1)
    z = jax.nn.gelu(y)
    zg = jnp.dot(z.astype(jnp.bfloat16), wglu_ref[...], preferred_element_type=jnp.float32) + bglu_ref[...]
    z = z * jax.nn.sigmoid(zg)
    a = jnp.dot(attn_ref[...], wa_ref[...], preferred_element_type=jnp.float32)
    sb = jnp.dot(z.astype(jnp.bfloat16), ws_ref[...], preferred_element_type=jnp.float32)
    merged = gate_ref[:, 0:d].astype(jnp.float32) * a + gate_ref[:, d:2 * d].astype(jnp.float32) * sb
    x1 = x_ref[...] + jnp.dot(merged.astype(jnp.bfloat16), wo_ref[...], preferred_element_type=jnp.float32)
    x1_ref[...] = x1
    inv = lax.rsqrt(jnp.mean(x1 * x1, axis=-1, keepdims=True) + EPS)
    h2 = x1 * inv * nm_ref[...]
    h_hi = h2.astype(jnp.bfloat16)
    h_pk = _pack_bf16_pairs(h_hi.astype(jnp.float32))
    for c in range(ROW_SUB):
        h2_ref[pl.ds(c, tm, stride=ROW_SUB), :] = h_pk[:, c * LANES:(c + 1) * LANES]

    h_lo = (h2 - h_hi.astype(jnp.float32)).astype(jnp.bfloat16)
    hh = jnp.dot(h_hi, wr_ref[...], preferred_element_type=jnp.float32)
    lh = jnp.dot(h_lo, wr_ref[:, 0:LANES], preferred_element_type=jnp.float32)
    logits = hh[:, 0:LANES] + hh[:, LANES:2 * LANES] + lh + br_ref[...]
    lane = lax.broadcasted_iota(jnp.int32, (tm, LANES), 1)
    ninf = -jnp.inf
    gl = jnp.where((lane >= N_EXPERTS) & (lane < N_EXPERTS + N_EXPERT_GROUPS), logits, ninf)
    gmax = jnp.max(gl, axis=-1, keepdims=True)
    gidx = jnp.min(jnp.where(gl == gmax, lane - N_EXPERTS, LANES), axis=-1, keepdims=True)
    group_p = 1.0 / jnp.sum(jnp.exp(gl - gmax), axis=-1, keepdims=True)
    el = jnp.where((lane < N_EXPERTS) & ((lane >> 3) == gidx), logits, ninf)
    m1 = jnp.max(el, axis=-1, keepdims=True)
    i1 = jnp.min(jnp.where(el == m1, lane, LANES), axis=-1, keepdims=True)
    el2 = jnp.where(lane == i1, ninf, el)
    m2 = jnp.max(el2, axis=-1, keepdims=True)
    i2 = jnp.min(jnp.where(el2 == m2, lane, LANES), axis=-1, keepdims=True)
    e2 = jnp.exp(m2 - m1)
    w1 = group_p / (1.0 + e2)
    w2 = group_p * e2 / (1.0 + e2)
    ids_ref[...] = jnp.where(lane == 0, i1, jnp.where(lane == 1, i2, 0))
    rw_ref[...] = jnp.where(lane == 0, w1, jnp.where(lane == 1, w2, 0.0))


def _merge(x2, attn, y4, gates, w_glu, b_glu, w_a, w_s, w_o, norm_moe, w_router, b_router):
    t, d = x2.shape
    tm = TM_MERGE
    n_slab = y4.shape[0]
    full = lambda a: pl.BlockSpec(a.shape, lambda i: (0,) * a.ndim)
    row = lambda c: pl.BlockSpec((tm, c), lambda i: (i, 0))
    return pl.pallas_call(
        _merge_kernel,
        grid=(t // tm,),
        in_specs=[
            row(d), row(attn.shape[1]),
            pl.BlockSpec((n_slab, tm, LANES), lambda i: (0, i, 0)),
            row(gates.shape[1]),
            full(w_glu), full(b_glu), full(w_a), full(w_s), full(w_o), full(norm_moe),
            full(w_router), full(b_router),
        ],
        out_specs=[row(d), pl.BlockSpec((tm * ROW_SUB, LANES), lambda i: (i, 0)), row(LANES), row(LANES)],
        out_shape=[
            jax.ShapeDtypeStruct((t, d), jnp.float32),
            jax.ShapeDtypeStruct((t * ROW_SUB, LANES), jnp.uint32),
            jax.ShapeDtypeStruct((t, LANES), jnp.int32),
            jax.ShapeDtypeStruct((t, LANES), jnp.float32),
        ],
        compiler_params=_cparams("arbitrary"),
        name="merge_router",
    )(x2, attn, y4, gates, w_glu, b_glu, w_a, w_s, w_o, norm_moe, w_router, b_router)


def _rank_kernel(ids_ref, rank_ref, cnt_ref, carry_ref):
    i = pl.program_id(0)
    tm = ids_ref.shape[0]

    @pl.when(i == 0)
    def _():
        carry_ref[...] = jnp.zeros_like(carry_ref)

    lane = lax.broadcasted_iota(jnp.int32, (tm, LANES), 1)
    ids = ids_ref[...]
    e0 = ids[:, 0:1]
    e1 = ids[:, 1:2]
    oh0 = lane == e0
    oh1 = lane == e1
    oh = oh0.astype(jnp.float32) + oh1.astype(jnp.float32)
    r_i = lax.broadcasted_iota(jnp.int32, (tm, tm), 0)
    c_i = lax.broadcasted_iota(jnp.int32, (tm, tm), 1)
    tri = (c_i < r_i).astype(jnp.bfloat16)
    cum = jnp.dot(tri, oh.astype(jnp.bfloat16), preferred_element_type=jnp.float32) + carry_ref[...]
    r0 = jnp.sum(jnp.where(oh0, cum, 0.0), axis=-1, keepdims=True)
    r1 = jnp.sum(jnp.where(oh1, cum, 0.0), axis=-1, keepdims=True)
    rank_ref[...] = jnp.where(lane == 0, r0, jnp.where(lane == 1, r1, 0.0)).astype(jnp.int32)
    carry_ref[...] = carry_ref[...] + jnp.sum(oh, axis=0, keepdims=True)
    cnt_ref[...] = carry_ref[...].astype(jnp.int32)


def _ranks(ids):
    t = ids.shape[0]
    tm = TM_RANK
    return pl.pallas_call(
        _rank_kernel,
        grid=(t // tm,),
        in_specs=[pl.BlockSpec((tm, LANES), lambda i: (i, 0))],
        out_specs=[pl.BlockSpec((tm, LANES), lambda i: (i, 0)),
                   pl.BlockSpec((1, LANES), lambda i: (0, 0))],
        out_shape=[jax.ShapeDtypeStruct((t, LANES), jnp.int32),
                   jax.ShapeDtypeStruct((1, LANES), jnp.int32)],
        scratch_shapes=[pltpu.VMEM((1, LANES), jnp.float32)],
        compiler_params=_cparams("arbitrary"),
        name="route_rank",
    )(ids)


def _inverse_kernel(pos_ref, src_ref, dst_ref):
    n_slots = pos_ref.shape[0]
    t = n_slots // TOP_K

    def body(tok, c):
        for k in range(TOP_K):
            p = pos_ref[k * t + tok]
            src_ref[p] = tok * ROW_SUB
            dst_ref[p] = (k * t + tok) * ROW_SUB
        return c

    lax.fori_loop(0, t, body, 0, unroll=8)


def _inverse(pos1d):
    n_slots = pos1d.shape[0]
    smem = pl.BlockSpec(memory_space=pltpu.SMEM)
    return pl.pallas_call(
        _inverse_kernel,
        in_specs=[smem],
        out_specs=[smem, smem],
        out_shape=[jax.ShapeDtypeStruct((n_slots,), jnp.int32)] * 2,
        name="route_inverse",
    )(pos1d)


def _expert_kernel(ta_ref, tn_ref, tnew_ref, sexp_ref, slo_ref, shi_ref, meta_ref, src_ref, dst_ref,
                   h_ref, wg_hbm, wu_hbm, wd_hbm, y_ref,
                   xbuf_ref, ybuf_ref, wgs_ref, wus_ref, wds_ref, wgb_ref, wub_ref, wdb_ref,
                   gsem, ssem, wsem):
    j = pl.program_id(0)
    n_tiles = pl.num_programs(0)
    last = n_tiles - 1
    tm = TM_EXPERT
    rs = ROW_SUB
    nb = EXPERT_BUFS
    n_used = meta_ref[0]
    slot = j % nb
    a = ta_ref[j]

    def buf_rows(ref, sl, r0=0, n=tm * rs):
        return ref.at[pl.ds(pl.multiple_of(sl * (tm * rs) + r0, rs), n)]

    def gather(tile, sl, start):
        if start:
            base = tile * tm
            for r in range(tm):
                pltpu.make_async_copy(h_ref.at[pl.ds(pl.multiple_of(src_ref[base + r], rs), rs)],
                                      buf_rows(xbuf_ref, sl, r * rs, rs), gsem.at[sl]).start(priority=r % 2)
        else:
            pltpu.make_async_copy(h_ref.at[pl.ds(0, tm * rs)], buf_rows(xbuf_ref, sl), gsem.at[sl]).wait()

    def scatter(tile, sl, start):
        if start:
            base = tile * tm
            for r in range(tm):
                pltpu.make_async_copy(buf_rows(ybuf_ref, sl, r * rs, rs),
                                      y_ref.at[pl.ds(pl.multiple_of(dst_ref[base + r], rs), rs)],
                                      ssem.at[sl]).start(priority=r % 2)
        else:
            pltpu.make_async_copy(buf_rows(ybuf_ref, sl), y_ref.at[pl.ds(0, tm * rs)], ssem.at[sl]).wait()

    def weights(m, start):
        sl = m % 2
        e = sexp_ref[m]
        for hbm, stage in ((wg_hbm, wgs_ref), (wu_hbm, wus_ref), (wd_hbm, wds_ref)):
            cp = pltpu.make_async_copy(hbm.at[e], stage.at[sl], wsem.at[sl])
            cp.start() if start else cp.wait()

    def new_expert(m):
        weights(m, False)
        sl = m % 2
        wgb_ref[...] = wgs_ref[sl].astype(jnp.bfloat16)
        wub_ref[...] = wus_ref[sl].astype(jnp.bfloat16)
        wdb_ref[...] = wds_ref[sl].astype(jnp.bfloat16)

        @pl.when(m + 1 < n_used)
        def _():
            weights(m + 1, True)

    def load_rows():
        return jnp.concatenate(
            [xbuf_ref[pl.ds(slot * tm * rs + c, tm, stride=rs), :] for c in range(rs)], axis=1)

    def expert_out(xp):
        x = _unpack_bf16_pairs(xp).astype(jnp.bfloat16)
        hg = jnp.dot(x, wgb_ref[...], preferred_element_type=jnp.float32)
        hu = jnp.dot(x, wub_ref[...], preferred_element_type=jnp.float32)
        act = (jax.nn.silu(hg) * hu).astype(jnp.bfloat16)
        y = jnp.dot(act, wdb_ref[...], preferred_element_type=jnp.float32)
        return _pack_bf16_pairs(y.astype(jnp.bfloat16).astype(jnp.float32))

    @pl.when(j == 0)
    def _():
        weights(0, True)
        gather(0, 0, True)
        gather(1, 1, True)

    @pl.when(tnew_ref[j] == 1)
    def _():
        new_expert(a)

    @pl.when(j >= nb)
    def _():
        scatter(j - nb, slot, False)

    gather(j, slot, False)

    def main(with_scatter):
        xp = load_rows()
        gather(jnp.minimum(j + 2, last), (j + 2) % nb, True)
        if with_scatter:
            scatter(j - 1, (j - 1) % nb, True)
        yp = expert_out(xp)
        for c in range(rs):
            ybuf_ref[pl.ds(slot * tm * rs + c, tm, stride=rs), :] = yp[:, c * LANES:(c + 1) * LANES]

    @pl.when(j == 0)
    def _():
        main(False)

    @pl.when(j > 0)
    def _():
        main(True)

    def extra(i, carry):
        m = a + i
        new_expert(m)
        yp = expert_out(load_rows())
        row = lax.broadcasted_iota(jnp.int32, (tm, LANES), 0) + j * tm
        mine = (row >= slo_ref[m]) & (row < shi_ref[m])
        for c in range(rs):
            idx = pl.ds(slot * tm * rs + c, tm, stride=rs)
            ybuf_ref[idx, :] = jnp.where(mine, yp[:, c * LANES:(c + 1) * LANES], ybuf_ref[idx, :])
        return carry

    lax.fori_loop(1, tn_ref[j], extra, 0)

    @pl.when(j == last)
    def _():
        scatter(j - 2, (j - 2) % nb, False)
        scatter(j - 1, (j - 1) % nb, False)
        scatter(j, slot, True)
        scatter(j, slot, False)
        gather(j, (j + 1) % nb, False)
        gather(j, (j + 2) % nb, False)


def _experts(meta, src_tok, dst_row, h_fat, w_gate, w_up, w_down):
    n_slots = src_tok.shape[0]
    ne, d, dff = w_gate.shape
    tm = TM_EXPERT
    rs = ROW_SUB
    assert n_slots // tm >= EXPERT_BUFS
    any_spec = pl.BlockSpec(memory_space=pl.ANY)
    grid_spec = pltpu.PrefetchScalarGridSpec(
        num_scalar_prefetch=9,
        grid=(n_slots // tm,),
        in_specs=[any_spec, any_spec, any_spec, any_spec],
        out_specs=any_spec,
        scratch_shapes=[
            pltpu.VMEM((EXPERT_BUFS * tm * rs, LANES), jnp.uint32),
            pltpu.VMEM((EXPERT_BUFS * tm * rs, LANES), jnp.uint32),
            pltpu.VMEM((2, d, dff), jnp.float32),
            pltpu.VMEM((2, d, dff), jnp.float32),
            pltpu.VMEM((2, dff, d), jnp.float32),
            pltpu.VMEM((d, dff), jnp.bfloat16),
            pltpu.VMEM((d, dff), jnp.bfloat16),
            pltpu.VMEM((dff, d), jnp.bfloat16),
            pltpu.SemaphoreType.DMA((EXPERT_BUFS,)),
            pltpu.SemaphoreType.DMA((EXPERT_BUFS,)),
            pltpu.SemaphoreType.DMA((2,)),
        ],
    )
    return pl.pallas_call(
        _expert_kernel,
        grid_spec=grid_spec,
        out_shape=jax.ShapeDtypeStruct((n_slots * rs, LANES), jnp.uint32),
        compiler_params=_cparams("arbitrary"),
        name="moe_experts",
    )(*meta, src_tok, dst_row, h_fat, w_gate, w_up, w_down)


def _combine_kernel(x1_ref, rw_ref, nf_ref, y0_ref, y1_ref, o_ref):
    rw = rw_ref[...]
    tm = x1_ref.shape[0]
    y0 = _unpack_bf16_pairs(jnp.concatenate(
        [y0_ref[pl.ds(c, tm, stride=ROW_SUB), :] for c in range(ROW_SUB)], axis=1))
    y1 = _unpack_bf16_pairs(jnp.concatenate(
        [y1_ref[pl.ds(c, tm, stride=ROW_SUB), :] for c in range(ROW_SUB)], axis=1))
    x = x1_ref[...] + rw[:, 0:1] * y0 + rw[:, 1:2] * y1
    inv = lax.rsqrt(jnp.mean(x * x, axis=-1, keepdims=True) + EPS)
    o_ref[...] = x * inv * nf_ref[...]


def _combine(x1, rw, norm_final, y_fat):
    t, d = x1.shape
    tm = TM_MOVE
    nt = t // tm
    return pl.pallas_call(
        _combine_kernel,
        grid=(nt,),
        in_specs=[
            pl.BlockSpec((tm, d), lambda i: (i, 0)),
            pl.BlockSpec((tm, LANES), lambda i: (i, 0)),
            pl.BlockSpec((1, d), lambda i: (0, 0)),
            pl.BlockSpec((tm * ROW_SUB, LANES), lambda i: (i, 0)),
            pl.BlockSpec((tm * ROW_SUB, LANES), lambda i: (i + nt, 0)),
        ],
        out_specs=pl.BlockSpec((tm, d), lambda i: (i, 0)),
        out_shape=jax.ShapeDtypeStruct((t, d), jnp.float32),
        compiler_params=_cparams("arbitrary"),
        name="moe_combine",
    )(x1, rw, norm_final, y_fat, y_fat)


def _expert_meta(counts, n_slots):
    tm = TM_EXPERT
    n_tiles = n_slots // tm
    i32 = jnp.int32
    ends = jnp.cumsum(counts).astype(i32)
    starts = ends - counts
    used = counts > 0
    n_used = jnp.sum(used).astype(i32)
    seq_of = jnp.cumsum(used).astype(i32) - 1
    m_idx = jnp.arange(N_EXPERTS, dtype=i32)
    pick = used[None, :] & (seq_of[None, :] == m_idx[:, None])
    s_exp = jnp.sum(jnp.where(pick, m_idx[None, :], 0), axis=1).astype(i32)
    s_lo = jnp.sum(jnp.where(pick, starts[None, :], 0), axis=1).astype(i32)
    s_hi = jnp.sum(jnp.where(pick, ends[None, :], 0), axis=1).astype(i32)
    row0 = jnp.arange(n_tiles, dtype=i32)[:, None] * tm
    t_a = jnp.sum(used[None, :] & (ends[None, :] <= row0), axis=1).astype(i32)
    t_b = jnp.sum(used[None, :] & (starts[None, :] < row0 + tm), axis=1).astype(i32) - 1
    t_new = jnp.any(used[None, :] & (starts[None, :] == row0), axis=1).astype(i32)
    return t_a, (t_b - t_a + 1).astype(i32), t_new, s_exp, s_lo, s_hi, n_used.reshape(1)


def _moe(x1, h_fat, ids, rw, w_gate, w_up, w_down, norm_final):
    t, d = x1.shape
    rank, cnt = _ranks(ids)
    counts = cnt[0, :N_EXPERTS]
    offs = (jnp.cumsum(counts) - counts).astype(jnp.int32)
    ids2 = ids[:, :TOP_K].T
    onehot = ids2[:, :, None] == jnp.arange(N_EXPERTS, dtype=jnp.int32)
    pos1d = (jnp.sum(jnp.where(onehot, offs, 0), axis=-1) + rank[:, :TOP_K].T).reshape(-1)
    src_tok, dst_row = _inverse(pos1d)
    meta = _expert_meta(counts, t * TOP_K)
    y_fat = _experts(meta, src_tok, dst_row, h_fat, w_gate, w_up, w_down)
    return _combine(x1, rw, norm_final, y_fat)


def kernel(x, norm_mix, w_in, b_gate, attn_sinks, ssm_a_re, ssm_a_im, ssm_b_re, ssm_b_im, ssm_c_re, ssm_c_im, ssm_d, ssm_log_dt, w_glu, b_glu, w_attn_branch, w_ssm_branch, w_out, norm_moe, w_router_group, b_router_group, w_router_expert, b_router_expert, w_expert_gate, w_expert_up, w_expert_down, norm_final):
    b, l, d = x.shape
    depth = w_in.shape[0]
    assert depth == 1, "the final norm is fused into the last layer's combine kernel"
    d_attn = N_HEADS * HEAD_DIM
    kv_cols = N_KV_HEADS * HEAD_DIM
    d_ssm = ssm_d.shape[-1]
    bf16 = jnp.bfloat16
    x2 = x.reshape(b * l, d)
    i = 0
    q, k, v, u4, gates = _in_proj(x2, norm_mix[i][None], w_in[i].astype(bf16), b_gate[i][None],
                                  d_attn, kv_cols, d_ssm)
    attn = _attention(q, k, v, attn_sinks[i], b, l)
    tables = _ssm_tables(ssm_a_re[i], ssm_a_im[i], ssm_b_re[i], ssm_b_im[i], ssm_c_re[i], ssm_c_im[i],
                         ssm_d[i], ssm_log_dt[i], SSM_SEG)
    y4 = _ssm(u4, tables, b, l)
    pad = LANES - N_EXPERTS - N_EXPERT_GROUPS
    w_router = jnp.concatenate([w_router_expert[i], w_router_group[i], jnp.zeros((d, pad), jnp.float32)], axis=1)
    w_r_hi = w_router.astype(bf16)
    w_router = jnp.concatenate([w_r_hi, (w_router - w_r_hi.astype(jnp.float32)).astype(bf16)], axis=1)
    b_router = jnp.concatenate([b_router_expert[i], b_router_group[i], jnp.zeros((pad,), jnp.float32)])[None]
    x1, h_fat, ids, rw = _merge(x2, attn, y4, gates, w_glu[i].astype(bf16), b_glu[i][None],
                                w_attn_branch[i].astype(bf16), w_ssm_branch[i].astype(bf16),
                                w_out[i].astype(bf16), norm_moe[i][None], w_router, b_router)
    out = _moe(x1, h_fat, ids, rw, w_expert_gate[i], w_expert_up[i], w_expert_down[i], norm_final[None])
    return out.reshape(b, l, d)
```

```python
import functools
import math

import jax
import jax.numpy as jnp
from jax import lax
from jax.experimental import pallas as pl
from jax.experimental.pallas import tpu as pltpu

EPS = 1e-6
HEAD_DIM = 64
N_HEADS = 8
N_KV_HEADS = 2
Q_PER_KV = N_HEADS // N_KV_HEADS
ATTN_BLOCK = 128
ATTN_QB = 2
SSM_GROUP = 16
SSM_STATE = 64
N_EXPERT_GROUPS = 4
EXPERTS_PER_GROUP = 8
N_EXPERTS = N_EXPERT_GROUPS * EXPERTS_PER_GROUP
TOP_K = 2

LANES = 128
SUBLANES = 8
SSM_CHUNK_GROUPS = LANES // SSM_GROUP
SSM_CHUNK_STATES = SSM_CHUNK_GROUPS * SSM_STATE

TM_PROJ = 512
TM_SSM = 512
SSM_SEG = TM_SSM // SUBLANES
TM_MERGE = 512
TM_RANK = 512
TM_MOVE = 256
TM_EXPERT = 256
ROW_SUB = 4
ROW_BUFS = 3
VMEM_LIMIT = 56 * 1024 * 1024


def _cparams(*sem):
    return pltpu.CompilerParams(dimension_semantics=sem, vmem_limit_bytes=VMEM_LIMIT)


def _pack_bf16_pairs(x):
    half = x.shape[1] // 2
    bits = lax.bitcast_convert_type(x, jnp.uint32)
    return (bits[:, :half] & jnp.uint32(0xFFFF0000)) | (bits[:, half:] >> 16)


def _unpack_bf16_pairs(p):
    hi = lax.bitcast_convert_type(p & jnp.uint32(0xFFFF0000), jnp.float32)
    lo = lax.bitcast_convert_type(p << 16, jnp.float32)
    return jnp.concatenate([hi, lo], axis=1)


def _load_rows(ref, first, n):
    return jnp.concatenate(
        [ref[pl.ds(first * ROW_SUB + c, n, stride=ROW_SUB), :] for c in range(ROW_SUB)], axis=1)


def _store_rows(ref, first, val, mask=None):
    n = val.shape[0]
    for c in range(ROW_SUB):
        idx = pl.ds(first * ROW_SUB + c, n, stride=ROW_SUB)
        v = val[:, c * LANES:(c + 1) * LANES]
        ref[idx, :] = v if mask is None else jnp.where(mask, v, ref[idx, :])


def _row_gather(src_hbm, idx_ref, idx0, dst_ref, dst0, n, sem, start):
    rs = ROW_SUB
    if start:
        for r in range(n):
            pltpu.make_async_copy(src_hbm.at[pl.ds(pl.multiple_of(idx_ref[idx0 + r], rs), rs)],
                                  dst_ref.at[pl.ds(pl.multiple_of((dst0 + r) * rs, rs), rs)],
                                  sem).start(priority=r % 2)
    else:
        pltpu.make_async_copy(src_hbm.at[pl.ds(0, n * rs)],
                              dst_ref.at[pl.ds(pl.multiple_of(dst0 * rs, rs), n * rs)], sem).wait()


def _proj_kernel(x_ref, g_ref, w_ref, bg_ref, q_ref, k_ref, v_ref, u_ref, gate_ref, *, cols):
    q_c, kv_c, d_ssm = cols
    xf = x_ref[...]
    inv = lax.rsqrt(jnp.mean(xf * xf, axis=-1, keepdims=True) + EPS)
    h = (xf * inv * g_ref[...]).astype(jnp.bfloat16)
    o = 0
    q_ref[...] = (jnp.dot(h, w_ref[:, o:o + q_c], preferred_element_type=jnp.float32)
                  * (1.0 / math.sqrt(HEAD_DIM))).astype(q_ref.dtype)
    o += q_c
    k_ref[...] = jnp.dot(h, w_ref[:, o:o + kv_c], preferred_element_type=jnp.float32).astype(k_ref.dtype)
    o += kv_c
    v_ref[...] = jnp.dot(h, w_ref[:, o:o + kv_c], preferred_element_type=jnp.float32).astype(v_ref.dtype)
    o += kv_c
    for s in range(0, d_ssm // LANES, 2):
        uu = jnp.dot(h, w_ref[:, o:o + 2 * LANES], preferred_element_type=jnp.float32)
        u_ref[s] = uu[:, :LANES]
        u_ref[s + 1] = uu[:, LANES:]
        o += 2 * LANES
    gl = jnp.dot(h, w_ref[:, o:], preferred_element_type=jnp.float32) + bg_ref[...]
    gate_ref[...] = jax.nn.sigmoid(gl).astype(gate_ref.dtype)


def _in_proj(x2, norm_w, w_in, b_gate, d_attn, kv_cols, d_ssm):
    t, d = x2.shape
    gate_cols = b_gate.shape[-1]
    n_slab = d_ssm // LANES
    tm = TM_PROJ
    kern = functools.partial(_proj_kernel, cols=(d_attn, kv_cols, d_ssm))
    return pl.pallas_call(
        kern,
        grid=(t // tm,),
        in_specs=[
            pl.BlockSpec((tm, d), lambda i: (i, 0)),
            pl.BlockSpec((1, d), lambda i: (0, 0)),
            pl.BlockSpec(w_in.shape, lambda i: (0, 0)),
            pl.BlockSpec((1, gate_cols), lambda i: (0, 0)),
        ],
        out_specs=[
            pl.BlockSpec((tm, d_attn), lambda i: (i, 0)),
            pl.BlockSpec((tm, kv_cols), lambda i: (i, 0)),
            pl.BlockSpec((tm, kv_cols), lambda i: (i, 0)),
            pl.BlockSpec((n_slab, tm, LANES), lambda i: (0, i, 0)),
            pl.BlockSpec((tm, gate_cols), lambda i: (i, 0)),
        ],
        out_shape=[
            jax.ShapeDtypeStruct((t, d_attn), jnp.bfloat16),
            jax.ShapeDtypeStruct((t, kv_cols), jnp.bfloat16),
            jax.ShapeDtypeStruct((t, kv_cols), jnp.bfloat16),
            jax.ShapeDtypeStruct((n_slab, t, LANES), jnp.float32),
            jax.ShapeDtypeStruct((t, gate_cols), jnp.bfloat16),
        ],
        compiler_params=_cparams("arbitrary"),
        name="in_proj",
    )(x2, norm_w, w_in, b_gate)


def _attn_kernel(sink_ref, q_ref, kp_ref, kc_ref, vp_ref, vc_ref, rep_ref, mask_ref, o_ref):
    i = pl.program_id(1)
    blk = ATTN_BLOCK
    hw = Q_PER_KV * HEAD_DIM
    rows = Q_PER_KV * blk
    head_of_row = lax.broadcasted_iota(jnp.int32, (rows, 1), 0) >> 7
    lane_head_q = lax.broadcasted_iota(jnp.int32, (blk, hw), 1) >> 6
    lane_head_v = lax.broadcasted_iota(jnp.int32, (2 * blk, hw), 1) >> 6
    neg = jnp.finfo(jnp.float32).min
    for qb in range(ATTN_QB):
        if qb == 0:
            k2 = jnp.concatenate([kp_ref[...], kc_ref[0:blk, :]], axis=0)
            v2 = jnp.concatenate([vp_ref[...], vc_ref[0:blk, :]], axis=0)
            valid = mask_ref[jnp.where(i == 0, 1, 0)] > 0.0
        else:
            k2 = kc_ref[(qb - 1) * blk:(qb + 1) * blk, :]
            v2 = vc_ref[(qb - 1) * blk:(qb + 1) * blk, :]
            valid = mask_ref[0] > 0.0
        for kh in range(N_KV_HEADS):
            rep = rep_ref[kh]
            k4 = jnp.dot(k2, rep, preferred_element_type=jnp.float32).astype(jnp.bfloat16)
            v4 = jnp.dot(v2, rep, preferred_element_type=jnp.float32).astype(jnp.bfloat16)
            qh = q_ref[qb * blk:(qb + 1) * blk, kh * hw:(kh + 1) * hw]
            qm = jnp.concatenate(
                [jnp.where(lane_head_q == g, qh, jnp.zeros_like(qh)) for g in range(Q_PER_KV)], axis=0)
            s = lax.dot_general(qm, k4, (((1,), (1,)), ((), ())), preferred_element_type=jnp.float32)
            s = jnp.where(valid, s, neg)
            sink = jnp.zeros((rows, 1), jnp.float32)
            for g in range(Q_PER_KV):
                sink = jnp.where(head_of_row == g, sink_ref[kh * Q_PER_KV + g], sink)
            m = jnp.maximum(jnp.max(s, axis=-1, keepdims=True), sink)
            p = jnp.exp(s - m)
            rinv = 1.0 / (jnp.sum(p, axis=-1, keepdims=True) + jnp.exp(sink - m))
            p = p.astype(jnp.bfloat16)
            p_cat = jnp.concatenate([p[g * blk:(g + 1) * blk, :] for g in range(Q_PER_KV)], axis=1)
            vm = jnp.concatenate(
                [jnp.where(lane_head_v == g, v4, jnp.zeros_like(v4)) for g in range(Q_PER_KV)], axis=0)
            o = jnp.dot(p_cat, vm, preferred_element_type=jnp.float32)
            scale = jnp.zeros((blk, hw), jnp.float32)
            for g in range(Q_PER_KV):
                scale = jnp.where(lane_head_q == g, rinv[g * blk:(g + 1) * blk, :], scale)
            o_ref[qb * blk:(qb + 1) * blk, kh * hw:(kh + 1) * hw] = (o * scale).astype(o_ref.dtype)


def _attention(q, k, v, sinks, b, l):
    d_attn = q.shape[-1]
    kv_cols = k.shape[-1]
    blk = ATTN_BLOCK
    tq = ATTN_QB * blk
    hw = Q_PER_KV * HEAD_DIM
    lane = jnp.arange(hw)[None, :]
    src = jnp.arange(kv_cols)[:, None]
    rep = jnp.stack([(src == kh * HEAD_DIM + (lane % HEAD_DIM)) for kh in range(N_KV_HEADS)]
                    ).astype(jnp.bfloat16)
    r = (jnp.arange(Q_PER_KV * blk) % blk)[:, None]
    c = jnp.arange(2 * blk)[None, :]
    band = (c > r) & (c <= r + blk)
    mask = jnp.stack([band, band & (c >= blk)]).astype(jnp.float32)
    q3 = q.reshape(b, l, d_attn)
    k3 = k.reshape(b, l, kv_cols)
    v3 = v.reshape(b, l, kv_cols)
    cur = lambda bi, i: (bi, i, 0)
    prev = lambda bi, i: (bi, jnp.maximum(ATTN_QB * i - 1, 0), 0)
    out = pl.pallas_call(
        _attn_kernel,
        grid=(b, l // tq),
        in_specs=[
            pl.BlockSpec(memory_space=pltpu.SMEM),
            pl.BlockSpec((None, tq, d_attn), cur),
            pl.BlockSpec((None, blk, kv_cols), prev),
            pl.BlockSpec((None, tq, kv_cols), cur),
            pl.BlockSpec((None, blk, kv_cols), prev),
            pl.BlockSpec((None, tq, kv_cols), cur),
            pl.BlockSpec(rep.shape, lambda bi, i: (0, 0, 0)),
            pl.BlockSpec(mask.shape, lambda bi, i: (0, 0, 0)),
        ],
        out_specs=pl.BlockSpec((None, tq, d_attn), cur),
        out_shape=jax.ShapeDtypeStruct((b, l, d_attn), jnp.bfloat16),
        compiler_params=_cparams("arbitrary", "arbitrary"),
        name="swa",
    )(sinks, q3, k3, k3, v3, v3, rep, mask)
    return out.reshape(b * l, d_attn)


def _ssm_kernel(u_ref, bmat_ref, cmat_ref, lam_ref, pw_ref, lamseg_ref, d_ref, y_ref,
                up_ref, bu_ref, yp_ref, carry_ref):
    i = pl.program_id(1)
    n_slab = u_ref.shape[0]
    tm = u_ref.shape[1]
    seg = tm // SUBLANES
    ns = SSM_CHUNK_STATES

    @pl.when(i == 0)
    def _():
        carry_ref[...] = jnp.zeros_like(carry_ref)

    sub = lax.broadcasted_iota(jnp.int32, (SUBLANES, ns), 0)
    for s in range(n_slab):
        u_slab = u_ref.at[s]
        for r in range(seg):
            up_ref[r * SUBLANES:(r + 1) * SUBLANES, :] = u_slab[pl.ds(r, SUBLANES, stride=seg), :]
        up = up_ref[...]
        bu_ref[...] = jnp.dot(up.astype(jnp.bfloat16), bmat_ref[s], preferred_element_type=jnp.float32)
        lr = jnp.broadcast_to(lam_ref[s, 0:1, :], (SUBLANES, ns))
        li = jnp.broadcast_to(lam_ref[s, 1:2, :], (SUBLANES, ns))

        def step(r, st):
            sr, si = st
            rows = pl.ds(pl.multiple_of(r * SUBLANES, SUBLANES), SUBLANES)
            nr = lr * sr - li * si + bu_ref[rows, 0:ns]
            ni = lr * si + li * sr + bu_ref[rows, ns:2 * ns]
            bu_ref[rows, 0:ns] = nr
            bu_ref[rows, ns:2 * ns] = ni
            return nr, ni

        zero = jnp.zeros((SUBLANES, ns), jnp.float32)
        er, ei = lax.fori_loop(0, seg, step, (zero, zero), unroll=4)

        ar = lamseg_ref[s, 0:1, :]
        ai = lamseg_ref[s, 1:2, :]
        cr = carry_ref[s, 0:1, :]
        ci = carry_ref[s, 1:2, :]
        car = jnp.zeros((SUBLANES, ns), jnp.float32)
        cai = jnp.zeros((SUBLANES, ns), jnp.float32)
        for j in range(SUBLANES):
            car = jnp.where(sub == j, jnp.broadcast_to(cr, (SUBLANES, ns)), car)
            cai = jnp.where(sub == j, jnp.broadcast_to(ci, (SUBLANES, ns)), cai)
            ejr = jnp.sum(jnp.where(sub == j, er, 0.0), axis=0, keepdims=True)
            eji = jnp.sum(jnp.where(sub == j, ei, 0.0), axis=0, keepdims=True)
            cr, ci = ar * cr - ai * ci + ejr, ar * ci + ai * cr + eji
        carry_ref[s, 0:1, :] = cr
        carry_ref[s, 1:2, :] = ci

        ctr = jnp.broadcast_to(car[None], (seg, SUBLANES, ns)).reshape(tm, ns)
        cti = jnp.broadcast_to(cai[None], (seg, SUBLANES, ns)).reshape(tm, ns)
        pr = pw_ref[s, 0]
        pi = pw_ref[s, 1]
        st_r = bu_ref[:, 0:ns] + pr * ctr - pi * cti
        st_i = bu_ref[:, ns:2 * ns] + pr * cti + pi * ctr
        st = jnp.concatenate([st_r, st_i], axis=1).astype(jnp.bfloat16)
        yp_ref[...] = jnp.dot(st, cmat_ref[s], preferred_element_type=jnp.float32) + d_ref[s] * up
        y_slab = y_ref.at[s]
        for r in range(seg):
            y_slab[pl.ds(r, SUBLANES, stride=seg), :] = yp_ref[r * SUBLANES:(r + 1) * SUBLANES, :]


def _ssm_tables(a_re, a_im, b_re, b_im, c_re, c_im, d_skip, log_dt, seg):
    f32 = jnp.float32
    g, p = a_re.shape
    c = b_re.shape[-1]
    ng = SSM_CHUNK_GROUPS
    n_slab = g // ng
    lam = lax.complex(a_re.astype(f32), a_im.astype(f32))
    dt = jnp.exp(log_dt.astype(f32))[:, None]
    lam_bar = jnp.exp(lam * dt)
    b_bar = ((lam_bar - 1.0) / lam)[:, :, None] * lax.complex(b_re.astype(f32), b_im.astype(f32))
    c_mat = lax.complex(c_re.astype(f32), c_im.astype(f32))
    eye = jnp.eye(ng, dtype=f32)

    def bdiag_b(m):
        m = m.reshape(n_slab, ng, p, c)
        return jnp.einsum('ab,kbpc->kacbp', eye, m).reshape(n_slab, ng * c, ng * p)

    def bdiag_c(m):
        m = m.reshape(n_slab, ng, c, p)
        return jnp.einsum('ab,kbcp->kbpac', eye, m).reshape(n_slab, ng * p, ng * c)

    bmat = jnp.concatenate([bdiag_b(jnp.real(b_bar)), bdiag_b(jnp.imag(b_bar))], axis=2).astype(jnp.bfloat16)
    cmat = jnp.concatenate([bdiag_c(jnp.real(c_mat)), -bdiag_c(jnp.imag(c_mat))], axis=1).astype(jnp.bfloat16)

    def slab_rows(z):
        z = z.reshape(n_slab, 1, ng * p)
        return jnp.concatenate([jnp.real(z), jnp.imag(z)], axis=1)

    lam_t = slab_rows(lam_bar)
    steps = jnp.arange(1, seg + 1, dtype=f32)
    pw = jnp.exp((lam * dt)[None] * steps[:, None, None])
    pw = pw.reshape(seg, n_slab, ng * p).transpose(1, 0, 2)
    pw = jnp.repeat(pw, SUBLANES, axis=1)
    pw_t = jnp.stack([jnp.real(pw), jnp.imag(pw)], axis=1)
    lamseg_t = slab_rows(jnp.exp(lam * dt * float(seg)))
    d_t = d_skip.astype(f32).reshape(n_slab, 1, ng * c)
    return bmat, cmat, lam_t, pw_t, lamseg_t, d_t


def _ssm(u4, tables, b, l):
    bmat, cmat, lam_t, pw_t, lamseg_t, d_t = tables
    n_slab, t, _ = u4.shape
    tm = TM_SSM
    nt = l // tm
    ns = SSM_CHUNK_STATES
    const = lambda nd: (lambda bi, i: (0,) * nd)
    return pl.pallas_call(
        _ssm_kernel,
        grid=(b, nt),
        in_specs=[
            pl.BlockSpec((n_slab, tm, LANES), lambda bi, i: (0, bi * nt + i, 0)),
            pl.BlockSpec(bmat.shape, const(3)),
            pl.BlockSpec(cmat.shape, const(3)),
            pl.BlockSpec(lam_t.shape, const(3)),
            pl.BlockSpec(pw_t.shape, const(4)),
            pl.BlockSpec(lamseg_t.shape, const(3)),
            pl.BlockSpec(d_t.shape, const(3)),
        ],
        out_specs=pl.BlockSpec((n_slab, tm, LANES), lambda bi, i: (0, bi * nt + i, 0)),
        out_shape=jax.ShapeDtypeStruct((n_slab, t, LANES), jnp.float32),
        scratch_shapes=[
            pltpu.VMEM((tm, LANES), jnp.float32),
            pltpu.VMEM((tm, 2 * ns), jnp.float32),
            pltpu.VMEM((tm, LANES), jnp.float32),
            pltpu.VMEM((n_slab, 2, ns), jnp.float32),
        ],
        compiler_params=_cparams("arbitrary", "arbitrary"),
        name="s5_scan",
    )(u4, bmat, cmat, lam_t, pw_t, lamseg_t, d_t)


def _merge_kernel(x_ref, attn_ref, y_ref, gate_ref, wglu_ref, bglu_ref, wa_ref, ws_ref, wo_ref,
                  nm_ref, wr_ref, br_ref, x1_ref, h2_ref, ids_ref, rw_ref):
    d = x_ref.shape[1]
    tm = x_ref.shape[0]
    y = jnp.concatenate([y_ref[s] for s in range(y_ref.shape[0])], axis=1)
    z = jax.nn.gelu(y)
    zg = jnp.dot(z.astype(jnp.bfloat16), wglu_ref[...], preferred_element_type=jnp.float32) + bglu_ref[...]
    z = z * jax.nn.sigmoid(zg)
    a = jnp.dot(attn_ref[...], wa_ref[...], preferred_element_type=jnp.float32)
    sb = jnp.dot(z.astype(jnp.bfloat16), ws_ref[...], preferred_element_type=jnp.float32)
    merged = gate_ref[:, 0:d].astype(jnp.float32) * a + gate_ref[:, d:2 * d].astype(jnp.float32) * sb
    x1 = x_ref[...] + jnp.dot(merged.astype(jnp.bfloat16), wo_ref[...], preferred_element_type=jnp.float32)
    x1_ref[...] = x1
    inv = lax.rsqrt(jnp.mean(x1 * x1, axis=-1, keepdims=True) + EPS)
    h2 = x1 * inv * nm_ref[...]
    h_hi = h2.astype(jnp.bfloat16)
    h_pk = _pack_bf16_pairs(h_hi.astype(jnp.float32))
    _store_rows(h2_ref, 0, h_pk)

    h_lo = (h2 - h_hi.astype(jnp.float32)).astype(jnp.bfloat16)
    hh = jnp.dot(h_hi, wr_ref[...], preferred_element_type=jnp.float32)
    lh = jnp.dot(h_lo, wr_ref[:, 0:LANES], preferred_element_type=jnp.float32)
    logits = hh[:, 0:LANES] + hh[:, LANES:2 * LANES] + lh + br_ref[...]
    lane = lax.broadcasted_iota(jnp.int32, (tm, LANES), 1)
    ninf = -jnp.inf
    gl = jnp.where((lane >= N_EXPERTS) & (lane < N_EXPERTS + N_EXPERT_GROUPS), logits, ninf)
    gmax = jnp.max(gl, axis=-1, keepdims=True)
    gidx = jnp.min(jnp.where(gl == gmax, lane - N_EXPERTS, LANES), axis=-1, keepdims=True)
    group_p = 1.0 / jnp.sum(jnp.exp(gl - gmax), axis=-1, keepdims=True)
    el = jnp.where((lane < N_EXPERTS) & ((lane >> 3) == gidx), logits, ninf)
    m1 = jnp.max(el, axis=-1, keepdims=True)
    i1 = jnp.min(jnp.where(el == m1, lane, LANES), axis=-1, keepdims=True)
    el2 = jnp.where(lane == i1, ninf, el)
    m2 = jnp.max(el2, axis=-1, keepdims=True)
    i2 = jnp.min(jnp.where(el2 == m2, lane, LANES), axis=-1, keepdims=True)
    e2 = jnp.exp(m2 - m1)
    w1 = group_p / (1.0 + e2)
    w2 = group_p * e2 / (1.0 + e2)
    ids_ref[...] = jnp.where(lane == 0, i1, jnp.where(lane == 1, i2, 0))
    rw_ref[...] = jnp.where(lane == 0, w1, jnp.where(lane == 1, w2, 0.0))


def _merge(x2, attn, y4, gates, w_glu, b_glu, w_a, w_s, w_o, norm_moe, w_router, b_router):
    t, d = x2.shape
    tm = TM_MERGE
    n_slab = y4.shape[0]
    full = lambda a: pl.BlockSpec(a.shape, lambda i: (0,) * a.ndim)
    row = lambda c: pl.BlockSpec((tm, c), lambda i: (i, 0))
    return pl.pallas_call(
        _merge_kernel,
        grid=(t // tm,),
        in_specs=[
            row(d), row(attn.shape[1]),
            pl.BlockSpec((n_slab, tm, LANES), lambda i: (0, i, 0)),
            row(gates.shape[1]),
            full(w_glu), full(b_glu), full(w_a), full(w_s), full(w_o), full(norm_moe),
            full(w_router), full(b_router),
        ],
        out_specs=[row(d), pl.BlockSpec((tm * ROW_SUB, LANES), lambda i: (i, 0)), row(LANES), row(LANES)],
        out_shape=[
            jax.ShapeDtypeStruct((t, d), jnp.float32),
            jax.ShapeDtypeStruct((t * ROW_SUB, LANES), jnp.uint32),
            jax.ShapeDtypeStruct((t, LANES), jnp.int32),
            jax.ShapeDtypeStruct((t, LANES), jnp.float32),
        ],
        compiler_params=_cparams("arbitrary"),
        name="merge_router",
    )(x2, attn, y4, gates, w_glu, b_glu, w_a, w_s, w_o, norm_moe, w_router, b_router)


def _rank_kernel(ids_ref, rank_ref, cnt_ref, carry_ref):
    i = pl.program_id(0)
    tm = ids_ref.shape[0]

    @pl.when(i == 0)
    def _():
        carry_ref[...] = jnp.zeros_like(carry_ref)

    lane = lax.broadcasted_iota(jnp.int32, (tm, LANES), 1)
    ids = ids_ref[...]
    e0 = ids[:, 0:1]
    e1 = ids[:, 1:2]
    oh0 = lane == e0
    oh1 = lane == e1
    oh = oh0.astype(jnp.float32) + oh1.astype(jnp.float32)
    r_i = lax.broadcasted_iota(jnp.int32, (tm, tm), 0)
    c_i = lax.broadcasted_iota(jnp.int32, (tm, tm), 1)
    tri = (c_i < r_i).astype(jnp.bfloat16)
    cum = jnp.dot(tri, oh.astype(jnp.bfloat16), preferred_element_type=jnp.float32) + carry_ref[...]
    r0 = jnp.sum(jnp.where(oh0, cum, 0.0), axis=-1, keepdims=True)
    r1 = jnp.sum(jnp.where(oh1, cum, 0.0), axis=-1, keepdims=True)
    rank_ref[...] = jnp.where(lane == 0, r0, jnp.where(lane == 1, r1, 0.0)).astype(jnp.int32)
    carry_ref[...] = carry_ref[...] + jnp.sum(oh, axis=0, keepdims=True)
    cnt_ref[...] = carry_ref[...].astype(jnp.int32)


def _ranks(ids):
    t = ids.shape[0]
    tm = TM_RANK
    return pl.pallas_call(
        _rank_kernel,
        grid=(t // tm,),
        in_specs=[pl.BlockSpec((tm, LANES), lambda i: (i, 0))],
        out_specs=[pl.BlockSpec((tm, LANES), lambda i: (i, 0)),
                   pl.BlockSpec((1, LANES), lambda i: (0, 0))],
        out_shape=[jax.ShapeDtypeStruct((t, LANES), jnp.int32),
                   jax.ShapeDtypeStruct((1, LANES), jnp.int32)],
        scratch_shapes=[pltpu.VMEM((1, LANES), jnp.float32)],
        compiler_params=_cparams("arbitrary"),
        name="route_rank",
    )(ids)


def _inverse_kernel(pos_ref, src_ref):
    n_slots = pos_ref.shape[0]
    t = n_slots // TOP_K

    def body(tok, c):
        for k in range(TOP_K):
            src_ref[pos_ref[k * t + tok]] = tok * ROW_SUB
        return c

    lax.fori_loop(0, t, body, 0, unroll=8)


def _inverse(pos1d):
    n_slots = pos1d.shape[0]
    smem = pl.BlockSpec(memory_space=pltpu.SMEM)
    return pl.pallas_call(
        _inverse_kernel,
        in_specs=[smem],
        out_specs=smem,
        out_shape=jax.ShapeDtypeStruct((n_slots,), jnp.int32),
        name="route_inverse",
    )(pos1d)


def _expert_kernel(ta_ref, tn_ref, tnew_ref, sexp_ref, slo_ref, shi_ref, meta_ref, src_ref,
                   h_ref, wg_hbm, wu_hbm, wd_hbm, y_ref,
                   xbuf_ref, wgs_ref, wus_ref, wds_ref, wgb_ref, wub_ref, wdb_ref, gsem, wsem):
    j = pl.program_id(0)
    n_tiles = pl.num_programs(0)
    last = n_tiles - 1
    tm = TM_EXPERT
    rs = ROW_SUB
    nb = ROW_BUFS
    n_used = meta_ref[0]
    slot = j % nb
    a = ta_ref[j]

    def gather(tile, sl, start):
        _row_gather(h_ref, src_ref, tile * tm, xbuf_ref, sl * tm, tm, gsem.at[sl], start)

    def weights(m, start):
        sl = m % 2
        e = sexp_ref[m]
        for hbm, stage in ((wg_hbm, wgs_ref), (wu_hbm, wus_ref), (wd_hbm, wds_ref)):
            cp = pltpu.make_async_copy(hbm.at[e], stage.at[sl], wsem.at[sl])
            cp.start() if start else cp.wait()

    def new_expert(m):
        weights(m, False)
        sl = m % 2
        wgb_ref[...] = wgs_ref[sl].astype(jnp.bfloat16)
        wub_ref[...] = wus_ref[sl].astype(jnp.bfloat16)
        wdb_ref[...] = wds_ref[sl].astype(jnp.bfloat16)

        @pl.when(m + 1 < n_used)
        def _():
            weights(m + 1, True)

    def expert_out(xp):
        x = _unpack_bf16_pairs(xp).astype(jnp.bfloat16)
        hg = jnp.dot(x, wgb_ref[...], preferred_element_type=jnp.float32)
        hu = jnp.dot(x, wub_ref[...], preferred_element_type=jnp.float32)
        act = (jax.nn.silu(hg) * hu).astype(jnp.bfloat16)
        y = jnp.dot(act, wdb_ref[...], preferred_element_type=jnp.float32)
        return _pack_bf16_pairs(y.astype(jnp.bfloat16).astype(jnp.float32))

    @pl.when(j == 0)
    def _():
        weights(0, True)
        gather(0, 0, True)
        gather(1, 1, True)

    @pl.when(tnew_ref[j] == 1)
    def _():
        new_expert(a)

    gather(j, slot, False)
    xp = _load_rows(xbuf_ref, slot * tm, tm)
    gather(jnp.minimum(j + 2, last), (j + 2) % nb, True)
    _store_rows(y_ref, 0, expert_out(xp))

    def extra(i, carry):
        m = a + i
        new_expert(m)
        yp = expert_out(_load_rows(xbuf_ref, slot * tm, tm))
        row = lax.broadcasted_iota(jnp.int32, (tm, LANES), 0) + j * tm
        mine = (row >= slo_ref[m]) & (row < shi_ref[m])
        _store_rows(y_ref, 0, yp, mine)
        return carry

    lax.fori_loop(1, tn_ref[j], extra, 0)

    @pl.when(j == last)
    def _():
        gather(j, (j + 1) % nb, False)
        gather(j, (j + 2) % nb, False)


def _experts(meta, src_tok, h_fat, w_gate, w_up, w_down):
    n_slots = src_tok.shape[0]
    ne, d, dff = w_gate.shape
    tm = TM_EXPERT
    rs = ROW_SUB
    assert n_slots // tm >= ROW_BUFS
    any_spec = pl.BlockSpec(memory_space=pl.ANY)
    grid_spec = pltpu.PrefetchScalarGridSpec(
        num_scalar_prefetch=8,
        grid=(n_slots // tm,),
        in_specs=[any_spec, any_spec, any_spec, any_spec],
        out_specs=pl.BlockSpec((tm * rs, LANES), lambda j, *_: (j, 0)),
        scratch_shapes=[
            pltpu.VMEM((ROW_BUFS * tm * rs, LANES), jnp.uint32),
            pltpu.VMEM((2, d, dff), jnp.float32),
            pltpu.VMEM((2, d, dff), jnp.float32),
            pltpu.VMEM((2, dff, d), jnp.float32),
            pltpu.VMEM((d, dff), jnp.bfloat16),
            pltpu.VMEM((d, dff), jnp.bfloat16),
            pltpu.VMEM((dff, d), jnp.bfloat16),
            pltpu.SemaphoreType.DMA((ROW_BUFS,)),
            pltpu.SemaphoreType.DMA((2,)),
        ],
    )
    return pl.pallas_call(
        _expert_kernel,
        grid_spec=grid_spec,
        out_shape=jax.ShapeDtypeStruct((n_slots * rs, LANES), jnp.uint32),
        compiler_params=_cparams("arbitrary"),
        name="moe_experts",
    )(*meta, src_tok, h_fat, w_gate, w_up, w_down)


def _combine_kernel(pos_ref, x1_ref, rw_ref, nf_ref, ys_ref, o_ref, buf_ref, sem):
    i = pl.program_id(0)
    n = pl.num_programs(0)
    last = n - 1
    tm = x1_ref.shape[0]
    t = n * tm
    nb = ROW_BUFS
    slot = i % nb

    def gather(tile, sl, start):
        for k in range(TOP_K):
            _row_gather(ys_ref, pos_ref, k * t + tile * tm, buf_ref, (sl * TOP_K + k) * tm, tm, sem.at[sl], start)

    @pl.when(i == 0)
    def _():
        gather(0, 0, True)
        gather(1, 1, True)

    gather(i, slot, False)
    y0 = _unpack_bf16_pairs(_load_rows(buf_ref, (slot * TOP_K) * tm, tm))
    y1 = _unpack_bf16_pairs(_load_rows(buf_ref, (slot * TOP_K + 1) * tm, tm))
    gather(jnp.minimum(i + 2, last), (i + 2) % nb, True)
    rw = rw_ref[...]
    x = x1_ref[...] + rw[:, 0:1] * y0 + rw[:, 1:2] * y1
    inv = lax.rsqrt(jnp.mean(x * x, axis=-1, keepdims=True) + EPS)
    o_ref[...] = x * inv * nf_ref[...]

    @pl.when(i == last)
    def _():
        gather(i, (i + 1) % nb, False)
        gather(i, (i + 2) % nb, False)


def _combine(pos_rows, x1, rw, norm_final, ys):
    t, d = x1.shape
    tm = TM_MOVE
    assert t // tm >= ROW_BUFS
    return pl.pallas_call(
        _combine_kernel,
        grid=(t // tm,),
        in_specs=[
            pl.BlockSpec(memory_space=pltpu.SMEM),
            pl.BlockSpec((tm, d), lambda i: (i, 0)),
            pl.BlockSpec((tm, LANES), lambda i: (i, 0)),
            pl.BlockSpec((1, d), lambda i: (0, 0)),
            pl.BlockSpec(memory_space=pl.ANY),
        ],
        out_specs=pl.BlockSpec((tm, d), lambda i: (i, 0)),
        out_shape=jax.ShapeDtypeStruct((t, d), jnp.float32),
        scratch_shapes=[pltpu.VMEM((ROW_BUFS * TOP_K * tm * ROW_SUB, LANES), jnp.uint32),
                        pltpu.SemaphoreType.DMA((ROW_BUFS,))],
        compiler_params=_cparams("arbitrary"),
        name="moe_combine",
    )(pos_rows, x1, rw, norm_final, ys)


def _expert_meta(counts, n_slots):
    tm = TM_EXPERT
    n_tiles = n_slots // tm
    i32 = jnp.int32
    ends = jnp.cumsum(counts).astype(i32)
    starts = ends - counts
    used = counts > 0
    n_used = jnp.sum(used).astype(i32)
    seq_of = jnp.cumsum(used).astype(i32) - 1
    m_idx = jnp.arange(N_EXPERTS, dtype=i32)
    pick = used[None, :] & (seq_of[None, :] == m_idx[:, None])
    s_exp = jnp.sum(jnp.where(pick, m_idx[None, :], 0), axis=1).astype(i32)
    s_lo = jnp.sum(jnp.where(pick, starts[None, :], 0), axis=1).astype(i32)
    s_hi = jnp.sum(jnp.where(pick, ends[None, :], 0), axis=1).astype(i32)
    row0 = jnp.arange(n_tiles, dtype=i32)[:, None] * tm
    t_a = jnp.sum(used[None, :] & (ends[None, :] <= row0), axis=1).astype(i32)
    t_b = jnp.sum(used[None, :] & (starts[None, :] < row0 + tm), axis=1).astype(i32) - 1
    t_new = jnp.any(used[None, :] & (starts[None, :] == row0), axis=1).astype(i32)
    return t_a, (t_b - t_a + 1).astype(i32), t_new, s_exp, s_lo, s_hi, n_used.reshape(1)


def _moe(x1, h_fat, ids, rw, w_gate, w_up, w_down, norm_final):
    t, d = x1.shape
    rank, cnt = _ranks(ids)
    counts = cnt[0, :N_EXPERTS]
    offs = (jnp.cumsum(counts) - counts).astype(jnp.int32)
    ids2 = ids[:, :TOP_K].T
    onehot = ids2[:, :, None] == jnp.arange(N_EXPERTS, dtype=jnp.int32)
    pos1d = (jnp.sum(jnp.where(onehot, offs, 0), axis=-1) + rank[:, :TOP_K].T).reshape(-1)
    src_tok = _inverse(pos1d)
    meta = _expert_meta(counts, t * TOP_K)
    ys = _experts(meta, src_tok, h_fat, w_gate, w_up, w_down)
    return _combine(pos1d * ROW_SUB, x1, rw, norm_final, ys)


def kernel(x, norm_mix, w_in, b_gate, attn_sinks, ssm_a_re, ssm_a_im, ssm_b_re, ssm_b_im, ssm_c_re, ssm_c_im, ssm_d, ssm_log_dt, w_glu, b_glu, w_attn_branch, w_ssm_branch, w_out, norm_moe, w_router_group, b_router_group, w_router_expert, b_router_expert, w_expert_gate, w_expert_up, w_expert_down, norm_final):
    b, l, d = x.shape
    depth = w_in.shape[0]
    assert depth == 1, "the final norm is fused into the last layer's combine kernel"
    d_attn = N_HEADS * HEAD_DIM
    kv_cols = N_KV_HEADS * HEAD_DIM
    d_ssm = ssm_d.shape[-1]
    bf16 = jnp.bfloat16
    x2 = x.reshape(b * l, d)
    i = 0
    q, k, v, u4, gates = _in_proj(x2, norm_mix[i][None], w_in[i].astype(bf16), b_gate[i][None],
                                  d_attn, kv_cols, d_ssm)
    attn = _attention(q, k, v, attn_sinks[i], b, l)
    tables = _ssm_tables(ssm_a_re[i], ssm_a_im[i], ssm_b_re[i], ssm_b_im[i], ssm_c_re[i], ssm_c_im[i],
                         ssm_d[i], ssm_log_dt[i], SSM_SEG)
    y4 = _ssm(u4, tables, b, l)
    pad = LANES - N_EXPERTS - N_EXPERT_GROUPS
    w_router = jnp.concatenate([w_router_expert[i], w_router_group[i], jnp.zeros((d, pad), jnp.float32)], axis=1)
    w_r_hi = w_router.astype(bf16)
    w_router = jnp.concatenate([w_r_hi, (w_router - w_r_hi.astype(jnp.float32)).astype(bf16)], axis=1)
    b_router = jnp.concatenate([b_router_expert[i], b_router_group[i], jnp.zeros((pad,), jnp.float32)])[None]
    x1, h_fat, ids, rw = _merge(x2, attn, y4, gates, w_glu[i].astype(bf16), b_glu[i][None],
                                w_attn_branch[i].astype(bf16), w_ssm_branch[i].astype(bf16),
                                w_out[i].astype(bf16), norm_moe[i][None], w_router, b_router)
    out = _moe(x1, h_fat, ids, rw, w_expert_gate[i], w_expert_up[i], w_expert_down[i], norm_final[None])
    return out.reshape(b, l, d)
```

```python
import functools
import math

import jax
import jax.numpy as jnp
from jax import lax
from jax.experimental import pallas as pl
from jax.experimental.pallas import tpu as pltpu

EPS = 1e-6
HEAD_DIM = 64
N_HEADS = 8
N_KV_HEADS = 2
Q_PER_KV = N_HEADS // N_KV_HEADS
ATTN_BLOCK = 128
ATTN_QB = 2
SSM_GROUP = 16
SSM_STATE = 64
N_EXPERT_GROUPS = 4
EXPERTS_PER_GROUP = 8
N_EXPERTS = N_EXPERT_GROUPS * EXPERTS_PER_GROUP
TOP_K = 2

LANES = 128
SUBLANES = 8
SSM_CHUNK_GROUPS = LANES // SSM_GROUP
SSM_CHUNK_STATES = SSM_CHUNK_GROUPS * SSM_STATE

TM_PROJ = 512
TM_SSM = 512
SSM_SEG = TM_SSM // SUBLANES
TM_MERGE = 512
TM_RANK = 512
TM_MOVE = 256
TM_EXPERT = 256
ROW_SUB = 4
ROW_BUFS = 3
VMEM_LIMIT = 56 * 1024 * 1024


def _cparams(*sem):
    return pltpu.CompilerParams(dimension_semantics=sem, vmem_limit_bytes=VMEM_LIMIT)


def _pack_bf16_pairs(x):
    half = x.shape[1] // 2
    bits = lax.bitcast_convert_type(x, jnp.uint32)
    return (bits[:, :half] & jnp.uint32(0xFFFF0000)) | (bits[:, half:] >> 16)


def _unpack_bf16_pairs(p):
    hi = lax.bitcast_convert_type(p & jnp.uint32(0xFFFF0000), jnp.float32)
    lo = lax.bitcast_convert_type(p << 16, jnp.float32)
    return jnp.concatenate([hi, lo], axis=1)


def _load_rows(ref, first, n):
    return jnp.concatenate(
        [ref[pl.ds(first * ROW_SUB + c, n, stride=ROW_SUB), :] for c in range(ROW_SUB)], axis=1)


def _store_rows(ref, first, val, mask=None):
    n = val.shape[0]
    for c in range(ROW_SUB):
        idx = pl.ds(first * ROW_SUB + c, n, stride=ROW_SUB)
        v = val[:, c * LANES:(c + 1) * LANES]
        ref[idx, :] = v if mask is None else jnp.where(mask, v, ref[idx, :])


def _row_gather(src_hbm, idx_ref, idx0, dst_ref, dst0, n, sem, start):
    rs = ROW_SUB
    if start:
        for r in range(n):
            pltpu.make_async_copy(src_hbm.at[pl.ds(pl.multiple_of(idx_ref[idx0 + r], rs), rs)],
                                  dst_ref.at[pl.ds(pl.multiple_of((dst0 + r) * rs, rs), rs)],
                                  sem).start(priority=r % 2)
    else:
        pltpu.make_async_copy(src_hbm.at[pl.ds(0, n * rs)],
                              dst_ref.at[pl.ds(pl.multiple_of(dst0 * rs, rs), n * rs)], sem).wait()


def _proj_kernel(x_ref, g_ref, w_ref, bg_ref, q_ref, k_ref, v_ref, u_ref, gate_ref, *, cols):
    q_c, kv_c, d_ssm = cols
    xf = x_ref[...]
    inv = lax.rsqrt(jnp.mean(xf * xf, axis=-1, keepdims=True) + EPS)
    h = (xf * inv * g_ref[...]).astype(jnp.bfloat16)
    o = 0
    q_ref[...] = (jnp.dot(h, w_ref[:, o:o + q_c], preferred_element_type=jnp.float32)
                  * (1.0 / math.sqrt(HEAD_DIM))).astype(q_ref.dtype)
    o += q_c
    k_ref[...] = jnp.dot(h, w_ref[:, o:o + kv_c], preferred_element_type=jnp.float32).astype(k_ref.dtype)
    o += kv_c
    v_ref[...] = jnp.dot(h, w_ref[:, o:o + kv_c], preferred_element_type=jnp.float32).astype(v_ref.dtype)
    o += kv_c
    seg = x_ref.shape[0] // SUBLANES
    for s in range(0, d_ssm // LANES, 2):
        uu = jnp.dot(h, w_ref[:, o:o + 2 * LANES], preferred_element_type=jnp.float32)
        for half in range(2):
            for j in range(SUBLANES):
                u_ref[s + half, pl.ds(j, seg, stride=SUBLANES), :] = (
                    uu[j * seg:(j + 1) * seg, half * LANES:(half + 1) * LANES])
        o += 2 * LANES
    gl = jnp.dot(h, w_ref[:, o:], preferred_element_type=jnp.float32) + bg_ref[...]
    gate_ref[...] = jax.nn.sigmoid(gl).astype(gate_ref.dtype)


def _in_proj(x2, norm_w, w_in, b_gate, d_attn, kv_cols, d_ssm):
    t, d = x2.shape
    gate_cols = b_gate.shape[-1]
    n_slab = d_ssm // LANES
    tm = TM_PROJ
    kern = functools.partial(_proj_kernel, cols=(d_attn, kv_cols, d_ssm))
    return pl.pallas_call(
        kern,
        grid=(t // tm,),
        in_specs=[
            pl.BlockSpec((tm, d), lambda i: (i, 0)),
            pl.BlockSpec((1, d), lambda i: (0, 0)),
            pl.BlockSpec(w_in.shape, lambda i: (0, 0)),
            pl.BlockSpec((1, gate_cols), lambda i: (0, 0)),
        ],
        out_specs=[
            pl.BlockSpec((tm, d_attn), lambda i: (i, 0)),
            pl.BlockSpec((tm, kv_cols), lambda i: (i, 0)),
            pl.BlockSpec((tm, kv_cols), lambda i: (i, 0)),
            pl.BlockSpec((n_slab, tm, LANES), lambda i: (0, i, 0)),
            pl.BlockSpec((tm, gate_cols), lambda i: (i, 0)),
        ],
        out_shape=[
            jax.ShapeDtypeStruct((t, d_attn), jnp.bfloat16),
            jax.ShapeDtypeStruct((t, kv_cols), jnp.bfloat16),
            jax.ShapeDtypeStruct((t, kv_cols), jnp.bfloat16),
            jax.ShapeDtypeStruct((n_slab, t, LANES), jnp.float32),
            jax.ShapeDtypeStruct((t, gate_cols), jnp.bfloat16),
        ],
        compiler_params=_cparams("arbitrary"),
        name="in_proj",
    )(x2, norm_w, w_in, b_gate)


def _attn_kernel(sink_ref, q_ref, kp_ref, kc_ref, vp_ref, vc_ref, rep_ref, mask_ref, o_ref):
    i = pl.program_id(1)
    blk = ATTN_BLOCK
    hw = Q_PER_KV * HEAD_DIM
    rows = Q_PER_KV * blk
    head_of_row = lax.broadcasted_iota(jnp.int32, (rows, 1), 0) >> 7
    lane_head_q = lax.broadcasted_iota(jnp.int32, (blk, hw), 1) >> 6
    lane_head_v = lax.broadcasted_iota(jnp.int32, (2 * blk, hw), 1) >> 6
    sinks = []
    for kh in range(N_KV_HEADS):
        sink = jnp.zeros((rows, 1), jnp.float32)
        for g in range(Q_PER_KV):
            sink = jnp.where(head_of_row == g, sink_ref[kh * Q_PER_KV + g], sink)
        sinks.append(sink)
    for qb in range(ATTN_QB):
        if qb == 0:
            k2 = jnp.concatenate([kp_ref[...], kc_ref[0:blk, :]], axis=0)
            v2 = jnp.concatenate([vp_ref[...], vc_ref[0:blk, :]], axis=0)
            bias = mask_ref[jnp.where(i == 0, 1, 0)]
        else:
            k2 = kc_ref[(qb - 1) * blk:(qb + 1) * blk, :]
            v2 = vc_ref[(qb - 1) * blk:(qb + 1) * blk, :]
            bias = mask_ref[0]
        for kh in range(N_KV_HEADS):
            rep = rep_ref[kh]
            k4 = jnp.dot(k2, rep, preferred_element_type=jnp.float32).astype(jnp.bfloat16)
            v4 = jnp.dot(v2, rep, preferred_element_type=jnp.float32).astype(jnp.bfloat16)
            qh = q_ref[qb * blk:(qb + 1) * blk, kh * hw:(kh + 1) * hw]
            qm = jnp.concatenate(
                [jnp.where(lane_head_q == g, qh, jnp.zeros_like(qh)) for g in range(Q_PER_KV)], axis=0)
            s = lax.dot_general(qm, k4, (((1,), (1,)), ((), ())), preferred_element_type=jnp.float32)
            s = s + bias
            sink = sinks[kh]
            m = jnp.maximum(jnp.max(s, axis=-1, keepdims=True), sink)
            p = jnp.exp(s - m)
            rinv = 1.0 / (jnp.sum(p, axis=-1, keepdims=True) + jnp.exp(sink - m))
            p = p.astype(jnp.bfloat16)
            p_cat = jnp.concatenate([p[g * blk:(g + 1) * blk, :] for g in range(Q_PER_KV)], axis=1)
            vm = jnp.concatenate(
                [jnp.where(lane_head_v == g, v4, jnp.zeros_like(v4)) for g in range(Q_PER_KV)], axis=0)
            o = jnp.dot(p_cat, vm, preferred_element_type=jnp.float32)
            scale = jnp.zeros((blk, hw), jnp.float32)
            for g in range(Q_PER_KV):
                scale = jnp.where(lane_head_q == g, rinv[g * blk:(g + 1) * blk, :], scale)
            o_ref[qb * blk:(qb + 1) * blk, kh * hw:(kh + 1) * hw] = (o * scale).astype(o_ref.dtype)


def _attention(q, k, v, sinks, b, l):
    d_attn = q.shape[-1]
    kv_cols = k.shape[-1]
    blk = ATTN_BLOCK
    tq = ATTN_QB * blk
    hw = Q_PER_KV * HEAD_DIM
    lane = jnp.arange(hw)[None, :]
    src = jnp.arange(kv_cols)[:, None]
    rep = jnp.stack([(src == kh * HEAD_DIM + (lane % HEAD_DIM)) for kh in range(N_KV_HEADS)]
                    ).astype(jnp.bfloat16)
    r = (jnp.arange(Q_PER_KV * blk) % blk)[:, None]
    c = jnp.arange(2 * blk)[None, :]
    band = (c > r) & (c <= r + blk)
    mask = jnp.where(jnp.stack([band, band & (c >= blk)]), 0.0, jnp.finfo(jnp.float32).min).astype(jnp.float32)
    q3 = q.reshape(b, l, d_attn)
    k3 = k.reshape(b, l, kv_cols)
    v3 = v.reshape(b, l, kv_cols)
    cur = lambda bi, i: (bi, i, 0)
    prev = lambda bi, i: (bi, jnp.maximum(ATTN_QB * i - 1, 0), 0)
    out = pl.pallas_call(
        _attn_kernel,
        grid=(b, l // tq),
        in_specs=[
            pl.BlockSpec(memory_space=pltpu.SMEM),
            pl.BlockSpec((None, tq, d_attn), cur),
            pl.BlockSpec((None, blk, kv_cols), prev),
            pl.BlockSpec((None, tq, kv_cols), cur),
            pl.BlockSpec((None, blk, kv_cols), prev),
            pl.BlockSpec((None, tq, kv_cols), cur),
            pl.BlockSpec(rep.shape, lambda bi, i: (0, 0, 0)),
            pl.BlockSpec(mask.shape, lambda bi, i: (0, 0, 0)),
        ],
        out_specs=pl.BlockSpec((None, tq, d_attn), cur),
        out_shape=jax.ShapeDtypeStruct((b, l, d_attn), jnp.bfloat16),
        compiler_params=_cparams("arbitrary", "arbitrary"),
        name="swa",
    )(sinks, q3, k3, k3, v3, v3, rep, mask)
    return out.reshape(b * l, d_attn)


def _ssm_kernel(u_ref, bmat_ref, cmat_ref, lam_ref, pw_ref, lamseg_ref, d_ref, y_ref, bu_ref, carry_ref):
    i = pl.program_id(1)
    n_slab = u_ref.shape[0]
    tm = u_ref.shape[1]
    seg = tm // SUBLANES
    ns = SSM_CHUNK_STATES
    npair = 2

    @pl.when(i == 0)
    def _():
        carry_ref[...] = jnp.zeros_like(carry_ref)

    sub = lax.broadcasted_iota(jnp.int32, (SUBLANES, ns), 0)
    for s0 in range(0, n_slab, npair):
        lam = []
        for q in range(npair):
            s = s0 + q
            bu_ref[q] = jnp.dot(u_ref[s].astype(jnp.bfloat16), bmat_ref[s], preferred_element_type=jnp.float32)
            lam.append((jnp.broadcast_to(lam_ref[s, 0:1, :], (SUBLANES, ns)),
                        jnp.broadcast_to(lam_ref[s, 1:2, :], (SUBLANES, ns))))

        def step(r, st):
            rows = pl.ds(pl.multiple_of(r * SUBLANES, SUBLANES), SUBLANES)
            out = []
            for q in range(npair):
                lr, li = lam[q]
                sr, si = st[q]
                nr = lr * sr - li * si + bu_ref[q, rows, 0:ns]
                ni = lr * si + li * sr + bu_ref[q, rows, ns:2 * ns]
                bu_ref[q, rows, 0:ns] = nr
                bu_ref[q, rows, ns:2 * ns] = ni
                out.append((nr, ni))
            return tuple(out)

        zero = jnp.zeros((SUBLANES, ns), jnp.float32)
        ends = lax.fori_loop(0, seg, step, ((zero, zero),) * npair, unroll=2)

        for q in range(npair):
            s = s0 + q
            er, ei = ends[q]
            ar = lamseg_ref[s, 0:1, :]
            ai = lamseg_ref[s, 1:2, :]
            cr = carry_ref[s, 0:1, :]
            ci = carry_ref[s, 1:2, :]
            car = jnp.zeros((SUBLANES, ns), jnp.float32)
            cai = jnp.zeros((SUBLANES, ns), jnp.float32)
            for j in range(SUBLANES):
                car = jnp.where(sub == j, jnp.broadcast_to(cr, (SUBLANES, ns)), car)
                cai = jnp.where(sub == j, jnp.broadcast_to(ci, (SUBLANES, ns)), cai)
                ejr = jnp.sum(jnp.where(sub == j, er, 0.0), axis=0, keepdims=True)
                eji = jnp.sum(jnp.where(sub == j, ei, 0.0), axis=0, keepdims=True)
                cr, ci = ar * cr - ai * ci + ejr, ar * ci + ai * cr + eji
            carry_ref[s, 0:1, :] = cr
            carry_ref[s, 1:2, :] = ci

            ctr = jnp.broadcast_to(car[None], (seg, SUBLANES, ns)).reshape(tm, ns)
            cti = jnp.broadcast_to(cai[None], (seg, SUBLANES, ns)).reshape(tm, ns)
            pr = pw_ref[s, 0]
            pi = pw_ref[s, 1]
            st_r = bu_ref[q, :, 0:ns] + pr * ctr - pi * cti
            st_i = bu_ref[q, :, ns:2 * ns] + pr * cti + pi * ctr
            st = jnp.concatenate([st_r, st_i], axis=1).astype(jnp.bfloat16)
            y_ref[s] = jnp.dot(st, cmat_ref[s], preferred_element_type=jnp.float32) + d_ref[s] * u_ref[s]


def _ssm_tables(a_re, a_im, b_re, b_im, c_re, c_im, d_skip, log_dt, seg):
    f32 = jnp.float32
    g, p = a_re.shape
    c = b_re.shape[-1]
    ng = SSM_CHUNK_GROUPS
    n_slab = g // ng
    lam = lax.complex(a_re.astype(f32), a_im.astype(f32))
    dt = jnp.exp(log_dt.astype(f32))[:, None]
    lam_bar = jnp.exp(lam * dt)
    b_bar = ((lam_bar - 1.0) / lam)[:, :, None] * lax.complex(b_re.astype(f32), b_im.astype(f32))
    c_mat = lax.complex(c_re.astype(f32), c_im.astype(f32))
    eye = jnp.eye(ng, dtype=f32)

    def bdiag_b(m):
        m = m.reshape(n_slab, ng, p, c)
        return jnp.einsum('ab,kbpc->kacbp', eye, m).reshape(n_slab, ng * c, ng * p)

    def bdiag_c(m):
        m = m.reshape(n_slab, ng, c, p)
        return jnp.einsum('ab,kbcp->kbpac', eye, m).reshape(n_slab, ng * p, ng * c)

    bmat = jnp.concatenate([bdiag_b(jnp.real(b_bar)), bdiag_b(jnp.imag(b_bar))], axis=2).astype(jnp.bfloat16)
    cmat = jnp.concatenate([bdiag_c(jnp.real(c_mat)), -bdiag_c(jnp.imag(c_mat))], axis=1).astype(jnp.bfloat16)

    def slab_rows(z):
        z = z.reshape(n_slab, 1, ng * p)
        return jnp.concatenate([jnp.real(z), jnp.imag(z)], axis=1)

    lam_t = slab_rows(lam_bar)
    steps = jnp.arange(1, seg + 1, dtype=f32)
    pw = jnp.exp((lam * dt)[None] * steps[:, None, None])
    pw = pw.reshape(seg, n_slab, ng * p).transpose(1, 0, 2)
    pw = jnp.repeat(pw, SUBLANES, axis=1)
    pw_t = jnp.stack([jnp.real(pw), jnp.imag(pw)], axis=1)
    lamseg_t = slab_rows(jnp.exp(lam * dt * float(seg)))
    d_t = d_skip.astype(f32).reshape(n_slab, 1, ng * c)
    return bmat, cmat, lam_t, pw_t, lamseg_t, d_t


def _ssm(u4, tables, b, l):
    bmat, cmat, lam_t, pw_t, lamseg_t, d_t = tables
    n_slab, t, _ = u4.shape
    tm = TM_SSM
    nt = l // tm
    ns = SSM_CHUNK_STATES
    const = lambda nd: (lambda bi, i: (0,) * nd)
    return pl.pallas_call(
        _ssm_kernel,
        grid=(b, nt),
        in_specs=[
            pl.BlockSpec((n_slab, tm, LANES), lambda bi, i: (0, bi * nt + i, 0)),
            pl.BlockSpec(bmat.shape, const(3)),
            pl.BlockSpec(cmat.shape, const(3)),
            pl.BlockSpec(lam_t.shape, const(3)),
            pl.BlockSpec(pw_t.shape, const(4)),
            pl.BlockSpec(lamseg_t.shape, const(3)),
            pl.BlockSpec(d_t.shape, const(3)),
        ],
        out_specs=pl.BlockSpec((n_slab, tm, LANES), lambda bi, i: (0, bi * nt + i, 0)),
        out_shape=jax.ShapeDtypeStruct((n_slab, t, LANES), jnp.float32),
        scratch_shapes=[
            pltpu.VMEM((2, tm, 2 * ns), jnp.float32),
            pltpu.VMEM((n_slab, 2, ns), jnp.float32),
        ],
        compiler_params=_cparams("arbitrary", "arbitrary"),
        name="s5_scan",
    )(u4, bmat, cmat, lam_t, pw_t, lamseg_t, d_t)


def _merge_kernel(x_ref, attn_ref, y_ref, gate_ref, wglu_ref, bglu_ref, wa_ref, ws_ref, wo_ref,
                  nm_ref, wr_ref, br_ref, x1_ref, h2_ref, ids_ref, rw_ref):
    d = x_ref.shape[1]
    tm = x_ref.shape[0]
    seg = tm // SUBLANES
    y = jnp.concatenate(
        [jnp.concatenate([y_ref[s, pl.ds(j, seg, stride=SUBLANES), :] for j in range(SUBLANES)], axis=0)
         for s in range(y_ref.shape[0])], axis=1)
    z = jax.nn.gelu(y)
    zg = jnp.dot(z.astype(jnp.bfloat16), wglu_ref[...], preferred_element_type=jnp.float32) + bglu_ref[...]
    z = z * jax.nn.sigmoid(zg)
    a = jnp.dot(attn_ref[...], wa_ref[...], preferred_element_type=jnp.float32)
    sb = jnp.dot(z.astype(jnp.bfloat16), ws_ref[...], preferred_element_type=jnp.float32)
    merged = gate_ref[:, 0:d].astype(jnp.float32) * a + gate_ref[:, d:2 * d].astype(jnp.float32) * sb
    x1 = x_ref[...] + jnp.dot(merged.astype(jnp.bfloat16), wo_ref[...], preferred_element_type=jnp.float32)
    x1_ref[...] = x1
    inv = lax.rsqrt(jnp.mean(x1 * x1, axis=-1, keepdims=True) + EPS)
    h2 = x1 * inv * nm_ref[...]
    h_hi = h2.astype(jnp.bfloat16)
    h_pk = _pack_bf16_pairs(h_hi.astype(jnp.float32))
    _store_rows(h2_ref, 0, h_pk)

    h_lo = (h2 - h_hi.astype(jnp.float32)).astype(jnp.bfloat16)
    hh = jnp.dot(h_hi, wr_ref[...], preferred_element_type=jnp.float32)
    lh = jnp.dot(h_lo, wr_ref[:, 0:LANES], preferred_element_type=jnp.float32)
    logits = hh[:, 0:LANES] + hh[:, LANES:2 * LANES] + lh + br_ref[...]
    lane = lax.broadcasted_iota(jnp.int32, (tm, LANES), 1)
    ninf = -jnp.inf
    gl = jnp.where((lane >= N_EXPERTS) & (lane < N_EXPERTS + N_EXPERT_GROUPS), logits, ninf)
    gmax = jnp.max(gl, axis=-1, keepdims=True)
    gidx = jnp.min(jnp.where(gl == gmax, lane - N_EXPERTS, LANES), axis=-1, keepdims=True)
    group_p = 1.0 / jnp.sum(jnp.exp(gl - gmax), axis=-1, keepdims=True)
    el = jnp.where((lane < N_EXPERTS) & ((lane >> 3) == gidx), logits, ninf)
    m1 = jnp.max(el, axis=-1, keepdims=True)
    i1 = jnp.min(jnp.where(el == m1, lane, LANES), axis=-1, keepdims=True)
    el2 = jnp.where(lane == i1, ninf, el)
    m2 = jnp.max(el2, axis=-1, keepdims=True)
    i2 = jnp.min(jnp.where(el2 == m2, lane, LANES), axis=-1, keepdims=True)
    e2 = jnp.exp(m2 - m1)
    w1 = group_p / (1.0 + e2)
    w2 = group_p * e2 / (1.0 + e2)
    ids_ref[...] = jnp.where(lane == 0, i1, jnp.where(lane == 1, i2, 0))
    rw_ref[...] = jnp.where(lane == 0, w1, jnp.where(lane == 1, w2, 0.0))


def _merge(x2, attn, y4, gates, w_glu, b_glu, w_a, w_s, w_o, norm_moe, w_router, b_router):
    t, d = x2.shape
    tm = TM_MERGE
    n_slab = y4.shape[0]
    full = lambda a: pl.BlockSpec(a.shape, lambda i: (0,) * a.ndim)
    row = lambda c: pl.BlockSpec((tm, c), lambda i: (i, 0))
    return pl.pallas_call(
        _merge_kernel,
        grid=(t // tm,),
        in_specs=[
            row(d), row(attn.shape[1]),
            pl.BlockSpec((n_slab, tm, LANES), lambda i: (0, i, 0)),
            row(gates.shape[1]),
            full(w_glu), full(b_glu), full(w_a), full(w_s), full(w_o), full(norm_moe),
            full(w_router), full(b_router),
        ],
        out_specs=[row(d), pl.BlockSpec((tm * ROW_SUB, LANES), lambda i: (i, 0)), row(LANES), row(LANES)],
        out_shape=[
            jax.ShapeDtypeStruct((t, d), jnp.float32),
            jax.ShapeDtypeStruct((t * ROW_SUB, LANES), jnp.uint32),
            jax.ShapeDtypeStruct((t, LANES), jnp.int32),
            jax.ShapeDtypeStruct((t, LANES), jnp.float32),
        ],
        compiler_params=_cparams("arbitrary"),
        name="merge_router",
    )(x2, attn, y4, gates, w_glu, b_glu, w_a, w_s, w_o, norm_moe, w_router, b_router)


def _rank_kernel(ids_ref, rank_ref, cnt_ref, carry_ref):
    i = pl.program_id(0)
    tm = ids_ref.shape[0]

    @pl.when(i == 0)
    def _():
        carry_ref[...] = jnp.zeros_like(carry_ref)

    lane = lax.broadcasted_iota(jnp.int32, (tm, LANES), 1)
    ids = ids_ref[...]
    e0 = ids[:, 0:1]
    e1 = ids[:, 1:2]
    oh0 = lane == e0
    oh1 = lane == e1
    oh = oh0.astype(jnp.float32) + oh1.astype(jnp.float32)
    r_i = lax.broadcasted_iota(jnp.int32, (tm, tm), 0)
    c_i = lax.broadcasted_iota(jnp.int32, (tm, tm), 1)
    tri = (c_i < r_i).astype(jnp.bfloat16)
    cum = jnp.dot(tri, oh.astype(jnp.bfloat16), preferred_element_type=jnp.float32) + carry_ref[...]
    r0 = jnp.sum(jnp.where(oh0, cum, 0.0), axis=-1, keepdims=True)
    r1 = jnp.sum(jnp.where(oh1, cum, 0.0), axis=-1, keepdims=True)
    rank_ref[...] = jnp.where(lane == 0, r0, jnp.where(lane == 1, r1, 0.0)).astype(jnp.int32)
    carry_ref[...] = carry_ref[...] + jnp.sum(oh, axis=0, keepdims=True)
    cnt_ref[...] = carry_ref[...].astype(jnp.int32)


def _ranks(ids):
    t = ids.shape[0]
    tm = TM_RANK
    return pl.pallas_call(
        _rank_kernel,
        grid=(t // tm,),
        in_specs=[pl.BlockSpec((tm, LANES), lambda i: (i, 0))],
        out_specs=[pl.BlockSpec((tm, LANES), lambda i: (i, 0)),
                   pl.BlockSpec((1, LANES), lambda i: (0, 0))],
        out_shape=[jax.ShapeDtypeStruct((t, LANES), jnp.int32),
                   jax.ShapeDtypeStruct((1, LANES), jnp.int32)],
        scratch_shapes=[pltpu.VMEM((1, LANES), jnp.float32)],
        compiler_params=_cparams("arbitrary"),
        name="route_rank",
    )(ids)


def _inverse_kernel(pos_ref, src_ref):
    n_slots = pos_ref.shape[0]
    t = n_slots // TOP_K

    def body(tok, c):
        for k in range(TOP_K):
            src_ref[pos_ref[k * t + tok]] = tok * ROW_SUB
        return c

    lax.fori_loop(0, t, body, 0, unroll=8)


def _inverse(pos1d):
    n_slots = pos1d.shape[0]
    smem = pl.BlockSpec(memory_space=pltpu.SMEM)
    return pl.pallas_call(
        _inverse_kernel,
        in_specs=[smem],
        out_specs=smem,
        out_shape=jax.ShapeDtypeStruct((n_slots,), jnp.int32),
        name="route_inverse",
    )(pos1d)


def _expert_kernel(ta_ref, tn_ref, tnew_ref, sexp_ref, slo_ref, shi_ref, meta_ref, src_ref,
                   h_ref, wg_hbm, wu_hbm, wd_hbm, y_ref,
                   xbuf_ref, wgs_ref, wus_ref, wds_ref, wgb_ref, wub_ref, wdb_ref, gsem, wsem):
    j = pl.program_id(0)
    n_tiles = pl.num_programs(0)
    last = n_tiles - 1
    tm = TM_EXPERT
    rs = ROW_SUB
    nb = ROW_BUFS
    n_used = meta_ref[0]
    slot = j % nb
    a = ta_ref[j]

    def gather(tile, sl, start):
        _row_gather(h_ref, src_ref, tile * tm, xbuf_ref, sl * tm, tm, gsem.at[sl], start)

    def weights(m, start):
        sl = m % 2
        e = sexp_ref[m]
        for hbm, stage in ((wg_hbm, wgs_ref), (wu_hbm, wus_ref), (wd_hbm, wds_ref)):
            cp = pltpu.make_async_copy(hbm.at[e], stage.at[sl], wsem.at[sl])
            cp.start() if start else cp.wait()

    def new_expert(m):
        weights(m, False)
        sl = m % 2
        wgb_ref[...] = wgs_ref[sl].astype(jnp.bfloat16)
        wub_ref[...] = wus_ref[sl].astype(jnp.bfloat16)
        wdb_ref[...] = wds_ref[sl].astype(jnp.bfloat16)

        @pl.when(m + 1 < n_used)
        def _():
            weights(m + 1, True)

    def expert_out(xp):
        x = _unpack_bf16_pairs(xp).astype(jnp.bfloat16)
        hg = jnp.dot(x, wgb_ref[...], preferred_element_type=jnp.float32)
        hu = jnp.dot(x, wub_ref[...], preferred_element_type=jnp.float32)
        act = (jax.nn.silu(hg) * hu).astype(jnp.bfloat16)
        y = jnp.dot(act, wdb_ref[...], preferred_element_type=jnp.float32)
        return _pack_bf16_pairs(y.astype(jnp.bfloat16).astype(jnp.float32))

    @pl.when(j == 0)
    def _():
        weights(0, True)
        gather(0, 0, True)
        gather(1, 1, True)

    @pl.when(tnew_ref[j] == 1)
    def _():
        new_expert(a)

    gather(j, slot, False)
    xp = _load_rows(xbuf_ref, slot * tm, tm)
    gather(jnp.minimum(j + 2, last), (j + 2) % nb, True)
    _store_rows(y_ref, 0, expert_out(xp))

    def extra(i, carry):
        m = a + i
        new_expert(m)
        yp = expert_out(_load_rows(xbuf_ref, slot * tm, tm))
        row = lax.broadcasted_iota(jnp.int32, (tm, LANES), 0) + j * tm
        mine = (row >= slo_ref[m]) & (row < shi_ref[m])
        _store_rows(y_ref, 0, yp, mine)
        return carry

    lax.fori_loop(1, tn_ref[j], extra, 0)

    @pl.when(j == last)
    def _():
        gather(j, (j + 1) % nb, False)
        gather(j, (j + 2) % nb, False)


def _experts(meta, src_tok, h_fat, w_gate, w_up, w_down):
    n_slots = src_tok.shape[0]
    ne, d, dff = w_gate.shape
    tm = TM_EXPERT
    rs = ROW_SUB
    assert n_slots // tm >= ROW_BUFS
    any_spec = pl.BlockSpec(memory_space=pl.ANY)
    grid_spec = pltpu.PrefetchScalarGridSpec(
        num_scalar_prefetch=8,
        grid=(n_slots // tm,),
        in_specs=[any_spec, any_spec, any_spec, any_spec],
        out_specs=pl.BlockSpec((tm * rs, LANES), lambda j, *_: (j, 0)),
        scratch_shapes=[
            pltpu.VMEM((ROW_BUFS * tm * rs, LANES), jnp.uint32),
            pltpu.VMEM((2, d, dff), jnp.float32),
            pltpu.VMEM((2, d, dff), jnp.float32),
            pltpu.VMEM((2, dff, d), jnp.float32),
            pltpu.VMEM((d, dff), jnp.bfloat16),
            pltpu.VMEM((d, dff), jnp.bfloat16),
            pltpu.VMEM((dff, d), jnp.bfloat16),
            pltpu.SemaphoreType.DMA((ROW_BUFS,)),
            pltpu.SemaphoreType.DMA((2,)),
        ],
    )
    return pl.pallas_call(
        _expert_kernel,
        grid_spec=grid_spec,
        out_shape=jax.ShapeDtypeStruct((n_slots * rs, LANES), jnp.uint32),
        compiler_params=_cparams("arbitrary"),
        name="moe_experts",
    )(*meta, src_tok, h_fat, w_gate, w_up, w_down)


def _combine_kernel(pos_ref, x1_ref, rw_ref, nf_ref, ys_ref, o_ref, buf_ref, sem):
    i = pl.program_id(0)
    n = pl.num_programs(0)
    last = n - 1
    tm = x1_ref.shape[0]
    t = n * tm
    nb = ROW_BUFS
    slot = i % nb

    def gather(tile, sl, start):
        for k in range(TOP_K):
            _row_gather(ys_ref, pos_ref, k * t + tile * tm, buf_ref, (sl * TOP_K + k) * tm, tm, sem.at[sl], start)

    @pl.when(i == 0)
    def _():
        gather(0, 0, True)
        gather(1, 1, True)

    gather(i, slot, False)
    y0 = _unpack_bf16_pairs(_load_rows(buf_ref, (slot * TOP_K) * tm, tm))
    y1 = _unpack_bf16_pairs(_load_rows(buf_ref, (slot * TOP_K + 1) * tm, tm))
    gather(jnp.minimum(i + 2, last), (i + 2) % nb, True)
    rw = rw_ref[...]
    x = x1_ref[...] + rw[:, 0:1] * y0 + rw[:, 1:2] * y1
    inv = lax.rsqrt(jnp.mean(x * x, axis=-1, keepdims=True) + EPS)
    o_ref[...] = x * inv * nf_ref[...]

    @pl.when(i == last)
    def _():
        gather(i, (i + 1) % nb, False)
        gather(i, (i + 2) % nb, False)


def _combine(pos_rows, x1, rw, norm_final, ys):
    t, d = x1.shape
    tm = TM_MOVE
    assert t // tm >= ROW_BUFS
    return pl.pallas_call(
        _combine_kernel,
        grid=(t // tm,),
        in_specs=[
            pl.BlockSpec(memory_space=pltpu.SMEM),
            pl.BlockSpec((tm, d), lambda i: (i, 0)),
            pl.BlockSpec((tm, LANES), lambda i: (i, 0)),
            pl.BlockSpec((1, d), lambda i: (0, 0)),
            pl.BlockSpec(memory_space=pl.ANY),
        ],
        out_specs=pl.BlockSpec((tm, d), lambda i: (i, 0)),
        out_shape=jax.ShapeDtypeStruct((t, d), jnp.float32),
        scratch_shapes=[pltpu.VMEM((ROW_BUFS * TOP_K * tm * ROW_SUB, LANES), jnp.uint32),
                        pltpu.SemaphoreType.DMA((ROW_BUFS,))],
        compiler_params=_cparams("arbitrary"),
        name="moe_combine",
    )(pos_rows, x1, rw, norm_final, ys)


def _expert_meta(counts, n_slots):
    tm = TM_EXPERT
    n_tiles = n_slots // tm
    i32 = jnp.int32
    ends = jnp.cumsum(counts).astype(i32)
    starts = ends - counts
    used = counts > 0
    n_used = jnp.sum(used).astype(i32)
    seq_of = jnp.cumsum(used).astype(i32) - 1
    m_idx = jnp.arange(N_EXPERTS, dtype=i32)
    pick = used[None, :] & (seq_of[None, :] == m_idx[:, None])
    s_exp = jnp.sum(jnp.where(pick, m_idx[None, :], 0), axis=1).astype(i32)
    s_lo = jnp.sum(jnp.where(pick, starts[None, :], 0), axis=1).astype(i32)
    s_hi = jnp.sum(jnp.where(pick, ends[None, :], 0), axis=1).astype(i32)
    row0 = jnp.arange(n_tiles, dtype=i32)[:, None] * tm
    t_a = jnp.sum(used[None, :] & (ends[None, :] <= row0), axis=1).astype(i32)
    t_b = jnp.sum(used[None, :] & (starts[None, :] < row0 + tm), axis=1).astype(i32) - 1
    t_new = jnp.any(used[None, :] & (starts[None, :] == row0), axis=1).astype(i32)
    return t_a, (t_b - t_a + 1).astype(i32), t_new, s_exp, s_lo, s_hi, n_used.reshape(1)


def _moe(x1, h_fat, ids, rw, w_gate, w_up, w_down, norm_final):
    t, d = x1.shape
    rank, cnt = _ranks(ids)
    counts = cnt[0, :N_EXPERTS]
    offs = (jnp.cumsum(counts) - counts).astype(jnp.int32)
    ids2 = ids[:, :TOP_K].T
    onehot = ids2[:, :, None] == jnp.arange(N_EXPERTS, dtype=jnp.int32)
    pos1d = (jnp.sum(jnp.where(onehot, offs, 0), axis=-1) + rank[:, :TOP_K].T).reshape(-1)
    src_tok = _inverse(pos1d)
    meta = _expert_meta(counts, t * TOP_K)
    ys = _experts(meta, src_tok, h_fat, w_gate, w_up, w_down)
    return _combine(pos1d * ROW_SUB, x1, rw, norm_final, ys)


def kernel(x, norm_mix, w_in, b_gate, attn_sinks, ssm_a_re, ssm_a_im, ssm_b_re, ssm_b_im, ssm_c_re, ssm_c_im, ssm_d, ssm_log_dt, w_glu, b_glu, w_attn_branch, w_ssm_branch, w_out, norm_moe, w_router_group, b_router_group, w_router_expert, b_router_expert, w_expert_gate, w_expert_up, w_expert_down, norm_final):
    b, l, d = x.shape
    depth = w_in.shape[0]
    assert depth == 1, "the final norm is fused into the last layer's combine kernel"
    d_attn = N_HEADS * HEAD_DIM
    kv_cols = N_KV_HEADS * HEAD_DIM
    d_ssm = ssm_d.shape[-1]
    bf16 = jnp.bfloat16
    x2 = x.reshape(b * l, d)
    assert TM_PROJ == TM_SSM == TM_MERGE and l % TM_SSM == 0
    i = 0
    q, k, v, u4, gates = _in_proj(x2, norm_mix[i][None], w_in[i].astype(bf16), b_gate[i][None],
                                  d_attn, kv_cols, d_ssm)
    attn = _attention(q, k, v, attn_sinks[i], b, l)
    tables = _ssm_tables(ssm_a_re[i], ssm_a_im[i], ssm_b_re[i], ssm_b_im[i], ssm_c_re[i], ssm_c_im[i],
                         ssm_d[i], ssm_log_dt[i], SSM_SEG)
    y4 = _ssm(u4, tables, b, l)
    pad = LANES - N_EXPERTS - N_EXPERT_GROUPS
    w_router = jnp.concatenate([w_router_expert[i], w_router_group[i], jnp.zeros((d, pad), jnp.float32)], axis=1)
    w_r_hi = w_router.astype(bf16)
    w_router = jnp.concatenate([w_r_hi, (w_router - w_r_hi.astype(jnp.float32)).astype(bf16)], axis=1)
    b_router = jnp.concatenate([b_router_expert[i], b_router_group[i], jnp.zeros((pad,), jnp.float32)])[None]
    x1, h_fat, ids, rw = _merge(x2, attn, y4, gates, w_glu[i].astype(bf16), b_glu[i][None],
                                w_attn_branch[i].astype(bf16), w_ssm_branch[i].astype(bf16),
                                w_out[i].astype(bf16), norm_moe[i][None], w_router, b_router)
    out = _moe(x1, h_fat, ids, rw, w_expert_gate[i], w_expert_up[i], w_expert_down[i], norm_final[None])
    return out.reshape(b, l, d)
```

```python
import functools
import math

import jax
import jax.numpy as jnp
from jax import lax
from jax.experimental import pallas as pl
from jax.experimental.pallas import tpu as pltpu

EPS = 1e-6
HEAD_DIM = 64
N_HEADS = 8
N_KV_HEADS = 2
Q_PER_KV = N_HEADS // N_KV_HEADS
ATTN_BLOCK = 128
ATTN_QB = 2
SSM_GROUP = 16
SSM_STATE = 64
N_EXPERT_GROUPS = 4
EXPERTS_PER_GROUP = 8
N_EXPERTS = N_EXPERT_GROUPS * EXPERTS_PER_GROUP
TOP_K = 2

LANES = 128
SUBLANES = 8
SSM_CHUNK_GROUPS = LANES // SSM_GROUP
SSM_CHUNK_STATES = SSM_CHUNK_GROUPS * SSM_STATE

TM_PROJ = 512
TM_SSM = 512
SSM_SEG = TM_SSM // SUBLANES
TM_MERGE = 512
TM_MOVE = 256
TM_EXPERT = 256
ROW_SUB = 4
ROW_BUFS = 3
VMEM_LIMIT = 56 * 1024 * 1024


def _cparams(*sem):
    return pltpu.CompilerParams(dimension_semantics=sem, vmem_limit_bytes=VMEM_LIMIT)


def _pack_bf16_pairs(x):
    half = x.shape[1] // 2
    bits = lax.bitcast_convert_type(x, jnp.uint32)
    return (bits[:, :half] & jnp.uint32(0xFFFF0000)) | (bits[:, half:] >> 16)


def _unpack_bf16_pairs(p):
    hi = lax.bitcast_convert_type(p & jnp.uint32(0xFFFF0000), jnp.float32)
    lo = lax.bitcast_convert_type(p << 16, jnp.float32)
    return jnp.concatenate([hi, lo], axis=1)


def _load_rows(ref, first, n):
    return jnp.concatenate(
        [ref[pl.ds(first * ROW_SUB + c, n, stride=ROW_SUB), :] for c in range(ROW_SUB)], axis=1)


def _store_rows(ref, first, val, mask=None):
    n = val.shape[0]
    for c in range(ROW_SUB):
        idx = pl.ds(first * ROW_SUB + c, n, stride=ROW_SUB)
        v = val[:, c * LANES:(c + 1) * LANES]
        ref[idx, :] = v if mask is None else jnp.where(mask, v, ref[idx, :])


def _row_gather(src_hbm, idx_ref, idx0, dst_ref, dst0, n, sem, start):
    rs = ROW_SUB
    if start:
        for r in range(n):
            pltpu.make_async_copy(src_hbm.at[pl.ds(pl.multiple_of(idx_ref[idx0 + r], rs), rs)],
                                  dst_ref.at[pl.ds(pl.multiple_of((dst0 + r) * rs, rs), rs)],
                                  sem).start(priority=r % 2)
    else:
        pltpu.make_async_copy(src_hbm.at[pl.ds(0, n * rs)],
                              dst_ref.at[pl.ds(pl.multiple_of(dst0 * rs, rs), n * rs)], sem).wait()


def _proj_kernel(x_ref, g_ref, w_ref, bg_ref, q_ref, k_ref, v_ref, u_ref, gate_ref, *, cols):
    q_c, kv_c, d_ssm = cols
    xf = x_ref[...]
    inv = lax.rsqrt(jnp.mean(xf * xf, axis=-1, keepdims=True) + EPS)
    h = (xf * inv * g_ref[...]).astype(jnp.bfloat16)
    o = 0
    q_ref[...] = (jnp.dot(h, w_ref[:, o:o + q_c], preferred_element_type=jnp.float32)
                  * (1.0 / math.sqrt(HEAD_DIM))).astype(q_ref.dtype)
    o += q_c
    k_ref[...] = jnp.dot(h, w_ref[:, o:o + kv_c], preferred_element_type=jnp.float32).astype(k_ref.dtype)
    o += kv_c
    v_ref[...] = jnp.dot(h, w_ref[:, o:o + kv_c], preferred_element_type=jnp.float32).astype(v_ref.dtype)
    o += kv_c
    seg = x_ref.shape[0] // SUBLANES
    for s in range(0, d_ssm // LANES, 2):
        uu = jnp.dot(h, w_ref[:, o:o + 2 * LANES], preferred_element_type=jnp.float32)
        for half in range(2):
            for j in range(SUBLANES):
                u_ref[s + half, pl.ds(j, seg, stride=SUBLANES), :] = (
                    uu[j * seg:(j + 1) * seg, half * LANES:(half + 1) * LANES])
        o += 2 * LANES
    gl = jnp.dot(h, w_ref[:, o:], preferred_element_type=jnp.float32) + bg_ref[...]
    gate_ref[...] = jax.nn.sigmoid(gl).astype(gate_ref.dtype)


def _in_proj(x2, norm_w, w_in, b_gate, d_attn, kv_cols, d_ssm):
    t, d = x2.shape
    gate_cols = b_gate.shape[-1]
    n_slab = d_ssm // LANES
    tm = TM_PROJ
    kern = functools.partial(_proj_kernel, cols=(d_attn, kv_cols, d_ssm))
    return pl.pallas_call(
        kern,
        grid=(t // tm,),
        in_specs=[
            pl.BlockSpec((tm, d), lambda i: (i, 0)),
            pl.BlockSpec((1, d), lambda i: (0, 0)),
            pl.BlockSpec(w_in.shape, lambda i: (0, 0)),
            pl.BlockSpec((1, gate_cols), lambda i: (0, 0)),
        ],
        out_specs=[
            pl.BlockSpec((tm, d_attn), lambda i: (i, 0)),
            pl.BlockSpec((tm, kv_cols), lambda i: (i, 0)),
            pl.BlockSpec((tm, kv_cols), lambda i: (i, 0)),
            pl.BlockSpec((n_slab, tm, LANES), lambda i: (0, i, 0)),
            pl.BlockSpec((tm, gate_cols), lambda i: (i, 0)),
        ],
        out_shape=[
            jax.ShapeDtypeStruct((t, d_attn), jnp.bfloat16),
            jax.ShapeDtypeStruct((t, kv_cols), jnp.bfloat16),
            jax.ShapeDtypeStruct((t, kv_cols), jnp.bfloat16),
            jax.ShapeDtypeStruct((n_slab, t, LANES), jnp.float32),
            jax.ShapeDtypeStruct((t, gate_cols), jnp.bfloat16),
        ],
        compiler_params=_cparams("arbitrary"),
        name="in_proj",
    )(x2, norm_w, w_in, b_gate)


def _attn_kernel(sink_ref, q_ref, kp_ref, kc_ref, vp_ref, vc_ref, rep_ref, mask_ref, o_ref):
    i = pl.program_id(1)
    blk = ATTN_BLOCK
    hw = Q_PER_KV * HEAD_DIM
    rows = Q_PER_KV * blk
    head_of_row = lax.broadcasted_iota(jnp.int32, (rows, 1), 0) >> 7
    lane_head_q = lax.broadcasted_iota(jnp.int32, (blk, hw), 1) >> 6
    lane_head_v = lax.broadcasted_iota(jnp.int32, (2 * blk, hw), 1) >> 6
    sinks = []
    for kh in range(N_KV_HEADS):
        sink = jnp.zeros((rows, 1), jnp.float32)
        for g in range(Q_PER_KV):
            sink = jnp.where(head_of_row == g, sink_ref[kh * Q_PER_KV + g], sink)
        sinks.append(sink)
    for qb in range(ATTN_QB):
        if qb == 0:
            k2 = jnp.concatenate([kp_ref[...], kc_ref[0:blk, :]], axis=0)
            v2 = jnp.concatenate([vp_ref[...], vc_ref[0:blk, :]], axis=0)
            bias = mask_ref[jnp.where(i == 0, 1, 0)]
        else:
            k2 = kc_ref[(qb - 1) * blk:(qb + 1) * blk, :]
            v2 = vc_ref[(qb - 1) * blk:(qb + 1) * blk, :]
            bias = mask_ref[0]
        for kh in range(N_KV_HEADS):
            rep = rep_ref[kh]
            k4 = jnp.dot(k2, rep, preferred_element_type=jnp.float32).astype(jnp.bfloat16)
            v4 = jnp.dot(v2, rep, preferred_element_type=jnp.float32).astype(jnp.bfloat16)
            qh = q_ref[qb * blk:(qb + 1) * blk, kh * hw:(kh + 1) * hw]
            qm = jnp.concatenate(
                [jnp.where(lane_head_q == g, qh, jnp.zeros_like(qh)) for g in range(Q_PER_KV)], axis=0)
            s = lax.dot_general(qm, k4, (((1,), (1,)), ((), ())), preferred_element_type=jnp.float32)
            s = s + bias
            sink = sinks[kh]
            m = jnp.maximum(jnp.max(s, axis=-1, keepdims=True), sink)
            p = jnp.exp(s - m)
            rinv = 1.0 / (jnp.sum(p, axis=-1, keepdims=True) + jnp.exp(sink - m))
            p = p.astype(jnp.bfloat16)
            p_cat = jnp.concatenate([p[g * blk:(g + 1) * blk, :] for g in range(Q_PER_KV)], axis=1)
            vm = jnp.concatenate(
                [jnp.where(lane_head_v == g, v4, jnp.zeros_like(v4)) for g in range(Q_PER_KV)], axis=0)
            o = jnp.dot(p_cat, vm, preferred_element_type=jnp.float32)
            scale = jnp.zeros((blk, hw), jnp.float32)
            for g in range(Q_PER_KV):
                scale = jnp.where(lane_head_q == g, rinv[g * blk:(g + 1) * blk, :], scale)
            o_ref[qb * blk:(qb + 1) * blk, kh * hw:(kh + 1) * hw] = (o * scale).astype(o_ref.dtype)


def _attention(q, k, v, sinks, b, l):
    d_attn = q.shape[-1]
    kv_cols = k.shape[-1]
    blk = ATTN_BLOCK
    tq = ATTN_QB * blk
    hw = Q_PER_KV * HEAD_DIM
    lane = jnp.arange(hw)[None, :]
    src = jnp.arange(kv_cols)[:, None]
    rep = jnp.stack([(src == kh * HEAD_DIM + (lane % HEAD_DIM)) for kh in range(N_KV_HEADS)]
                    ).astype(jnp.bfloat16)
    r = (jnp.arange(Q_PER_KV * blk) % blk)[:, None]
    c = jnp.arange(2 * blk)[None, :]
    band = (c > r) & (c <= r + blk)
    mask = jnp.where(jnp.stack([band, band & (c >= blk)]), 0.0, jnp.finfo(jnp.float32).min).astype(jnp.float32)
    q3 = q.reshape(b, l, d_attn)
    k3 = k.reshape(b, l, kv_cols)
    v3 = v.reshape(b, l, kv_cols)
    cur = lambda bi, i: (bi, i, 0)
    prev = lambda bi, i: (bi, jnp.maximum(ATTN_QB * i - 1, 0), 0)
    out = pl.pallas_call(
        _attn_kernel,
        grid=(b, l // tq),
        in_specs=[
            pl.BlockSpec(memory_space=pltpu.SMEM),
            pl.BlockSpec((None, tq, d_attn), cur),
            pl.BlockSpec((None, blk, kv_cols), prev),
            pl.BlockSpec((None, tq, kv_cols), cur),
            pl.BlockSpec((None, blk, kv_cols), prev),
            pl.BlockSpec((None, tq, kv_cols), cur),
            pl.BlockSpec(rep.shape, lambda bi, i: (0, 0, 0)),
            pl.BlockSpec(mask.shape, lambda bi, i: (0, 0, 0)),
        ],
        out_specs=pl.BlockSpec((None, tq, d_attn), cur),
        out_shape=jax.ShapeDtypeStruct((b, l, d_attn), jnp.bfloat16),
        compiler_params=_cparams("arbitrary", "arbitrary"),
        name="swa",
    )(sinks, q3, k3, k3, v3, v3, rep, mask)
    return out.reshape(b * l, d_attn)


def _ssm_kernel(u_ref, bmat_ref, cmat_ref, lam_ref, pw_ref, lamseg_ref, d_ref, y_ref, bu_ref, carry_ref):
    i = pl.program_id(1)
    n_slab = u_ref.shape[0]
    tm = u_ref.shape[1]
    seg = tm // SUBLANES
    ns = SSM_CHUNK_STATES
    npair = 2

    @pl.when(i == 0)
    def _():
        carry_ref[...] = jnp.zeros_like(carry_ref)

    sub = lax.broadcasted_iota(jnp.int32, (SUBLANES, ns), 0)
    for s0 in range(0, n_slab, npair):
        lam = []
        for q in range(npair):
            s = s0 + q
            bu_ref[q] = jnp.dot(u_ref[s].astype(jnp.bfloat16), bmat_ref[s], preferred_element_type=jnp.float32)
            lam.append((jnp.broadcast_to(lam_ref[s, 0:1, :], (SUBLANES, ns)),
                        jnp.broadcast_to(lam_ref[s, 1:2, :], (SUBLANES, ns))))

        def step(r, st):
            rows = pl.ds(pl.multiple_of(r * SUBLANES, SUBLANES), SUBLANES)
            out = []
            for q in range(npair):
                lr, li = lam[q]
                sr, si = st[q]
                nr = lr * sr - li * si + bu_ref[q, rows, 0:ns]
                ni = lr * si + li * sr + bu_ref[q, rows, ns:2 * ns]
                bu_ref[q, rows, 0:ns] = nr
                bu_ref[q, rows, ns:2 * ns] = ni
                out.append((nr, ni))
            return tuple(out)

        zero = jnp.zeros((SUBLANES, ns), jnp.float32)
        ends = lax.fori_loop(0, seg, step, ((zero, zero),) * npair, unroll=2)

        for q in range(npair):
            s = s0 + q
            er, ei = ends[q]
            ar = lamseg_ref[s, 0:1, :]
            ai = lamseg_ref[s, 1:2, :]
            cr = carry_ref[s, 0:1, :]
            ci = carry_ref[s, 1:2, :]
            car = jnp.zeros((SUBLANES, ns), jnp.float32)
            cai = jnp.zeros((SUBLANES, ns), jnp.float32)
            for j in range(SUBLANES):
                car = jnp.where(sub == j, jnp.broadcast_to(cr, (SUBLANES, ns)), car)
                cai = jnp.where(sub == j, jnp.broadcast_to(ci, (SUBLANES, ns)), cai)
                ejr = jnp.sum(jnp.where(sub == j, er, 0.0), axis=0, keepdims=True)
                eji = jnp.sum(jnp.where(sub == j, ei, 0.0), axis=0, keepdims=True)
                cr, ci = ar * cr - ai * ci + ejr, ar * ci + ai * cr + eji
            carry_ref[s, 0:1, :] = cr
            carry_ref[s, 1:2, :] = ci

            ctr = jnp.broadcast_to(car[None], (seg, SUBLANES, ns)).reshape(tm, ns)
            cti = jnp.broadcast_to(cai[None], (seg, SUBLANES, ns)).reshape(tm, ns)
            pr = pw_ref[s, 0]
            pi = pw_ref[s, 1]
            st_r = bu_ref[q, :, 0:ns] + pr * ctr - pi * cti
            st_i = bu_ref[q, :, ns:2 * ns] + pr * cti + pi * ctr
            st = jnp.concatenate([st_r, st_i], axis=1).astype(jnp.bfloat16)
            y_ref[s] = jnp.dot(st, cmat_ref[s], preferred_element_type=jnp.float32) + d_ref[s] * u_ref[s]


def _ssm_tables(a_re, a_im, b_re, b_im, c_re, c_im, d_skip, log_dt, seg):
    f32 = jnp.float32
    g, p = a_re.shape
    c = b_re.shape[-1]
    ng = SSM_CHUNK_GROUPS
    n_slab = g // ng
    lam = lax.complex(a_re.astype(f32), a_im.astype(f32))
    dt = jnp.exp(log_dt.astype(f32))[:, None]
    lam_bar = jnp.exp(lam * dt)
    b_bar = ((lam_bar - 1.0) / lam)[:, :, None] * lax.complex(b_re.astype(f32), b_im.astype(f32))
    c_mat = lax.complex(c_re.astype(f32), c_im.astype(f32))
    eye = jnp.eye(ng, dtype=f32)

    def bdiag_b(m):
        m = m.reshape(n_slab, ng, p, c)
        return jnp.einsum('ab,kbpc->kacbp', eye, m).reshape(n_slab, ng * c, ng * p)

    def bdiag_c(m):
        m = m.reshape(n_slab, ng, c, p)
        return jnp.einsum('ab,kbcp->kbpac', eye, m).reshape(n_slab, ng * p, ng * c)

    bmat = jnp.concatenate([bdiag_b(jnp.real(b_bar)), bdiag_b(jnp.imag(b_bar))], axis=2).astype(jnp.bfloat16)
    cmat = jnp.concatenate([bdiag_c(jnp.real(c_mat)), -bdiag_c(jnp.imag(c_mat))], axis=1).astype(jnp.bfloat16)

    def slab_rows(z):
        z = z.reshape(n_slab, 1, ng * p)
        return jnp.concatenate([jnp.real(z), jnp.imag(z)], axis=1)

    lam_t = slab_rows(lam_bar)
    steps = jnp.arange(1, seg + 1, dtype=f32)
    pw = jnp.exp((lam * dt)[None] * steps[:, None, None])
    pw = pw.reshape(seg, n_slab, ng * p).transpose(1, 0, 2)
    pw = jnp.repeat(pw, SUBLANES, axis=1)
    pw_t = jnp.stack([jnp.real(pw), jnp.imag(pw)], axis=1)
    lamseg_t = slab_rows(jnp.exp(lam * dt * float(seg)))
    d_t = d_skip.astype(f32).reshape(n_slab, 1, ng * c)
    return bmat, cmat, lam_t, pw_t, lamseg_t, d_t


def _ssm(u4, tables, b, l):
    bmat, cmat, lam_t, pw_t, lamseg_t, d_t = tables
    n_slab, t, _ = u4.shape
    tm = TM_SSM
    nt = l // tm
    ns = SSM_CHUNK_STATES
    const = lambda nd: (lambda bi, i: (0,) * nd)
    return pl.pallas_call(
        _ssm_kernel,
        grid=(b, nt),
        in_specs=[
            pl.BlockSpec((n_slab, tm, LANES), lambda bi, i: (0, bi * nt + i, 0)),
            pl.BlockSpec(bmat.shape, const(3)),
            pl.BlockSpec(cmat.shape, const(3)),
            pl.BlockSpec(lam_t.shape, const(3)),
            pl.BlockSpec(pw_t.shape, const(4)),
            pl.BlockSpec(lamseg_t.shape, const(3)),
            pl.BlockSpec(d_t.shape, const(3)),
        ],
        out_specs=pl.BlockSpec((n_slab, tm, LANES), lambda bi, i: (0, bi * nt + i, 0)),
        out_shape=jax.ShapeDtypeStruct((n_slab, t, LANES), jnp.float32),
        scratch_shapes=[
            pltpu.VMEM((2, tm, 2 * ns), jnp.float32),
            pltpu.VMEM((n_slab, 2, ns), jnp.float32),
        ],
        compiler_params=_cparams("arbitrary", "arbitrary"),
        name="s5_scan",
    )(u4, bmat, cmat, lam_t, pw_t, lamseg_t, d_t)


def _merge_kernel(x_ref, attn_ref, y_ref, gate_ref, wglu_ref, bglu_ref, wa_ref, ws_ref, wo_ref,
                  nm_ref, wr_ref, br_ref, tri_ref, x1_ref, h2_ref, ids_ref, rw_ref, rank_ref, cnt_ref,
                  lg_ref, carry_ref):
    i = pl.program_id(0)
    d = x_ref.shape[1]
    tm = x_ref.shape[0]

    @pl.when(i == 0)
    def _():
        lg_ref[...] = jnp.zeros_like(lg_ref)
        carry_ref[...] = jnp.zeros_like(carry_ref)

    logits = lg_ref[...]

    seg = tm // SUBLANES
    y = jnp.concatenate(
        [jnp.concatenate([y_ref[s, pl.ds(j, seg, stride=SUBLANES), :] for j in range(SUBLANES)], axis=0)
         for s in range(y_ref.shape[0])], axis=1)
    z = jax.nn.gelu(y)
    zg = jnp.dot(z.astype(jnp.bfloat16), wglu_ref[...], preferred_element_type=jnp.float32) + bglu_ref[...]
    z = z * jax.nn.sigmoid(zg)
    a = jnp.dot(attn_ref[...], wa_ref[...], preferred_element_type=jnp.float32)
    sb = jnp.dot(z.astype(jnp.bfloat16), ws_ref[...], preferred_element_type=jnp.float32)
    merged = gate_ref[:, 0:d].astype(jnp.float32) * a + gate_ref[:, d:2 * d].astype(jnp.float32) * sb
    x1 = x_ref[...] + jnp.dot(merged.astype(jnp.bfloat16), wo_ref[...], preferred_element_type=jnp.float32)
    x1_ref[...] = x1
    inv = lax.rsqrt(jnp.mean(x1 * x1, axis=-1, keepdims=True) + EPS)
    h2 = x1 * inv * nm_ref[...]
    h_hi = h2.astype(jnp.bfloat16)
    h_pk = _pack_bf16_pairs(h_hi.astype(jnp.float32))
    _store_rows(h2_ref, 0, h_pk)

    h_lo = (h2 - h_hi.astype(jnp.float32)).astype(jnp.bfloat16)
    hh = jnp.dot(h_hi, wr_ref[...], preferred_element_type=jnp.float32)
    lh = jnp.dot(h_lo, wr_ref[:, 0:LANES], preferred_element_type=jnp.float32)
    lg_ref[...] = hh[:, 0:LANES] + hh[:, LANES:2 * LANES] + lh + br_ref[...]

    lane = lax.broadcasted_iota(jnp.int32, (tm, LANES), 1)
    ninf = -jnp.inf
    gl = jnp.where((lane >= N_EXPERTS) & (lane < N_EXPERTS + N_EXPERT_GROUPS), logits, ninf)
    gmax = jnp.max(gl, axis=-1, keepdims=True)
    gidx = jnp.min(jnp.where(gl == gmax, lane - N_EXPERTS, LANES), axis=-1, keepdims=True)
    group_p = 1.0 / jnp.sum(jnp.exp(gl - gmax), axis=-1, keepdims=True)
    el = jnp.where((lane < N_EXPERTS) & ((lane >> 3) == gidx), logits, ninf)
    m1 = jnp.max(el, axis=-1, keepdims=True)
    i1 = jnp.min(jnp.where(el == m1, lane, LANES), axis=-1, keepdims=True)
    el2 = jnp.where(lane == i1, ninf, el)
    m2 = jnp.max(el2, axis=-1, keepdims=True)
    i2 = jnp.min(jnp.where(el2 == m2, lane, LANES), axis=-1, keepdims=True)
    e2 = jnp.exp(m2 - m1)
    w1 = group_p / (1.0 + e2)
    w2 = group_p * e2 / (1.0 + e2)
    ids_ref[...] = jnp.where(lane == 0, i1, jnp.where(lane == 1, i2, 0))
    rw_ref[...] = jnp.where(lane == 0, w1, jnp.where(lane == 1, w2, 0.0))

    oh0 = lane == i1
    oh1 = lane == i2
    live = jnp.where(i > 0, 1.0, 0.0)
    oh = (oh0.astype(jnp.float32) + oh1.astype(jnp.float32)) * live
    cum = jnp.dot(tri_ref[...], oh.astype(jnp.bfloat16), preferred_element_type=jnp.float32) + carry_ref[...]
    r0 = jnp.sum(jnp.where(oh0, cum, 0.0), axis=-1, keepdims=True)
    r1 = jnp.sum(jnp.where(oh1, cum, 0.0), axis=-1, keepdims=True)
    rank_ref[...] = jnp.where(lane == 0, r0, jnp.where(lane == 1, r1, 0.0)).astype(jnp.int32)
    carry_ref[...] = carry_ref[...] + jnp.sum(oh, axis=0, keepdims=True)
    cnt_ref[...] = carry_ref[...].astype(jnp.int32)


def _merge(x2, attn, y4, gates, w_glu, b_glu, w_a, w_s, w_o, norm_moe, w_router, b_router):
    t, d = x2.shape
    tm = TM_MERGE
    nt = t // tm
    n_slab = y4.shape[0]
    tri = (jnp.arange(tm)[None, :] < jnp.arange(tm)[:, None]).astype(jnp.bfloat16)
    full = lambda a: pl.BlockSpec(a.shape, lambda i: (0,) * a.ndim)
    cur = lambda i: (jnp.minimum(i, nt - 1), 0)
    prv = lambda i: (jnp.maximum(i - 1, 0), 0)
    row = lambda c, m=cur: pl.BlockSpec((tm, c), m)
    return pl.pallas_call(
        _merge_kernel,
        grid=(nt + 1,),
        in_specs=[
            row(d), row(attn.shape[1]),
            pl.BlockSpec((n_slab, tm, LANES), lambda i: (0, jnp.minimum(i, nt - 1), 0)),
            row(gates.shape[1]),
            full(w_glu), full(b_glu), full(w_a), full(w_s), full(w_o), full(norm_moe),
            full(w_router), full(b_router), full(tri),
        ],
        out_specs=[row(d), pl.BlockSpec((tm * ROW_SUB, LANES), cur), row(LANES, prv), row(LANES, prv),
                   row(LANES, prv), pl.BlockSpec((1, LANES), lambda i: (0, 0))],
        out_shape=[
            jax.ShapeDtypeStruct((t, d), jnp.float32),
            jax.ShapeDtypeStruct((t * ROW_SUB, LANES), jnp.uint32),
            jax.ShapeDtypeStruct((t, LANES), jnp.int32),
            jax.ShapeDtypeStruct((t, LANES), jnp.float32),
            jax.ShapeDtypeStruct((t, LANES), jnp.int32),
            jax.ShapeDtypeStruct((1, LANES), jnp.int32),
        ],
        scratch_shapes=[pltpu.VMEM((tm, LANES), jnp.float32), pltpu.VMEM((1, LANES), jnp.float32)],
        compiler_params=_cparams("arbitrary"),
        name="merge_router",
    )(x2, attn, y4, gates, w_glu, b_glu, w_a, w_s, w_o, norm_moe, w_router, b_router, tri)


def _inverse_kernel(pos_ref, src_ref):
    n_slots = pos_ref.shape[0]
    t = n_slots // TOP_K

    def body(tok, c):
        for k in range(TOP_K):
            src_ref[pos_ref[k * t + tok]] = tok * ROW_SUB
        return c

    lax.fori_loop(0, t, body, 0, unroll=8)


def _inverse(pos1d):
    n_slots = pos1d.shape[0]
    smem = pl.BlockSpec(memory_space=pltpu.SMEM)
    return pl.pallas_call(
        _inverse_kernel,
        in_specs=[smem],
        out_specs=smem,
        out_shape=jax.ShapeDtypeStruct((n_slots,), jnp.int32),
        name="route_inverse",
    )(pos1d)


def _expert_kernel(ta_ref, tn_ref, tnew_ref, sexp_ref, slo_ref, shi_ref, meta_ref, src_ref,
                   h_ref, wg_hbm, wu_hbm, wd_hbm, y_ref,
                   xbuf_ref, wgs_ref, wus_ref, wds_ref, wgb_ref, wub_ref, wdb_ref, gsem, wsem):
    j = pl.program_id(0)
    n_tiles = pl.num_programs(0)
    last = n_tiles - 1
    tm = TM_EXPERT
    rs = ROW_SUB
    nb = ROW_BUFS
    n_used = meta_ref[0]
    slot = j % nb
    a = ta_ref[j]

    def gather(tile, sl, start):
        _row_gather(h_ref, src_ref, tile * tm, xbuf_ref, sl * tm, tm, gsem.at[sl], start)

    def weights(m, start):
        sl = m % 2
        e = sexp_ref[m]
        for hbm, stage in ((wg_hbm, wgs_ref), (wu_hbm, wus_ref), (wd_hbm, wds_ref)):
            cp = pltpu.make_async_copy(hbm.at[e], stage.at[sl], wsem.at[sl])
            cp.start() if start else cp.wait()

    def new_expert(m):
        weights(m, False)
        sl = m % 2
        wgb_ref[...] = wgs_ref[sl].astype(jnp.bfloat16)
        wub_ref[...] = wus_ref[sl].astype(jnp.bfloat16)
        wdb_ref[...] = wds_ref[sl].astype(jnp.bfloat16)

        @pl.when(m + 1 < n_used)
        def _():
            weights(m + 1, True)

    def expert_out(xp):
        x = _unpack_bf16_pairs(xp).astype(jnp.bfloat16)
        hg = jnp.dot(x, wgb_ref[...], preferred_element_type=jnp.float32)
        hu = jnp.dot(x, wub_ref[...], preferred_element_type=jnp.float32)
        act = (jax.nn.silu(hg) * hu).astype(jnp.bfloat16)
        y = jnp.dot(act, wdb_ref[...], preferred_element_type=jnp.float32)
        return _pack_bf16_pairs(y.astype(jnp.bfloat16).astype(jnp.float32))

    @pl.when(j == 0)
    def _():
        weights(0, True)
        gather(0, 0, True)
        gather(1, 1, True)

    @pl.when(tnew_ref[j] == 1)
    def _():
        new_expert(a)

    gather(j, slot, False)
    xp = _load_rows(xbuf_ref, slot * tm, tm)
    gather(jnp.minimum(j + 2, last), (j + 2) % nb, True)
    _store_rows(y_ref, 0, expert_out(xp))

    def extra(i, carry):
        m = a + i
        new_expert(m)
        yp = expert_out(_load_rows(xbuf_ref, slot * tm, tm))
        row = lax.broadcasted_iota(jnp.int32, (tm, LANES), 0) + j * tm
        mine = (row >= slo_ref[m]) & (row < shi_ref[m])
        _store_rows(y_ref, 0, yp, mine)
        return carry

    lax.fori_loop(1, tn_ref[j], extra, 0)

    @pl.when(j == last)
    def _():
        gather(j, (j + 1) % nb, False)
        gather(j, (j + 2) % nb, False)


def _experts(meta, src_tok, h_fat, w_gate, w_up, w_down):
    n_slots = src_tok.shape[0]
    ne, d, dff = w_gate.shape
    tm = TM_EXPERT
    rs = ROW_SUB
    assert n_slots // tm >= ROW_BUFS
    any_spec = pl.BlockSpec(memory_space=pl.ANY)
    grid_spec = pltpu.PrefetchScalarGridSpec(
        num_scalar_prefetch=8,
        grid=(n_slots // tm,),
        in_specs=[any_spec, any_spec, any_spec, any_spec],
        out_specs=pl.BlockSpec((tm * rs, LANES), lambda j, *_: (j, 0)),
        scratch_shapes=[
            pltpu.VMEM((ROW_BUFS * tm * rs, LANES), jnp.uint32),
            pltpu.VMEM((2, d, dff), jnp.float32),
            pltpu.VMEM((2, d, dff), jnp.float32),
            pltpu.VMEM((2, dff, d), jnp.float32),
            pltpu.VMEM((d, dff), jnp.bfloat16),
            pltpu.VMEM((d, dff), jnp.bfloat16),
            pltpu.VMEM((dff, d), jnp.bfloat16),
            pltpu.SemaphoreType.DMA((ROW_BUFS,)),
            pltpu.SemaphoreType.DMA((2,)),
        ],
    )
    return pl.pallas_call(
        _expert_kernel,
        grid_spec=grid_spec,
        out_shape=jax.ShapeDtypeStruct((n_slots * rs, LANES), jnp.uint32),
        compiler_params=_cparams("arbitrary"),
        name="moe_experts",
    )(*meta, src_tok, h_fat, w_gate, w_up, w_down)


def _combine_kernel(pos_ref, x1_ref, rw_ref, nf_ref, ys_ref, o_ref, buf_ref, sem):
    i = pl.program_id(0)
    n = pl.num_programs(0)
    last = n - 1
    tm = x1_ref.shape[0]
    t = n * tm
    nb = ROW_BUFS
    slot = i % nb

    def gather(tile, sl, start):
        for k in range(TOP_K):
            _row_gather(ys_ref, pos_ref, k * t + tile * tm, buf_ref, (sl * TOP_K + k) * tm, tm, sem.at[sl], start)

    @pl.when(i == 0)
    def _():
        gather(0, 0, True)
        gather(1, 1, True)

    gather(i, slot, False)
    y0 = _unpack_bf16_pairs(_load_rows(buf_ref, (slot * TOP_K) * tm, tm))
    y1 = _unpack_bf16_pairs(_load_rows(buf_ref, (slot * TOP_K + 1) * tm, tm))
    gather(jnp.minimum(i + 2, last), (i + 2) % nb, True)
    rw = rw_ref[...]
    x = x1_ref[...] + rw[:, 0:1] * y0 + rw[:, 1:2] * y1
    inv = lax.rsqrt(jnp.mean(x * x, axis=-1, keepdims=True) + EPS)
    o_ref[...] = x * inv * nf_ref[...]

    @pl.when(i == last)
    def _():
        gather(i, (i + 1) % nb, False)
        gather(i, (i + 2) % nb, False)


def _combine(pos_rows, x1, rw, norm_final, ys):
    t, d = x1.shape
    tm = TM_MOVE
    assert t // tm >= ROW_BUFS
    return pl.pallas_call(
        _combine_kernel,
        grid=(t // tm,),
        in_specs=[
            pl.BlockSpec(memory_space=pltpu.SMEM),
            pl.BlockSpec((tm, d), lambda i: (i, 0)),
            pl.BlockSpec((tm, LANES), lambda i: (i, 0)),
            pl.BlockSpec((1, d), lambda i: (0, 0)),
            pl.BlockSpec(memory_space=pl.ANY),
        ],
        out_specs=pl.BlockSpec((tm, d), lambda i: (i, 0)),
        out_shape=jax.ShapeDtypeStruct((t, d), jnp.float32),
        scratch_shapes=[pltpu.VMEM((ROW_BUFS * TOP_K * tm * ROW_SUB, LANES), jnp.uint32),
                        pltpu.SemaphoreType.DMA((ROW_BUFS,))],
        compiler_params=_cparams("arbitrary"),
        name="moe_combine",
    )(pos_rows, x1, rw, norm_final, ys)


def _expert_meta(counts, n_slots):
    tm = TM_EXPERT
    n_tiles = n_slots // tm
    i32 = jnp.int32
    ends = jnp.cumsum(counts).astype(i32)
    starts = ends - counts
    used = counts > 0
    n_used = jnp.sum(used).astype(i32)
    seq_of = jnp.cumsum(used).astype(i32) - 1
    m_idx = jnp.arange(N_EXPERTS, dtype=i32)
    pick = used[None, :] & (seq_of[None, :] == m_idx[:, None])
    s_exp = jnp.sum(jnp.where(pick, m_idx[None, :], 0), axis=1).astype(i32)
    s_lo = jnp.sum(jnp.where(pick, starts[None, :], 0), axis=1).astype(i32)
    s_hi = jnp.sum(jnp.where(pick, ends[None, :], 0), axis=1).astype(i32)
    row0 = jnp.arange(n_tiles, dtype=i32)[:, None] * tm
    t_a = jnp.sum(used[None, :] & (ends[None, :] <= row0), axis=1).astype(i32)
    t_b = jnp.sum(used[None, :] & (starts[None, :] < row0 + tm), axis=1).astype(i32) - 1
    t_new = jnp.any(used[None, :] & (starts[None, :] == row0), axis=1).astype(i32)
    return t_a, (t_b - t_a + 1).astype(i32), t_new, s_exp, s_lo, s_hi, n_used.reshape(1)


def _moe(x1, h_fat, ids, rw, rank, cnt, w_gate, w_up, w_down, norm_final):
    t, d = x1.shape
    counts = cnt[0, :N_EXPERTS]
    offs = (jnp.cumsum(counts) - counts).astype(jnp.int32)
    ids2 = ids[:, :TOP_K].T
    onehot = ids2[:, :, None] == jnp.arange(N_EXPERTS, dtype=jnp.int32)
    pos1d = (jnp.sum(jnp.where(onehot, offs, 0), axis=-1) + rank[:, :TOP_K].T).reshape(-1)
    src_tok = _inverse(pos1d)
    meta = _expert_meta(counts, t * TOP_K)
    ys = _experts(meta, src_tok, h_fat, w_gate, w_up, w_down)
    return _combine(pos1d * ROW_SUB, x1, rw, norm_final, ys)


def kernel(x, norm_mix, w_in, b_gate, attn_sinks, ssm_a_re, ssm_a_im, ssm_b_re, ssm_b_im, ssm_c_re, ssm_c_im, ssm_d, ssm_log_dt, w_glu, b_glu, w_attn_branch, w_ssm_branch, w_out, norm_moe, w_router_group, b_router_group, w_router_expert, b_router_expert, w_expert_gate, w_expert_up, w_expert_down, norm_final):
    b, l, d = x.shape
    depth = w_in.shape[0]
    assert depth == 1, "the final norm is fused into the last layer's combine kernel"
    d_attn = N_HEADS * HEAD_DIM
    kv_cols = N_KV_HEADS * HEAD_DIM
    d_ssm = ssm_d.shape[-1]
    bf16 = jnp.bfloat16
    x2 = x.reshape(b * l, d)
    assert TM_PROJ == TM_SSM == TM_MERGE and l % TM_SSM == 0
    i = 0
    q, k, v, u4, gates = _in_proj(x2, norm_mix[i][None], w_in[i].astype(bf16), b_gate[i][None],
                                  d_attn, kv_cols, d_ssm)
    attn = _attention(q, k, v, attn_sinks[i], b, l)
    tables = _ssm_tables(ssm_a_re[i], ssm_a_im[i], ssm_b_re[i], ssm_b_im[i], ssm_c_re[i], ssm_c_im[i],
                         ssm_d[i], ssm_log_dt[i], SSM_SEG)
    y4 = _ssm(u4, tables, b, l)
    pad = LANES - N_EXPERTS - N_EXPERT_GROUPS
    w_router = jnp.concatenate([w_router_expert[i], w_router_group[i], jnp.zeros((d, pad), jnp.float32)], axis=1)
    w_r_hi = w_router.astype(bf16)
    w_router = jnp.concatenate([w_r_hi, (w_router - w_r_hi.astype(jnp.float32)).astype(bf16)], axis=1)
    b_router = jnp.concatenate([b_router_expert[i], b_router_group[i], jnp.zeros((pad,), jnp.float32)])[None]
    x1, h_fat, ids, rw, rank, cnt = _merge(x2, attn, y4, gates, w_glu[i].astype(bf16), b_glu[i][None],
                                w_attn_branch[i].astype(bf16), w_ssm_branch[i].astype(bf16),
                                w_out[i].astype(bf16), norm_moe[i][None], w_router, b_router)
    out = _moe(x1, h_fat, ids, rw, rank, cnt, w_expert_gate[i], w_expert_up[i], w_expert_down[i], norm_final[None])
    return out.reshape(b, l, d)
```

```python
import functools
import math

import jax
import jax.numpy as jnp
from jax import lax
from jax.experimental import pallas as pl
from jax.experimental.pallas import tpu as pltpu

EPS = 1e-6
HEAD_DIM = 64
N_HEADS = 8
N_KV_HEADS = 2
Q_PER_KV = N_HEADS // N_KV_HEADS
ATTN_BLOCK = 128
ATTN_QB = 2
SSM_GROUP = 16
SSM_STATE = 64
N_EXPERT_GROUPS = 4
EXPERTS_PER_GROUP = 8
N_EXPERTS = N_EXPERT_GROUPS * EXPERTS_PER_GROUP
TOP_K = 2

LANES = 128
SUBLANES = 8
SSM_CHUNK_GROUPS = LANES // SSM_GROUP
SSM_CHUNK_STATES = SSM_CHUNK_GROUPS * SSM_STATE

TM_PROJ = 512
TM_SSM = 512
SSM_SEG = TM_SSM // SUBLANES
TM_MERGE = 512
TM_MOVE = 256
TM_EXPERT = 256
ROW_SUB = 4
ROW_BUFS = 3
VMEM_LIMIT = 56 * 1024 * 1024


def _cparams(*sem):
    return pltpu.CompilerParams(dimension_semantics=sem, vmem_limit_bytes=VMEM_LIMIT)


def _pack_bf16_pairs(x):
    half = x.shape[1] // 2
    bits = lax.bitcast_convert_type(x, jnp.uint32)
    return (bits[:, :half] & jnp.uint32(0xFFFF0000)) | (bits[:, half:] >> 16)


def _unpack_bf16_pairs(p):
    hi = lax.bitcast_convert_type(p & jnp.uint32(0xFFFF0000), jnp.float32)
    lo = lax.bitcast_convert_type(p << 16, jnp.float32)
    return jnp.concatenate([hi, lo], axis=1)


def _load_rows(ref, first, n):
    return jnp.concatenate(
        [ref[pl.ds(first * ROW_SUB + c, n, stride=ROW_SUB), :] for c in range(ROW_SUB)], axis=1)


def _store_rows(ref, first, val, mask=None):
    n = val.shape[0]
    for c in range(ROW_SUB):
        idx = pl.ds(first * ROW_SUB + c, n, stride=ROW_SUB)
        v = val[:, c * LANES:(c + 1) * LANES]
        ref[idx, :] = v if mask is None else jnp.where(mask, v, ref[idx, :])


def _row_gather(src_hbm, idx_ref, idx0, dst_ref, dst0, n, sem, start):
    rs = ROW_SUB
    if start:
        for r in range(n):
            pltpu.make_async_copy(src_hbm.at[pl.ds(pl.multiple_of(idx_ref[idx0 + r], rs), rs)],
                                  dst_ref.at[pl.ds(pl.multiple_of((dst0 + r) * rs, rs), rs)],
                                  sem).start(priority=r % 2)
    else:
        pltpu.make_async_copy(src_hbm.at[pl.ds(0, n * rs)],
                              dst_ref.at[pl.ds(pl.multiple_of(dst0 * rs, rs), n * rs)], sem).wait()


def _proj_kernel(x_ref, g_ref, w_ref, bg_ref, q_ref, k_ref, v_ref, u_ref, gate_ref, *, cols):
    q_c, kv_c, d_ssm = cols
    xf = x_ref[...]
    inv = lax.rsqrt(jnp.mean(xf * xf, axis=-1, keepdims=True) + EPS)
    h = (xf * inv * g_ref[...]).astype(jnp.bfloat16)
    o = 0
    q_ref[...] = (jnp.dot(h, w_ref[:, o:o + q_c], preferred_element_type=jnp.float32)
                  * (1.0 / math.sqrt(HEAD_DIM))).astype(q_ref.dtype)
    o += q_c
    k_ref[...] = jnp.dot(h, w_ref[:, o:o + kv_c], preferred_element_type=jnp.float32).astype(k_ref.dtype)
    o += kv_c
    v_ref[...] = jnp.dot(h, w_ref[:, o:o + kv_c], preferred_element_type=jnp.float32).astype(v_ref.dtype)
    o += kv_c
    seg = x_ref.shape[0] // SUBLANES
    for s in range(0, d_ssm // LANES, 2):
        uu = jnp.dot(h, w_ref[:, o:o + 2 * LANES], preferred_element_type=jnp.float32)
        for half in range(2):
            for j in range(SUBLANES):
                u_ref[s + half, pl.ds(j, seg, stride=SUBLANES), :] = (
                    uu[j * seg:(j + 1) * seg, half * LANES:(half + 1) * LANES])
        o += 2 * LANES
    gl = jnp.dot(h, w_ref[:, o:], preferred_element_type=jnp.float32) + bg_ref[...]
    gate_ref[...] = jax.nn.sigmoid(gl).astype(gate_ref.dtype)


def _in_proj(x2, norm_w, w_in, b_gate, d_attn, kv_cols, d_ssm):
    t, d = x2.shape
    gate_cols = b_gate.shape[-1]
    n_slab = d_ssm // LANES
    tm = TM_PROJ
    kern = functools.partial(_proj_kernel, cols=(d_attn, kv_cols, d_ssm))
    return pl.pallas_call(
        kern,
        grid=(t // tm,),
        in_specs=[
            pl.BlockSpec((tm, d), lambda i: (i, 0)),
            pl.BlockSpec((1, d), lambda i: (0, 0)),
            pl.BlockSpec(w_in.shape, lambda i: (0, 0)),
            pl.BlockSpec((1, gate_cols), lambda i: (0, 0)),
        ],
        out_specs=[
            pl.BlockSpec((tm, d_attn), lambda i: (i, 0)),
            pl.BlockSpec((tm, kv_cols), lambda i: (i, 0)),
            pl.BlockSpec((tm, kv_cols), lambda i: (i, 0)),
            pl.BlockSpec((n_slab, tm, LANES), lambda i: (0, i, 0)),
            pl.BlockSpec((tm, gate_cols), lambda i: (i, 0)),
        ],
        out_shape=[
            jax.ShapeDtypeStruct((t, d_attn), jnp.bfloat16),
            jax.ShapeDtypeStruct((t, kv_cols), jnp.bfloat16),
            jax.ShapeDtypeStruct((t, kv_cols), jnp.bfloat16),
            jax.ShapeDtypeStruct((n_slab, t, LANES), jnp.float32),
            jax.ShapeDtypeStruct((t, gate_cols), jnp.bfloat16),
        ],
        compiler_params=_cparams("arbitrary"),
        name="in_proj",
    )(x2, norm_w, w_in, b_gate)


def _attn_kernel(sink_ref, q_ref, kp_ref, kc_ref, vp_ref, vc_ref, rep_ref, mask_ref, o_ref):
    i = pl.program_id(1)
    blk = ATTN_BLOCK
    hw = Q_PER_KV * HEAD_DIM
    rows = Q_PER_KV * blk
    head_of_row = lax.broadcasted_iota(jnp.int32, (rows, 1), 0) >> 7
    lane_head_q = lax.broadcasted_iota(jnp.int32, (blk, hw), 1) >> 6
    lane_head_v = lax.broadcasted_iota(jnp.int32, (2 * blk, hw), 1) >> 6
    sinks = []
    for kh in range(N_KV_HEADS):
        sink = jnp.zeros((rows, 1), jnp.float32)
        for g in range(Q_PER_KV):
            sink = jnp.where(head_of_row == g, sink_ref[kh * Q_PER_KV + g], sink)
        sinks.append(sink)
    for qb in range(ATTN_QB):
        if qb == 0:
            k2 = jnp.concatenate([kp_ref[...], kc_ref[0:blk, :]], axis=0)
            v2 = jnp.concatenate([vp_ref[...], vc_ref[0:blk, :]], axis=0)
            bias = mask_ref[jnp.where(i == 0, 1, 0)]
        else:
            k2 = kc_ref[(qb - 1) * blk:(qb + 1) * blk, :]
            v2 = vc_ref[(qb - 1) * blk:(qb + 1) * blk, :]
            bias = mask_ref[0]
        for kh in range(N_KV_HEADS):
            rep = rep_ref[kh]
            k4 = jnp.dot(k2, rep, preferred_element_type=jnp.float32).astype(jnp.bfloat16)
            v4 = jnp.dot(v2, rep, preferred_element_type=jnp.float32).astype(jnp.bfloat16)
            qh = q_ref[qb * blk:(qb + 1) * blk, kh * hw:(kh + 1) * hw]
            qm = jnp.concatenate(
                [jnp.where(lane_head_q == g, qh, jnp.zeros_like(qh)) for g in range(Q_PER_KV)], axis=0)
            s = lax.dot_general(qm, k4, (((1,), (1,)), ((), ())), preferred_element_type=jnp.float32)
            s = s + bias
            sink = sinks[kh]
            m = jnp.maximum(jnp.max(s, axis=-1, keepdims=True), sink)
            p = jnp.exp(s - m)
            rinv = 1.0 / (jnp.sum(p, axis=-1, keepdims=True) + jnp.exp(sink - m))
            p = p.astype(jnp.bfloat16)
            p_cat = jnp.concatenate([p[g * blk:(g + 1) * blk, :] for g in range(Q_PER_KV)], axis=1)
            vm = jnp.concatenate(
                [jnp.where(lane_head_v == g, v4, jnp.zeros_like(v4)) for g in range(Q_PER_KV)], axis=0)
            o = jnp.dot(p_cat, vm, preferred_element_type=jnp.float32)
            scale = jnp.zeros((blk, hw), jnp.float32)
            for g in range(Q_PER_KV):
                scale = jnp.where(lane_head_q == g, rinv[g * blk:(g + 1) * blk, :], scale)
            o_ref[qb * blk:(qb + 1) * blk, kh * hw:(kh + 1) * hw] = (o * scale).astype(o_ref.dtype)


def _attention(q, k, v, sinks, b, l):
    d_attn = q.shape[-1]
    kv_cols = k.shape[-1]
    blk = ATTN_BLOCK
    tq = ATTN_QB * blk
    hw = Q_PER_KV * HEAD_DIM
    lane = jnp.arange(hw)[None, :]
    src = jnp.arange(kv_cols)[:, None]
    rep = jnp.stack([(src == kh * HEAD_DIM + (lane % HEAD_DIM)) for kh in range(N_KV_HEADS)]
                    ).astype(jnp.bfloat16)
    r = (jnp.arange(Q_PER_KV * blk) % blk)[:, None]
    c = jnp.arange(2 * blk)[None, :]
    band = (c > r) & (c <= r + blk)
    mask = jnp.where(jnp.stack([band, band & (c >= blk)]), 0.0, jnp.finfo(jnp.float32).min).astype(jnp.float32)
    q3 = q.reshape(b, l, d_attn)
    k3 = k.reshape(b, l, kv_cols)
    v3 = v.reshape(b, l, kv_cols)
    cur = lambda bi, i: (bi, i, 0)
    prev = lambda bi, i: (bi, jnp.maximum(ATTN_QB * i - 1, 0), 0)
    out = pl.pallas_call(
        _attn_kernel,
        grid=(b, l // tq),
        in_specs=[
            pl.BlockSpec(memory_space=pltpu.SMEM),
            pl.BlockSpec((None, tq, d_attn), cur),
            pl.BlockSpec((None, blk, kv_cols), prev),
            pl.BlockSpec((None, tq, kv_cols), cur),
            pl.BlockSpec((None, blk, kv_cols), prev),
            pl.BlockSpec((None, tq, kv_cols), cur),
            pl.BlockSpec(rep.shape, lambda bi, i: (0, 0, 0)),
            pl.BlockSpec(mask.shape, lambda bi, i: (0, 0, 0)),
        ],
        out_specs=pl.BlockSpec((None, tq, d_attn), cur),
        out_shape=jax.ShapeDtypeStruct((b, l, d_attn), jnp.bfloat16),
        compiler_params=_cparams("arbitrary", "arbitrary"),
        name="swa",
    )(sinks, q3, k3, k3, v3, v3, rep, mask)
    return out.reshape(b * l, d_attn)


def _ssm_kernel(u_ref, bmat_ref, cmat_ref, lam_ref, pw_ref, lamseg_ref, d_ref, y_ref, bu_ref, carry_ref):
    i = pl.program_id(1)
    n_slab = u_ref.shape[0]
    tm = u_ref.shape[1]
    seg = tm // SUBLANES
    ns = SSM_CHUNK_STATES
    npair = 2

    @pl.when(i == 0)
    def _():
        carry_ref[...] = jnp.zeros_like(carry_ref)

    sub = lax.broadcasted_iota(jnp.int32, (SUBLANES, ns), 0)
    for s0 in range(0, n_slab, npair):
        lam = []
        for q in range(npair):
            s = s0 + q
            bu_ref[q] = jnp.dot(u_ref[s].astype(jnp.bfloat16), bmat_ref[s], preferred_element_type=jnp.float32)
            lam.append((jnp.broadcast_to(lam_ref[s, 0:1, :], (SUBLANES, ns)),
                        jnp.broadcast_to(lam_ref[s, 1:2, :], (SUBLANES, ns))))

        def step(r, st):
            rows = pl.ds(pl.multiple_of(r * SUBLANES, SUBLANES), SUBLANES)
            out = []
            for q in range(npair):
                lr, li = lam[q]
                sr, si = st[q]
                nr = lr * sr - li * si + bu_ref[q, rows, 0:ns]
                ni = lr * si + li * sr + bu_ref[q, rows, ns:2 * ns]
                bu_ref[q, rows, 0:ns] = nr
                bu_ref[q, rows, ns:2 * ns] = ni
                out.append((nr, ni))
            return tuple(out)

        zero = jnp.zeros((SUBLANES, ns), jnp.float32)
        ends = lax.fori_loop(0, seg, step, ((zero, zero),) * npair, unroll=2)

        for q in range(npair):
            s = s0 + q
            er, ei = ends[q]
            ar = lamseg_ref[s, 0:1, :]
            ai = lamseg_ref[s, 1:2, :]
            cr = carry_ref[s, 0:1, :]
            ci = carry_ref[s, 1:2, :]
            car = jnp.zeros((SUBLANES, ns), jnp.float32)
            cai = jnp.zeros((SUBLANES, ns), jnp.float32)
            for j in range(SUBLANES):
                car = jnp.where(sub == j, jnp.broadcast_to(cr, (SUBLANES, ns)), car)
                cai = jnp.where(sub == j, jnp.broadcast_to(ci, (SUBLANES, ns)), cai)
                ejr = jnp.sum(jnp.where(sub == j, er, 0.0), axis=0, keepdims=True)
                eji = jnp.sum(jnp.where(sub == j, ei, 0.0), axis=0, keepdims=True)
                cr, ci = ar * cr - ai * ci + ejr, ar * ci + ai * cr + eji
            carry_ref[s, 0:1, :] = cr
            carry_ref[s, 1:2, :] = ci

            ctr = jnp.broadcast_to(car[None], (seg, SUBLANES, ns)).reshape(tm, ns)
            cti = jnp.broadcast_to(cai[None], (seg, SUBLANES, ns)).reshape(tm, ns)
            pr = pw_ref[s, 0]
            pi = pw_ref[s, 1]
            st_r = bu_ref[q, :, 0:ns] + pr * ctr - pi * cti
            st_i = bu_ref[q, :, ns:2 * ns] + pr * cti + pi * ctr
            st = jnp.concatenate([st_r, st_i], axis=1).astype(jnp.bfloat16)
            y_ref[s] = jnp.dot(st, cmat_ref[s], preferred_element_type=jnp.float32) + d_ref[s] * u_ref[s]


def _ssm_tables(a_re, a_im, b_re, b_im, c_re, c_im, d_skip, log_dt, seg):
    f32 = jnp.float32
    g, p = a_re.shape
    c = b_re.shape[-1]
    ng = SSM_CHUNK_GROUPS
    n_slab = g // ng
    lam = lax.complex(a_re.astype(f32), a_im.astype(f32))
    dt = jnp.exp(log_dt.astype(f32))[:, None]
    lam_bar = jnp.exp(lam * dt)
    b_bar = ((lam_bar - 1.0) / lam)[:, :, None] * lax.complex(b_re.astype(f32), b_im.astype(f32))
    c_mat = lax.complex(c_re.astype(f32), c_im.astype(f32))
    eye = jnp.eye(ng, dtype=f32)

    def bdiag_b(m):
        m = m.reshape(n_slab, ng, p, c)
        return jnp.einsum('ab,kbpc->kacbp', eye, m).reshape(n_slab, ng * c, ng * p)

    def bdiag_c(m):
        m = m.reshape(n_slab, ng, c, p)
        return jnp.einsum('ab,kbcp->kbpac', eye, m).reshape(n_slab, ng * p, ng * c)

    bmat = jnp.concatenate([bdiag_b(jnp.real(b_bar)), bdiag_b(jnp.imag(b_bar))], axis=2).astype(jnp.bfloat16)
    cmat = jnp.concatenate([bdiag_c(jnp.real(c_mat)), -bdiag_c(jnp.imag(c_mat))], axis=1).astype(jnp.bfloat16)

    def slab_rows(z):
        z = z.reshape(n_slab, 1, ng * p)
        return jnp.concatenate([jnp.real(z), jnp.imag(z)], axis=1)

    lam_t = slab_rows(lam_bar)
    steps = jnp.arange(1, seg + 1, dtype=f32)
    pw = jnp.exp((lam * dt)[None] * steps[:, None, None])
    pw = pw.reshape(seg, n_slab, ng * p).transpose(1, 0, 2)
    pw = jnp.repeat(pw, SUBLANES, axis=1)
    pw_t = jnp.stack([jnp.real(pw), jnp.imag(pw)], axis=1)
    lamseg_t = slab_rows(jnp.exp(lam * dt * float(seg)))
    d_t = d_skip.astype(f32).reshape(n_slab, 1, ng * c)
    return bmat, cmat, lam_t, pw_t, lamseg_t, d_t


def _ssm(u4, tables, b, l):
    bmat, cmat, lam_t, pw_t, lamseg_t, d_t = tables
    n_slab, t, _ = u4.shape
    tm = TM_SSM
    nt = l // tm
    ns = SSM_CHUNK_STATES
    const = lambda nd: (lambda bi, i: (0,) * nd)
    return pl.pallas_call(
        _ssm_kernel,
        grid=(b, nt),
        in_specs=[
            pl.BlockSpec((n_slab, tm, LANES), lambda bi, i: (0, bi * nt + i, 0)),
            pl.BlockSpec(bmat.shape, const(3)),
            pl.BlockSpec(cmat.shape, const(3)),
            pl.BlockSpec(lam_t.shape, const(3)),
            pl.BlockSpec(pw_t.shape, const(4)),
            pl.BlockSpec(lamseg_t.shape, const(3)),
            pl.BlockSpec(d_t.shape, const(3)),
        ],
        out_specs=pl.BlockSpec((n_slab, tm, LANES), lambda bi, i: (0, bi * nt + i, 0)),
        out_shape=jax.ShapeDtypeStruct((n_slab, t, LANES), jnp.float32),
        scratch_shapes=[
            pltpu.VMEM((2, tm, 2 * ns), jnp.float32),
            pltpu.VMEM((n_slab, 2, ns), jnp.float32),
        ],
        compiler_params=_cparams("arbitrary", "arbitrary"),
        name="s5_scan",
    )(u4, bmat, cmat, lam_t, pw_t, lamseg_t, d_t)


def _merge_kernel(x_ref, attn_ref, y_ref, gate_ref, wglu_ref, bglu_ref, wa_ref, ws_ref, wo_ref,
                  nm_ref, wr_ref, br_ref, tri_ref, x1_ref, h2_ref, ids_ref, rw_ref, rank_ref, cnt_ref,
                  lg_ref, carry_ref):
    i = pl.program_id(0)
    d = x_ref.shape[1]
    tm = x_ref.shape[0]

    @pl.when(i == 0)
    def _():
        lg_ref[...] = jnp.zeros_like(lg_ref)
        carry_ref[...] = jnp.zeros_like(carry_ref)

    logits = lg_ref[...]

    seg = tm // SUBLANES
    y = jnp.concatenate(
        [jnp.concatenate([y_ref[s, pl.ds(j, seg, stride=SUBLANES), :] for j in range(SUBLANES)], axis=0)
         for s in range(y_ref.shape[0])], axis=1)
    z = jax.nn.gelu(y)
    zg = jnp.dot(z.astype(jnp.bfloat16), wglu_ref[...], preferred_element_type=jnp.float32) + bglu_ref[...]
    z = z * jax.nn.sigmoid(zg)
    a = jnp.dot(attn_ref[...], wa_ref[...], preferred_element_type=jnp.float32)
    sb = jnp.dot(z.astype(jnp.bfloat16), ws_ref[...], preferred_element_type=jnp.float32)
    merged = gate_ref[:, 0:d].astype(jnp.float32) * a + gate_ref[:, d:2 * d].astype(jnp.float32) * sb
    x1 = x_ref[...] + jnp.dot(merged.astype(jnp.bfloat16), wo_ref[...], preferred_element_type=jnp.float32)
    x1_ref[...] = x1
    inv = lax.rsqrt(jnp.mean(x1 * x1, axis=-1, keepdims=True) + EPS)
    h2 = x1 * inv * nm_ref[...]
    h_hi = h2.astype(jnp.bfloat16)
    h_pk = _pack_bf16_pairs(h_hi.astype(jnp.float32))
    _store_rows(h2_ref, 0, h_pk)

    h_lo = (h2 - h_hi.astype(jnp.float32)).astype(jnp.bfloat16)
    hh = jnp.dot(h_hi, wr_ref[...], preferred_element_type=jnp.float32)
    lh = jnp.dot(h_lo, wr_ref[:, 0:LANES], preferred_element_type=jnp.float32)
    lg_ref[...] = hh[:, 0:LANES] + hh[:, LANES:2 * LANES] + lh + br_ref[...]

    lane = lax.broadcasted_iota(jnp.int32, (tm, LANES), 1)
    ninf = -jnp.inf
    gl = jnp.where((lane >= N_EXPERTS) & (lane < N_EXPERTS + N_EXPERT_GROUPS), logits, ninf)
    gmax = jnp.max(gl, axis=-1, keepdims=True)
    gidx = jnp.min(jnp.where(gl == gmax, lane - N_EXPERTS, LANES), axis=-1, keepdims=True)
    group_p = 1.0 / jnp.sum(jnp.exp(gl - gmax), axis=-1, keepdims=True)
    el = jnp.where((lane < N_EXPERTS) & ((lane >> 3) == gidx), logits, ninf)
    m1 = jnp.max(el, axis=-1, keepdims=True)
    i1 = jnp.min(jnp.where(el == m1, lane, LANES), axis=-1, keepdims=True)
    el2 = jnp.where(lane == i1, ninf, el)
    m2 = jnp.max(el2, axis=-1, keepdims=True)
    i2 = jnp.min(jnp.where(el2 == m2, lane, LANES), axis=-1, keepdims=True)
    e2 = jnp.exp(m2 - m1)
    w1 = group_p / (1.0 + e2)
    w2 = group_p * e2 / (1.0 + e2)
    ids_ref[...] = jnp.where(lane == 0, i1, jnp.where(lane == 1, i2, 0))
    rw_ref[...] = jnp.where(lane == 0, w1, jnp.where(lane == 1, w2, 0.0))

    oh0 = lane == i1
    oh1 = lane == i2
    live = jnp.where(i > 0, 1.0, 0.0)
    oh = (oh0.astype(jnp.float32) + oh1.astype(jnp.float32)) * live
    cum = jnp.dot(tri_ref[...], oh.astype(jnp.bfloat16), preferred_element_type=jnp.float32) + carry_ref[...]
    r0 = jnp.sum(jnp.where(oh0, cum, 0.0), axis=-1, keepdims=True)
    r1 = jnp.sum(jnp.where(oh1, cum, 0.0), axis=-1, keepdims=True)
    rank_ref[...] = jnp.where(lane == 0, r0, jnp.where(lane == 1, r1, 0.0)).astype(jnp.int32)
    carry_ref[...] = carry_ref[...] + jnp.sum(oh, axis=0, keepdims=True)
    cnt_ref[...] = carry_ref[...].astype(jnp.int32)


def _merge(x2, attn, y4, gates, w_glu, b_glu, w_a, w_s, w_o, norm_moe, w_router, b_router):
    t, d = x2.shape
    tm = TM_MERGE
    nt = t // tm
    n_slab = y4.shape[0]
    tri = (jnp.arange(tm)[None, :] < jnp.arange(tm)[:, None]).astype(jnp.bfloat16)
    full = lambda a: pl.BlockSpec(a.shape, lambda i: (0,) * a.ndim)
    cur = lambda i: (jnp.minimum(i, nt - 1), 0)
    prv = lambda i: (jnp.maximum(i - 1, 0), 0)
    row = lambda c, m=cur: pl.BlockSpec((tm, c), m)
    return pl.pallas_call(
        _merge_kernel,
        grid=(nt + 1,),
        in_specs=[
            row(d), row(attn.shape[1]),
            pl.BlockSpec((n_slab, tm, LANES), lambda i: (0, jnp.minimum(i, nt - 1), 0)),
            row(gates.shape[1]),
            full(w_glu), full(b_glu), full(w_a), full(w_s), full(w_o), full(norm_moe),
            full(w_router), full(b_router), full(tri),
        ],
        out_specs=[row(d), pl.BlockSpec((tm * ROW_SUB, LANES), cur), row(LANES, prv), row(LANES, prv),
                   row(LANES, prv), pl.BlockSpec((1, LANES), lambda i: (0, 0))],
        out_shape=[
            jax.ShapeDtypeStruct((t, d), jnp.float32),
            jax.ShapeDtypeStruct((t * ROW_SUB, LANES), jnp.uint32),
            jax.ShapeDtypeStruct((t, LANES), jnp.int32),
            jax.ShapeDtypeStruct((t, LANES), jnp.float32),
            jax.ShapeDtypeStruct((t, LANES), jnp.int32),
            jax.ShapeDtypeStruct((1, LANES), jnp.int32),
        ],
        scratch_shapes=[pltpu.VMEM((tm, LANES), jnp.float32), pltpu.VMEM((1, LANES), jnp.float32)],
        compiler_params=_cparams("arbitrary"),
        name="merge_router",
    )(x2, attn, y4, gates, w_glu, b_glu, w_a, w_s, w_o, norm_moe, w_router, b_router, tri)


def _inverse_kernel(pos_ref, pad_lo_ref, pad_hi_ref, src_ref):
    n_slots = pos_ref.shape[0]
    t = n_slots // TOP_K

    def pad_expert(e, c):
        def pad_row(p, c2):
            src_ref[p] = 0
            return c2
        return lax.fori_loop(pad_lo_ref[e], pad_hi_ref[e], pad_row, c)

    lax.fori_loop(0, pad_lo_ref.shape[0], pad_expert, 0)

    def body(tok, c):
        for k in range(TOP_K):
            src_ref[pos_ref[k * t + tok]] = tok * ROW_SUB
        return c

    lax.fori_loop(0, t, body, 0, unroll=8)


def _inverse(pos1d, pad_lo, pad_hi, n_rows):
    smem = pl.BlockSpec(memory_space=pltpu.SMEM)
    return pl.pallas_call(
        _inverse_kernel,
        in_specs=[smem, smem, smem],
        out_specs=smem,
        out_shape=jax.ShapeDtypeStruct((n_rows,), jnp.int32),
        name="route_inverse",
    )(pos1d, pad_lo, pad_hi)


def _expert_kernel(te_ref, tnew_ref, tseq_ref, sexp_ref, meta_ref, src_ref,
                   h_ref, wg_hbm, wu_hbm, wd_hbm, y_ref,
                   xbuf_ref, wgs_ref, wus_ref, wds_ref, wgb_ref, wub_ref, wdb_ref, gsem, wsem):
    j = pl.program_id(0)
    tm = TM_EXPERT
    nb = ROW_BUFS
    n_used = meta_ref[0]
    last = meta_ref[1] - 1
    slot = j % nb

    def gather(tile, sl, start):
        _row_gather(h_ref, src_ref, tile * tm, xbuf_ref, sl * tm, tm, gsem.at[sl], start)

    def weights(m, start):
        sl = m % 2
        e = sexp_ref[m]
        for hbm, stage in ((wg_hbm, wgs_ref), (wu_hbm, wus_ref), (wd_hbm, wds_ref)):
            cp = pltpu.make_async_copy(hbm.at[e], stage.at[sl], wsem.at[sl])
            cp.start() if start else cp.wait()

    @pl.when(j == 0)
    def _():
        weights(0, True)
        gather(0, 0, True)
        gather(1, 1, True)

    @pl.when((j <= last) & (tnew_ref[j] == 1))
    def _():
        m = tseq_ref[j]
        weights(m, False)
        sl = m % 2
        wgb_ref[...] = wgs_ref[sl].astype(jnp.bfloat16)
        wub_ref[...] = wus_ref[sl].astype(jnp.bfloat16)
        wdb_ref[...] = wds_ref[sl].astype(jnp.bfloat16)

        @pl.when(m + 1 < n_used)
        def _():
            weights(m + 1, True)

    @pl.when(j <= last)
    def _():
        gather(j, slot, False)
        xp = _load_rows(xbuf_ref, slot * tm, tm)
        gather(jnp.minimum(j + 2, last), (j + 2) % nb, True)
        x = _unpack_bf16_pairs(xp).astype(jnp.bfloat16)
        hg = jnp.dot(x, wgb_ref[...], preferred_element_type=jnp.float32)
        hu = jnp.dot(x, wub_ref[...], preferred_element_type=jnp.float32)
        act = (jax.nn.silu(hg) * hu).astype(jnp.bfloat16)
        y = jnp.dot(act, wdb_ref[...], preferred_element_type=jnp.float32)
        _store_rows(y_ref, 0, _pack_bf16_pairs(y.astype(jnp.bfloat16).astype(jnp.float32)))

    @pl.when(j > last)
    def _():
        y_ref[...] = jnp.zeros_like(y_ref)

    @pl.when(j == last)
    def _():
        gather(j, (j + 1) % nb, False)
        gather(j, (j + 2) % nb, False)


def _experts(meta, src_tok, h_fat, w_gate, w_up, w_down):
    n_rows = src_tok.shape[0]
    ne, d, dff = w_gate.shape
    tm = TM_EXPERT
    rs = ROW_SUB
    any_spec = pl.BlockSpec(memory_space=pl.ANY)
    grid_spec = pltpu.PrefetchScalarGridSpec(
        num_scalar_prefetch=6,
        grid=(n_rows // tm,),
        in_specs=[any_spec, any_spec, any_spec, any_spec],
        out_specs=pl.BlockSpec((tm * rs, LANES), lambda j, *_: (j, 0)),
        scratch_shapes=[
            pltpu.VMEM((ROW_BUFS * tm * rs, LANES), jnp.uint32),
            pltpu.VMEM((2, d, dff), jnp.float32),
            pltpu.VMEM((2, d, dff), jnp.float32),
            pltpu.VMEM((2, dff, d), jnp.float32),
            pltpu.VMEM((d, dff), jnp.bfloat16),
            pltpu.VMEM((d, dff), jnp.bfloat16),
            pltpu.VMEM((dff, d), jnp.bfloat16),
            pltpu.SemaphoreType.DMA((ROW_BUFS,)),
            pltpu.SemaphoreType.DMA((2,)),
        ],
    )
    return pl.pallas_call(
        _expert_kernel,
        grid_spec=grid_spec,
        out_shape=jax.ShapeDtypeStruct((n_rows * rs, LANES), jnp.uint32),
        compiler_params=_cparams("arbitrary"),
        name="moe_experts",
    )(*meta, src_tok, h_fat, w_gate, w_up, w_down)


def _combine_kernel(pos_ref, x1_ref, rw_ref, nf_ref, ys_ref, o_ref, buf_ref, sem):
    i = pl.program_id(0)
    n = pl.num_programs(0)
    last = n - 1
    tm = x1_ref.shape[0]
    t = n * tm
    nb = ROW_BUFS
    slot = i % nb

    def gather(tile, sl, start):
        for k in range(TOP_K):
            _row_gather(ys_ref, pos_ref, k * t + tile * tm, buf_ref, (sl * TOP_K + k) * tm, tm, sem.at[sl], start)

    @pl.when(i == 0)
    def _():
        gather(0, 0, True)
        gather(1, 1, True)

    gather(i, slot, False)
    y0 = _unpack_bf16_pairs(_load_rows(buf_ref, (slot * TOP_K) * tm, tm))
    y1 = _unpack_bf16_pairs(_load_rows(buf_ref, (slot * TOP_K + 1) * tm, tm))
    gather(jnp.minimum(i + 2, last), (i + 2) % nb, True)
    rw = rw_ref[...]
    x = x1_ref[...] + rw[:, 0:1] * y0 + rw[:, 1:2] * y1
    inv = lax.rsqrt(jnp.mean(x * x, axis=-1, keepdims=True) + EPS)
    o_ref[...] = x * inv * nf_ref[...]

    @pl.when(i == last)
    def _():
        gather(i, (i + 1) % nb, False)
        gather(i, (i + 2) % nb, False)


def _combine(pos_rows, x1, rw, norm_final, ys):
    t, d = x1.shape
    tm = TM_MOVE
    assert t // tm >= ROW_BUFS
    return pl.pallas_call(
        _combine_kernel,
        grid=(t // tm,),
        in_specs=[
            pl.BlockSpec(memory_space=pltpu.SMEM),
            pl.BlockSpec((tm, d), lambda i: (i, 0)),
            pl.BlockSpec((tm, LANES), lambda i: (i, 0)),
            pl.BlockSpec((1, d), lambda i: (0, 0)),
            pl.BlockSpec(memory_space=pl.ANY),
        ],
        out_specs=pl.BlockSpec((tm, d), lambda i: (i, 0)),
        out_shape=jax.ShapeDtypeStruct((t, d), jnp.float32),
        scratch_shapes=[pltpu.VMEM((ROW_BUFS * TOP_K * tm * ROW_SUB, LANES), jnp.uint32),
                        pltpu.SemaphoreType.DMA((ROW_BUFS,))],
        compiler_params=_cparams("arbitrary"),
        name="moe_combine",
    )(pos_rows, x1, rw, norm_final, ys)


def _expert_meta(counts, n_tiles):
    tm = TM_EXPERT
    i32 = jnp.int32
    padded = ((counts + tm - 1) // tm) * tm
    ends = jnp.cumsum(padded).astype(i32)
    offs = ends - padded
    used = counts > 0
    n_used = jnp.sum(used).astype(i32)
    n_used_tiles = ends[-1] // tm
    seq_of = jnp.cumsum(used).astype(i32) - 1
    m_idx = jnp.arange(N_EXPERTS, dtype=i32)
    pick = used[None, :] & (seq_of[None, :] == m_idx[:, None])
    s_exp = jnp.sum(jnp.where(pick, m_idx[None, :], 0), axis=1).astype(i32)
    row0 = jnp.minimum(jnp.arange(n_tiles, dtype=i32), n_used_tiles - 1)[:, None] * tm
    t_exp = jnp.sum(ends[None, :] <= row0, axis=1).astype(i32)
    t_new = jnp.any(used[None, :] & (offs[None, :] == row0), axis=1).astype(i32)
    t_seq = jnp.sum(jnp.where(m_idx[None, :] == t_exp[:, None], seq_of[None, :], 0), axis=1).astype(i32)
    pads = (jnp.append(offs + counts, ends[-1]).astype(i32), jnp.append(ends, n_tiles * tm).astype(i32))
    return offs, pads, (t_exp, t_new, t_seq, s_exp, jnp.stack([n_used, n_used_tiles]).astype(i32))


def _moe(x1, h_fat, ids, rw, rank, cnt, w_gate, w_up, w_down, norm_final):
    t, d = x1.shape
    counts = cnt[0, :N_EXPERTS]
    n_rows = t * TOP_K + N_EXPERTS * TM_EXPERT
    offs, pads, meta = _expert_meta(counts, n_rows // TM_EXPERT)
    ids2 = ids[:, :TOP_K].T
    onehot = ids2[:, :, None] == jnp.arange(N_EXPERTS, dtype=jnp.int32)
    pos1d = (jnp.sum(jnp.where(onehot, offs, 0), axis=-1) + rank[:, :TOP_K].T).reshape(-1)
    src_tok = _inverse(pos1d, *pads, n_rows)
    ys = _experts(meta, src_tok, h_fat, w_gate, w_up, w_down)
    return _combine(pos1d * ROW_SUB, x1, rw, norm_final, ys)


def kernel(x, norm_mix, w_in, b_gate, attn_sinks, ssm_a_re, ssm_a_im, ssm_b_re, ssm_b_im, ssm_c_re, ssm_c_im, ssm_d, ssm_log_dt, w_glu, b_glu, w_attn_branch, w_ssm_branch, w_out, norm_moe, w_router_group, b_router_group, w_router_expert, b_router_expert, w_expert_gate, w_expert_up, w_expert_down, norm_final):
    b, l, d = x.shape
    depth = w_in.shape[0]
    assert depth == 1, "the final norm is fused into the last layer's combine kernel"
    d_attn = N_HEADS * HEAD_DIM
    kv_cols = N_KV_HEADS * HEAD_DIM
    d_ssm = ssm_d.shape[-1]
    bf16 = jnp.bfloat16
    x2 = x.reshape(b * l, d)
    assert TM_PROJ == TM_SSM == TM_MERGE and l % TM_SSM == 0
    i = 0
    q, k, v, u4, gates = _in_proj(x2, norm_mix[i][None], w_in[i].astype(bf16), b_gate[i][None],
                                  d_attn, kv_cols, d_ssm)
    attn = _attention(q, k, v, attn_sinks[i], b, l)
    tables = _ssm_tables(ssm_a_re[i], ssm_a_im[i], ssm_b_re[i], ssm_b_im[i], ssm_c_re[i], ssm_c_im[i],
                         ssm_d[i], ssm_log_dt[i], SSM_SEG)
    y4 = _ssm(u4, tables, b, l)
    pad = LANES - N_EXPERTS - N_EXPERT_GROUPS
    w_router = jnp.concatenate([w_router_expert[i], w_router_group[i], jnp.zeros((d, pad), jnp.float32)], axis=1)
    w_r_hi = w_router.astype(bf16)
    w_router = jnp.concatenate([w_r_hi, (w_router - w_r_hi.astype(jnp.float32)).astype(bf16)], axis=1)
    b_router = jnp.concatenate([b_router_expert[i], b_router_group[i], jnp.zeros((pad,), jnp.float32)])[None]
    x1, h_fat, ids, rw, rank, cnt = _merge(x2, attn, y4, gates, w_glu[i].astype(bf16), b_glu[i][None],
                                w_attn_branch[i].astype(bf16), w_ssm_branch[i].astype(bf16),
                                w_out[i].astype(bf16), norm_moe[i][None], w_router, b_router)
    out = _moe(x1, h_fat, ids, rw, rank, cnt, w_expert_gate[i], w_expert_up[i], w_expert_down[i], norm_final[None])
    return out.reshape(b, l, d)
```

```python
import functools
import math

import jax
import jax.numpy as jnp
from jax import lax
from jax.experimental import pallas as pl
from jax.experimental.pallas import tpu as pltpu

EPS = 1e-6
HEAD_DIM = 64
N_HEADS = 8
N_KV_HEADS = 2
Q_PER_KV = N_HEADS // N_KV_HEADS
ATTN_BLOCK = 128
ATTN_QB = 8
SSM_GROUP = 16
SSM_STATE = 64
N_EXPERT_GROUPS = 4
EXPERTS_PER_GROUP = 8
N_EXPERTS = N_EXPERT_GROUPS * EXPERTS_PER_GROUP
TOP_K = 2

LANES = 128
SUBLANES = 8
SSM_CHUNK_GROUPS = LANES // SSM_GROUP
SSM_CHUNK_STATES = SSM_CHUNK_GROUPS * SSM_STATE

TM_PROJ = 512
TM_SSM = 512
SSM_SEG = TM_SSM // SUBLANES
TM_MERGE = 512
TM_MOVE = 256
TM_EXPERT = 256
ROW_SUB = 4
ROW_BUFS = 3
VMEM_LIMIT = 56 * 1024 * 1024


def _cparams(*sem):
    return pltpu.CompilerParams(dimension_semantics=sem, vmem_limit_bytes=VMEM_LIMIT)


def _pack_bf16_pairs(x):
    half = x.shape[1] // 2
    bits = lax.bitcast_convert_type(x, jnp.uint32)
    return (bits[:, :half] & jnp.uint32(0xFFFF0000)) | (bits[:, half:] >> 16)


def _unpack_bf16_pairs(p):
    hi = lax.bitcast_convert_type(p & jnp.uint32(0xFFFF0000), jnp.float32)
    lo = lax.bitcast_convert_type(p << 16, jnp.float32)
    return jnp.concatenate([hi, lo], axis=1)


def _load_rows(ref, first, n):
    return jnp.concatenate(
        [ref[pl.ds(first * ROW_SUB + c, n, stride=ROW_SUB), :] for c in range(ROW_SUB)], axis=1)


def _store_rows(ref, first, val, mask=None):
    n = val.shape[0]
    for c in range(ROW_SUB):
        idx = pl.ds(first * ROW_SUB + c, n, stride=ROW_SUB)
        v = val[:, c * LANES:(c + 1) * LANES]
        ref[idx, :] = v if mask is None else jnp.where(mask, v, ref[idx, :])


def _row_gather(src_hbm, idx_ref, idx0, dst_ref, dst0, n, sem, start):
    rs = ROW_SUB
    if start:
        for r in range(n):
            pltpu.make_async_copy(src_hbm.at[pl.ds(pl.multiple_of(idx_ref[idx0 + r], rs), rs)],
                                  dst_ref.at[pl.ds(pl.multiple_of((dst0 + r) * rs, rs), rs)],
                                  sem).start(priority=r % 2)
    else:
        pltpu.make_async_copy(src_hbm.at[pl.ds(0, n * rs)],
                              dst_ref.at[pl.ds(pl.multiple_of(dst0 * rs, rs), n * rs)], sem).wait()


def _proj_kernel(x_ref, g_ref, w_ref, bg_ref, q_ref, k_ref, v_ref, u_ref, gate_ref, *, cols):
    q_c, kv_c, d_ssm = cols
    xf = x_ref[...]
    inv = lax.rsqrt(jnp.mean(xf * xf, axis=-1, keepdims=True) + EPS)
    h = (xf * inv * g_ref[...]).astype(jnp.bfloat16)
    o = 0
    q_ref[...] = (jnp.dot(h, w_ref[:, o:o + q_c], preferred_element_type=jnp.float32)
                  * (1.0 / math.sqrt(HEAD_DIM))).astype(q_ref.dtype)
    o += q_c
    k_ref[...] = jnp.dot(h, w_ref[:, o:o + kv_c], preferred_element_type=jnp.float32).astype(k_ref.dtype)
    o += kv_c
    v_ref[...] = jnp.dot(h, w_ref[:, o:o + kv_c], preferred_element_type=jnp.float32).astype(v_ref.dtype)
    o += kv_c
    seg = x_ref.shape[0] // SUBLANES
    for s in range(0, d_ssm // LANES, 2):
        uu = jnp.dot(h, w_ref[:, o:o + 2 * LANES], preferred_element_type=jnp.float32)
        for half in range(2):
            for j in range(SUBLANES):
                u_ref[s + half, pl.ds(j, seg, stride=SUBLANES), :] = (
                    uu[j * seg:(j + 1) * seg, half * LANES:(half + 1) * LANES])
        o += 2 * LANES
    gl = jnp.dot(h, w_ref[:, o:], preferred_element_type=jnp.float32) + bg_ref[...]
    gate_ref[...] = jax.nn.sigmoid(gl).astype(gate_ref.dtype)


def _in_proj(x2, norm_w, w_in, b_gate, d_attn, kv_cols, d_ssm):
    t, d = x2.shape
    gate_cols = b_gate.shape[-1]
    n_slab = d_ssm // LANES
    tm = TM_PROJ
    kern = functools.partial(_proj_kernel, cols=(d_attn, kv_cols, d_ssm))
    return pl.pallas_call(
        kern,
        grid=(t // tm,),
        in_specs=[
            pl.BlockSpec((tm, d), lambda i: (i, 0)),
            pl.BlockSpec((1, d), lambda i: (0, 0)),
            pl.BlockSpec(w_in.shape, lambda i: (0, 0)),
            pl.BlockSpec((1, gate_cols), lambda i: (0, 0)),
        ],
        out_specs=[
            pl.BlockSpec((tm, d_attn), lambda i: (i, 0)),
            pl.BlockSpec((tm, kv_cols), lambda i: (i, 0)),
            pl.BlockSpec((tm, kv_cols), lambda i: (i, 0)),
            pl.BlockSpec((n_slab, tm, LANES), lambda i: (0, i, 0)),
            pl.BlockSpec((tm, gate_cols), lambda i: (i, 0)),
        ],
        out_shape=[
            jax.ShapeDtypeStruct((t, d_attn), jnp.bfloat16),
            jax.ShapeDtypeStruct((t, kv_cols), jnp.bfloat16),
            jax.ShapeDtypeStruct((t, kv_cols), jnp.bfloat16),
            jax.ShapeDtypeStruct((n_slab, t, LANES), jnp.float32),
            jax.ShapeDtypeStruct((t, gate_cols), jnp.bfloat16),
        ],
        compiler_params=_cparams("arbitrary"),
        name="in_proj",
    )(x2, norm_w, w_in, b_gate)


def _attn_kernel(sink_ref, q_ref, kp_ref, kc_ref, vp_ref, vc_ref, rep_ref, mask_ref, o_ref):
    i = pl.program_id(1)
    blk = ATTN_BLOCK
    hw = Q_PER_KV * HEAD_DIM
    rows = Q_PER_KV * blk
    head_of_row = lax.broadcasted_iota(jnp.int32, (rows, 1), 0) >> 7
    lane_head_q = lax.broadcasted_iota(jnp.int32, (blk, hw), 1) >> 6
    lane_head_v = lax.broadcasted_iota(jnp.int32, (2 * blk, hw), 1) >> 6
    sinks = []
    for kh in range(N_KV_HEADS):
        sink = jnp.zeros((rows, 1), jnp.float32)
        for g in range(Q_PER_KV):
            sink = jnp.where(head_of_row == g, sink_ref[kh * Q_PER_KV + g], sink)
        sinks.append(sink)
    for qb in range(ATTN_QB):
        if qb == 0:
            k2 = jnp.concatenate([kp_ref[...], kc_ref[0:blk, :]], axis=0)
            v2 = jnp.concatenate([vp_ref[...], vc_ref[0:blk, :]], axis=0)
            bias = mask_ref[jnp.where(i == 0, 1, 0)]
        else:
            k2 = kc_ref[(qb - 1) * blk:(qb + 1) * blk, :]
            v2 = vc_ref[(qb - 1) * blk:(qb + 1) * blk, :]
            bias = mask_ref[0]
        for kh in range(N_KV_HEADS):
            rep = rep_ref[kh]
            k4 = jnp.dot(k2, rep, preferred_element_type=jnp.float32).astype(jnp.bfloat16)
            v4 = jnp.dot(v2, rep, preferred_element_type=jnp.float32).astype(jnp.bfloat16)
            qh = q_ref[qb * blk:(qb + 1) * blk, kh * hw:(kh + 1) * hw]
            qm = jnp.concatenate(
                [jnp.where(lane_head_q == g, qh, jnp.zeros_like(qh)) for g in range(Q_PER_KV)], axis=0)
            s = lax.dot_general(qm, k4, (((1,), (1,)), ((), ())), preferred_element_type=jnp.float32)
            s = s + bias
            sink = sinks[kh]
            m = jnp.maximum(jnp.max(s, axis=-1, keepdims=True), sink)
            p = jnp.exp(s - m)
            rinv = 1.0 / (jnp.sum(p, axis=-1, keepdims=True) + jnp.exp(sink - m))
            p = p.astype(jnp.bfloat16)
            p_cat = jnp.concatenate([p[g * blk:(g + 1) * blk, :] for g in range(Q_PER_KV)], axis=1)
            vm = jnp.concatenate(
                [jnp.where(lane_head_v == g, v4, jnp.zeros_like(v4)) for g in range(Q_PER_KV)], axis=0)
            o = jnp.dot(p_cat, vm, preferred_element_type=jnp.float32)
            scale = jnp.zeros((blk, hw), jnp.float32)
            for g in range(Q_PER_KV):
                scale = jnp.where(lane_head_q == g, rinv[g * blk:(g + 1) * blk, :], scale)
            o_ref[qb * blk:(qb + 1) * blk, kh * hw:(kh + 1) * hw] = (o * scale).astype(o_ref.dtype)


def _attention(q, k, v, sinks, b, l):
    d_attn = q.shape[-1]
    kv_cols = k.shape[-1]
    blk = ATTN_BLOCK
    tq = ATTN_QB * blk
    hw = Q_PER_KV * HEAD_DIM
    lane = jnp.arange(hw)[None, :]
    src = jnp.arange(kv_cols)[:, None]
    rep = jnp.stack([(src == kh * HEAD_DIM + (lane % HEAD_DIM)) for kh in range(N_KV_HEADS)]
                    ).astype(jnp.bfloat16)
    r = (jnp.arange(Q_PER_KV * blk) % blk)[:, None]
    c = jnp.arange(2 * blk)[None, :]
    band = (c > r) & (c <= r + blk)
    mask = jnp.where(jnp.stack([band, band & (c >= blk)]), 0.0, jnp.finfo(jnp.float32).min).astype(jnp.float32)
    q3 = q.reshape(b, l, d_attn)
    k3 = k.reshape(b, l, kv_cols)
    v3 = v.reshape(b, l, kv_cols)
    cur = lambda bi, i: (bi, i, 0)
    prev = lambda bi, i: (bi, jnp.maximum(ATTN_QB * i - 1, 0), 0)
    out = pl.pallas_call(
        _attn_kernel,
        grid=(b, l // tq),
        in_specs=[
            pl.BlockSpec(memory_space=pltpu.SMEM),
            pl.BlockSpec((None, tq, d_attn), cur),
            pl.BlockSpec((None, blk, kv_cols), prev),
            pl.BlockSpec((None, tq, kv_cols), cur),
            pl.BlockSpec((None, blk, kv_cols), prev),
            pl.BlockSpec((None, tq, kv_cols), cur),
            pl.BlockSpec(rep.shape, lambda bi, i: (0, 0, 0)),
            pl.BlockSpec(mask.shape, lambda bi, i: (0, 0, 0)),
        ],
        out_specs=pl.BlockSpec((None, tq, d_attn), cur),
        out_shape=jax.ShapeDtypeStruct((b, l, d_attn), jnp.bfloat16),
        compiler_params=_cparams("arbitrary", "arbitrary"),
        name="swa",
    )(sinks, q3, k3, k3, v3, v3, rep, mask)
    return out.reshape(b * l, d_attn)


def _ssm_kernel(u_ref, bmat_ref, cmat_ref, lam_ref, pw_ref, lamseg_ref, d_ref, y_ref, bu_ref, carry_ref):
    i = pl.program_id(1)
    n_slab = u_ref.shape[0]
    tm = u_ref.shape[1]
    seg = tm // SUBLANES
    ns = SSM_CHUNK_STATES
    npair = 2

    @pl.when(i == 0)
    def _():
        carry_ref[...] = jnp.zeros_like(carry_ref)

    sub = lax.broadcasted_iota(jnp.int32, (SUBLANES, ns), 0)
    for s0 in range(0, n_slab, npair):
        lam = []
        for q in range(npair):
            s = s0 + q
            bu_ref[q] = jnp.dot(u_ref[s].astype(jnp.bfloat16), bmat_ref[s], preferred_element_type=jnp.float32)
            lam.append((jnp.broadcast_to(lam_ref[s, 0:1, :], (SUBLANES, ns)),
                        jnp.broadcast_to(lam_ref[s, 1:2, :], (SUBLANES, ns))))

        def step(r, st):
            rows = pl.ds(pl.multiple_of(r * SUBLANES, SUBLANES), SUBLANES)
            out = []
            for q in range(npair):
                lr, li = lam[q]
                sr, si = st[q]
                nr = lr * sr - li * si + bu_ref[q, rows, 0:ns]
                ni = lr * si + li * sr + bu_ref[q, rows, ns:2 * ns]
                bu_ref[q, rows, 0:ns] = nr
                bu_ref[q, rows, ns:2 * ns] = ni
                out.append((nr, ni))
            return tuple(out)

        zero = jnp.zeros((SUBLANES, ns), jnp.float32)
        ends = lax.fori_loop(0, seg, step, ((zero, zero),) * npair, unroll=2)

        for q in range(npair):
            s = s0 + q
            er, ei = ends[q]
            ar = lamseg_ref[s, 0:1, :]
            ai = lamseg_ref[s, 1:2, :]
            cr = carry_ref[s, 0:1, :]
            ci = carry_ref[s, 1:2, :]
            car = jnp.zeros((SUBLANES, ns), jnp.float32)
            cai = jnp.zeros((SUBLANES, ns), jnp.float32)
            for j in range(SUBLANES):
                car = jnp.where(sub == j, jnp.broadcast_to(cr, (SUBLANES, ns)), car)
                cai = jnp.where(sub == j, jnp.broadcast_to(ci, (SUBLANES, ns)), cai)
                ejr = jnp.sum(jnp.where(sub == j, er, 0.0), axis=0, keepdims=True)
                eji = jnp.sum(jnp.where(sub == j, ei, 0.0), axis=0, keepdims=True)
                cr, ci = ar * cr - ai * ci + ejr, ar * ci + ai * cr + eji
            carry_ref[s, 0:1, :] = cr
            carry_ref[s, 1:2, :] = ci

            ctr = jnp.broadcast_to(car[None], (seg, SUBLANES, ns)).reshape(tm, ns)
            cti = jnp.broadcast_to(cai[None], (seg, SUBLANES, ns)).reshape(tm, ns)
            pr = pw_ref[s, 0]
            pi = pw_ref[s, 1]
            st_r = bu_ref[q, :, 0:ns] + pr * ctr - pi * cti
            st_i = bu_ref[q, :, ns:2 * ns] + pr * cti + pi * ctr
            st = jnp.concatenate([st_r, st_i], axis=1).astype(jnp.bfloat16)
            y_ref[s] = jnp.dot(st, cmat_ref[s], preferred_element_type=jnp.float32) + d_ref[s] * u_ref[s]


def _ssm_tables(a_re, a_im, b_re, b_im, c_re, c_im, d_skip, log_dt, seg):
    f32 = jnp.float32
    g, p = a_re.shape
    c = b_re.shape[-1]
    ng = SSM_CHUNK_GROUPS
    n_slab = g // ng
    lam = lax.complex(a_re.astype(f32), a_im.astype(f32))
    dt = jnp.exp(log_dt.astype(f32))[:, None]
    lam_bar = jnp.exp(lam * dt)
    b_bar = ((lam_bar - 1.0) / lam)[:, :, None] * lax.complex(b_re.astype(f32), b_im.astype(f32))
    c_mat = lax.complex(c_re.astype(f32), c_im.astype(f32))
    eye = jnp.eye(ng, dtype=f32)

    def bdiag_b(m):
        m = m.reshape(n_slab, ng, p, c)
        return jnp.einsum('ab,kbpc->kacbp', eye, m).reshape(n_slab, ng * c, ng * p)

    def bdiag_c(m):
        m = m.reshape(n_slab, ng, c, p)
        return jnp.einsum('ab,kbcp->kbpac', eye, m).reshape(n_slab, ng * p, ng * c)

    bmat = jnp.concatenate([bdiag_b(jnp.real(b_bar)), bdiag_b(jnp.imag(b_bar))], axis=2).astype(jnp.bfloat16)
    cmat = jnp.concatenate([bdiag_c(jnp.real(c_mat)), -bdiag_c(jnp.imag(c_mat))], axis=1).astype(jnp.bfloat16)

    def slab_rows(z):
        z = z.reshape(n_slab, 1, ng * p)
        return jnp.concatenate([jnp.real(z), jnp.imag(z)], axis=1)

    lam_t = slab_rows(lam_bar)
    steps = jnp.arange(1, seg + 1, dtype=f32)
    pw = jnp.exp((lam * dt)[None] * steps[:, None, None])
    pw = pw.reshape(seg, n_slab, ng * p).transpose(1, 0, 2)
    pw = jnp.repeat(pw, SUBLANES, axis=1)
    pw_t = jnp.stack([jnp.real(pw), jnp.imag(pw)], axis=1)
    lamseg_t = slab_rows(jnp.exp(lam * dt * float(seg)))
    d_t = d_skip.astype(f32).reshape(n_slab, 1, ng * c)
    return bmat, cmat, lam_t, pw_t, lamseg_t, d_t


def _ssm(u4, tables, b, l):
    bmat, cmat, lam_t, pw_t, lamseg_t, d_t = tables
    n_slab, t, _ = u4.shape
    tm = TM_SSM
    nt = l // tm
    ns = SSM_CHUNK_STATES
    const = lambda nd: (lambda bi, i: (0,) * nd)
    return pl.pallas_call(
        _ssm_kernel,
        grid=(b, nt),
        in_specs=[
            pl.BlockSpec((n_slab, tm, LANES), lambda bi, i: (0, bi * nt + i, 0)),
            pl.BlockSpec(bmat.shape, const(3)),
            pl.BlockSpec(cmat.shape, const(3)),
            pl.BlockSpec(lam_t.shape, const(3)),
            pl.BlockSpec(pw_t.shape, const(4)),
            pl.BlockSpec(lamseg_t.shape, const(3)),
            pl.BlockSpec(d_t.shape, const(3)),
        ],
        out_specs=pl.BlockSpec((n_slab, tm, LANES), lambda bi, i: (0, bi * nt + i, 0)),
        out_shape=jax.ShapeDtypeStruct((n_slab, t, LANES), jnp.float32),
        scratch_shapes=[
            pltpu.VMEM((2, tm, 2 * ns), jnp.float32),
            pltpu.VMEM((n_slab, 2, ns), jnp.float32),
        ],
        compiler_params=_cparams("arbitrary", "arbitrary"),
        name="s5_scan",
    )(u4, bmat, cmat, lam_t, pw_t, lamseg_t, d_t)


def _merge_kernel(x_ref, attn_ref, y_ref, gate_ref, wglu_ref, bglu_ref, wa_ref, ws_ref, wo_ref,
                  nm_ref, wr_ref, br_ref, tri_ref, x1_ref, h2_ref, ids_ref, rw_ref, rank_ref, cnt_ref,
                  lg_ref, carry_ref):
    i = pl.program_id(0)
    d = x_ref.shape[1]
    tm = x_ref.shape[0]

    @pl.when(i == 0)
    def _():
        lg_ref[...] = jnp.zeros_like(lg_ref)
        carry_ref[...] = jnp.zeros_like(carry_ref)

    logits = lg_ref[...]

    seg = tm // SUBLANES
    y = jnp.concatenate(
        [jnp.concatenate([y_ref[s, pl.ds(j, seg, stride=SUBLANES), :] for j in range(SUBLANES)], axis=0)
         for s in range(y_ref.shape[0])], axis=1)
    z = jax.nn.gelu(y)
    zg = jnp.dot(z.astype(jnp.bfloat16), wglu_ref[...], preferred_element_type=jnp.float32) + bglu_ref[...]
    z = z * jax.nn.sigmoid(zg)
    a = jnp.dot(attn_ref[...], wa_ref[...], preferred_element_type=jnp.float32)
    sb = jnp.dot(z.astype(jnp.bfloat16), ws_ref[...], preferred_element_type=jnp.float32)
    merged = gate_ref[:, 0:d].astype(jnp.float32) * a + gate_ref[:, d:2 * d].astype(jnp.float32) * sb
    x1 = x_ref[...] + jnp.dot(merged.astype(jnp.bfloat16), wo_ref[...], preferred_element_type=jnp.float32)
    x1_ref[...] = x1
    inv = lax.rsqrt(jnp.mean(x1 * x1, axis=-1, keepdims=True) + EPS)
    h2 = x1 * inv * nm_ref[...]
    h_hi = h2.astype(jnp.bfloat16)
    h_pk = _pack_bf16_pairs(h_hi.astype(jnp.float32))
    _store_rows(h2_ref, 0, h_pk)

    h_lo = (h2 - h_hi.astype(jnp.float32)).astype(jnp.bfloat16)
    hh = jnp.dot(h_hi, wr_ref[...], preferred_element_type=jnp.float32)
    lh = jnp.dot(h_lo, wr_ref[:, 0:LANES], preferred_element_type=jnp.float32)
    lg_ref[...] = hh[:, 0:LANES] + hh[:, LANES:2 * LANES] + lh + br_ref[...]

    lane = lax.broadcasted_iota(jnp.int32, (tm, LANES), 1)
    ninf = -jnp.inf
    gl = jnp.where((lane >= N_EXPERTS) & (lane < N_EXPERTS + N_EXPERT_GROUPS), logits, ninf)
    gmax = jnp.max(gl, axis=-1, keepdims=True)
    gidx = jnp.min(jnp.where(gl == gmax, lane - N_EXPERTS, LANES), axis=-1, keepdims=True)
    group_p = 1.0 / jnp.sum(jnp.exp(gl - gmax), axis=-1, keepdims=True)
    el = jnp.where((lane < N_EXPERTS) & ((lane >> 3) == gidx), logits, ninf)
    m1 = jnp.max(el, axis=-1, keepdims=True)
    i1 = jnp.min(jnp.where(el == m1, lane, LANES), axis=-1, keepdims=True)
    el2 = jnp.where(lane == i1, ninf, el)
    m2 = jnp.max(el2, axis=-1, keepdims=True)
    i2 = jnp.min(jnp.where(el2 == m2, lane, LANES), axis=-1, keepdims=True)
    e2 = jnp.exp(m2 - m1)
    w1 = group_p / (1.0 + e2)
    w2 = group_p * e2 / (1.0 + e2)
    ids_ref[...] = jnp.where(lane == 0, i1, jnp.where(lane == 1, i2, 0))
    rw_ref[...] = jnp.where(lane == 0, w1, jnp.where(lane == 1, w2, 0.0))

    oh0 = lane == i1
    oh1 = lane == i2
    live = jnp.where(i > 0, 1.0, 0.0)
    oh = (oh0.astype(jnp.float32) + oh1.astype(jnp.float32)) * live
    cum = jnp.dot(tri_ref[...], oh.astype(jnp.bfloat16), preferred_element_type=jnp.float32) + carry_ref[...]
    r0 = jnp.sum(jnp.where(oh0, cum, 0.0), axis=-1, keepdims=True)
    r1 = jnp.sum(jnp.where(oh1, cum, 0.0), axis=-1, keepdims=True)
    rank_ref[...] = jnp.where(lane == 0, r0, jnp.where(lane == 1, r1, 0.0)).astype(jnp.int32)
    carry_ref[...] = carry_ref[...] + jnp.sum(oh, axis=0, keepdims=True)
    cnt_ref[...] = carry_ref[...].astype(jnp.int32)


def _merge(x2, attn, y4, gates, w_glu, b_glu, w_a, w_s, w_o, norm_moe, w_router, b_router):
    t, d = x2.shape
    tm = TM_MERGE
    nt = t // tm
    n_slab = y4.shape[0]
    tri = (jnp.arange(tm)[None, :] < jnp.arange(tm)[:, None]).astype(jnp.bfloat16)
    full = lambda a: pl.BlockSpec(a.shape, lambda i: (0,) * a.ndim)
    cur = lambda i: (jnp.minimum(i, nt - 1), 0)
    prv = lambda i: (jnp.maximum(i - 1, 0), 0)
    row = lambda c, m=cur: pl.BlockSpec((tm, c), m)
    return pl.pallas_call(
        _merge_kernel,
        grid=(nt + 1,),
        in_specs=[
            row(d), row(attn.shape[1]),
            pl.BlockSpec((n_slab, tm, LANES), lambda i: (0, jnp.minimum(i, nt - 1), 0)),
            row(gates.shape[1]),
            full(w_glu), full(b_glu), full(w_a), full(w_s), full(w_o), full(norm_moe),
            full(w_router), full(b_router), full(tri),
        ],
        out_specs=[row(d), pl.BlockSpec((tm * ROW_SUB, LANES), cur), row(LANES, prv), row(LANES, prv),
                   row(LANES, prv), pl.BlockSpec((1, LANES), lambda i: (0, 0))],
        out_shape=[
            jax.ShapeDtypeStruct((t, d), jnp.float32),
            jax.ShapeDtypeStruct((t * ROW_SUB, LANES), jnp.uint32),
            jax.ShapeDtypeStruct((t, LANES), jnp.int32),
            jax.ShapeDtypeStruct((t, LANES), jnp.float32),
            jax.ShapeDtypeStruct((t, LANES), jnp.int32),
            jax.ShapeDtypeStruct((1, LANES), jnp.int32),
        ],
        scratch_shapes=[pltpu.VMEM((tm, LANES), jnp.float32), pltpu.VMEM((1, LANES), jnp.float32)],
        compiler_params=_cparams("arbitrary"),
        name="merge_router",
    )(x2, attn, y4, gates, w_glu, b_glu, w_a, w_s, w_o, norm_moe, w_router, b_router, tri)


def _inverse_kernel(pos_ref, src_ref):
    n_slots = pos_ref.shape[0]
    t = n_slots // TOP_K

    def body(tok, c):
        for k in range(TOP_K):
            src_ref[pos_ref[k * t + tok]] = tok * ROW_SUB
        return c

    lax.fori_loop(0, t, body, 0, unroll=8)


def _inverse(pos1d):
    n_slots = pos1d.shape[0]
    smem = pl.BlockSpec(memory_space=pltpu.SMEM)
    return pl.pallas_call(
        _inverse_kernel,
        in_specs=[smem],
        out_specs=smem,
        out_shape=jax.ShapeDtypeStruct((n_slots,), jnp.int32),
        name="route_inverse",
    )(pos1d)


def _expert_kernel(ta_ref, tn_ref, tnew_ref, sexp_ref, slo_ref, shi_ref, meta_ref, src_ref,
                   h_ref, wg_hbm, wu_hbm, wd_hbm, y_ref,
                   xbuf_ref, wgs_ref, wus_ref, wds_ref, wgb_ref, wub_ref, wdb_ref, gsem, wsem):
    j = pl.program_id(0)
    n_tiles = pl.num_programs(0)
    last = n_tiles - 1
    tm = TM_EXPERT
    rs = ROW_SUB
    nb = ROW_BUFS
    n_used = meta_ref[0]
    slot = j % nb
    a = ta_ref[j]

    def gather(tile, sl, start):
        _row_gather(h_ref, src_ref, tile * tm, xbuf_ref, sl * tm, tm, gsem.at[sl], start)

    def weights(m, start):
        sl = m % 2
        e = sexp_ref[m]
        for hbm, stage in ((wg_hbm, wgs_ref), (wu_hbm, wus_ref), (wd_hbm, wds_ref)):
            cp = pltpu.make_async_copy(hbm.at[e], stage.at[sl], wsem.at[sl])
            cp.start() if start else cp.wait()

    def new_expert(m):
        weights(m, False)
        sl = m % 2
        wgb_ref[...] = wgs_ref[sl].astype(jnp.bfloat16)
        wub_ref[...] = wus_ref[sl].astype(jnp.bfloat16)
        wdb_ref[...] = wds_ref[sl].astype(jnp.bfloat16)

        @pl.when(m + 1 < n_used)
        def _():
            weights(m + 1, True)

    def expert_out(xp):
        x = _unpack_bf16_pairs(xp).astype(jnp.bfloat16)
        hg = jnp.dot(x, wgb_ref[...], preferred_element_type=jnp.float32)
        hu = jnp.dot(x, wub_ref[...], preferred_element_type=jnp.float32)
        act = (jax.nn.silu(hg) * hu).astype(jnp.bfloat16)
        y = jnp.dot(act, wdb_ref[...], preferred_element_type=jnp.float32)
        return _pack_bf16_pairs(y.astype(jnp.bfloat16).astype(jnp.float32))

    @pl.when(j == 0)
    def _():
        weights(0, True)
        gather(0, 0, True)
        gather(1, 1, True)

    @pl.when(tnew_ref[j] == 1)
    def _():
        new_expert(a)

    gather(j, slot, False)
    xp = _load_rows(xbuf_ref, slot * tm, tm)
    gather(jnp.minimum(j + 2, last), (j + 2) % nb, True)
    _store_rows(y_ref, 0, expert_out(xp))

    def extra(i, carry):
        m = a + i
        new_expert(m)
        yp = expert_out(_load_rows(xbuf_ref, slot * tm, tm))
        row = lax.broadcasted_iota(jnp.int32, (tm, LANES), 0) + j * tm
        mine = (row >= slo_ref[m]) & (row < shi_ref[m])
        _store_rows(y_ref, 0, yp, mine)
        return carry

    lax.fori_loop(1, tn_ref[j], extra, 0)

    @pl.when(j == last)
    def _():
        gather(j, (j + 1) % nb, False)
        gather(j, (j + 2) % nb, False)


def _experts(meta, src_tok, h_fat, w_gate, w_up, w_down):
    n_slots = src_tok.shape[0]
    ne, d, dff = w_gate.shape
    tm = TM_EXPERT
    rs = ROW_SUB
    assert n_slots // tm >= ROW_BUFS
    any_spec = pl.BlockSpec(memory_space=pl.ANY)
    grid_spec = pltpu.PrefetchScalarGridSpec(
        num_scalar_prefetch=8,
        grid=(n_slots // tm,),
        in_specs=[any_spec, any_spec, any_spec, any_spec],
        out_specs=pl.BlockSpec((tm * rs, LANES), lambda j, *_: (j, 0)),
        scratch_shapes=[
            pltpu.VMEM((ROW_BUFS * tm * rs, LANES), jnp.uint32),
            pltpu.VMEM((2, d, dff), jnp.float32),
            pltpu.VMEM((2, d, dff), jnp.float32),
            pltpu.VMEM((2, dff, d), jnp.float32),
            pltpu.VMEM((d, dff), jnp.bfloat16),
            pltpu.VMEM((d, dff), jnp.bfloat16),
            pltpu.VMEM((dff, d), jnp.bfloat16),
            pltpu.SemaphoreType.DMA((ROW_BUFS,)),
            pltpu.SemaphoreType.DMA((2,)),
        ],
    )
    return pl.pallas_call(
        _expert_kernel,
        grid_spec=grid_spec,
        out_shape=jax.ShapeDtypeStruct((n_slots * rs, LANES), jnp.uint32),
        compiler_params=_cparams("arbitrary"),
        name="moe_experts",
    )(*meta, src_tok, h_fat, w_gate, w_up, w_down)


def _combine_kernel(pos_ref, x1_ref, rw_ref, nf_ref, ys_ref, o_ref, buf_ref, sem):
    i = pl.program_id(0)
    n = pl.num_programs(0)
    last = n - 1
    tm = x1_ref.shape[0]
    t = n * tm
    nb = ROW_BUFS
    slot = i % nb

    def gather(tile, sl, start):
        for k in range(TOP_K):
            _row_gather(ys_ref, pos_ref, k * t + tile * tm, buf_ref, (sl * TOP_K + k) * tm, tm, sem.at[sl], start)

    @pl.when(i == 0)
    def _():
        gather(0, 0, True)
        gather(1, 1, True)

    gather(i, slot, False)
    y0 = _unpack_bf16_pairs(_load_rows(buf_ref, (slot * TOP_K) * tm, tm))
    y1 = _unpack_bf16_pairs(_load_rows(buf_ref, (slot * TOP_K + 1) * tm, tm))
    gather(jnp.minimum(i + 2, last), (i + 2) % nb, True)
    rw = rw_ref[...]
    x = x1_ref[...] + rw[:, 0:1] * y0 + rw[:, 1:2] * y1
    inv = lax.rsqrt(jnp.mean(x * x, axis=-1, keepdims=True) + EPS)
    o_ref[...] = x * inv * nf_ref[...]

    @pl.when(i == last)
    def _():
        gather(i, (i + 1) % nb, False)
        gather(i, (i + 2) % nb, False)


def _combine(pos_rows, x1, rw, norm_final, ys):
    t, d = x1.shape
    tm = TM_MOVE
    assert t // tm >= ROW_BUFS
    return pl.pallas_call(
        _combine_kernel,
        grid=(t // tm,),
        in_specs=[
            pl.BlockSpec(memory_space=pltpu.SMEM),
            pl.BlockSpec((tm, d), lambda i: (i, 0)),
            pl.BlockSpec((tm, LANES), lambda i: (i, 0)),
            pl.BlockSpec((1, d), lambda i: (0, 0)),
            pl.BlockSpec(memory_space=pl.ANY),
        ],
        out_specs=pl.BlockSpec((tm, d), lambda i: (i, 0)),
        out_shape=jax.ShapeDtypeStruct((t, d), jnp.float32),
        scratch_shapes=[pltpu.VMEM((ROW_BUFS * TOP_K * tm * ROW_SUB, LANES), jnp.uint32),
                        pltpu.SemaphoreType.DMA((ROW_BUFS,))],
        compiler_params=_cparams("arbitrary"),
        name="moe_combine",
    )(pos_rows, x1, rw, norm_final, ys)


def _expert_meta(counts, n_slots):
    tm = TM_EXPERT
    n_tiles = n_slots // tm
    i32 = jnp.int32
    ends = jnp.cumsum(counts).astype(i32)
    starts = ends - counts
    used = counts > 0
    n_used = jnp.sum(used).astype(i32)
    seq_of = jnp.cumsum(used).astype(i32) - 1
    m_idx = jnp.arange(N_EXPERTS, dtype=i32)
    pick = used[None, :] & (seq_of[None, :] == m_idx[:, None])
    s_exp = jnp.sum(jnp.where(pick, m_idx[None, :], 0), axis=1).astype(i32)
    s_lo = jnp.sum(jnp.where(pick, starts[None, :], 0), axis=1).astype(i32)
    s_hi = jnp.sum(jnp.where(pick, ends[None, :], 0), axis=1).astype(i32)
    row0 = jnp.arange(n_tiles, dtype=i32)[:, None] * tm
    t_a = jnp.sum(used[None, :] & (ends[None, :] <= row0), axis=1).astype(i32)
    t_b = jnp.sum(used[None, :] & (starts[None, :] < row0 + tm), axis=1).astype(i32) - 1
    t_new = jnp.any(used[None, :] & (starts[None, :] == row0), axis=1).astype(i32)
    return t_a, (t_b - t_a + 1).astype(i32), t_new, s_exp, s_lo, s_hi, n_used.reshape(1)


def _moe(x1, h_fat, ids, rw, rank, cnt, w_gate, w_up, w_down, norm_final):
    t, d = x1.shape
    counts = cnt[0, :N_EXPERTS]
    offs = (jnp.cumsum(counts) - counts).astype(jnp.int32)
    ids2 = ids[:, :TOP_K].T
    onehot = ids2[:, :, None] == jnp.arange(N_EXPERTS, dtype=jnp.int32)
    pos1d = (jnp.sum(jnp.where(onehot, offs, 0), axis=-1) + rank[:, :TOP_K].T).reshape(-1)
    src_tok = _inverse(pos1d)
    meta = _expert_meta(counts, t * TOP_K)
    ys = _experts(meta, src_tok, h_fat, w_gate, w_up, w_down)
    return _combine(pos1d * ROW_SUB, x1, rw, norm_final, ys)


def kernel(x, norm_mix, w_in, b_gate, attn_sinks, ssm_a_re, ssm_a_im, ssm_b_re, ssm_b_im, ssm_c_re, ssm_c_im, ssm_d, ssm_log_dt, w_glu, b_glu, w_attn_branch, w_ssm_branch, w_out, norm_moe, w_router_group, b_router_group, w_router_expert, b_router_expert, w_expert_gate, w_expert_up, w_expert_down, norm_final):
    b, l, d = x.shape
    depth = w_in.shape[0]
    assert depth == 1, "the final norm is fused into the last layer's combine kernel"
    d_attn = N_HEADS * HEAD_DIM
    kv_cols = N_KV_HEADS * HEAD_DIM
    d_ssm = ssm_d.shape[-1]
    bf16 = jnp.bfloat16
    x2 = x.reshape(b * l, d)
    assert TM_PROJ == TM_SSM == TM_MERGE and l % TM_SSM == 0
    i = 0
    q, k, v, u4, gates = _in_proj(x2, norm_mix[i][None], w_in[i].astype(bf16), b_gate[i][None],
                                  d_attn, kv_cols, d_ssm)
    attn = _attention(q, k, v, attn_sinks[i], b, l)
    tables = _ssm_tables(ssm_a_re[i], ssm_a_im[i], ssm_b_re[i], ssm_b_im[i], ssm_c_re[i], ssm_c_im[i],
                         ssm_d[i], ssm_log_dt[i], SSM_SEG)
    y4 = _ssm(u4, tables, b, l)
    pad = LANES - N_EXPERTS - N_EXPERT_GROUPS
    w_router = jnp.concatenate([w_router_expert[i], w_router_group[i], jnp.zeros((d, pad), jnp.float32)], axis=1)
    w_r_hi = w_router.astype(bf16)
    w_router = jnp.concatenate([w_r_hi, (w_router - w_r_hi.astype(jnp.float32)).astype(bf16)], axis=1)
    b_router = jnp.concatenate([b_router_expert[i], b_router_group[i], jnp.zeros((pad,), jnp.float32)])[None]
    x1, h_fat, ids, rw, rank, cnt = _merge(x2, attn, y4, gates, w_glu[i].astype(bf16), b_glu[i][None],
                                w_attn_branch[i].astype(bf16), w_ssm_branch[i].astype(bf16),
                                w_out[i].astype(bf16), norm_moe[i][None], w_router, b_router)
    out = _moe(x1, h_fat, ids, rw, rank, cnt, w_expert_gate[i], w_expert_up[i], w_expert_down[i], norm_final[None])
    return out.reshape(b, l, d)
```

```python
import functools
import math

import jax
import jax.numpy as jnp
from jax import lax
from jax.experimental import pallas as pl
from jax.experimental.pallas import tpu as pltpu

EPS = 1e-6
HEAD_DIM = 64
N_HEADS = 8
N_KV_HEADS = 2
Q_PER_KV = N_HEADS // N_KV_HEADS
ATTN_BLOCK = 128
ATTN_QB = 8
SSM_GROUP = 16
SSM_STATE = 64
N_EXPERT_GROUPS = 4
EXPERTS_PER_GROUP = 8
N_EXPERTS = N_EXPERT_GROUPS * EXPERTS_PER_GROUP
TOP_K = 2

LANES = 128
SUBLANES = 8
SSM_CHUNK_GROUPS = LANES // SSM_GROUP
SSM_CHUNK_STATES = SSM_CHUNK_GROUPS * SSM_STATE

TM_PROJ = 512
TM_SSM = 512
SSM_SEG = TM_SSM // SUBLANES
TM_MERGE = 512
TM_MOVE = 512
TM_EXPERT = 256
ROW_SUB = 4
ROW_BUFS = 3
VMEM_LIMIT = 56 * 1024 * 1024


def _cparams(*sem):
    return pltpu.CompilerParams(dimension_semantics=sem, vmem_limit_bytes=VMEM_LIMIT)


def _pack_bf16_pairs(x):
    half = x.shape[1] // 2
    bits = lax.bitcast_convert_type(x, jnp.uint32)
    return (bits[:, :half] & jnp.uint32(0xFFFF0000)) | (bits[:, half:] >> 16)


def _unpack_bf16_pairs(p):
    hi = lax.bitcast_convert_type(p & jnp.uint32(0xFFFF0000), jnp.float32)
    lo = lax.bitcast_convert_type(p << 16, jnp.float32)
    return jnp.concatenate([hi, lo], axis=1)


def _load_rows(ref, first, n):
    return jnp.concatenate(
        [ref[pl.ds(first * ROW_SUB + c, n, stride=ROW_SUB), :] for c in range(ROW_SUB)], axis=1)


def _store_rows(ref, first, val, mask=None):
    n = val.shape[0]
    for c in range(ROW_SUB):
        idx = pl.ds(first * ROW_SUB + c, n, stride=ROW_SUB)
        v = val[:, c * LANES:(c + 1) * LANES]
        ref[idx, :] = v if mask is None else jnp.where(mask, v, ref[idx, :])


def _row_gather(src_hbm, idx_ref, idx0, dst_ref, dst0, n, sem, start):
    rs = ROW_SUB
    if start:
        for r in range(n):
            pltpu.make_async_copy(src_hbm.at[pl.ds(pl.multiple_of(idx_ref[idx0 + r], rs), rs)],
                                  dst_ref.at[pl.ds(pl.multiple_of((dst0 + r) * rs, rs), rs)],
                                  sem).start(priority=r % 2)
    else:
        pltpu.make_async_copy(src_hbm.at[pl.ds(0, n * rs)],
                              dst_ref.at[pl.ds(pl.multiple_of(dst0 * rs, rs), n * rs)], sem).wait()


def _proj_kernel(x_ref, g_ref, w_ref, bg_ref, q_ref, k_ref, v_ref, u_ref, gate_ref, *, cols):
    q_c, kv_c, d_ssm = cols
    xf = x_ref[...]
    inv = lax.rsqrt(jnp.mean(xf * xf, axis=-1, keepdims=True) + EPS)
    h = (xf * inv * g_ref[...]).astype(jnp.bfloat16)
    o = 0
    q_ref[...] = (jnp.dot(h, w_ref[:, o:o + q_c], preferred_element_type=jnp.float32)
                  * (1.0 / math.sqrt(HEAD_DIM))).astype(q_ref.dtype)
    o += q_c
    k_ref[...] = jnp.dot(h, w_ref[:, o:o + kv_c], preferred_element_type=jnp.float32).astype(k_ref.dtype)
    o += kv_c
    v_ref[...] = jnp.dot(h, w_ref[:, o:o + kv_c], preferred_element_type=jnp.float32).astype(v_ref.dtype)
    o += kv_c
    seg = x_ref.shape[0] // SUBLANES
    for s in range(0, d_ssm // LANES, 2):
        uu = jnp.dot(h, w_ref[:, o:o + 2 * LANES], preferred_element_type=jnp.float32)
        for half in range(2):
            for j in range(SUBLANES):
                u_ref[s + half, pl.ds(j, seg, stride=SUBLANES), :] = (
                    uu[j * seg:(j + 1) * seg, half * LANES:(half + 1) * LANES])
        o += 2 * LANES
    gl = jnp.dot(h, w_ref[:, o:], preferred_element_type=jnp.float32) + bg_ref[...]
    gate_ref[...] = jax.nn.sigmoid(gl).astype(gate_ref.dtype)


def _in_proj(x2, norm_w, w_in, b_gate, d_attn, kv_cols, d_ssm):
    t, d = x2.shape
    gate_cols = b_gate.shape[-1]
    n_slab = d_ssm // LANES
    tm = TM_PROJ
    kern = functools.partial(_proj_kernel, cols=(d_attn, kv_cols, d_ssm))
    return pl.pallas_call(
        kern,
        grid=(t // tm,),
        in_specs=[
            pl.BlockSpec((tm, d), lambda i: (i, 0)),
            pl.BlockSpec((1, d), lambda i: (0, 0)),
            pl.BlockSpec(w_in.shape, lambda i: (0, 0)),
            pl.BlockSpec((1, gate_cols), lambda i: (0, 0)),
        ],
        out_specs=[
            pl.BlockSpec((tm, d_attn), lambda i: (i, 0)),
            pl.BlockSpec((tm, kv_cols), lambda i: (i, 0)),
            pl.BlockSpec((tm, kv_cols), lambda i: (i, 0)),
            pl.BlockSpec((n_slab, tm, LANES), lambda i: (0, i, 0)),
            pl.BlockSpec((tm, gate_cols), lambda i: (i, 0)),
        ],
        out_shape=[
            jax.ShapeDtypeStruct((t, d_attn), jnp.bfloat16),
            jax.ShapeDtypeStruct((t, kv_cols), jnp.bfloat16),
            jax.ShapeDtypeStruct((t, kv_cols), jnp.bfloat16),
            jax.ShapeDtypeStruct((n_slab, t, LANES), jnp.float32),
            jax.ShapeDtypeStruct((t, gate_cols), jnp.bfloat16),
        ],
        compiler_params=_cparams("arbitrary"),
        name="in_proj",
    )(x2, norm_w, w_in, b_gate)


def _attn_kernel(sink_ref, q_ref, kp_ref, kc_ref, vp_ref, vc_ref, rep_ref, mask_ref, o_ref):
    i = pl.program_id(1)
    blk = ATTN_BLOCK
    hw = Q_PER_KV * HEAD_DIM
    rows = Q_PER_KV * blk
    head_of_row = lax.broadcasted_iota(jnp.int32, (rows, 1), 0) >> 7
    lane_head_q = lax.broadcasted_iota(jnp.int32, (blk, hw), 1) >> 6
    lane_head_v = lax.broadcasted_iota(jnp.int32, (2 * blk, hw), 1) >> 6
    sinks = []
    for kh in range(N_KV_HEADS):
        sink = jnp.zeros((rows, 1), jnp.float32)
        for g in range(Q_PER_KV):
            sink = jnp.where(head_of_row == g, sink_ref[kh * Q_PER_KV + g], sink)
        sinks.append(sink)
    for qb in range(ATTN_QB):
        if qb == 0:
            k2 = jnp.concatenate([kp_ref[...], kc_ref[0:blk, :]], axis=0)
            v2 = jnp.concatenate([vp_ref[...], vc_ref[0:blk, :]], axis=0)
            bias = mask_ref[jnp.where(i == 0, 1, 0)]
        else:
            k2 = kc_ref[(qb - 1) * blk:(qb + 1) * blk, :]
            v2 = vc_ref[(qb - 1) * blk:(qb + 1) * blk, :]
            bias = mask_ref[0]
        for kh in range(N_KV_HEADS):
            rep = rep_ref[kh]
            k4 = jnp.dot(k2, rep, preferred_element_type=jnp.float32).astype(jnp.bfloat16)
            v4 = jnp.dot(v2, rep, preferred_element_type=jnp.float32).astype(jnp.bfloat16)
            qh = q_ref[qb * blk:(qb + 1) * blk, kh * hw:(kh + 1) * hw]
            qm = jnp.concatenate(
                [jnp.where(lane_head_q == g, qh, jnp.zeros_like(qh)) for g in range(Q_PER_KV)], axis=0)
            s = lax.dot_general(qm, k4, (((1,), (1,)), ((), ())), preferred_element_type=jnp.float32)
            s = s + bias
            sink = sinks[kh]
            m = jnp.maximum(jnp.max(s, axis=-1, keepdims=True), sink)
            p = jnp.exp(s - m)
            rinv = 1.0 / (jnp.sum(p, axis=-1, keepdims=True) + jnp.exp(sink - m))
            p = p.astype(jnp.bfloat16)
            p_cat = jnp.concatenate([p[g * blk:(g + 1) * blk, :] for g in range(Q_PER_KV)], axis=1)
            vm = jnp.concatenate(
                [jnp.where(lane_head_v == g, v4, jnp.zeros_like(v4)) for g in range(Q_PER_KV)], axis=0)
            o = jnp.dot(p_cat, vm, preferred_element_type=jnp.float32)
            scale = jnp.zeros((blk, hw), jnp.float32)
            for g in range(Q_PER_KV):
                scale = jnp.where(lane_head_q == g, rinv[g * blk:(g + 1) * blk, :], scale)
            o_ref[qb * blk:(qb + 1) * blk, kh * hw:(kh + 1) * hw] = (o * scale).astype(o_ref.dtype)


def _attention(q, k, v, sinks, b, l):
    d_attn = q.shape[-1]
    kv_cols = k.shape[-1]
    blk = ATTN_BLOCK
    tq = ATTN_QB * blk
    hw = Q_PER_KV * HEAD_DIM
    lane = jnp.arange(hw)[None, :]
    src = jnp.arange(kv_cols)[:, None]
    rep = jnp.stack([(src == kh * HEAD_DIM + (lane % HEAD_DIM)) for kh in range(N_KV_HEADS)]
                    ).astype(jnp.bfloat16)
    r = (jnp.arange(Q_PER_KV * blk) % blk)[:, None]
    c = jnp.arange(2 * blk)[None, :]
    band = (c > r) & (c <= r + blk)
    mask = jnp.where(jnp.stack([band, band & (c >= blk)]), 0.0, jnp.finfo(jnp.float32).min).astype(jnp.float32)
    q3 = q.reshape(b, l, d_attn)
    k3 = k.reshape(b, l, kv_cols)
    v3 = v.reshape(b, l, kv_cols)
    cur = lambda bi, i: (bi, i, 0)
    prev = lambda bi, i: (bi, jnp.maximum(ATTN_QB * i - 1, 0), 0)
    out = pl.pallas_call(
        _attn_kernel,
        grid=(b, l // tq),
        in_specs=[
            pl.BlockSpec(memory_space=pltpu.SMEM),
            pl.BlockSpec((None, tq, d_attn), cur),
            pl.BlockSpec((None, blk, kv_cols), prev),
            pl.BlockSpec((None, tq, kv_cols), cur),
            pl.BlockSpec((None, blk, kv_cols), prev),
            pl.BlockSpec((None, tq, kv_cols), cur),
            pl.BlockSpec(rep.shape, lambda bi, i: (0, 0, 0)),
            pl.BlockSpec(mask.shape, lambda bi, i: (0, 0, 0)),
        ],
        out_specs=pl.BlockSpec((None, tq, d_attn), cur),
        out_shape=jax.ShapeDtypeStruct((b, l, d_attn), jnp.bfloat16),
        compiler_params=_cparams("arbitrary", "arbitrary"),
        name="swa",
    )(sinks, q3, k3, k3, v3, v3, rep, mask)
    return out.reshape(b * l, d_attn)


def _ssm_kernel(u_ref, bmat_ref, cmat_ref, lam_ref, pw_ref, lamseg_ref, d_ref, y_ref, bu_ref, carry_ref):
    i = pl.program_id(1)
    n_slab = u_ref.shape[0]
    tm = u_ref.shape[1]
    seg = tm // SUBLANES
    ns = SSM_CHUNK_STATES
    npair = 2

    @pl.when(i == 0)
    def _():
        carry_ref[...] = jnp.zeros_like(carry_ref)

    sub = lax.broadcasted_iota(jnp.int32, (SUBLANES, ns), 0)
    for s0 in range(0, n_slab, npair):
        lam = []
        for q in range(npair):
            s = s0 + q
            bu_ref[q] = jnp.dot(u_ref[s].astype(jnp.bfloat16), bmat_ref[s], preferred_element_type=jnp.float32)
            lam.append((jnp.broadcast_to(lam_ref[s, 0:1, :], (SUBLANES, ns)),
                        jnp.broadcast_to(lam_ref[s, 1:2, :], (SUBLANES, ns))))

        def step(r, st):
            rows = pl.ds(pl.multiple_of(r * SUBLANES, SUBLANES), SUBLANES)
            out = []
            for q in range(npair):
                lr, li = lam[q]
                sr, si = st[q]
                nr = lr * sr - li * si + bu_ref[q, rows, 0:ns]
                ni = lr * si + li * sr + bu_ref[q, rows, ns:2 * ns]
                bu_ref[q, rows, 0:ns] = nr
                bu_ref[q, rows, ns:2 * ns] = ni
                out.append((nr, ni))
            return tuple(out)

        zero = jnp.zeros((SUBLANES, ns), jnp.float32)
        ends = lax.fori_loop(0, seg, step, ((zero, zero),) * npair, unroll=2)

        for q in range(npair):
            s = s0 + q
            er, ei = ends[q]
            ar = lamseg_ref[s, 0:1, :]
            ai = lamseg_ref[s, 1:2, :]
            cr = carry_ref[s, 0:1, :]
            ci = carry_ref[s, 1:2, :]
            car = jnp.zeros((SUBLANES, ns), jnp.float32)
            cai = jnp.zeros((SUBLANES, ns), jnp.float32)
            for j in range(SUBLANES):
                car = jnp.where(sub == j, jnp.broadcast_to(cr, (SUBLANES, ns)), car)
                cai = jnp.where(sub == j, jnp.broadcast_to(ci, (SUBLANES, ns)), cai)
                ejr = jnp.sum(jnp.where(sub == j, er, 0.0), axis=0, keepdims=True)
                eji = jnp.sum(jnp.where(sub == j, ei, 0.0), axis=0, keepdims=True)
                cr, ci = ar * cr - ai * ci + ejr, ar * ci + ai * cr + eji
            carry_ref[s, 0:1, :] = cr
            carry_ref[s, 1:2, :] = ci

            ctr = jnp.broadcast_to(car[None], (seg, SUBLANES, ns)).reshape(tm, ns)
            cti = jnp.broadcast_to(cai[None], (seg, SUBLANES, ns)).reshape(tm, ns)
            pr = pw_ref[s, 0]
            pi = pw_ref[s, 1]
            st_r = bu_ref[q, :, 0:ns] + pr * ctr - pi * cti
            st_i = bu_ref[q, :, ns:2 * ns] + pr * cti + pi * ctr
            st = jnp.concatenate([st_r, st_i], axis=1).astype(jnp.bfloat16)
            y_ref[s] = jnp.dot(st, cmat_ref[s], preferred_element_type=jnp.float32) + d_ref[s] * u_ref[s]


def _ssm_tables(a_re, a_im, b_re, b_im, c_re, c_im, d_skip, log_dt, seg):
    f32 = jnp.float32
    g, p = a_re.shape
    c = b_re.shape[-1]
    ng = SSM_CHUNK_GROUPS
    n_slab = g // ng
    lam = lax.complex(a_re.astype(f32), a_im.astype(f32))
    dt = jnp.exp(log_dt.astype(f32))[:, None]
    lam_bar = jnp.exp(lam * dt)
    b_bar = ((lam_bar - 1.0) / lam)[:, :, None] * lax.complex(b_re.astype(f32), b_im.astype(f32))
    c_mat = lax.complex(c_re.astype(f32), c_im.astype(f32))
    eye = jnp.eye(ng, dtype=f32)

    def bdiag_b(m):
        m = m.reshape(n_slab, ng, p, c)
        return jnp.einsum('ab,kbpc->kacbp', eye, m).reshape(n_slab, ng * c, ng * p)

    def bdiag_c(m):
        m = m.reshape(n_slab, ng, c, p)
        return jnp.einsum('ab,kbcp->kbpac', eye, m).reshape(n_slab, ng * p, ng * c)

    bmat = jnp.concatenate([bdiag_b(jnp.real(b_bar)), bdiag_b(jnp.imag(b_bar))], axis=2).astype(jnp.bfloat16)
    cmat = jnp.concatenate([bdiag_c(jnp.real(c_mat)), -bdiag_c(jnp.imag(c_mat))], axis=1).astype(jnp.bfloat16)

    def slab_rows(z):
        z = z.reshape(n_slab, 1, ng * p)
        return jnp.concatenate([jnp.real(z), jnp.imag(z)], axis=1)

    lam_t = slab_rows(lam_bar)
    steps = jnp.arange(1, seg + 1, dtype=f32)
    pw = jnp.exp((lam * dt)[None] * steps[:, None, None])
    pw = pw.reshape(seg, n_slab, ng * p).transpose(1, 0, 2)
    pw = jnp.repeat(pw, SUBLANES, axis=1)
    pw_t = jnp.stack([jnp.real(pw), jnp.imag(pw)], axis=1)
    lamseg_t = slab_rows(jnp.exp(lam * dt * float(seg)))
    d_t = d_skip.astype(f32).reshape(n_slab, 1, ng * c)
    return bmat, cmat, lam_t, pw_t, lamseg_t, d_t


def _ssm(u4, tables, b, l):
    bmat, cmat, lam_t, pw_t, lamseg_t, d_t = tables
    n_slab, t, _ = u4.shape
    tm = TM_SSM
    nt = l // tm
    ns = SSM_CHUNK_STATES
    const = lambda nd: (lambda bi, i: (0,) * nd)
    return pl.pallas_call(
        _ssm_kernel,
        grid=(b, nt),
        in_specs=[
            pl.BlockSpec((n_slab, tm, LANES), lambda bi, i: (0, bi * nt + i, 0)),
            pl.BlockSpec(bmat.shape, const(3)),
            pl.BlockSpec(cmat.shape, const(3)),
            pl.BlockSpec(lam_t.shape, const(3)),
            pl.BlockSpec(pw_t.shape, const(4)),
            pl.BlockSpec(lamseg_t.shape, const(3)),
            pl.BlockSpec(d_t.shape, const(3)),
        ],
        out_specs=pl.BlockSpec((n_slab, tm, LANES), lambda bi, i: (0, bi * nt + i, 0)),
        out_shape=jax.ShapeDtypeStruct((n_slab, t, LANES), jnp.float32),
        scratch_shapes=[
            pltpu.VMEM((2, tm, 2 * ns), jnp.float32),
            pltpu.VMEM((n_slab, 2, ns), jnp.float32),
        ],
        compiler_params=_cparams("arbitrary", "arbitrary"),
        name="s5_scan",
    )(u4, bmat, cmat, lam_t, pw_t, lamseg_t, d_t)


def _merge_kernel(x_ref, attn_ref, y_ref, gate_ref, wglu_ref, bglu_ref, wa_ref, ws_ref, wo_ref,
                  nm_ref, wr_ref, br_ref, tri_ref, x1_ref, h2_ref, ids_ref, rw_ref, rank_ref, cnt_ref,
                  lg_ref, carry_ref):
    i = pl.program_id(0)
    d = x_ref.shape[1]
    tm = x_ref.shape[0]

    @pl.when(i == 0)
    def _():
        lg_ref[...] = jnp.zeros_like(lg_ref)
        carry_ref[...] = jnp.zeros_like(carry_ref)

    logits = lg_ref[...]

    seg = tm // SUBLANES
    y = jnp.concatenate(
        [jnp.concatenate([y_ref[s, pl.ds(j, seg, stride=SUBLANES), :] for j in range(SUBLANES)], axis=0)
         for s in range(y_ref.shape[0])], axis=1)
    z = jax.nn.gelu(y)
    zg = jnp.dot(z.astype(jnp.bfloat16), wglu_ref[...], preferred_element_type=jnp.float32) + bglu_ref[...]
    z = z * jax.nn.sigmoid(zg)
    a = jnp.dot(attn_ref[...], wa_ref[...], preferred_element_type=jnp.float32)
    sb = jnp.dot(z.astype(jnp.bfloat16), ws_ref[...], preferred_element_type=jnp.float32)
    merged = gate_ref[:, 0:d].astype(jnp.float32) * a + gate_ref[:, d:2 * d].astype(jnp.float32) * sb
    x1 = x_ref[...] + jnp.dot(merged.astype(jnp.bfloat16), wo_ref[...], preferred_element_type=jnp.float32)
    x1_ref[...] = x1
    inv = lax.rsqrt(jnp.mean(x1 * x1, axis=-1, keepdims=True) + EPS)
    h2 = x1 * inv * nm_ref[...]
    h_hi = h2.astype(jnp.bfloat16)
    h_pk = _pack_bf16_pairs(h_hi.astype(jnp.float32))
    _store_rows(h2_ref, 0, h_pk)

    h_lo = (h2 - h_hi.astype(jnp.float32)).astype(jnp.bfloat16)
    hh = jnp.dot(h_hi, wr_ref[...], preferred_element_type=jnp.float32)
    lh = jnp.dot(h_lo, wr_ref[:, 0:LANES], preferred_element_type=jnp.float32)
    lg_ref[...] = hh[:, 0:LANES] + hh[:, LANES:2 * LANES] + lh + br_ref[...]

    lane = lax.broadcasted_iota(jnp.int32, (tm, LANES), 1)
    ninf = -jnp.inf
    gl = jnp.where((lane >= N_EXPERTS) & (lane < N_EXPERTS + N_EXPERT_GROUPS), logits, ninf)
    gmax = jnp.max(gl, axis=-1, keepdims=True)
    gidx = jnp.min(jnp.where(gl == gmax, lane - N_EXPERTS, LANES), axis=-1, keepdims=True)
    group_p = 1.0 / jnp.sum(jnp.exp(gl - gmax), axis=-1, keepdims=True)
    el = jnp.where((lane < N_EXPERTS) & ((lane >> 3) == gidx), logits, ninf)
    m1 = jnp.max(el, axis=-1, keepdims=True)
    i1 = jnp.min(jnp.where(el == m1, lane, LANES), axis=-1, keepdims=True)
    el2 = jnp.where(lane == i1, ninf, el)
    m2 = jnp.max(el2, axis=-1, keepdims=True)
    i2 = jnp.min(jnp.where(el2 == m2, lane, LANES), axis=-1, keepdims=True)
    e2 = jnp.exp(m2 - m1)
    w1 = group_p / (1.0 + e2)
    w2 = group_p * e2 / (1.0 + e2)
    ids_ref[...] = jnp.transpose(jnp.where(lane == 0, i1, jnp.where(lane == 1, i2, 0)))[0:SUBLANES, :]
    rw_ref[...] = jnp.where(lane == 0, w1, jnp.where(lane == 1, w2, 0.0))

    oh0 = lane == i1
    oh1 = lane == i2
    live = jnp.where(i > 0, 1.0, 0.0)
    oh = (oh0.astype(jnp.float32) + oh1.astype(jnp.float32)) * live
    cum = jnp.dot(tri_ref[...], oh.astype(jnp.bfloat16), preferred_element_type=jnp.float32) + carry_ref[...]
    r0 = jnp.sum(jnp.where(oh0, cum, 0.0), axis=-1, keepdims=True)
    r1 = jnp.sum(jnp.where(oh1, cum, 0.0), axis=-1, keepdims=True)
    rank_ref[...] = jnp.transpose(
        jnp.where(lane == 0, r0, jnp.where(lane == 1, r1, 0.0)).astype(jnp.int32))[0:SUBLANES, :]
    carry_ref[...] = carry_ref[...] + jnp.sum(oh, axis=0, keepdims=True)
    cnt_ref[...] = carry_ref[...].astype(jnp.int32)


def _merge(x2, attn, y4, gates, w_glu, b_glu, w_a, w_s, w_o, norm_moe, w_router, b_router):
    t, d = x2.shape
    tm = TM_MERGE
    nt = t // tm
    n_slab = y4.shape[0]
    tri = (jnp.arange(tm)[None, :] < jnp.arange(tm)[:, None]).astype(jnp.bfloat16)
    full = lambda a: pl.BlockSpec(a.shape, lambda i: (0,) * a.ndim)
    cur = lambda i: (jnp.minimum(i, nt - 1), 0)
    prv = lambda i: (jnp.maximum(i - 1, 0), 0)
    row = lambda c, m=cur: pl.BlockSpec((tm, c), m)
    slots = pl.BlockSpec((SUBLANES, tm), lambda i: (0, jnp.maximum(i - 1, 0)))
    return pl.pallas_call(
        _merge_kernel,
        grid=(nt + 1,),
        in_specs=[
            row(d), row(attn.shape[1]),
            pl.BlockSpec((n_slab, tm, LANES), lambda i: (0, jnp.minimum(i, nt - 1), 0)),
            row(gates.shape[1]),
            full(w_glu), full(b_glu), full(w_a), full(w_s), full(w_o), full(norm_moe),
            full(w_router), full(b_router), full(tri),
        ],
        out_specs=[row(d), pl.BlockSpec((tm * ROW_SUB, LANES), cur), slots, row(LANES, prv),
                   slots, pl.BlockSpec((1, LANES), lambda i: (0, 0))],
        out_shape=[
            jax.ShapeDtypeStruct((t, d), jnp.float32),
            jax.ShapeDtypeStruct((t * ROW_SUB, LANES), jnp.uint32),
            jax.ShapeDtypeStruct((SUBLANES, t), jnp.int32),
            jax.ShapeDtypeStruct((t, LANES), jnp.float32),
            jax.ShapeDtypeStruct((SUBLANES, t), jnp.int32),
            jax.ShapeDtypeStruct((1, LANES), jnp.int32),
        ],
        scratch_shapes=[pltpu.VMEM((tm, LANES), jnp.float32), pltpu.VMEM((1, LANES), jnp.float32)],
        compiler_params=_cparams("arbitrary"),
        name="merge_router",
    )(x2, attn, y4, gates, w_glu, b_glu, w_a, w_s, w_o, norm_moe, w_router, b_router, tri)


def _inverse_kernel(pos_ref, src_ref):
    n_slots = pos_ref.shape[0]
    t = n_slots // TOP_K

    def body(tok, c):
        for k in range(TOP_K):
            src_ref[pos_ref[k * t + tok]] = tok * ROW_SUB
        return c

    lax.fori_loop(0, t, body, 0, unroll=8)


def _inverse(pos1d):
    n_slots = pos1d.shape[0]
    smem = pl.BlockSpec(memory_space=pltpu.SMEM)
    return pl.pallas_call(
        _inverse_kernel,
        in_specs=[smem],
        out_specs=smem,
        out_shape=jax.ShapeDtypeStruct((n_slots,), jnp.int32),
        name="route_inverse",
    )(pos1d)


def _expert_kernel(ta_ref, tn_ref, tnew_ref, sexp_ref, slo_ref, shi_ref, meta_ref, src_ref,
                   h_ref, wg_hbm, wu_hbm, wd_hbm, y_ref,
                   xbuf_ref, wgs_ref, wus_ref, wds_ref, wgb_ref, wub_ref, wdb_ref, gsem, wsem):
    j = pl.program_id(0)
    n_tiles = pl.num_programs(0)
    last = n_tiles - 1
    tm = TM_EXPERT
    rs = ROW_SUB
    nb = ROW_BUFS
    n_used = meta_ref[0]
    slot = j % nb
    a = ta_ref[j]

    def gather(tile, sl, start):
        _row_gather(h_ref, src_ref, tile * tm, xbuf_ref, sl * tm, tm, gsem.at[sl], start)

    def weights(m, start):
        sl = m % 2
        e = sexp_ref[m]
        for hbm, stage in ((wg_hbm, wgs_ref), (wu_hbm, wus_ref), (wd_hbm, wds_ref)):
            cp = pltpu.make_async_copy(hbm.at[e], stage.at[sl], wsem.at[sl])
            cp.start() if start else cp.wait()

    def new_expert(m):
        weights(m, False)
        sl = m % 2
        wgb_ref[...] = wgs_ref[sl].astype(jnp.bfloat16)
        wub_ref[...] = wus_ref[sl].astype(jnp.bfloat16)
        wdb_ref[...] = wds_ref[sl].astype(jnp.bfloat16)

        @pl.when(m + 1 < n_used)
        def _():
            weights(m + 1, True)

    def expert_out(xp):
        x = _unpack_bf16_pairs(xp).astype(jnp.bfloat16)
        hg = jnp.dot(x, wgb_ref[...], preferred_element_type=jnp.float32)
        hu = jnp.dot(x, wub_ref[...], preferred_element_type=jnp.float32)
        act = (jax.nn.silu(hg) * hu).astype(jnp.bfloat16)
        y = jnp.dot(act, wdb_ref[...], preferred_element_type=jnp.float32)
        return _pack_bf16_pairs(y.astype(jnp.bfloat16).astype(jnp.float32))

    @pl.when(j == 0)
    def _():
        weights(0, True)
        gather(0, 0, True)
        gather(1, 1, True)

    @pl.when(tnew_ref[j] == 1)
    def _():
        new_expert(a)

    gather(j, slot, False)
    xp = _load_rows(xbuf_ref, slot * tm, tm)
    gather(jnp.minimum(j + 2, last), (j + 2) % nb, True)
    _store_rows(y_ref, 0, expert_out(xp))

    def extra(i, carry):
        m = a + i
        new_expert(m)
        yp = expert_out(_load_rows(xbuf_ref, slot * tm, tm))
        row = lax.broadcasted_iota(jnp.int32, (tm, LANES), 0) + j * tm
        mine = (row >= slo_ref[m]) & (row < shi_ref[m])
        _store_rows(y_ref, 0, yp, mine)
        return carry

    lax.fori_loop(1, tn_ref[j], extra, 0)

    @pl.when(j == last)
    def _():
        gather(j, (j + 1) % nb, False)
        gather(j, (j + 2) % nb, False)


def _experts(meta, src_tok, h_fat, w_gate, w_up, w_down):
    n_slots = src_tok.shape[0]
    ne, d, dff = w_gate.shape
    tm = TM_EXPERT
    rs = ROW_SUB
    assert n_slots // tm >= ROW_BUFS
    any_spec = pl.BlockSpec(memory_space=pl.ANY)
    grid_spec = pltpu.PrefetchScalarGridSpec(
        num_scalar_prefetch=8,
        grid=(n_slots // tm,),
        in_specs=[any_spec, any_spec, any_spec, any_spec],
        out_specs=pl.BlockSpec((tm * rs, LANES), lambda j, *_: (j, 0)),
        scratch_shapes=[
            pltpu.VMEM((ROW_BUFS * tm * rs, LANES), jnp.uint32),
            pltpu.VMEM((2, d, dff), jnp.float32),
            pltpu.VMEM((2, d, dff), jnp.float32),
            pltpu.VMEM((2, dff, d), jnp.float32),
            pltpu.VMEM((d, dff), jnp.bfloat16),
            pltpu.VMEM((d, dff), jnp.bfloat16),
            pltpu.VMEM((dff, d), jnp.bfloat16),
            pltpu.SemaphoreType.DMA((ROW_BUFS,)),
            pltpu.SemaphoreType.DMA((2,)),
        ],
    )
    return pl.pallas_call(
        _expert_kernel,
        grid_spec=grid_spec,
        out_shape=jax.ShapeDtypeStruct((n_slots * rs, LANES), jnp.uint32),
        compiler_params=_cparams("arbitrary"),
        name="moe_experts",
    )(*meta, src_tok, h_fat, w_gate, w_up, w_down)


def _combine_kernel(pos_ref, x1_ref, rw_ref, nf_ref, ys_ref, o_ref, buf_ref, sem):
    i = pl.program_id(0)
    n = pl.num_programs(0)
    last = n - 1
    tm = x1_ref.shape[0]
    t = n * tm
    nb = ROW_BUFS
    slot = i % nb

    def gather(tile, sl, start):
        for k in range(TOP_K):
            _row_gather(ys_ref, pos_ref, k * t + tile * tm, buf_ref, (sl * TOP_K + k) * tm, tm, sem.at[sl], start)

    @pl.when(i == 0)
    def _():
        gather(0, 0, True)
        gather(1, 1, True)

    gather(i, slot, False)
    y0 = _unpack_bf16_pairs(_load_rows(buf_ref, (slot * TOP_K) * tm, tm))
    y1 = _unpack_bf16_pairs(_load_rows(buf_ref, (slot * TOP_K + 1) * tm, tm))
    gather(jnp.minimum(i + 2, last), (i + 2) % nb, True)
    rw = rw_ref[...]
    x = x1_ref[...] + rw[:, 0:1] * y0 + rw[:, 1:2] * y1
    inv = lax.rsqrt(jnp.mean(x * x, axis=-1, keepdims=True) + EPS)
    o_ref[...] = x * inv * nf_ref[...]

    @pl.when(i == last)
    def _():
        gather(i, (i + 1) % nb, False)
        gather(i, (i + 2) % nb, False)


def _combine(pos_rows, x1, rw, norm_final, ys):
    t, d = x1.shape
    tm = TM_MOVE
    assert t // tm >= ROW_BUFS
    return pl.pallas_call(
        _combine_kernel,
        grid=(t // tm,),
        in_specs=[
            pl.BlockSpec(memory_space=pltpu.SMEM),
            pl.BlockSpec((tm, d), lambda i: (i, 0)),
            pl.BlockSpec((tm, LANES), lambda i: (i, 0)),
            pl.BlockSpec((1, d), lambda i: (0, 0)),
            pl.BlockSpec(memory_space=pl.ANY),
        ],
        out_specs=pl.BlockSpec((tm, d), lambda i: (i, 0)),
        out_shape=jax.ShapeDtypeStruct((t, d), jnp.float32),
        scratch_shapes=[pltpu.VMEM((ROW_BUFS * TOP_K * tm * ROW_SUB, LANES), jnp.uint32),
                        pltpu.SemaphoreType.DMA((ROW_BUFS,))],
        compiler_params=_cparams("arbitrary"),
        name="moe_combine",
    )(pos_rows, x1, rw, norm_final, ys)


def _expert_meta(counts, n_slots):
    tm = TM_EXPERT
    n_tiles = n_slots // tm
    i32 = jnp.int32
    ends = jnp.cumsum(counts).astype(i32)
    starts = ends - counts
    used = counts > 0
    n_used = jnp.sum(used).astype(i32)
    seq_of = jnp.cumsum(used).astype(i32) - 1
    m_idx = jnp.arange(N_EXPERTS, dtype=i32)
    pick = used[None, :] & (seq_of[None, :] == m_idx[:, None])
    s_exp = jnp.sum(jnp.where(pick, m_idx[None, :], 0), axis=1).astype(i32)
    s_lo = jnp.sum(jnp.where(pick, starts[None, :], 0), axis=1).astype(i32)
    s_hi = jnp.sum(jnp.where(pick, ends[None, :], 0), axis=1).astype(i32)
    row0 = jnp.arange(n_tiles, dtype=i32)[:, None] * tm
    t_a = jnp.sum(used[None, :] & (ends[None, :] <= row0), axis=1).astype(i32)
    t_b = jnp.sum(used[None, :] & (starts[None, :] < row0 + tm), axis=1).astype(i32) - 1
    t_new = jnp.any(used[None, :] & (starts[None, :] == row0), axis=1).astype(i32)
    return t_a, (t_b - t_a + 1).astype(i32), t_new, s_exp, s_lo, s_hi, n_used.reshape(1)


def _moe(x1, h_fat, ids, rw, rank, cnt, w_gate, w_up, w_down, norm_final):
    t, d = x1.shape
    counts = cnt[0, :N_EXPERTS]
    offs = (jnp.cumsum(counts) - counts).astype(jnp.int32)
    onehot = ids[:TOP_K, :, None] == jnp.arange(N_EXPERTS, dtype=jnp.int32)
    pos1d = (jnp.sum(jnp.where(onehot, offs, 0), axis=-1) + rank[:TOP_K]).reshape(-1)
    src_tok = _inverse(pos1d)
    meta = _expert_meta(counts, t * TOP_K)
    ys = _experts(meta, src_tok, h_fat, w_gate, w_up, w_down)
    return _combine(pos1d * ROW_SUB, x1, rw, norm_final, ys)


def kernel(x, norm_mix, w_in, b_gate, attn_sinks, ssm_a_re, ssm_a_im, ssm_b_re, ssm_b_im, ssm_c_re, ssm_c_im, ssm_d, ssm_log_dt, w_glu, b_glu, w_attn_branch, w_ssm_branch, w_out, norm_moe, w_router_group, b_router_group, w_router_expert, b_router_expert, w_expert_gate, w_expert_up, w_expert_down, norm_final):
    b, l, d = x.shape
    depth = w_in.shape[0]
    assert depth == 1, "the final norm is fused into the last layer's combine kernel"
    d_attn = N_HEADS * HEAD_DIM
    kv_cols = N_KV_HEADS * HEAD_DIM
    d_ssm = ssm_d.shape[-1]
    bf16 = jnp.bfloat16
    x2 = x.reshape(b * l, d)
    assert TM_PROJ == TM_SSM == TM_MERGE and l % TM_SSM == 0
    i = 0
    q, k, v, u4, gates = _in_proj(x2, norm_mix[i][None], w_in[i].astype(bf16), b_gate[i][None],
                                  d_attn, kv_cols, d_ssm)
    attn = _attention(q, k, v, attn_sinks[i], b, l)
    tables = _ssm_tables(ssm_a_re[i], ssm_a_im[i], ssm_b_re[i], ssm_b_im[i], ssm_c_re[i], ssm_c_im[i],
                         ssm_d[i], ssm_log_dt[i], SSM_SEG)
    y4 = _ssm(u4, tables, b, l)
    pad = LANES - N_EXPERTS - N_EXPERT_GROUPS
    w_router = jnp.concatenate([w_router_expert[i], w_router_group[i], jnp.zeros((d, pad), jnp.float32)], axis=1)
    w_r_hi = w_router.astype(bf16)
    w_router = jnp.concatenate([w_r_hi, (w_router - w_r_hi.astype(jnp.float32)).astype(bf16)], axis=1)
    b_router = jnp.concatenate([b_router_expert[i], b_router_group[i], jnp.zeros((pad,), jnp.float32)])[None]
    x1, h_fat, ids, rw, rank, cnt = _merge(x2, attn, y4, gates, w_glu[i].astype(bf16), b_glu[i][None],
                                w_attn_branch[i].astype(bf16), w_ssm_branch[i].astype(bf16),
                                w_out[i].astype(bf16), norm_moe[i][None], w_router, b_router)
    out = _moe(x1, h_fat, ids, rw, rank, cnt, w_expert_gate[i], w_expert_up[i], w_expert_down[i], norm_final[None])
    return out.reshape(b, l, d)
```

```python
import functools
import math

import jax
import jax.numpy as jnp
from jax import lax
from jax.experimental import pallas as pl
from jax.experimental.pallas import tpu as pltpu

EPS = 1e-6
HEAD_DIM = 64
N_HEADS = 8
N_KV_HEADS = 2
Q_PER_KV = N_HEADS // N_KV_HEADS
ATTN_BLOCK = 128
ATTN_QB = 8
SSM_GROUP = 16
SSM_STATE = 64
N_EXPERT_GROUPS = 4
EXPERTS_PER_GROUP = 8
N_EXPERTS = N_EXPERT_GROUPS * EXPERTS_PER_GROUP
TOP_K = 2

LANES = 128
SUBLANES = 8
SSM_CHUNK_GROUPS = LANES // SSM_GROUP
SSM_CHUNK_STATES = SSM_CHUNK_GROUPS * SSM_STATE

TM_PROJ = 512
TM_SSM = 512
SSM_SEG = TM_SSM // SUBLANES
TM_MERGE = 512
TM_MOVE = 512
TM_EXPERT = 256
ROW_SUB = 4
ROW_BUFS = 3
VMEM_LIMIT = 56 * 1024 * 1024


def _cparams(*sem):
    return pltpu.CompilerParams(dimension_semantics=sem, vmem_limit_bytes=VMEM_LIMIT)


def _pack_bf16_pairs(x):
    half = x.shape[1] // 2
    bits = lax.bitcast_convert_type(x, jnp.uint32)
    return (bits[:, :half] & jnp.uint32(0xFFFF0000)) | (bits[:, half:] >> 16)


def _unpack_bf16_pairs(p):
    hi = lax.bitcast_convert_type(p & jnp.uint32(0xFFFF0000), jnp.float32)
    lo = lax.bitcast_convert_type(p << 16, jnp.float32)
    return jnp.concatenate([hi, lo], axis=1)


def _load_rows(ref, first, n):
    return jnp.concatenate(
        [ref[pl.ds(first * ROW_SUB + c, n, stride=ROW_SUB), :] for c in range(ROW_SUB)], axis=1)


def _store_rows(ref, first, val, mask=None):
    n = val.shape[0]
    for c in range(ROW_SUB):
        idx = pl.ds(first * ROW_SUB + c, n, stride=ROW_SUB)
        v = val[:, c * LANES:(c + 1) * LANES]
        ref[idx, :] = v if mask is None else jnp.where(mask, v, ref[idx, :])


def _row_gather(src_hbm, idx_ref, idx0, dst_ref, dst0, n, sem, start):
    rs = ROW_SUB
    if start:
        for r in range(n):
            pltpu.make_async_copy(src_hbm.at[pl.ds(pl.multiple_of(idx_ref[idx0 + r], rs), rs)],
                                  dst_ref.at[pl.ds(pl.multiple_of((dst0 + r) * rs, rs), rs)],
                                  sem).start(priority=r % 2)
    else:
        pltpu.make_async_copy(src_hbm.at[pl.ds(0, n * rs)],
                              dst_ref.at[pl.ds(pl.multiple_of(dst0 * rs, rs), n * rs)], sem).wait()


def _proj_kernel(x_ref, g_ref, w_ref, bg_ref, q_ref, k_ref, v_ref, u_ref, gate_ref, *, cols):
    q_c, kv_c, d_ssm = cols
    xf = x_ref[...]
    inv = lax.rsqrt(jnp.mean(xf * xf, axis=-1, keepdims=True) + EPS)
    h = (xf * inv * g_ref[...]).astype(jnp.bfloat16)
    o_gate = q_c + 2 * kv_c + d_ssm
    gl = jnp.dot(h, w_ref[:, o_gate:], preferred_element_type=jnp.float32) + bg_ref[...]
    gate_ref[...] = jax.nn.sigmoid(gl).astype(gate_ref.dtype)
    o = 0
    q_ref[...] = (jnp.dot(h, w_ref[:, o:o + q_c], preferred_element_type=jnp.float32)
                  * (1.0 / math.sqrt(HEAD_DIM))).astype(q_ref.dtype)
    o += q_c
    k_ref[...] = jnp.dot(h, w_ref[:, o:o + kv_c], preferred_element_type=jnp.float32).astype(k_ref.dtype)
    o += kv_c
    v_ref[...] = jnp.dot(h, w_ref[:, o:o + kv_c], preferred_element_type=jnp.float32).astype(v_ref.dtype)
    o += kv_c
    seg = x_ref.shape[0] // SUBLANES
    for s in range(0, d_ssm // LANES, 2):
        uu = jnp.dot(h, w_ref[:, o:o + 2 * LANES], preferred_element_type=jnp.float32)
        for half in range(2):
            for j in range(SUBLANES):
                u_ref[s + half, pl.ds(j, seg, stride=SUBLANES), :] = (
                    uu[j * seg:(j + 1) * seg, half * LANES:(half + 1) * LANES])
        o += 2 * LANES


def _in_proj(x2, norm_w, w_in, b_gate, d_attn, kv_cols, d_ssm):
    t, d = x2.shape
    gate_cols = b_gate.shape[-1]
    n_slab = d_ssm // LANES
    tm = TM_PROJ
    kern = functools.partial(_proj_kernel, cols=(d_attn, kv_cols, d_ssm))
    return pl.pallas_call(
        kern,
        grid=(t // tm,),
        in_specs=[
            pl.BlockSpec((tm, d), lambda i: (i, 0)),
            pl.BlockSpec((1, d), lambda i: (0, 0)),
            pl.BlockSpec(w_in.shape, lambda i: (0, 0)),
            pl.BlockSpec((1, gate_cols), lambda i: (0, 0)),
        ],
        out_specs=[
            pl.BlockSpec((tm, d_attn), lambda i: (i, 0)),
            pl.BlockSpec((tm, kv_cols), lambda i: (i, 0)),
            pl.BlockSpec((tm, kv_cols), lambda i: (i, 0)),
            pl.BlockSpec((n_slab, tm, LANES), lambda i: (0, i, 0)),
            pl.BlockSpec((tm, gate_cols), lambda i: (i, 0)),
        ],
        out_shape=[
            jax.ShapeDtypeStruct((t, d_attn), jnp.bfloat16),
            jax.ShapeDtypeStruct((t, kv_cols), jnp.bfloat16),
            jax.ShapeDtypeStruct((t, kv_cols), jnp.bfloat16),
            jax.ShapeDtypeStruct((n_slab, t, LANES), jnp.float32),
            jax.ShapeDtypeStruct((t, gate_cols), jnp.bfloat16),
        ],
        compiler_params=_cparams("arbitrary"),
        name="in_proj",
    )(x2, norm_w, w_in, b_gate)


def _attn_kernel(sink_ref, q_ref, kp_ref, kc_ref, vp_ref, vc_ref, rep_ref, mask_ref, o_ref):
    i = pl.program_id(1)
    blk = ATTN_BLOCK
    hw = Q_PER_KV * HEAD_DIM
    rows = Q_PER_KV * blk
    head_of_row = lax.broadcasted_iota(jnp.int32, (rows, 1), 0) >> 7
    lane_head_q = lax.broadcasted_iota(jnp.int32, (blk, hw), 1) >> 6
    lane_head_v = lax.broadcasted_iota(jnp.int32, (2 * blk, hw), 1) >> 6
    sinks = []
    for kh in range(N_KV_HEADS):
        sink = jnp.zeros((rows, 1), jnp.float32)
        for g in range(Q_PER_KV):
            sink = jnp.where(head_of_row == g, sink_ref[kh * Q_PER_KV + g], sink)
        sinks.append(sink)
    for qb in range(ATTN_QB):
        if qb == 0:
            k2 = jnp.concatenate([kp_ref[...], kc_ref[0:blk, :]], axis=0)
            v2 = jnp.concatenate([vp_ref[...], vc_ref[0:blk, :]], axis=0)
            bias = mask_ref[jnp.where(i == 0, 1, 0)]
        else:
            k2 = kc_ref[(qb - 1) * blk:(qb + 1) * blk, :]
            v2 = vc_ref[(qb - 1) * blk:(qb + 1) * blk, :]
            bias = mask_ref[0]
        for kh in range(N_KV_HEADS):
            rep = rep_ref[kh]
            k4 = jnp.dot(k2, rep, preferred_element_type=jnp.float32).astype(jnp.bfloat16)
            v4 = jnp.dot(v2, rep, preferred_element_type=jnp.float32).astype(jnp.bfloat16)
            qh = q_ref[qb * blk:(qb + 1) * blk, kh * hw:(kh + 1) * hw]
            qm = jnp.concatenate(
                [jnp.where(lane_head_q == g, qh, jnp.zeros_like(qh)) for g in range(Q_PER_KV)], axis=0)
            s = lax.dot_general(qm, k4, (((1,), (1,)), ((), ())), preferred_element_type=jnp.float32)
            s = s + bias
            sink = sinks[kh]
            m = jnp.maximum(jnp.max(s, axis=-1, keepdims=True), sink)
            p = jnp.exp(s - m)
            rinv = 1.0 / (jnp.sum(p, axis=-1, keepdims=True) + jnp.exp(sink - m))
            p = p.astype(jnp.bfloat16)
            p_cat = jnp.concatenate([p[g * blk:(g + 1) * blk, :] for g in range(Q_PER_KV)], axis=1)
            vm = jnp.concatenate(
                [jnp.where(lane_head_v == g, v4, jnp.zeros_like(v4)) for g in range(Q_PER_KV)], axis=0)
            o = jnp.dot(p_cat, vm, preferred_element_type=jnp.float32)
            scale = jnp.zeros((blk, hw), jnp.float32)
            for g in range(Q_PER_KV):
                scale = jnp.where(lane_head_q == g, rinv[g * blk:(g + 1) * blk, :], scale)
            o_ref[qb * blk:(qb + 1) * blk, kh * hw:(kh + 1) * hw] = (o * scale).astype(o_ref.dtype)


def _attention(q, k, v, sinks, b, l):
    d_attn = q.shape[-1]
    kv_cols = k.shape[-1]
    blk = ATTN_BLOCK
    tq = ATTN_QB * blk
    hw = Q_PER_KV * HEAD_DIM
    lane = jnp.arange(hw)[None, :]
    src = jnp.arange(kv_cols)[:, None]
    rep = jnp.stack([(src == kh * HEAD_DIM + (lane % HEAD_DIM)) for kh in range(N_KV_HEADS)]
                    ).astype(jnp.bfloat16)
    r = (jnp.arange(Q_PER_KV * blk) % blk)[:, None]
    c = jnp.arange(2 * blk)[None, :]
    band = (c > r) & (c <= r + blk)
    mask = jnp.where(jnp.stack([band, band & (c >= blk)]), 0.0, jnp.finfo(jnp.float32).min).astype(jnp.float32)
    q3 = q.reshape(b, l, d_attn)
    k3 = k.reshape(b, l, kv_cols)
    v3 = v.reshape(b, l, kv_cols)
    cur = lambda bi, i: (bi, i, 0)
    prev = lambda bi, i: (bi, jnp.maximum(ATTN_QB * i - 1, 0), 0)
    out = pl.pallas_call(
        _attn_kernel,
        grid=(b, l // tq),
        in_specs=[
            pl.BlockSpec(memory_space=pltpu.SMEM),
            pl.BlockSpec((None, tq, d_attn), cur),
            pl.BlockSpec((None, blk, kv_cols), prev),
            pl.BlockSpec((None, tq, kv_cols), cur),
            pl.BlockSpec((None, blk, kv_cols), prev),
            pl.BlockSpec((None, tq, kv_cols), cur),
            pl.BlockSpec(rep.shape, lambda bi, i: (0, 0, 0)),
            pl.BlockSpec(mask.shape, lambda bi, i: (0, 0, 0)),
        ],
        out_specs=pl.BlockSpec((None, tq, d_attn), cur),
        out_shape=jax.ShapeDtypeStruct((b, l, d_attn), jnp.bfloat16),
        compiler_params=_cparams("arbitrary", "arbitrary"),
        name="swa",
    )(sinks, q3, k3, k3, v3, v3, rep, mask)
    return out.reshape(b * l, d_attn)


def _ssm_kernel(u_ref, bmat_ref, cmat_ref, lam_ref, pw_ref, lamseg_ref, d_ref, y_ref, bu_ref, carry_ref):
    i = pl.program_id(1)
    n_slab = u_ref.shape[0]
    tm = u_ref.shape[1]
    seg = tm // SUBLANES
    ns = SSM_CHUNK_STATES
    npair = 2

    @pl.when(i == 0)
    def _():
        carry_ref[...] = jnp.zeros_like(carry_ref)

    sub = lax.broadcasted_iota(jnp.int32, (SUBLANES, ns), 0)
    for s0 in range(0, n_slab, npair):
        lam = []
        for q in range(npair):
            s = s0 + q
            bu_ref[q] = jnp.dot(u_ref[s].astype(jnp.bfloat16), bmat_ref[s], preferred_element_type=jnp.float32)
            lam.append((jnp.broadcast_to(lam_ref[s, 0:1, :], (SUBLANES, ns)),
                        jnp.broadcast_to(lam_ref[s, 1:2, :], (SUBLANES, ns))))

        def step(r, st):
            rows = pl.ds(r * SUBLANES, SUBLANES)
            out = []
            for q in range(npair):
                lr, li = lam[q]
                sr, si = st[q]
                nr = lr * sr - li * si + bu_ref[q, rows, 0:ns]
                ni = lr * si + li * sr + bu_ref[q, rows, ns:2 * ns]
                bu_ref[q, rows, 0:ns] = nr
                bu_ref[q, rows, ns:2 * ns] = ni
                out.append((nr, ni))
            return tuple(out)

        zero = jnp.zeros((SUBLANES, ns), jnp.float32)
        ends = ((zero, zero),) * npair
        for r in range(seg):
            ends = step(r, ends)

        for q in range(npair):
            s = s0 + q
            er, ei = ends[q]
            ar = lamseg_ref[s, 0:1, :]
            ai = lamseg_ref[s, 1:2, :]
            cr = carry_ref[s, 0:1, :]
            ci = carry_ref[s, 1:2, :]
            car = jnp.zeros((SUBLANES, ns), jnp.float32)
            cai = jnp.zeros((SUBLANES, ns), jnp.float32)
            for j in range(SUBLANES):
                car = jnp.where(sub == j, jnp.broadcast_to(cr, (SUBLANES, ns)), car)
                cai = jnp.where(sub == j, jnp.broadcast_to(ci, (SUBLANES, ns)), cai)
                ejr = jnp.sum(jnp.where(sub == j, er, 0.0), axis=0, keepdims=True)
                eji = jnp.sum(jnp.where(sub == j, ei, 0.0), axis=0, keepdims=True)
                cr, ci = ar * cr - ai * ci + ejr, ar * ci + ai * cr + eji
            carry_ref[s, 0:1, :] = cr
            carry_ref[s, 1:2, :] = ci

            ctr = jnp.broadcast_to(car[None], (seg, SUBLANES, ns)).reshape(tm, ns)
            cti = jnp.broadcast_to(cai[None], (seg, SUBLANES, ns)).reshape(tm, ns)
            pr = pw_ref[s, 0]
            pi = pw_ref[s, 1]
            st_r = bu_ref[q, :, 0:ns] + pr * ctr - pi * cti
            st_i = bu_ref[q, :, ns:2 * ns] + pr * cti + pi * ctr
            st = jnp.concatenate([st_r, st_i], axis=1).astype(jnp.bfloat16)
            y_ref[s] = jnp.dot(st, cmat_ref[s], preferred_element_type=jnp.float32) + d_ref[s] * u_ref[s]


def _ssm_tables(a_re, a_im, b_re, b_im, c_re, c_im, d_skip, log_dt, seg):
    f32 = jnp.float32
    g, p = a_re.shape
    c = b_re.shape[-1]
    ng = SSM_CHUNK_GROUPS
    n_slab = g // ng
    lam = lax.complex(a_re.astype(f32), a_im.astype(f32))
    dt = jnp.exp(log_dt.astype(f32))[:, None]
    lam_bar = jnp.exp(lam * dt)
    b_bar = ((lam_bar - 1.0) / lam)[:, :, None] * lax.complex(b_re.astype(f32), b_im.astype(f32))
    c_mat = lax.complex(c_re.astype(f32), c_im.astype(f32))
    eye = jnp.eye(ng, dtype=f32)

    def bdiag_b(m):
        m = m.reshape(n_slab, ng, p, c)
        return jnp.einsum('ab,kbpc->kacbp', eye, m).reshape(n_slab, ng * c, ng * p)

    def bdiag_c(m):
        m = m.reshape(n_slab, ng, c, p)
        return jnp.einsum('ab,kbcp->kbpac', eye, m).reshape(n_slab, ng * p, ng * c)

    bmat = jnp.concatenate([bdiag_b(jnp.real(b_bar)), bdiag_b(jnp.imag(b_bar))], axis=2).astype(jnp.bfloat16)
    cmat = jnp.concatenate([bdiag_c(jnp.real(c_mat)), -bdiag_c(jnp.imag(c_mat))], axis=1).astype(jnp.bfloat16)

    def slab_rows(z):
        z = z.reshape(n_slab, 1, ng * p)
        return jnp.concatenate([jnp.real(z), jnp.imag(z)], axis=1)

    lam_t = slab_rows(lam_bar)
    steps = jnp.arange(1, seg + 1, dtype=f32)
    pw = jnp.exp((lam * dt)[None] * steps[:, None, None])
    pw = pw.reshape(seg, n_slab, ng * p).transpose(1, 0, 2)
    pw = jnp.repeat(pw, SUBLANES, axis=1)
    pw_t = jnp.stack([jnp.real(pw), jnp.imag(pw)], axis=1)
    lamseg_t = slab_rows(jnp.exp(lam * dt * float(seg)))
    d_t = d_skip.astype(f32).reshape(n_slab, 1, ng * c)
    return bmat, cmat, lam_t, pw_t, lamseg_t, d_t


def _ssm(u4, tables, b, l):
    bmat, cmat, lam_t, pw_t, lamseg_t, d_t = tables
    n_slab, t, _ = u4.shape
    tm = TM_SSM
    nt = l // tm
    ns = SSM_CHUNK_STATES
    const = lambda nd: (lambda bi, i: (0,) * nd)
    return pl.pallas_call(
        _ssm_kernel,
        grid=(b, nt),
        in_specs=[
            pl.BlockSpec((n_slab, tm, LANES), lambda bi, i: (0, bi * nt + i, 0)),
            pl.BlockSpec(bmat.shape, const(3)),
            pl.BlockSpec(cmat.shape, const(3)),
            pl.BlockSpec(lam_t.shape, const(3)),
            pl.BlockSpec(pw_t.shape, const(4)),
            pl.BlockSpec(lamseg_t.shape, const(3)),
            pl.BlockSpec(d_t.shape, const(3)),
        ],
        out_specs=pl.BlockSpec((n_slab, tm, LANES), lambda bi, i: (0, bi * nt + i, 0)),
        out_shape=jax.ShapeDtypeStruct((n_slab, t, LANES), jnp.float32),
        scratch_shapes=[
            pltpu.VMEM((2, tm, 2 * ns), jnp.float32),
            pltpu.VMEM((n_slab, 2, ns), jnp.float32),
        ],
        compiler_params=_cparams("arbitrary", "arbitrary"),
        name="s5_scan",
    )(u4, bmat, cmat, lam_t, pw_t, lamseg_t, d_t)


def _merge_kernel(x_ref, attn_ref, y_ref, gate_ref, wglu_ref, bglu_ref, wa_ref, ws_ref, wo_ref,
                  nm_ref, wr_ref, br_ref, tri_ref, x1_ref, h2_ref, ids_ref, rw_ref, rank_ref, cnt_ref,
                  lg_ref, carry_ref):
    i = pl.program_id(0)
    d = x_ref.shape[1]
    tm = x_ref.shape[0]

    @pl.when(i == 0)
    def _():
        lg_ref[...] = jnp.zeros_like(lg_ref)
        carry_ref[...] = jnp.zeros_like(carry_ref)

    logits = lg_ref[...]

    seg = tm // SUBLANES
    y = jnp.concatenate(
        [jnp.concatenate([y_ref[s, pl.ds(j, seg, stride=SUBLANES), :] for j in range(SUBLANES)], axis=0)
         for s in range(y_ref.shape[0])], axis=1)
    z = jax.nn.gelu(y)
    zg = jnp.dot(z.astype(jnp.bfloat16), wglu_ref[...], preferred_element_type=jnp.float32) + bglu_ref[...]
    z = z * jax.nn.sigmoid(zg)
    a = jnp.dot(attn_ref[...], wa_ref[...], preferred_element_type=jnp.float32)
    sb = jnp.dot(z.astype(jnp.bfloat16), ws_ref[...], preferred_element_type=jnp.float32)
    merged = gate_ref[:, 0:d].astype(jnp.float32) * a + gate_ref[:, d:2 * d].astype(jnp.float32) * sb
    x1 = x_ref[...] + jnp.dot(merged.astype(jnp.bfloat16), wo_ref[...], preferred_element_type=jnp.float32)
    x1_ref[...] = x1
    inv = lax.rsqrt(jnp.mean(x1 * x1, axis=-1, keepdims=True) + EPS)
    h2 = x1 * inv * nm_ref[...]
    h_hi = h2.astype(jnp.bfloat16)
    h_pk = _pack_bf16_pairs(h_hi.astype(jnp.float32))
    _store_rows(h2_ref, 0, h_pk)

    h_lo = (h2 - h_hi.astype(jnp.float32)).astype(jnp.bfloat16)
    hh = jnp.dot(h_hi, wr_ref[...], preferred_element_type=jnp.float32)
    lh = jnp.dot(h_lo, wr_ref[:, 0:LANES], preferred_element_type=jnp.float32)
    lg_ref[...] = hh[:, 0:LANES] + hh[:, LANES:2 * LANES] + lh + br_ref[...]

    lane = lax.broadcasted_iota(jnp.int32, (tm, LANES), 1)
    ninf = -jnp.inf
    gl = jnp.where((lane >= N_EXPERTS) & (lane < N_EXPERTS + N_EXPERT_GROUPS), logits, ninf)
    gmax = jnp.max(gl, axis=-1, keepdims=True)
    gidx = jnp.min(jnp.where(gl == gmax, lane - N_EXPERTS, LANES), axis=-1, keepdims=True)
    group_p = 1.0 / jnp.sum(jnp.exp(gl - gmax), axis=-1, keepdims=True)
    el = jnp.where((lane < N_EXPERTS) & ((lane >> 3) == gidx), logits, ninf)
    m1 = jnp.max(el, axis=-1, keepdims=True)
    i1 = jnp.min(jnp.where(el == m1, lane, LANES), axis=-1, keepdims=True)
    el2 = jnp.where(lane == i1, ninf, el)
    m2 = jnp.max(el2, axis=-1, keepdims=True)
    i2 = jnp.min(jnp.where(el2 == m2, lane, LANES), axis=-1, keepdims=True)
    e2 = jnp.exp(m2 - m1)
    w1 = group_p / (1.0 + e2)
    w2 = group_p * e2 / (1.0 + e2)
    ids_ref[...] = jnp.transpose(jnp.where(lane == 0, i1, jnp.where(lane == 1, i2, 0)))[0:SUBLANES, :]
    rw_ref[...] = jnp.where(lane == 0, w1, jnp.where(lane == 1, w2, 0.0))

    oh0 = lane == i1
    oh1 = lane == i2
    live = jnp.where(i > 0, 1.0, 0.0)
    oh = (oh0.astype(jnp.float32) + oh1.astype(jnp.float32)) * live
    cum = jnp.dot(tri_ref[...], oh.astype(jnp.bfloat16), preferred_element_type=jnp.float32) + carry_ref[...]
    r0 = jnp.sum(jnp.where(oh0, cum, 0.0), axis=-1, keepdims=True)
    r1 = jnp.sum(jnp.where(oh1, cum, 0.0), axis=-1, keepdims=True)
    rank_ref[...] = jnp.transpose(
        jnp.where(lane == 0, r0, jnp.where(lane == 1, r1, 0.0)).astype(jnp.int32))[0:SUBLANES, :]
    carry_ref[...] = carry_ref[...] + jnp.sum(oh, axis=0, keepdims=True)
    cnt_ref[...] = carry_ref[...].astype(jnp.int32)


def _merge(x2, attn, y4, gates, w_glu, b_glu, w_a, w_s, w_o, norm_moe, w_router, b_router):
    t, d = x2.shape
    tm = TM_MERGE
    nt = t // tm
    n_slab = y4.shape[0]
    tri = (jnp.arange(tm)[None, :] < jnp.arange(tm)[:, None]).astype(jnp.bfloat16)
    full = lambda a: pl.BlockSpec(a.shape, lambda i: (0,) * a.ndim)
    cur = lambda i: (jnp.minimum(i, nt - 1), 0)
    prv = lambda i: (jnp.maximum(i - 1, 0), 0)
    row = lambda c, m=cur: pl.BlockSpec((tm, c), m)
    slots = pl.BlockSpec((SUBLANES, tm), lambda i: (0, jnp.maximum(i - 1, 0)))
    return pl.pallas_call(
        _merge_kernel,
        grid=(nt + 1,),
        in_specs=[
            row(d), row(attn.shape[1]),
            pl.BlockSpec((n_slab, tm, LANES), lambda i: (0, jnp.minimum(i, nt - 1), 0)),
            row(gates.shape[1]),
            full(w_glu), full(b_glu), full(w_a), full(w_s), full(w_o), full(norm_moe),
            full(w_router), full(b_router), full(tri),
        ],
        out_specs=[row(d), pl.BlockSpec((tm * ROW_SUB, LANES), cur), slots, row(LANES, prv),
                   slots, pl.BlockSpec((1, LANES), lambda i: (0, 0))],
        out_shape=[
            jax.ShapeDtypeStruct((t, d), jnp.float32),
            jax.ShapeDtypeStruct((t * ROW_SUB, LANES), jnp.uint32),
            jax.ShapeDtypeStruct((SUBLANES, t), jnp.int32),
            jax.ShapeDtypeStruct((t, LANES), jnp.float32),
            jax.ShapeDtypeStruct((SUBLANES, t), jnp.int32),
            jax.ShapeDtypeStruct((1, LANES), jnp.int32),
        ],
        scratch_shapes=[pltpu.VMEM((tm, LANES), jnp.float32), pltpu.VMEM((1, LANES), jnp.float32)],
        compiler_params=_cparams("arbitrary"),
        name="merge_router",
    )(x2, attn, y4, gates, w_glu, b_glu, w_a, w_s, w_o, norm_moe, w_router, b_router, tri)


def _inverse_kernel(pos_ref, src_ref):
    n_slots = pos_ref.shape[0]
    t = n_slots // TOP_K

    def body(tok, c):
        for k in range(TOP_K):
            src_ref[pos_ref[k * t + tok]] = tok * ROW_SUB
        return c

    lax.fori_loop(0, t, body, 0, unroll=8)


def _inverse(pos1d):
    n_slots = pos1d.shape[0]
    smem = pl.BlockSpec(memory_space=pltpu.SMEM)
    return pl.pallas_call(
        _inverse_kernel,
        in_specs=[smem],
        out_specs=smem,
        out_shape=jax.ShapeDtypeStruct((n_slots,), jnp.int32),
        name="route_inverse",
    )(pos1d)


def _expert_kernel(ta_ref, tn_ref, tnew_ref, sexp_ref, slo_ref, shi_ref, meta_ref, src_ref,
                   h_ref, wg_hbm, wu_hbm, wd_hbm, y_ref,
                   xbuf_ref, wgs_ref, wus_ref, wds_ref, wgb_ref, wub_ref, wdb_ref, gsem, wsem):
    j = pl.program_id(0)
    n_tiles = pl.num_programs(0)
    last = n_tiles - 1
    tm = TM_EXPERT
    rs = ROW_SUB
    nb = ROW_BUFS
    n_used = meta_ref[0]
    slot = j % nb
    a = ta_ref[j]

    def gather(tile, sl, start):
        _row_gather(h_ref, src_ref, tile * tm, xbuf_ref, sl * tm, tm, gsem.at[sl], start)

    def weights(m, start):
        sl = m % 2
        e = sexp_ref[m]
        for hbm, stage in ((wg_hbm, wgs_ref), (wu_hbm, wus_ref), (wd_hbm, wds_ref)):
            cp = pltpu.make_async_copy(hbm.at[e], stage.at[sl], wsem.at[sl])
            cp.start() if start else cp.wait()

    def new_expert(m):
        weights(m, False)
        sl = m % 2
        wgb_ref[...] = wgs_ref[sl].astype(jnp.bfloat16)
        wub_ref[...] = wus_ref[sl].astype(jnp.bfloat16)
        wdb_ref[...] = wds_ref[sl].astype(jnp.bfloat16)

        @pl.when(m + 1 < n_used)
        def _():
            weights(m + 1, True)

    def expert_out(xp):
        x = _unpack_bf16_pairs(xp).astype(jnp.bfloat16)
        hg = jnp.dot(x, wgb_ref[...], preferred_element_type=jnp.float32)
        hu = jnp.dot(x, wub_ref[...], preferred_element_type=jnp.float32)
        act = (jax.nn.silu(hg) * hu).astype(jnp.bfloat16)
        y = jnp.dot(act, wdb_ref[...], preferred_element_type=jnp.float32)
        return _pack_bf16_pairs(y.astype(jnp.bfloat16).astype(jnp.float32))

    @pl.when(j == 0)
    def _():
        weights(0, True)
        gather(0, 0, True)
        gather(1, 1, True)

    @pl.when(tnew_ref[j] == 1)
    def _():
        new_expert(a)

    gather(j, slot, False)
    xp = _load_rows(xbuf_ref, slot * tm, tm)
    gather(jnp.minimum(j + 2, last), (j + 2) % nb, True)
    _store_rows(y_ref, 0, expert_out(xp))

    def extra(i, carry):
        m = a + i
        new_expert(m)
        yp = expert_out(_load_rows(xbuf_ref, slot * tm, tm))
        row = lax.broadcasted_iota(jnp.int32, (tm, LANES), 0) + j * tm
        mine = (row >= slo_ref[m]) & (row < shi_ref[m])
        _store_rows(y_ref, 0, yp, mine)
        return carry

    lax.fori_loop(1, tn_ref[j], extra, 0)

    @pl.when(j == last)
    def _():
        gather(j, (j + 1) % nb, False)
        gather(j, (j + 2) % nb, False)


def _experts(meta, src_tok, h_fat, w_gate, w_up, w_down):
    n_slots = src_tok.shape[0]
    ne, d, dff = w_gate.shape
    tm = TM_EXPERT
    rs = ROW_SUB
    assert n_slots // tm >= ROW_BUFS
    any_spec = pl.BlockSpec(memory_space=pl.ANY)
    grid_spec = pltpu.PrefetchScalarGridSpec(
        num_scalar_prefetch=8,
        grid=(n_slots // tm,),
        in_specs=[any_spec, any_spec, any_spec, any_spec],
        out_specs=pl.BlockSpec((tm * rs, LANES), lambda j, *_: (j, 0)),
        scratch_shapes=[
            pltpu.VMEM((ROW_BUFS * tm * rs, LANES), jnp.uint32),
            pltpu.VMEM((2, d, dff), jnp.float32),
            pltpu.VMEM((2, d, dff), jnp.float32),
            pltpu.VMEM((2, dff, d), jnp.float32),
            pltpu.VMEM((d, dff), jnp.bfloat16),
            pltpu.VMEM((d, dff), jnp.bfloat16),
            pltpu.VMEM((dff, d), jnp.bfloat16),
            pltpu.SemaphoreType.DMA((ROW_BUFS,)),
            pltpu.SemaphoreType.DMA((2,)),
        ],
    )
    return pl.pallas_call(
        _expert_kernel,
        grid_spec=grid_spec,
        out_shape=jax.ShapeDtypeStruct((n_slots * rs, LANES), jnp.uint32),
        compiler_params=_cparams("arbitrary"),
        name="moe_experts",
    )(*meta, src_tok, h_fat, w_gate, w_up, w_down)


def _combine_kernel(pos_ref, x1_ref, rw_ref, nf_ref, ys_ref, o_ref, buf_ref, sem):
    i = pl.program_id(0)
    n = pl.num_programs(0)
    last = n - 1
    tm = x1_ref.shape[0]
    t = n * tm
    nb = ROW_BUFS
    slot = i % nb

    def gather(tile, sl, start):
        for k in range(TOP_K):
            _row_gather(ys_ref, pos_ref, k * t + tile * tm, buf_ref, (sl * TOP_K + k) * tm, tm, sem.at[sl], start)

    @pl.when(i == 0)
    def _():
        gather(0, 0, True)
        gather(1, 1, True)

    gather(i, slot, False)
    y0 = _unpack_bf16_pairs(_load_rows(buf_ref, (slot * TOP_K) * tm, tm))
    y1 = _unpack_bf16_pairs(_load_rows(buf_ref, (slot * TOP_K + 1) * tm, tm))
    gather(jnp.minimum(i + 2, last), (i + 2) % nb, True)
    rw = rw_ref[...]
    x = x1_ref[...] + rw[:, 0:1] * y0 + rw[:, 1:2] * y1
    inv = lax.rsqrt(jnp.mean(x * x, axis=-1, keepdims=True) + EPS)
    o_ref[...] = x * inv * nf_ref[...]

    @pl.when(i == last)
    def _():
        gather(i, (i + 1) % nb, False)
        gather(i, (i + 2) % nb, False)


def _combine(pos_rows, x1, rw, norm_final, ys):
    t, d = x1.shape
    tm = TM_MOVE
    assert t // tm >= ROW_BUFS
    return pl.pallas_call(
        _combine_kernel,
        grid=(t // tm,),
        in_specs=[
            pl.BlockSpec(memory_space=pltpu.SMEM),
            pl.BlockSpec((tm, d), lambda i: (i, 0)),
            pl.BlockSpec((tm, LANES), lambda i: (i, 0)),
            pl.BlockSpec((1, d), lambda i: (0, 0)),
            pl.BlockSpec(memory_space=pl.ANY),
        ],
        out_specs=pl.BlockSpec((tm, d), lambda i: (i, 0)),
        out_shape=jax.ShapeDtypeStruct((t, d), jnp.float32),
        scratch_shapes=[pltpu.VMEM((ROW_BUFS * TOP_K * tm * ROW_SUB, LANES), jnp.uint32),
                        pltpu.SemaphoreType.DMA((ROW_BUFS,))],
        compiler_params=_cparams("arbitrary"),
        name="moe_combine",
    )(pos_rows, x1, rw, norm_final, ys)


def _expert_meta(counts, n_slots):
    tm = TM_EXPERT
    n_tiles = n_slots // tm
    i32 = jnp.int32
    ends = jnp.cumsum(counts).astype(i32)
    starts = ends - counts
    used = counts > 0
    n_used = jnp.sum(used).astype(i32)
    seq_of = jnp.cumsum(used).astype(i32) - 1
    m_idx = jnp.arange(N_EXPERTS, dtype=i32)
    pick = used[None, :] & (seq_of[None, :] == m_idx[:, None])
    s_exp = jnp.sum(jnp.where(pick, m_idx[None, :], 0), axis=1).astype(i32)
    s_lo = jnp.sum(jnp.where(pick, starts[None, :], 0), axis=1).astype(i32)
    s_hi = jnp.sum(jnp.where(pick, ends[None, :], 0), axis=1).astype(i32)
    row0 = jnp.arange(n_tiles, dtype=i32)[:, None] * tm
    t_a = jnp.sum(used[None, :] & (ends[None, :] <= row0), axis=1).astype(i32)
    t_b = jnp.sum(used[None, :] & (starts[None, :] < row0 + tm), axis=1).astype(i32) - 1
    t_new = jnp.any(used[None, :] & (starts[None, :] == row0), axis=1).astype(i32)
    return t_a, (t_b - t_a + 1).astype(i32), t_new, s_exp, s_lo, s_hi, n_used.reshape(1)


def _moe(x1, h_fat, ids, rw, rank, cnt, w_gate, w_up, w_down, norm_final):
    t, d = x1.shape
    counts = cnt[0, :N_EXPERTS]
    offs = (jnp.cumsum(counts) - counts).astype(jnp.int32)
    onehot = ids[:TOP_K, :, None] == jnp.arange(N_EXPERTS, dtype=jnp.int32)
    pos1d = (jnp.sum(jnp.where(onehot, offs, 0), axis=-1) + rank[:TOP_K]).reshape(-1)
    src_tok = _inverse(pos1d)
    meta = _expert_meta(counts, t * TOP_K)
    ys = _experts(meta, src_tok, h_fat, w_gate, w_up, w_down)
    return _combine(pos1d * ROW_SUB, x1, rw, norm_final, ys)


def kernel(x, norm_mix, w_in, b_gate, attn_sinks, ssm_a_re, ssm_a_im, ssm_b_re, ssm_b_im, ssm_c_re, ssm_c_im, ssm_d, ssm_log_dt, w_glu, b_glu, w_attn_branch, w_ssm_branch, w_out, norm_moe, w_router_group, b_router_group, w_router_expert, b_router_expert, w_expert_gate, w_expert_up, w_expert_down, norm_final):
    b, l, d = x.shape
    depth = w_in.shape[0]
    assert depth == 1, "the final norm is fused into the last layer's combine kernel"
    d_attn = N_HEADS * HEAD_DIM
    kv_cols = N_KV_HEADS * HEAD_DIM
    d_ssm = ssm_d.shape[-1]
    bf16 = jnp.bfloat16
    x2 = x.reshape(b * l, d)
    assert TM_PROJ == TM_SSM == TM_MERGE and l % TM_SSM == 0
    i = 0
    q, k, v, u4, gates = _in_proj(x2, norm_mix[i][None], w_in[i].astype(bf16), b_gate[i][None],
                                  d_attn, kv_cols, d_ssm)
    attn = _attention(q, k, v, attn_sinks[i], b, l)
    tables = _ssm_tables(ssm_a_re[i], ssm_a_im[i], ssm_b_re[i], ssm_b_im[i], ssm_c_re[i], ssm_c_im[i],
                         ssm_d[i], ssm_log_dt[i], SSM_SEG)
    y4 = _ssm(u4, tables, b, l)
    pad = LANES - N_EXPERTS - N_EXPERT_GROUPS
    w_router = jnp.concatenate([w_router_expert[i], w_router_group[i], jnp.zeros((d, pad), jnp.float32)], axis=1)
    w_r_hi = w_router.astype(bf16)
    w_router = jnp.concatenate([w_r_hi, (w_router - w_r_hi.astype(jnp.float32)).astype(bf16)], axis=1)
    b_router = jnp.concatenate([b_router_expert[i], b_router_group[i], jnp.zeros((pad,), jnp.float32)])[None]
    x1, h_fat, ids, rw, rank, cnt = _merge(x2, attn, y4, gates, w_glu[i].astype(bf16), b_glu[i][None],
                                w_attn_branch[i].astype(bf16), w_ssm_branch[i].astype(bf16),
                                w_out[i].astype(bf16), norm_moe[i][None], w_router, b_router)
    out = _moe(x1, h_fat, ids, rw, rank, cnt, w_expert_gate[i], w_expert_up[i], w_expert_down[i], norm_final[None])
    return out.reshape(b, l, d)
```

```python
import functools
import math

import jax
import jax.numpy as jnp
from jax import lax
from jax.experimental import pallas as pl
from jax.experimental.pallas import tpu as pltpu

EPS = 1e-6
HEAD_DIM = 64
N_HEADS = 8
N_KV_HEADS = 2
Q_PER_KV = N_HEADS // N_KV_HEADS
ATTN_BLOCK = 128
ATTN_QB = 8
SSM_GROUP = 16
SSM_STATE = 64
N_EXPERT_GROUPS = 4
EXPERTS_PER_GROUP = 8
N_EXPERTS = N_EXPERT_GROUPS * EXPERTS_PER_GROUP
TOP_K = 2

LANES = 128
SUBLANES = 8
SSM_CHUNK_GROUPS = LANES // SSM_GROUP
SSM_CHUNK_STATES = SSM_CHUNK_GROUPS * SSM_STATE

TM_PROJ = 512
TM_SSM = 512
SSM_SEG = TM_SSM // SUBLANES
TM_MERGE = 512
TM_MOVE = 512
TM_EXPERT = 256
ROW_SUB = 4
ROW_BUFS = 4
VMEM_LIMIT = 56 * 1024 * 1024


def _cparams(*sem):
    return pltpu.CompilerParams(dimension_semantics=sem, vmem_limit_bytes=VMEM_LIMIT)


def _pack_bf16_pairs(x):
    half = x.shape[1] // 2
    bits = lax.bitcast_convert_type(x, jnp.uint32)
    return (bits[:, :half] & jnp.uint32(0xFFFF0000)) | (bits[:, half:] >> 16)


def _unpack_bf16_pairs(p):
    hi = lax.bitcast_convert_type(p & jnp.uint32(0xFFFF0000), jnp.float32)
    lo = lax.bitcast_convert_type(p << 16, jnp.float32)
    return jnp.concatenate([hi, lo], axis=1)


def _load_rows(ref, first, n):
    return jnp.concatenate(
        [ref[pl.ds(first * ROW_SUB + c, n, stride=ROW_SUB), :] for c in range(ROW_SUB)], axis=1)


def _store_rows(ref, first, val, mask=None):
    n = val.shape[0]
    for c in range(ROW_SUB):
        idx = pl.ds(first * ROW_SUB + c, n, stride=ROW_SUB)
        v = val[:, c * LANES:(c + 1) * LANES]
        ref[idx, :] = v if mask is None else jnp.where(mask, v, ref[idx, :])


def _row_gather(src_hbm, idx_ref, idx0, dst_ref, dst0, n, sem, start):
    rs = ROW_SUB
    if start:
        for r in range(n):
            pltpu.make_async_copy(src_hbm.at[pl.ds(pl.multiple_of(idx_ref[idx0 + r], rs), rs)],
                                  dst_ref.at[pl.ds(pl.multiple_of((dst0 + r) * rs, rs), rs)],
                                  sem).start(priority=r % 2)
    else:
        pltpu.make_async_copy(src_hbm.at[pl.ds(0, n * rs)],
                              dst_ref.at[pl.ds(pl.multiple_of(dst0 * rs, rs), n * rs)], sem).wait()


def _proj_kernel(x_ref, g_ref, w_ref, bg_ref, q_ref, k_ref, v_ref, u_ref, gate_ref, *, cols):
    q_c, kv_c, d_ssm = cols
    xf = x_ref[...]
    inv = lax.rsqrt(jnp.mean(xf * xf, axis=-1, keepdims=True) + EPS)
    h = (xf * inv * g_ref[...]).astype(jnp.bfloat16)
    o_gate = q_c + 2 * kv_c + d_ssm
    gl = jnp.dot(h, w_ref[:, o_gate:], preferred_element_type=jnp.float32) + bg_ref[...]
    gate_ref[...] = jax.nn.sigmoid(gl).astype(gate_ref.dtype)
    o = 0
    q_ref[...] = (jnp.dot(h, w_ref[:, o:o + q_c], preferred_element_type=jnp.float32)
                  * (1.0 / math.sqrt(HEAD_DIM))).astype(q_ref.dtype)
    o += q_c
    k_ref[...] = jnp.dot(h, w_ref[:, o:o + kv_c], preferred_element_type=jnp.float32).astype(k_ref.dtype)
    o += kv_c
    v_ref[...] = jnp.dot(h, w_ref[:, o:o + kv_c], preferred_element_type=jnp.float32).astype(v_ref.dtype)
    o += kv_c
    seg = x_ref.shape[0] // SUBLANES
    for s in range(0, d_ssm // LANES, 2):
        uu = jnp.dot(h, w_ref[:, o:o + 2 * LANES], preferred_element_type=jnp.float32)
        for half in range(2):
            for j in range(SUBLANES):
                u_ref[s + half, pl.ds(j, seg, stride=SUBLANES), :] = (
                    uu[j * seg:(j + 1) * seg, half * LANES:(half + 1) * LANES])
        o += 2 * LANES


def _in_proj(x2, norm_w, w_in, b_gate, d_attn, kv_cols, d_ssm):
    t, d = x2.shape
    gate_cols = b_gate.shape[-1]
    n_slab = d_ssm // LANES
    tm = TM_PROJ
    kern = functools.partial(_proj_kernel, cols=(d_attn, kv_cols, d_ssm))
    return pl.pallas_call(
        kern,
        grid=(t // tm,),
        in_specs=[
            pl.BlockSpec((tm, d), lambda i: (i, 0)),
            pl.BlockSpec((1, d), lambda i: (0, 0)),
            pl.BlockSpec(w_in.shape, lambda i: (0, 0)),
            pl.BlockSpec((1, gate_cols), lambda i: (0, 0)),
        ],
        out_specs=[
            pl.BlockSpec((tm, d_attn), lambda i: (i, 0)),
            pl.BlockSpec((tm, kv_cols), lambda i: (i, 0)),
            pl.BlockSpec((tm, kv_cols), lambda i: (i, 0)),
            pl.BlockSpec((n_slab, tm, LANES), lambda i: (0, i, 0)),
            pl.BlockSpec((tm, gate_cols), lambda i: (i, 0)),
        ],
        out_shape=[
            jax.ShapeDtypeStruct((t, d_attn), jnp.bfloat16),
            jax.ShapeDtypeStruct((t, kv_cols), jnp.bfloat16),
            jax.ShapeDtypeStruct((t, kv_cols), jnp.bfloat16),
            jax.ShapeDtypeStruct((n_slab, t, LANES), jnp.float32),
            jax.ShapeDtypeStruct((t, gate_cols), jnp.bfloat16),
        ],
        compiler_params=_cparams("arbitrary"),
        name="in_proj",
    )(x2, norm_w, w_in, b_gate)


def _attn_kernel(sink_ref, q_ref, kp_ref, kc_ref, vp_ref, vc_ref, rep_ref, mask_ref, o_ref):
    i = pl.program_id(1)
    blk = ATTN_BLOCK
    hw = Q_PER_KV * HEAD_DIM
    rows = Q_PER_KV * blk
    head_of_row = lax.broadcasted_iota(jnp.int32, (rows, 1), 0) >> 7
    lane_head_q = lax.broadcasted_iota(jnp.int32, (blk, hw), 1) >> 6
    lane_head_v = lax.broadcasted_iota(jnp.int32, (2 * blk, hw), 1) >> 6
    sinks = []
    for kh in range(N_KV_HEADS):
        sink = jnp.zeros((rows, 1), jnp.float32)
        for g in range(Q_PER_KV):
            sink = jnp.where(head_of_row == g, sink_ref[kh * Q_PER_KV + g], sink)
        sinks.append(sink)
    for qb in range(ATTN_QB):
        if qb == 0:
            k2 = jnp.concatenate([kp_ref[...], kc_ref[0:blk, :]], axis=0)
            v2 = jnp.concatenate([vp_ref[...], vc_ref[0:blk, :]], axis=0)
            bias = mask_ref[jnp.where(i == 0, 1, 0)]
        else:
            k2 = kc_ref[(qb - 1) * blk:(qb + 1) * blk, :]
            v2 = vc_ref[(qb - 1) * blk:(qb + 1) * blk, :]
            bias = mask_ref[0]
        for kh in range(N_KV_HEADS):
            rep = rep_ref[kh]
            k4 = jnp.dot(k2, rep, preferred_element_type=jnp.float32).astype(jnp.bfloat16)
            v4 = jnp.dot(v2, rep, preferred_element_type=jnp.float32).astype(jnp.bfloat16)
            qh = q_ref[qb * blk:(qb + 1) * blk, kh * hw:(kh + 1) * hw]
            qm = jnp.concatenate(
                [jnp.where(lane_head_q == g, qh, jnp.zeros_like(qh)) for g in range(Q_PER_KV)], axis=0)
            s = lax.dot_general(qm, k4, (((1,), (1,)), ((), ())), preferred_element_type=jnp.float32)
            s = s + bias
            sink = sinks[kh]
            m = jnp.maximum(jnp.max(s, axis=-1, keepdims=True), sink)
            p = jnp.exp(s - m)
            rinv = 1.0 / (jnp.sum(p, axis=-1, keepdims=True) + jnp.exp(sink - m))
            p = p.astype(jnp.bfloat16)
            p_cat = jnp.concatenate([p[g * blk:(g + 1) * blk, :] for g in range(Q_PER_KV)], axis=1)
            vm = jnp.concatenate(
                [jnp.where(lane_head_v == g, v4, jnp.zeros_like(v4)) for g in range(Q_PER_KV)], axis=0)
            o = jnp.dot(p_cat, vm, preferred_element_type=jnp.float32)
            scale = jnp.zeros((blk, hw), jnp.float32)
            for g in range(Q_PER_KV):
                scale = jnp.where(lane_head_q == g, rinv[g * blk:(g + 1) * blk, :], scale)
            o_ref[qb * blk:(qb + 1) * blk, kh * hw:(kh + 1) * hw] = (o * scale).astype(o_ref.dtype)


def _attention(q, k, v, sinks, b, l):
    d_attn = q.shape[-1]
    kv_cols = k.shape[-1]
    blk = ATTN_BLOCK
    tq = ATTN_QB * blk
    hw = Q_PER_KV * HEAD_DIM
    lane = jnp.arange(hw)[None, :]
    src = jnp.arange(kv_cols)[:, None]
    rep = jnp.stack([(src == kh * HEAD_DIM + (lane % HEAD_DIM)) for kh in range(N_KV_HEADS)]
                    ).astype(jnp.bfloat16)
    r = (jnp.arange(Q_PER_KV * blk) % blk)[:, None]
    c = jnp.arange(2 * blk)[None, :]
    band = (c > r) & (c <= r + blk)
    mask = jnp.where(jnp.stack([band, band & (c >= blk)]), 0.0, jnp.finfo(jnp.float32).min).astype(jnp.float32)
    q3 = q.reshape(b, l, d_attn)
    k3 = k.reshape(b, l, kv_cols)
    v3 = v.reshape(b, l, kv_cols)
    cur = lambda bi, i: (bi, i, 0)
    prev = lambda bi, i: (bi, jnp.maximum(ATTN_QB * i - 1, 0), 0)
    out = pl.pallas_call(
        _attn_kernel,
        grid=(b, l // tq),
        in_specs=[
            pl.BlockSpec(memory_space=pltpu.SMEM),
            pl.BlockSpec((None, tq, d_attn), cur),
            pl.BlockSpec((None, blk, kv_cols), prev),
            pl.BlockSpec((None, tq, kv_cols), cur),
            pl.BlockSpec((None, blk, kv_cols), prev),
            pl.BlockSpec((None, tq, kv_cols), cur),
            pl.BlockSpec(rep.shape, lambda bi, i: (0, 0, 0)),
            pl.BlockSpec(mask.shape, lambda bi, i: (0, 0, 0)),
        ],
        out_specs=pl.BlockSpec((None, tq, d_attn), cur),
        out_shape=jax.ShapeDtypeStruct((b, l, d_attn), jnp.bfloat16),
        compiler_params=_cparams("arbitrary", "arbitrary"),
        name="swa",
    )(sinks, q3, k3, k3, v3, v3, rep, mask)
    return out.reshape(b * l, d_attn)


def _ssm_kernel(u_ref, bmat_ref, cmat_ref, lam_ref, pw_ref, lamseg_ref, d_ref, y_ref, bu_ref, carry_ref):
    i = pl.program_id(1)
    n_slab = u_ref.shape[0]
    tm = u_ref.shape[1]
    seg = tm // SUBLANES
    ns = SSM_CHUNK_STATES
    npair = 2

    @pl.when(i == 0)
    def _():
        carry_ref[...] = jnp.zeros_like(carry_ref)

    sub = lax.broadcasted_iota(jnp.int32, (SUBLANES, ns), 0)
    for s0 in range(0, n_slab, npair):
        lam = []
        for q in range(npair):
            s = s0 + q
            bu_ref[q] = jnp.dot(u_ref[s].astype(jnp.bfloat16), bmat_ref[s], preferred_element_type=jnp.float32)
            lam.append((jnp.broadcast_to(lam_ref[s, 0:1, :], (SUBLANES, ns)),
                        jnp.broadcast_to(lam_ref[s, 1:2, :], (SUBLANES, ns))))

        def step(r, st):
            rows = pl.ds(r * SUBLANES, SUBLANES)
            out = []
            for q in range(npair):
                lr, li = lam[q]
                sr, si = st[q]
                nr = lr * sr - li * si + bu_ref[q, rows, 0:ns]
                ni = lr * si + li * sr + bu_ref[q, rows, ns:2 * ns]
                bu_ref[q, rows, 0:ns] = nr
                bu_ref[q, rows, ns:2 * ns] = ni
                out.append((nr, ni))
            return tuple(out)

        zero = jnp.zeros((SUBLANES, ns), jnp.float32)
        ends = ((zero, zero),) * npair
        for r in range(seg):
            ends = step(r, ends)

        for q in range(npair):
            s = s0 + q
            er, ei = ends[q]
            ar = lamseg_ref[s, 0:1, :]
            ai = lamseg_ref[s, 1:2, :]
            cr = carry_ref[s, 0:1, :]
            ci = carry_ref[s, 1:2, :]
            car = jnp.zeros((SUBLANES, ns), jnp.float32)
            cai = jnp.zeros((SUBLANES, ns), jnp.float32)
            for j in range(SUBLANES):
                car = jnp.where(sub == j, jnp.broadcast_to(cr, (SUBLANES, ns)), car)
                cai = jnp.where(sub == j, jnp.broadcast_to(ci, (SUBLANES, ns)), cai)
                ejr = jnp.sum(jnp.where(sub == j, er, 0.0), axis=0, keepdims=True)
                eji = jnp.sum(jnp.where(sub == j, ei, 0.0), axis=0, keepdims=True)
                cr, ci = ar * cr - ai * ci + ejr, ar * ci + ai * cr + eji
            carry_ref[s, 0:1, :] = cr
            carry_ref[s, 1:2, :] = ci

            ctr = jnp.broadcast_to(car[None], (seg, SUBLANES, ns)).reshape(tm, ns)
            cti = jnp.broadcast_to(cai[None], (seg, SUBLANES, ns)).reshape(tm, ns)
            pr = pw_ref[s, 0]
            pi = pw_ref[s, 1]
            st_r = bu_ref[q, :, 0:ns] + pr * ctr - pi * cti
            st_i = bu_ref[q, :, ns:2 * ns] + pr * cti + pi * ctr
            st = jnp.concatenate([st_r, st_i], axis=1).astype(jnp.bfloat16)
            y_ref[s] = jnp.dot(st, cmat_ref[s], preferred_element_type=jnp.float32) + d_ref[s] * u_ref[s]


def _ssm_tables(a_re, a_im, b_re, b_im, c_re, c_im, d_skip, log_dt, seg):
    f32 = jnp.float32
    g, p = a_re.shape
    c = b_re.shape[-1]
    ng = SSM_CHUNK_GROUPS
    n_slab = g // ng
    lam = lax.complex(a_re.astype(f32), a_im.astype(f32))
    dt = jnp.exp(log_dt.astype(f32))[:, None]
    lam_bar = jnp.exp(lam * dt)
    b_bar = ((lam_bar - 1.0) / lam)[:, :, None] * lax.complex(b_re.astype(f32), b_im.astype(f32))
    c_mat = lax.complex(c_re.astype(f32), c_im.astype(f32))
    eye = jnp.eye(ng, dtype=f32)

    def bdiag_b(m):
        m = m.reshape(n_slab, ng, p, c)
        return jnp.einsum('ab,kbpc->kacbp', eye, m).reshape(n_slab, ng * c, ng * p)

    def bdiag_c(m):
        m = m.reshape(n_slab, ng, c, p)
        return jnp.einsum('ab,kbcp->kbpac', eye, m).reshape(n_slab, ng * p, ng * c)

    bmat = jnp.concatenate([bdiag_b(jnp.real(b_bar)), bdiag_b(jnp.imag(b_bar))], axis=2).astype(jnp.bfloat16)
    cmat = jnp.concatenate([bdiag_c(jnp.real(c_mat)), -bdiag_c(jnp.imag(c_mat))], axis=1).astype(jnp.bfloat16)

    def slab_rows(z):
        z = z.reshape(n_slab, 1, ng * p)
        return jnp.concatenate([jnp.real(z), jnp.imag(z)], axis=1)

    lam_t = slab_rows(lam_bar)
    steps = jnp.arange(1, seg + 1, dtype=f32)
    pw = jnp.exp((lam * dt)[None] * steps[:, None, None])
    pw = pw.reshape(seg, n_slab, ng * p).transpose(1, 0, 2)
    pw = jnp.repeat(pw, SUBLANES, axis=1)
    pw_t = jnp.stack([jnp.real(pw), jnp.imag(pw)], axis=1)
    lamseg_t = slab_rows(jnp.exp(lam * dt * float(seg)))
    d_t = d_skip.astype(f32).reshape(n_slab, 1, ng * c)
    return bmat, cmat, lam_t, pw_t, lamseg_t, d_t


def _ssm(u4, tables, b, l):
    bmat, cmat, lam_t, pw_t, lamseg_t, d_t = tables
    n_slab, t, _ = u4.shape
    tm = TM_SSM
    nt = l // tm
    ns = SSM_CHUNK_STATES
    const = lambda nd: (lambda bi, i: (0,) * nd)
    return pl.pallas_call(
        _ssm_kernel,
        grid=(b, nt),
        in_specs=[
            pl.BlockSpec((n_slab, tm, LANES), lambda bi, i: (0, bi * nt + i, 0)),
            pl.BlockSpec(bmat.shape, const(3)),
            pl.BlockSpec(cmat.shape, const(3)),
            pl.BlockSpec(lam_t.shape, const(3)),
            pl.BlockSpec(pw_t.shape, const(4)),
            pl.BlockSpec(lamseg_t.shape, const(3)),
            pl.BlockSpec(d_t.shape, const(3)),
        ],
        out_specs=pl.BlockSpec((n_slab, tm, LANES), lambda bi, i: (0, bi * nt + i, 0)),
        out_shape=jax.ShapeDtypeStruct((n_slab, t, LANES), jnp.float32),
        scratch_shapes=[
            pltpu.VMEM((2, tm, 2 * ns), jnp.float32),
            pltpu.VMEM((n_slab, 2, ns), jnp.float32),
        ],
        compiler_params=_cparams("arbitrary", "arbitrary"),
        name="s5_scan",
    )(u4, bmat, cmat, lam_t, pw_t, lamseg_t, d_t)


def _merge_kernel(x_ref, attn_ref, y_ref, gate_ref, wglu_ref, bglu_ref, wa_ref, ws_ref, wo_ref,
                  nm_ref, wr_ref, br_ref, tri_ref, x1_ref, h2_ref, ids_ref, rw_ref, rank_ref, cnt_ref,
                  lg_ref, carry_ref):
    i = pl.program_id(0)
    d = x_ref.shape[1]
    tm = x_ref.shape[0]

    @pl.when(i == 0)
    def _():
        lg_ref[...] = jnp.zeros_like(lg_ref)
        carry_ref[...] = jnp.zeros_like(carry_ref)

    logits = lg_ref[...]

    seg = tm // SUBLANES
    y = jnp.concatenate(
        [jnp.concatenate([y_ref[s, pl.ds(j, seg, stride=SUBLANES), :] for j in range(SUBLANES)], axis=0)
         for s in range(y_ref.shape[0])], axis=1)
    z = jax.nn.gelu(y)
    zg = jnp.dot(z.astype(jnp.bfloat16), wglu_ref[...], preferred_element_type=jnp.float32) + bglu_ref[...]
    z = z * jax.nn.sigmoid(zg)
    a = jnp.dot(attn_ref[...], wa_ref[...], preferred_element_type=jnp.float32)
    sb = jnp.dot(z.astype(jnp.bfloat16), ws_ref[...], preferred_element_type=jnp.float32)
    merged = gate_ref[:, 0:d].astype(jnp.float32) * a + gate_ref[:, d:2 * d].astype(jnp.float32) * sb
    x1 = x_ref[...] + jnp.dot(merged.astype(jnp.bfloat16), wo_ref[...], preferred_element_type=jnp.float32)
    x1_ref[...] = x1
    inv = lax.rsqrt(jnp.mean(x1 * x1, axis=-1, keepdims=True) + EPS)
    h2 = x1 * inv * nm_ref[...]
    h_hi = h2.astype(jnp.bfloat16)
    h_pk = _pack_bf16_pairs(h_hi.astype(jnp.float32))
    _store_rows(h2_ref, 0, h_pk)

    h_lo = (h2 - h_hi.astype(jnp.float32)).astype(jnp.bfloat16)
    hh = jnp.dot(h_hi, wr_ref[...], preferred_element_type=jnp.float32)
    lh = jnp.dot(h_lo, wr_ref[:, 0:LANES], preferred_element_type=jnp.float32)
    lg_ref[...] = hh[:, 0:LANES] + hh[:, LANES:2 * LANES] + lh + br_ref[...]

    lane = lax.broadcasted_iota(jnp.int32, (tm, LANES), 1)
    ninf = -jnp.inf
    gl = jnp.where((lane >= N_EXPERTS) & (lane < N_EXPERTS + N_EXPERT_GROUPS), logits, ninf)
    gmax = jnp.max(gl, axis=-1, keepdims=True)
    gidx = jnp.min(jnp.where(gl == gmax, lane - N_EXPERTS, LANES), axis=-1, keepdims=True)
    group_p = 1.0 / jnp.sum(jnp.exp(gl - gmax), axis=-1, keepdims=True)
    el = jnp.where((lane < N_EXPERTS) & ((lane >> 3) == gidx), logits, ninf)
    m1 = jnp.max(el, axis=-1, keepdims=True)
    i1 = jnp.min(jnp.where(el == m1, lane, LANES), axis=-1, keepdims=True)
    el2 = jnp.where(lane == i1, ninf, el)
    m2 = jnp.max(el2, axis=-1, keepdims=True)
    i2 = jnp.min(jnp.where(el2 == m2, lane, LANES), axis=-1, keepdims=True)
    e2 = jnp.exp(m2 - m1)
    w1 = group_p / (1.0 + e2)
    w2 = group_p * e2 / (1.0 + e2)
    ids_ref[...] = jnp.transpose(jnp.where(lane == 0, i1, jnp.where(lane == 1, i2, 0)))[0:SUBLANES, :]
    rw_ref[...] = jnp.where(lane == 0, w1, jnp.where(lane == 1, w2, 0.0))

    oh0 = lane == i1
    oh1 = lane == i2
    live = jnp.where(i > 0, 1.0, 0.0)
    oh = (oh0.astype(jnp.float32) + oh1.astype(jnp.float32)) * live
    cum = jnp.dot(tri_ref[...], oh.astype(jnp.bfloat16), preferred_element_type=jnp.float32) + carry_ref[...]
    r0 = jnp.sum(jnp.where(oh0, cum, 0.0), axis=-1, keepdims=True)
    r1 = jnp.sum(jnp.where(oh1, cum, 0.0), axis=-1, keepdims=True)
    rank_ref[...] = jnp.transpose(
        jnp.where(lane == 0, r0, jnp.where(lane == 1, r1, 0.0)).astype(jnp.int32))[0:SUBLANES, :]
    carry_ref[...] = carry_ref[...] + jnp.sum(oh, axis=0, keepdims=True)
    cnt_ref[...] = carry_ref[...].astype(jnp.int32)


def _merge(x2, attn, y4, gates, w_glu, b_glu, w_a, w_s, w_o, norm_moe, w_router, b_router):
    t, d = x2.shape
    tm = TM_MERGE
    nt = t // tm
    n_slab = y4.shape[0]
    tri = (jnp.arange(tm)[None, :] < jnp.arange(tm)[:, None]).astype(jnp.bfloat16)
    full = lambda a: pl.BlockSpec(a.shape, lambda i: (0,) * a.ndim)
    cur = lambda i: (jnp.minimum(i, nt - 1), 0)
    prv = lambda i: (jnp.maximum(i - 1, 0), 0)
    row = lambda c, m=cur: pl.BlockSpec((tm, c), m)
    slots = pl.BlockSpec((SUBLANES, tm), lambda i: (0, jnp.maximum(i - 1, 0)))
    return pl.pallas_call(
        _merge_kernel,
        grid=(nt + 1,),
        in_specs=[
            row(d), row(attn.shape[1]),
            pl.BlockSpec((n_slab, tm, LANES), lambda i: (0, jnp.minimum(i, nt - 1), 0)),
            row(gates.shape[1]),
            full(w_glu), full(b_glu), full(w_a), full(w_s), full(w_o), full(norm_moe),
            full(w_router), full(b_router), full(tri),
        ],
        out_specs=[row(d), pl.BlockSpec((tm * ROW_SUB, LANES), cur), slots, row(LANES, prv),
                   slots, pl.BlockSpec((1, LANES), lambda i: (0, 0))],
        out_shape=[
            jax.ShapeDtypeStruct((t, d), jnp.float32),
            jax.ShapeDtypeStruct((t * ROW_SUB, LANES), jnp.uint32),
            jax.ShapeDtypeStruct((SUBLANES, t), jnp.int32),
            jax.ShapeDtypeStruct((t, LANES), jnp.float32),
            jax.ShapeDtypeStruct((SUBLANES, t), jnp.int32),
            jax.ShapeDtypeStruct((1, LANES), jnp.int32),
        ],
        scratch_shapes=[pltpu.VMEM((tm, LANES), jnp.float32), pltpu.VMEM((1, LANES), jnp.float32)],
        compiler_params=_cparams("arbitrary"),
        name="merge_router",
    )(x2, attn, y4, gates, w_glu, b_glu, w_a, w_s, w_o, norm_moe, w_router, b_router, tri)


def _inverse_kernel(pos_ref, src_ref):
    n_slots = pos_ref.shape[0]
    t = n_slots // TOP_K

    def body(tok, c):
        for k in range(TOP_K):
            src_ref[pos_ref[k * t + tok]] = tok * ROW_SUB
        return c

    lax.fori_loop(0, t, body, 0, unroll=8)


def _inverse(pos1d):
    n_slots = pos1d.shape[0]
    smem = pl.BlockSpec(memory_space=pltpu.SMEM)
    return pl.pallas_call(
        _inverse_kernel,
        in_specs=[smem],
        out_specs=smem,
        out_shape=jax.ShapeDtypeStruct((n_slots,), jnp.int32),
        name="route_inverse",
    )(pos1d)


def _expert_kernel(ta_ref, tn_ref, tnew_ref, sexp_ref, slo_ref, shi_ref, meta_ref, src_ref,
                   h_ref, wg_hbm, wu_hbm, wd_hbm, y_ref,
                   xbuf_ref, wgs_ref, wus_ref, wds_ref, wgb_ref, wub_ref, wdb_ref, gsem, wsem):
    j = pl.program_id(0)
    n_tiles = pl.num_programs(0)
    last = n_tiles - 1
    tm = TM_EXPERT
    rs = ROW_SUB
    nb = ROW_BUFS
    n_used = meta_ref[0]
    slot = j % nb
    a = ta_ref[j]

    def gather(tile, sl, start):
        _row_gather(h_ref, src_ref, tile * tm, xbuf_ref, sl * tm, tm, gsem.at[sl], start)

    def weights(m, start):
        sl = m % 2
        e = sexp_ref[m]
        for hbm, stage in ((wg_hbm, wgs_ref), (wu_hbm, wus_ref), (wd_hbm, wds_ref)):
            cp = pltpu.make_async_copy(hbm.at[e], stage.at[sl], wsem.at[sl])
            cp.start() if start else cp.wait()

    def new_expert(m):
        weights(m, False)
        sl = m % 2
        wgb_ref[...] = wgs_ref[sl].astype(jnp.bfloat16)
        wub_ref[...] = wus_ref[sl].astype(jnp.bfloat16)
        wdb_ref[...] = wds_ref[sl].astype(jnp.bfloat16)

        @pl.when(m + 1 < n_used)
        def _():
            weights(m + 1, True)

    def expert_out(xp):
        x = _unpack_bf16_pairs(xp).astype(jnp.bfloat16)
        hg = jnp.dot(x, wgb_ref[...], preferred_element_type=jnp.float32)
        hu = jnp.dot(x, wub_ref[...], preferred_element_type=jnp.float32)
        act = (jax.nn.silu(hg) * hu).astype(jnp.bfloat16)
        y = jnp.dot(act, wdb_ref[...], preferred_element_type=jnp.float32)
        return _pack_bf16_pairs(y.astype(jnp.bfloat16).astype(jnp.float32))

    @pl.when(j == 0)
    def _():
        weights(0, True)
        for b in range(nb - 1):
            gather(b, b, True)

    @pl.when(tnew_ref[j] == 1)
    def _():
        new_expert(a)

    gather(j, slot, False)
    xp = _load_rows(xbuf_ref, slot * tm, tm)
    gather(jnp.minimum(j + nb - 1, last), (j + nb - 1) % nb, True)
    _store_rows(y_ref, 0, expert_out(xp))

    def extra(i, carry):
        m = a + i
        new_expert(m)
        yp = expert_out(_load_rows(xbuf_ref, slot * tm, tm))
        row = lax.broadcasted_iota(jnp.int32, (tm, LANES), 0) + j * tm
        mine = (row >= slo_ref[m]) & (row < shi_ref[m])
        _store_rows(y_ref, 0, yp, mine)
        return carry

    lax.fori_loop(1, tn_ref[j], extra, 0)

    @pl.when(j == last)
    def _():
        for b in range(1, nb):
            gather(j, (j + b) % nb, False)


def _experts(meta, src_tok, h_fat, w_gate, w_up, w_down):
    n_slots = src_tok.shape[0]
    ne, d, dff = w_gate.shape
    tm = TM_EXPERT
    rs = ROW_SUB
    assert n_slots // tm >= ROW_BUFS
    any_spec = pl.BlockSpec(memory_space=pl.ANY)
    grid_spec = pltpu.PrefetchScalarGridSpec(
        num_scalar_prefetch=8,
        grid=(n_slots // tm,),
        in_specs=[any_spec, any_spec, any_spec, any_spec],
        out_specs=pl.BlockSpec((tm * rs, LANES), lambda j, *_: (j, 0)),
        scratch_shapes=[
            pltpu.VMEM((ROW_BUFS * tm * rs, LANES), jnp.uint32),
            pltpu.VMEM((2, d, dff), jnp.float32),
            pltpu.VMEM((2, d, dff), jnp.float32),
            pltpu.VMEM((2, dff, d), jnp.float32),
            pltpu.VMEM((d, dff), jnp.bfloat16),
            pltpu.VMEM((d, dff), jnp.bfloat16),
            pltpu.VMEM((dff, d), jnp.bfloat16),
            pltpu.SemaphoreType.DMA((ROW_BUFS,)),
            pltpu.SemaphoreType.DMA((2,)),
        ],
    )
    return pl.pallas_call(
        _expert_kernel,
        grid_spec=grid_spec,
        out_shape=jax.ShapeDtypeStruct((n_slots * rs, LANES), jnp.uint32),
        compiler_params=_cparams("arbitrary"),
        name="moe_experts",
    )(*meta, src_tok, h_fat, w_gate, w_up, w_down)


def _combine_kernel(pos_ref, x1_ref, rw_ref, nf_ref, ys_ref, o_ref, buf_ref, sem):
    i = pl.program_id(0)
    n = pl.num_programs(0)
    last = n - 1
    tm = x1_ref.shape[0]
    t = n * tm
    nb = ROW_BUFS
    slot = i % nb

    def gather(tile, sl, start):
        for k in range(TOP_K):
            _row_gather(ys_ref, pos_ref, k * t + tile * tm, buf_ref, (sl * TOP_K + k) * tm, tm, sem.at[sl], start)

    @pl.when(i == 0)
    def _():
        for b in range(nb - 1):
            gather(b, b, True)

    gather(i, slot, False)
    y0 = _unpack_bf16_pairs(_load_rows(buf_ref, (slot * TOP_K) * tm, tm))
    y1 = _unpack_bf16_pairs(_load_rows(buf_ref, (slot * TOP_K + 1) * tm, tm))
    gather(jnp.minimum(i + nb - 1, last), (i + nb - 1) % nb, True)
    rw = rw_ref[...]
    x = x1_ref[...] + rw[:, 0:1] * y0 + rw[:, 1:2] * y1
    inv = lax.rsqrt(jnp.mean(x * x, axis=-1, keepdims=True) + EPS)
    o_ref[...] = x * inv * nf_ref[...]

    @pl.when(i == last)
    def _():
        for b in range(1, nb):
            gather(i, (i + b) % nb, False)


def _combine(pos_rows, x1, rw, norm_final, ys):
    t, d = x1.shape
    tm = TM_MOVE
    assert t // tm >= ROW_BUFS
    return pl.pallas_call(
        _combine_kernel,
        grid=(t // tm,),
        in_specs=[
            pl.BlockSpec(memory_space=pltpu.SMEM),
            pl.BlockSpec((tm, d), lambda i: (i, 0)),
            pl.BlockSpec((tm, LANES), lambda i: (i, 0)),
            pl.BlockSpec((1, d), lambda i: (0, 0)),
            pl.BlockSpec(memory_space=pl.ANY),
        ],
        out_specs=pl.BlockSpec((tm, d), lambda i: (i, 0)),
        out_shape=jax.ShapeDtypeStruct((t, d), jnp.float32),
        scratch_shapes=[pltpu.VMEM((ROW_BUFS * TOP_K * tm * ROW_SUB, LANES), jnp.uint32),
                        pltpu.SemaphoreType.DMA((ROW_BUFS,))],
        compiler_params=_cparams("arbitrary"),
        name="moe_combine",
    )(pos_rows, x1, rw, norm_final, ys)


def _expert_meta(counts, n_slots):
    tm = TM_EXPERT
    n_tiles = n_slots // tm
    i32 = jnp.int32
    ends = jnp.cumsum(counts).astype(i32)
    starts = ends - counts
    used = counts > 0
    n_used = jnp.sum(used).astype(i32)
    seq_of = jnp.cumsum(used).astype(i32) - 1
    m_idx = jnp.arange(N_EXPERTS, dtype=i32)
    pick = used[None, :] & (seq_of[None, :] == m_idx[:, None])
    s_exp = jnp.sum(jnp.where(pick, m_idx[None, :], 0), axis=1).astype(i32)
    s_lo = jnp.sum(jnp.where(pick, starts[None, :], 0), axis=1).astype(i32)
    s_hi = jnp.sum(jnp.where(pick, ends[None, :], 0), axis=1).astype(i32)
    row0 = jnp.arange(n_tiles, dtype=i32)[:, None] * tm
    t_a = jnp.sum(used[None, :] & (ends[None, :] <= row0), axis=1).astype(i32)
    t_b = jnp.sum(used[None, :] & (starts[None, :] < row0 + tm), axis=1).astype(i32) - 1
    t_new = jnp.any(used[None, :] & (starts[None, :] == row0), axis=1).astype(i32)
    return t_a, (t_b - t_a + 1).astype(i32), t_new, s_exp, s_lo, s_hi, n_used.reshape(1)


def _moe(x1, h_fat, ids, rw, rank, cnt, w_gate, w_up, w_down, norm_final):
    t, d = x1.shape
    counts = cnt[0, :N_EXPERTS]
    offs = (jnp.cumsum(counts) - counts).astype(jnp.int32)
    onehot = ids[:TOP_K, :, None] == jnp.arange(N_EXPERTS, dtype=jnp.int32)
    pos1d = (jnp.sum(jnp.where(onehot, offs, 0), axis=-1) + rank[:TOP_K]).reshape(-1)
    src_tok = _inverse(pos1d)
    meta = _expert_meta(counts, t * TOP_K)
    ys = _experts(meta, src_tok, h_fat, w_gate, w_up, w_down)
    return _combine(pos1d * ROW_SUB, x1, rw, norm_final, ys)


def kernel(x, norm_mix, w_in, b_gate, attn_sinks, ssm_a_re, ssm_a_im, ssm_b_re, ssm_b_im, ssm_c_re, ssm_c_im, ssm_d, ssm_log_dt, w_glu, b_glu, w_attn_branch, w_ssm_branch, w_out, norm_moe, w_router_group, b_router_group, w_router_expert, b_router_expert, w_expert_gate, w_expert_up, w_expert_down, norm_final):
    b, l, d = x.shape
    depth = w_in.shape[0]
    assert depth == 1, "the final norm is fused into the last layer's combine kernel"
    d_attn = N_HEADS * HEAD_DIM
    kv_cols = N_KV_HEADS * HEAD_DIM
    d_ssm = ssm_d.shape[-1]
    bf16 = jnp.bfloat16
    x2 = x.reshape(b * l, d)
    assert TM_PROJ == TM_SSM == TM_MERGE and l % TM_SSM == 0
    i = 0
    q, k, v, u4, gates = _in_proj(x2, norm_mix[i][None], w_in[i].astype(bf16), b_gate[i][None],
                                  d_attn, kv_cols, d_ssm)
    attn = _attention(q, k, v, attn_sinks[i], b, l)
    tables = _ssm_tables(ssm_a_re[i], ssm_a_im[i], ssm_b_re[i], ssm_b_im[i], ssm_c_re[i], ssm_c_im[i],
                         ssm_d[i], ssm_log_dt[i], SSM_SEG)
    y4 = _ssm(u4, tables, b, l)
    pad = LANES - N_EXPERTS - N_EXPERT_GROUPS
    w_router = jnp.concatenate([w_router_expert[i], w_router_group[i], jnp.zeros((d, pad), jnp.float32)], axis=1)
    w_r_hi = w_router.astype(bf16)
    w_router = jnp.concatenate([w_r_hi, (w_router - w_r_hi.astype(jnp.float32)).astype(bf16)], axis=1)
    b_router = jnp.concatenate([b_router_expert[i], b_router_group[i], jnp.zeros((pad,), jnp.float32)])[None]
    x1, h_fat, ids, rw, rank, cnt = _merge(x2, attn, y4, gates, w_glu[i].astype(bf16), b_glu[i][None],
                                w_attn_branch[i].astype(bf16), w_ssm_branch[i].astype(bf16),
                                w_out[i].astype(bf16), norm_moe[i][None], w_router, b_router)
    out = _moe(x1, h_fat, ids, rw, rank, cnt, w_expert_gate[i], w_expert_up[i], w_expert_down[i], norm_final[None])
    return out.reshape(b, l, d)
```

```python
import functools
import math

import jax
import jax.numpy as jnp
from jax import lax
from jax.experimental import pallas as pl
from jax.experimental.pallas import tpu as pltpu

EPS = 1e-6
HEAD_DIM = 64
N_HEADS = 8
N_KV_HEADS = 2
Q_PER_KV = N_HEADS // N_KV_HEADS
ATTN_BLOCK = 128
ATTN_QB = 8
SSM_GROUP = 16
SSM_STATE = 64
N_EXPERT_GROUPS = 4
EXPERTS_PER_GROUP = 8
N_EXPERTS = N_EXPERT_GROUPS * EXPERTS_PER_GROUP
TOP_K = 2

LANES = 128
SUBLANES = 8
SSM_CHUNK_GROUPS = LANES // SSM_GROUP
SSM_CHUNK_STATES = SSM_CHUNK_GROUPS * SSM_STATE

TM_PROJ = 512
TM_SSM = 512
SSM_SEG = TM_SSM // SUBLANES
TM_MERGE = 512
TM_MOVE = 512
TM_EXPERT = 256
ROW_SUB = 4
ROW_BUFS = 6
VMEM_LIMIT = 56 * 1024 * 1024


def _cparams(*sem):
    return pltpu.CompilerParams(dimension_semantics=sem, vmem_limit_bytes=VMEM_LIMIT)


def _pack_bf16_pairs(x):
    half = x.shape[1] // 2
    bits = lax.bitcast_convert_type(x, jnp.uint32)
    return (bits[:, :half] & jnp.uint32(0xFFFF0000)) | (bits[:, half:] >> 16)


def _unpack_bf16_pairs(p):
    hi = lax.bitcast_convert_type(p & jnp.uint32(0xFFFF0000), jnp.float32)
    lo = lax.bitcast_convert_type(p << 16, jnp.float32)
    return jnp.concatenate([hi, lo], axis=1)


def _load_rows(ref, first, n):
    return jnp.concatenate(
        [ref[pl.ds(first * ROW_SUB + c, n, stride=ROW_SUB), :] for c in range(ROW_SUB)], axis=1)


def _store_rows(ref, first, val, mask=None):
    n = val.shape[0]
    for c in range(ROW_SUB):
        idx = pl.ds(first * ROW_SUB + c, n, stride=ROW_SUB)
        v = val[:, c * LANES:(c + 1) * LANES]
        ref[idx, :] = v if mask is None else jnp.where(mask, v, ref[idx, :])


def _row_gather(src_hbm, idx_ref, idx0, dst_ref, dst0, n, sem, start):
    rs = ROW_SUB
    if start:
        for r in range(n):
            pltpu.make_async_copy(src_hbm.at[pl.ds(pl.multiple_of(idx_ref[idx0 + r], rs), rs)],
                                  dst_ref.at[pl.ds(pl.multiple_of((dst0 + r) * rs, rs), rs)],
                                  sem).start(priority=r % 2)
    else:
        pltpu.make_async_copy(src_hbm.at[pl.ds(0, n * rs)],
                              dst_ref.at[pl.ds(pl.multiple_of(dst0 * rs, rs), n * rs)], sem).wait()


def _proj_kernel(x_ref, g_ref, w_ref, bg_ref, q_ref, k_ref, v_ref, u_ref, gate_ref, *, cols):
    q_c, kv_c, d_ssm = cols
    xf = x_ref[...]
    inv = lax.rsqrt(jnp.mean(xf * xf, axis=-1, keepdims=True) + EPS)
    h = (xf * inv * g_ref[...]).astype(jnp.bfloat16)
    o_gate = q_c + 2 * kv_c + d_ssm
    gl = jnp.dot(h, w_ref[:, o_gate:], preferred_element_type=jnp.float32) + bg_ref[...]
    gate_ref[...] = jax.nn.sigmoid(gl).astype(gate_ref.dtype)
    o = 0
    q_ref[...] = (jnp.dot(h, w_ref[:, o:o + q_c], preferred_element_type=jnp.float32)
                  * (1.0 / math.sqrt(HEAD_DIM))).astype(q_ref.dtype)
    o += q_c
    k_ref[...] = jnp.dot(h, w_ref[:, o:o + kv_c], preferred_element_type=jnp.float32).astype(k_ref.dtype)
    o += kv_c
    v_ref[...] = jnp.dot(h, w_ref[:, o:o + kv_c], preferred_element_type=jnp.float32).astype(v_ref.dtype)
    o += kv_c
    seg = x_ref.shape[0] // SUBLANES
    for s in range(0, d_ssm // LANES, 2):
        uu = jnp.dot(h, w_ref[:, o:o + 2 * LANES], preferred_element_type=jnp.float32)
        for half in range(2):
            for j in range(SUBLANES):
                u_ref[s + half, pl.ds(j, seg, stride=SUBLANES), :] = (
                    uu[j * seg:(j + 1) * seg, half * LANES:(half + 1) * LANES])
        o += 2 * LANES


def _in_proj(x2, norm_w, w_in, b_gate, d_attn, kv_cols, d_ssm):
    t, d = x2.shape
    gate_cols = b_gate.shape[-1]
    n_slab = d_ssm // LANES
    tm = TM_PROJ
    kern = functools.partial(_proj_kernel, cols=(d_attn, kv_cols, d_ssm))
    return pl.pallas_call(
        kern,
        grid=(t // tm,),
        in_specs=[
            pl.BlockSpec((tm, d), lambda i: (i, 0)),
            pl.BlockSpec((1, d), lambda i: (0, 0)),
            pl.BlockSpec(w_in.shape, lambda i: (0, 0)),
            pl.BlockSpec((1, gate_cols), lambda i: (0, 0)),
        ],
        out_specs=[
            pl.BlockSpec((tm, d_attn), lambda i: (i, 0)),
            pl.BlockSpec((tm, kv_cols), lambda i: (i, 0)),
            pl.BlockSpec((tm, kv_cols), lambda i: (i, 0)),
            pl.BlockSpec((n_slab, tm, LANES), lambda i: (0, i, 0)),
            pl.BlockSpec((tm, gate_cols), lambda i: (i, 0)),
        ],
        out_shape=[
            jax.ShapeDtypeStruct((t, d_attn), jnp.bfloat16),
            jax.ShapeDtypeStruct((t, kv_cols), jnp.bfloat16),
            jax.ShapeDtypeStruct((t, kv_cols), jnp.bfloat16),
            jax.ShapeDtypeStruct((n_slab, t, LANES), jnp.float32),
            jax.ShapeDtypeStruct((t, gate_cols), jnp.bfloat16),
        ],
        compiler_params=_cparams("arbitrary"),
        name="in_proj",
    )(x2, norm_w, w_in, b_gate)


def _attn_kernel(sink_ref, q_ref, kp_ref, kc_ref, vp_ref, vc_ref, rep_ref, mask_ref, o_ref):
    i = pl.program_id(1)
    blk = ATTN_BLOCK
    hw = Q_PER_KV * HEAD_DIM
    rows = Q_PER_KV * blk
    head_of_row = lax.broadcasted_iota(jnp.int32, (rows, 1), 0) >> 7
    lane_head_q = lax.broadcasted_iota(jnp.int32, (blk, hw), 1) >> 6
    lane_head_v = lax.broadcasted_iota(jnp.int32, (2 * blk, hw), 1) >> 6
    sinks = []
    for kh in range(N_KV_HEADS):
        sink = jnp.zeros((rows, 1), jnp.float32)
        for g in range(Q_PER_KV):
            sink = jnp.where(head_of_row == g, sink_ref[kh * Q_PER_KV + g], sink)
        sinks.append(sink)
    for qb in range(ATTN_QB):
        if qb == 0:
            k2 = jnp.concatenate([kp_ref[...], kc_ref[0:blk, :]], axis=0)
            v2 = jnp.concatenate([vp_ref[...], vc_ref[0:blk, :]], axis=0)
            bias = mask_ref[jnp.where(i == 0, 1, 0)]
        else:
            k2 = kc_ref[(qb - 1) * blk:(qb + 1) * blk, :]
            v2 = vc_ref[(qb - 1) * blk:(qb + 1) * blk, :]
            bias = mask_ref[0]
        for kh in range(N_KV_HEADS):
            rep = rep_ref[kh]
            k4 = jnp.dot(k2, rep, preferred_element_type=jnp.float32).astype(jnp.bfloat16)
            v4 = jnp.dot(v2, rep, preferred_element_type=jnp.float32).astype(jnp.bfloat16)
            qh = q_ref[qb * blk:(qb + 1) * blk, kh * hw:(kh + 1) * hw]
            qm = jnp.concatenate(
                [jnp.where(lane_head_q == g, qh, jnp.zeros_like(qh)) for g in range(Q_PER_KV)], axis=0)
            s = lax.dot_general(qm, k4, (((1,), (1,)), ((), ())), preferred_element_type=jnp.float32)
            s = s + bias
            sink = sinks[kh]
            m = jnp.maximum(jnp.max(s, axis=-1, keepdims=True), sink)
            p = jnp.exp(s - m)
            rinv = 1.0 / (jnp.sum(p, axis=-1, keepdims=True) + jnp.exp(sink - m))
            p = p.astype(jnp.bfloat16)
            p_cat = jnp.concatenate([p[g * blk:(g + 1) * blk, :] for g in range(Q_PER_KV)], axis=1)
            vm = jnp.concatenate(
                [jnp.where(lane_head_v == g, v4, jnp.zeros_like(v4)) for g in range(Q_PER_KV)], axis=0)
            o = jnp.dot(p_cat, vm, preferred_element_type=jnp.float32)
            scale = jnp.zeros((blk, hw), jnp.float32)
            for g in range(Q_PER_KV):
                scale = jnp.where(lane_head_q == g, rinv[g * blk:(g + 1) * blk, :], scale)
            o_ref[qb * blk:(qb + 1) * blk, kh * hw:(kh + 1) * hw] = (o * scale).astype(o_ref.dtype)


def _attention(q, k, v, sinks, b, l):
    d_attn = q.shape[-1]
    kv_cols = k.shape[-1]
    blk = ATTN_BLOCK
    tq = ATTN_QB * blk
    hw = Q_PER_KV * HEAD_DIM
    lane = jnp.arange(hw)[None, :]
    src = jnp.arange(kv_cols)[:, None]
    rep = jnp.stack([(src == kh * HEAD_DIM + (lane % HEAD_DIM)) for kh in range(N_KV_HEADS)]
                    ).astype(jnp.bfloat16)
    r = (jnp.arange(Q_PER_KV * blk) % blk)[:, None]
    c = jnp.arange(2 * blk)[None, :]
    band = (c > r) & (c <= r + blk)
    mask = jnp.where(jnp.stack([band, band & (c >= blk)]), 0.0, jnp.finfo(jnp.float32).min).astype(jnp.float32)
    q3 = q.reshape(b, l, d_attn)
    k3 = k.reshape(b, l, kv_cols)
    v3 = v.reshape(b, l, kv_cols)
    cur = lambda bi, i: (bi, i, 0)
    prev = lambda bi, i: (bi, jnp.maximum(ATTN_QB * i - 1, 0), 0)
    out = pl.pallas_call(
        _attn_kernel,
        grid=(b, l // tq),
        in_specs=[
            pl.BlockSpec(memory_space=pltpu.SMEM),
            pl.BlockSpec((None, tq, d_attn), cur),
            pl.BlockSpec((None, blk, kv_cols), prev),
            pl.BlockSpec((None, tq, kv_cols), cur),
            pl.BlockSpec((None, blk, kv_cols), prev),
            pl.BlockSpec((None, tq, kv_cols), cur),
            pl.BlockSpec(rep.shape, lambda bi, i: (0, 0, 0)),
            pl.BlockSpec(mask.shape, lambda bi, i: (0, 0, 0)),
        ],
        out_specs=pl.BlockSpec((None, tq, d_attn), cur),
        out_shape=jax.ShapeDtypeStruct((b, l, d_attn), jnp.bfloat16),
        compiler_params=_cparams("arbitrary", "arbitrary"),
        name="swa",
    )(sinks, q3, k3, k3, v3, v3, rep, mask)
    return out.reshape(b * l, d_attn)


def _ssm_kernel(u_ref, bmat_ref, cmat_ref, lam_ref, pw_ref, lamseg_ref, d_ref, y_ref, bu_ref, carry_ref):
    i = pl.program_id(1)
    n_slab = u_ref.shape[0]
    tm = u_ref.shape[1]
    seg = tm // SUBLANES
    ns = SSM_CHUNK_STATES
    npair = 2

    @pl.when(i == 0)
    def _():
        carry_ref[...] = jnp.zeros_like(carry_ref)

    sub = lax.broadcasted_iota(jnp.int32, (SUBLANES, ns), 0)
    for s0 in range(0, n_slab, npair):
        lam = []
        for q in range(npair):
            s = s0 + q
            bu_ref[q] = jnp.dot(u_ref[s].astype(jnp.bfloat16), bmat_ref[s], preferred_element_type=jnp.float32)
            lam.append((jnp.broadcast_to(lam_ref[s, 0:1, :], (SUBLANES, ns)),
                        jnp.broadcast_to(lam_ref[s, 1:2, :], (SUBLANES, ns))))

        def step(r, st):
            rows = pl.ds(r * SUBLANES, SUBLANES)
            out = []
            for q in range(npair):
                lr, li = lam[q]
                sr, si = st[q]
                nr = lr * sr - li * si + bu_ref[q, rows, 0:ns]
                ni = lr * si + li * sr + bu_ref[q, rows, ns:2 * ns]
                bu_ref[q, rows, 0:ns] = nr
                bu_ref[q, rows, ns:2 * ns] = ni
                out.append((nr, ni))
            return tuple(out)

        zero = jnp.zeros((SUBLANES, ns), jnp.float32)
        ends = ((zero, zero),) * npair
        for r in range(seg):
            ends = step(r, ends)

        for q in range(npair):
            s = s0 + q
            er, ei = ends[q]
            ar = lamseg_ref[s, 0:1, :]
            ai = lamseg_ref[s, 1:2, :]
            cr = carry_ref[s, 0:1, :]
            ci = carry_ref[s, 1:2, :]
            car = jnp.zeros((SUBLANES, ns), jnp.float32)
            cai = jnp.zeros((SUBLANES, ns), jnp.float32)
            for j in range(SUBLANES):
                car = jnp.where(sub == j, jnp.broadcast_to(cr, (SUBLANES, ns)), car)
                cai = jnp.where(sub == j, jnp.broadcast_to(ci, (SUBLANES, ns)), cai)
                ejr = jnp.sum(jnp.where(sub == j, er, 0.0), axis=0, keepdims=True)
                eji = jnp.sum(jnp.where(sub == j, ei, 0.0), axis=0, keepdims=True)
                cr, ci = ar * cr - ai * ci + ejr, ar * ci + ai * cr + eji
            carry_ref[s, 0:1, :] = cr
            carry_ref[s, 1:2, :] = ci

            ctr = jnp.broadcast_to(car[None], (seg, SUBLANES, ns)).reshape(tm, ns)
            cti = jnp.broadcast_to(cai[None], (seg, SUBLANES, ns)).reshape(tm, ns)
            pr = pw_ref[s, 0]
            pi = pw_ref[s, 1]
            st_r = bu_ref[q, :, 0:ns] + pr * ctr - pi * cti
            st_i = bu_ref[q, :, ns:2 * ns] + pr * cti + pi * ctr
            st = jnp.concatenate([st_r, st_i], axis=1).astype(jnp.bfloat16)
            y_ref[s] = jnp.dot(st, cmat_ref[s], preferred_element_type=jnp.float32) + d_ref[s] * u_ref[s]


def _ssm_tables(a_re, a_im, b_re, b_im, c_re, c_im, d_skip, log_dt, seg):
    f32 = jnp.float32
    g, p = a_re.shape
    c = b_re.shape[-1]
    ng = SSM_CHUNK_GROUPS
    n_slab = g // ng
    lam = lax.complex(a_re.astype(f32), a_im.astype(f32))
    dt = jnp.exp(log_dt.astype(f32))[:, None]
    lam_bar = jnp.exp(lam * dt)
    b_bar = ((lam_bar - 1.0) / lam)[:, :, None] * lax.complex(b_re.astype(f32), b_im.astype(f32))
    c_mat = lax.complex(c_re.astype(f32), c_im.astype(f32))
    eye = jnp.eye(ng, dtype=f32)

    def bdiag_b(m):
        m = m.reshape(n_slab, ng, p, c)
        return jnp.einsum('ab,kbpc->kacbp', eye, m).reshape(n_slab, ng * c, ng * p)

    def bdiag_c(m):
        m = m.reshape(n_slab, ng, c, p)
        return jnp.einsum('ab,kbcp->kbpac', eye, m).reshape(n_slab, ng * p, ng * c)

    bmat = jnp.concatenate([bdiag_b(jnp.real(b_bar)), bdiag_b(jnp.imag(b_bar))], axis=2).astype(jnp.bfloat16)
    cmat = jnp.concatenate([bdiag_c(jnp.real(c_mat)), -bdiag_c(jnp.imag(c_mat))], axis=1).astype(jnp.bfloat16)

    def slab_rows(z):
        z = z.reshape(n_slab, 1, ng * p)
        return jnp.concatenate([jnp.real(z), jnp.imag(z)], axis=1)

    lam_t = slab_rows(lam_bar)
    steps = jnp.arange(1, seg + 1, dtype=f32)
    pw = jnp.exp((lam * dt)[None] * steps[:, None, None])
    pw = pw.reshape(seg, n_slab, ng * p).transpose(1, 0, 2)
    pw = jnp.repeat(pw, SUBLANES, axis=1)
    pw_t = jnp.stack([jnp.real(pw), jnp.imag(pw)], axis=1)
    lamseg_t = slab_rows(jnp.exp(lam * dt * float(seg)))
    d_t = d_skip.astype(f32).reshape(n_slab, 1, ng * c)
    return bmat, cmat, lam_t, pw_t, lamseg_t, d_t


def _ssm(u4, tables, b, l):
    bmat, cmat, lam_t, pw_t, lamseg_t, d_t = tables
    n_slab, t, _ = u4.shape
    tm = TM_SSM
    nt = l // tm
    ns = SSM_CHUNK_STATES
    const = lambda nd: (lambda bi, i: (0,) * nd)
    return pl.pallas_call(
        _ssm_kernel,
        grid=(b, nt),
        in_specs=[
            pl.BlockSpec((n_slab, tm, LANES), lambda bi, i: (0, bi * nt + i, 0)),
            pl.BlockSpec(bmat.shape, const(3)),
            pl.BlockSpec(cmat.shape, const(3)),
            pl.BlockSpec(lam_t.shape, const(3)),
            pl.BlockSpec(pw_t.shape, const(4)),
            pl.BlockSpec(lamseg_t.shape, const(3)),
            pl.BlockSpec(d_t.shape, const(3)),
        ],
        out_specs=pl.BlockSpec((n_slab, tm, LANES), lambda bi, i: (0, bi * nt + i, 0)),
        out_shape=jax.ShapeDtypeStruct((n_slab, t, LANES), jnp.float32),
        scratch_shapes=[
            pltpu.VMEM((2, tm, 2 * ns), jnp.float32),
            pltpu.VMEM((n_slab, 2, ns), jnp.float32),
        ],
        compiler_params=_cparams("arbitrary", "arbitrary"),
        name="s5_scan",
    )(u4, bmat, cmat, lam_t, pw_t, lamseg_t, d_t)


def _merge_kernel(x_ref, attn_ref, y_ref, gate_ref, wglu_ref, bglu_ref, wa_ref, ws_ref, wo_ref,
                  nm_ref, wr_ref, br_ref, tri_ref, x1_ref, h2_ref, ids_ref, rw_ref, rank_ref, cnt_ref,
                  lg_ref, carry_ref):
    i = pl.program_id(0)
    d = x_ref.shape[1]
    tm = x_ref.shape[0]

    @pl.when(i == 0)
    def _():
        lg_ref[...] = jnp.zeros_like(lg_ref)
        carry_ref[...] = jnp.zeros_like(carry_ref)

    logits = lg_ref[...]

    seg = tm // SUBLANES
    y = jnp.concatenate(
        [jnp.concatenate([y_ref[s, pl.ds(j, seg, stride=SUBLANES), :] for j in range(SUBLANES)], axis=0)
         for s in range(y_ref.shape[0])], axis=1)
    z = jax.nn.gelu(y)
    zg = jnp.dot(z.astype(jnp.bfloat16), wglu_ref[...], preferred_element_type=jnp.float32) + bglu_ref[...]
    z = z * jax.nn.sigmoid(zg)
    a = jnp.dot(attn_ref[...], wa_ref[...], preferred_element_type=jnp.float32)
    sb = jnp.dot(z.astype(jnp.bfloat16), ws_ref[...], preferred_element_type=jnp.float32)
    merged = gate_ref[:, 0:d].astype(jnp.float32) * a + gate_ref[:, d:2 * d].astype(jnp.float32) * sb
    x1 = x_ref[...] + jnp.dot(merged.astype(jnp.bfloat16), wo_ref[...], preferred_element_type=jnp.float32)
    x1_ref[...] = x1
    inv = lax.rsqrt(jnp.mean(x1 * x1, axis=-1, keepdims=True) + EPS)
    h2 = x1 * inv * nm_ref[...]
    h_hi = h2.astype(jnp.bfloat16)
    h_pk = _pack_bf16_pairs(h_hi.astype(jnp.float32))
    _store_rows(h2_ref, 0, h_pk)

    h_lo = (h2 - h_hi.astype(jnp.float32)).astype(jnp.bfloat16)
    hh = jnp.dot(h_hi, wr_ref[...], preferred_element_type=jnp.float32)
    lh = jnp.dot(h_lo, wr_ref[:, 0:LANES], preferred_element_type=jnp.float32)
    lg_ref[...] = hh[:, 0:LANES] + hh[:, LANES:2 * LANES] + lh + br_ref[...]

    lane = lax.broadcasted_iota(jnp.int32, (tm, LANES), 1)
    ninf = -jnp.inf
    gl = jnp.where((lane >= N_EXPERTS) & (lane < N_EXPERTS + N_EXPERT_GROUPS), logits, ninf)
    gmax = jnp.max(gl, axis=-1, keepdims=True)
    gidx = jnp.min(jnp.where(gl == gmax, lane - N_EXPERTS, LANES), axis=-1, keepdims=True)
    group_p = 1.0 / jnp.sum(jnp.exp(gl - gmax), axis=-1, keepdims=True)
    el = jnp.where((lane < N_EXPERTS) & ((lane >> 3) == gidx), logits, ninf)
    m1 = jnp.max(el, axis=-1, keepdims=True)
    i1 = jnp.min(jnp.where(el == m1, lane, LANES), axis=-1, keepdims=True)
    el2 = jnp.where(lane == i1, ninf, el)
    m2 = jnp.max(el2, axis=-1, keepdims=True)
    i2 = jnp.min(jnp.where(el2 == m2, lane, LANES), axis=-1, keepdims=True)
    e2 = jnp.exp(m2 - m1)
    w1 = group_p / (1.0 + e2)
    w2 = group_p * e2 / (1.0 + e2)
    ids_ref[...] = jnp.transpose(jnp.where(lane == 0, i1, jnp.where(lane == 1, i2, 0)))[0:SUBLANES, :]
    rw_ref[...] = jnp.where(lane == 0, w1, jnp.where(lane == 1, w2, 0.0))

    oh0 = lane == i1
    oh1 = lane == i2
    live = jnp.where(i > 0, 1.0, 0.0)
    oh = (oh0.astype(jnp.float32) + oh1.astype(jnp.float32)) * live
    cum = jnp.dot(tri_ref[...], oh.astype(jnp.bfloat16), preferred_element_type=jnp.float32) + carry_ref[...]
    r0 = jnp.sum(jnp.where(oh0, cum, 0.0), axis=-1, keepdims=True)
    r1 = jnp.sum(jnp.where(oh1, cum, 0.0), axis=-1, keepdims=True)
    rank_ref[...] = jnp.transpose(
        jnp.where(lane == 0, r0, jnp.where(lane == 1, r1, 0.0)).astype(jnp.int32))[0:SUBLANES, :]
    carry_ref[...] = carry_ref[...] + jnp.sum(oh, axis=0, keepdims=True)
    cnt_ref[...] = carry_ref[...].astype(jnp.int32)


def _merge(x2, attn, y4, gates, w_glu, b_glu, w_a, w_s, w_o, norm_moe, w_router, b_router):
    t, d = x2.shape
    tm = TM_MERGE
    nt = t // tm
    n_slab = y4.shape[0]
    tri = (jnp.arange(tm)[None, :] < jnp.arange(tm)[:, None]).astype(jnp.bfloat16)
    full = lambda a: pl.BlockSpec(a.shape, lambda i: (0,) * a.ndim)
    cur = lambda i: (jnp.minimum(i, nt - 1), 0)
    prv = lambda i: (jnp.maximum(i - 1, 0), 0)
    row = lambda c, m=cur: pl.BlockSpec((tm, c), m)
    slots = pl.BlockSpec((SUBLANES, tm), lambda i: (0, jnp.maximum(i - 1, 0)))
    return pl.pallas_call(
        _merge_kernel,
        grid=(nt + 1,),
        in_specs=[
            row(d), row(attn.shape[1]),
            pl.BlockSpec((n_slab, tm, LANES), lambda i: (0, jnp.minimum(i, nt - 1), 0)),
            row(gates.shape[1]),
            full(w_glu), full(b_glu), full(w_a), full(w_s), full(w_o), full(norm_moe),
            full(w_router), full(b_router), full(tri),
        ],
        out_specs=[row(d), pl.BlockSpec((tm * ROW_SUB, LANES), cur), slots, row(LANES, prv),
                   slots, pl.BlockSpec((1, LANES), lambda i: (0, 0))],
        out_shape=[
            jax.ShapeDtypeStruct((t, d), jnp.float32),
            jax.ShapeDtypeStruct((t * ROW_SUB, LANES), jnp.uint32),
            jax.ShapeDtypeStruct((SUBLANES, t), jnp.int32),
            jax.ShapeDtypeStruct((t, LANES), jnp.float32),
            jax.ShapeDtypeStruct((SUBLANES, t), jnp.int32),
            jax.ShapeDtypeStruct((1, LANES), jnp.int32),
        ],
        scratch_shapes=[pltpu.VMEM((tm, LANES), jnp.float32), pltpu.VMEM((1, LANES), jnp.float32)],
        compiler_params=_cparams("arbitrary"),
        name="merge_router",
    )(x2, attn, y4, gates, w_glu, b_glu, w_a, w_s, w_o, norm_moe, w_router, b_router, tri)


def _inverse_kernel(pos_ref, src_ref):
    n_slots = pos_ref.shape[0]
    t = n_slots // TOP_K

    def body(tok, c):
        for k in range(TOP_K):
            src_ref[pos_ref[k * t + tok]] = tok * ROW_SUB
        return c

    lax.fori_loop(0, t, body, 0, unroll=8)


def _inverse(pos1d):
    n_slots = pos1d.shape[0]
    smem = pl.BlockSpec(memory_space=pltpu.SMEM)
    return pl.pallas_call(
        _inverse_kernel,
        in_specs=[smem],
        out_specs=smem,
        out_shape=jax.ShapeDtypeStruct((n_slots,), jnp.int32),
        name="route_inverse",
    )(pos1d)


def _expert_kernel(ta_ref, tn_ref, tnew_ref, sexp_ref, slo_ref, shi_ref, meta_ref, src_ref,
                   h_ref, wg_hbm, wu_hbm, wd_hbm, y_ref,
                   xbuf_ref, wgs_ref, wus_ref, wds_ref, wgb_ref, wub_ref, wdb_ref, gsem, wsem):
    j = pl.program_id(0)
    n_tiles = pl.num_programs(0)
    last = n_tiles - 1
    tm = TM_EXPERT
    rs = ROW_SUB
    nb = ROW_BUFS
    n_used = meta_ref[0]
    slot = j % nb
    a = ta_ref[j]

    def gather(tile, sl, start):
        _row_gather(h_ref, src_ref, tile * tm, xbuf_ref, sl * tm, tm, gsem.at[sl], start)

    def weights(m, start):
        sl = m % 2
        e = sexp_ref[m]
        for hbm, stage in ((wg_hbm, wgs_ref), (wu_hbm, wus_ref), (wd_hbm, wds_ref)):
            cp = pltpu.make_async_copy(hbm.at[e], stage.at[sl], wsem.at[sl])
            cp.start() if start else cp.wait()

    def new_expert(m):
        weights(m, False)
        sl = m % 2
        wgb_ref[...] = wgs_ref[sl].astype(jnp.bfloat16)
        wub_ref[...] = wus_ref[sl].astype(jnp.bfloat16)
        wdb_ref[...] = wds_ref[sl].astype(jnp.bfloat16)

        @pl.when(m + 1 < n_used)
        def _():
            weights(m + 1, True)

    def expert_out(xp):
        x = _unpack_bf16_pairs(xp).astype(jnp.bfloat16)
        hg = jnp.dot(x, wgb_ref[...], preferred_element_type=jnp.float32)
        hu = jnp.dot(x, wub_ref[...], preferred_element_type=jnp.float32)
        act = (jax.nn.silu(hg) * hu).astype(jnp.bfloat16)
        y = jnp.dot(act, wdb_ref[...], preferred_element_type=jnp.float32)
        return _pack_bf16_pairs(y.astype(jnp.bfloat16).astype(jnp.float32))

    @pl.when(j == 0)
    def _():
        weights(0, True)
        for b in range(nb - 1):
            gather(b, b, True)

    @pl.when(tnew_ref[j] == 1)
    def _():
        new_expert(a)

    gather(j, slot, False)
    xp = _load_rows(xbuf_ref, slot * tm, tm)
    gather(jnp.minimum(j + nb - 1, last), (j + nb - 1) % nb, True)
    _store_rows(y_ref, 0, expert_out(xp))

    def extra(i, carry):
        m = a + i
        new_expert(m)
        yp = expert_out(_load_rows(xbuf_ref, slot * tm, tm))
        row = lax.broadcasted_iota(jnp.int32, (tm, LANES), 0) + j * tm
        mine = (row >= slo_ref[m]) & (row < shi_ref[m])
        _store_rows(y_ref, 0, yp, mine)
        return carry

    lax.fori_loop(1, tn_ref[j], extra, 0)

    @pl.when(j == last)
    def _():
        for b in range(1, nb):
            gather(j, (j + b) % nb, False)


def _experts(meta, src_tok, h_fat, w_gate, w_up, w_down):
    n_slots = src_tok.shape[0]
    ne, d, dff = w_gate.shape
    tm = TM_EXPERT
    rs = ROW_SUB
    assert n_slots // tm >= ROW_BUFS
    any_spec = pl.BlockSpec(memory_space=pl.ANY)
    grid_spec = pltpu.PrefetchScalarGridSpec(
        num_scalar_prefetch=8,
        grid=(n_slots // tm,),
        in_specs=[any_spec, any_spec, any_spec, any_spec],
        out_specs=pl.BlockSpec((tm * rs, LANES), lambda j, *_: (j, 0)),
        scratch_shapes=[
            pltpu.VMEM((ROW_BUFS * tm * rs, LANES), jnp.uint32),
            pltpu.VMEM((2, d, dff), jnp.float32),
            pltpu.VMEM((2, d, dff), jnp.float32),
            pltpu.VMEM((2, dff, d), jnp.float32),
            pltpu.VMEM((d, dff), jnp.bfloat16),
            pltpu.VMEM((d, dff), jnp.bfloat16),
            pltpu.VMEM((dff, d), jnp.bfloat16),
            pltpu.SemaphoreType.DMA((ROW_BUFS,)),
            pltpu.SemaphoreType.DMA((2,)),
        ],
    )
    return pl.pallas_call(
        _expert_kernel,
        grid_spec=grid_spec,
        out_shape=jax.ShapeDtypeStruct((n_slots * rs, LANES), jnp.uint32),
        compiler_params=_cparams("arbitrary"),
        name="moe_experts",
    )(*meta, src_tok, h_fat, w_gate, w_up, w_down)


def _combine_kernel(pos_ref, x1_ref, rw_ref, nf_ref, ys_ref, o_ref, buf_ref, sem):
    i = pl.program_id(0)
    n = pl.num_programs(0)
    last = n - 1
    tm = x1_ref.shape[0]
    t = n * tm
    nb = ROW_BUFS
    slot = i % nb

    def gather(tile, sl, start):
        for k in range(TOP_K):
            _row_gather(ys_ref, pos_ref, k * t + tile * tm, buf_ref, (sl * TOP_K + k) * tm, tm, sem.at[sl], start)

    @pl.when(i == 0)
    def _():
        for b in range(nb - 1):
            gather(b, b, True)

    gather(i, slot, False)
    y0 = _unpack_bf16_pairs(_load_rows(buf_ref, (slot * TOP_K) * tm, tm))
    y1 = _unpack_bf16_pairs(_load_rows(buf_ref, (slot * TOP_K + 1) * tm, tm))
    gather(jnp.minimum(i + nb - 1, last), (i + nb - 1) % nb, True)
    rw = rw_ref[...]
    x = x1_ref[...] + rw[:, 0:1] * y0 + rw[:, 1:2] * y1
    inv = lax.rsqrt(jnp.mean(x * x, axis=-1, keepdims=True) + EPS)
    o_ref[...] = x * inv * nf_ref[...]

    @pl.when(i == last)
    def _():
        for b in range(1, nb):
            gather(i, (i + b) % nb, False)


def _combine(pos_rows, x1, rw, norm_final, ys):
    t, d = x1.shape
    tm = TM_MOVE
    assert t // tm >= ROW_BUFS
    return pl.pallas_call(
        _combine_kernel,
        grid=(t // tm,),
        in_specs=[
            pl.BlockSpec(memory_space=pltpu.SMEM),
            pl.BlockSpec((tm, d), lambda i: (i, 0)),
            pl.BlockSpec((tm, LANES), lambda i: (i, 0)),
            pl.BlockSpec((1, d), lambda i: (0, 0)),
            pl.BlockSpec(memory_space=pl.ANY),
        ],
        out_specs=pl.BlockSpec((tm, d), lambda i: (i, 0)),
        out_shape=jax.ShapeDtypeStruct((t, d), jnp.float32),
        scratch_shapes=[pltpu.VMEM((ROW_BUFS * TOP_K * tm * ROW_SUB, LANES), jnp.uint32),
                        pltpu.SemaphoreType.DMA((ROW_BUFS,))],
        compiler_params=_cparams("arbitrary"),
        name="moe_combine",
    )(pos_rows, x1, rw, norm_final, ys)


def _expert_meta(counts, n_slots):
    tm = TM_EXPERT
    n_tiles = n_slots // tm
    i32 = jnp.int32
    ends = jnp.cumsum(counts).astype(i32)
    starts = ends - counts
    used = counts > 0
    n_used = jnp.sum(used).astype(i32)
    seq_of = jnp.cumsum(used).astype(i32) - 1
    m_idx = jnp.arange(N_EXPERTS, dtype=i32)
    pick = used[None, :] & (seq_of[None, :] == m_idx[:, None])
    s_exp = jnp.sum(jnp.where(pick, m_idx[None, :], 0), axis=1).astype(i32)
    s_lo = jnp.sum(jnp.where(pick, starts[None, :], 0), axis=1).astype(i32)
    s_hi = jnp.sum(jnp.where(pick, ends[None, :], 0), axis=1).astype(i32)
    row0 = jnp.arange(n_tiles, dtype=i32)[:, None] * tm
    t_a = jnp.sum(used[None, :] & (ends[None, :] <= row0), axis=1).astype(i32)
    t_b = jnp.sum(used[None, :] & (starts[None, :] < row0 + tm), axis=1).astype(i32) - 1
    t_new = jnp.any(used[None, :] & (starts[None, :] == row0), axis=1).astype(i32)
    return t_a, (t_b - t_a + 1).astype(i32), t_new, s_exp, s_lo, s_hi, n_used.reshape(1)


def _moe(x1, h_fat, ids, rw, rank, cnt, w_gate, w_up, w_down, norm_final):
    t, d = x1.shape
    counts = cnt[0, :N_EXPERTS]
    offs = (jnp.cumsum(counts) - counts).astype(jnp.int32)
    onehot = ids[:TOP_K, :, None] == jnp.arange(N_EXPERTS, dtype=jnp.int32)
    pos1d = (jnp.sum(jnp.where(onehot, offs, 0), axis=-1) + rank[:TOP_K]).reshape(-1)
    src_tok = _inverse(pos1d)
    meta = _expert_meta(counts, t * TOP_K)
    ys = _experts(meta, src_tok, h_fat, w_gate, w_up, w_down)
    return _combine(pos1d * ROW_SUB, x1, rw, norm_final, ys)


def kernel(x, norm_mix, w_in, b_gate, attn_sinks, ssm_a_re, ssm_a_im, ssm_b_re, ssm_b_im, ssm_c_re, ssm_c_im, ssm_d, ssm_log_dt, w_glu, b_glu, w_attn_branch, w_ssm_branch, w_out, norm_moe, w_router_group, b_router_group, w_router_expert, b_router_expert, w_expert_gate, w_expert_up, w_expert_down, norm_final):
    b, l, d = x.shape
    depth = w_in.shape[0]
    assert depth == 1, "the final norm is fused into the last layer's combine kernel"
    d_attn = N_HEADS * HEAD_DIM
    kv_cols = N_KV_HEADS * HEAD_DIM
    d_ssm = ssm_d.shape[-1]
    bf16 = jnp.bfloat16
    x2 = x.reshape(b * l, d)
    assert TM_PROJ == TM_SSM == TM_MERGE and l % TM_SSM == 0
    i = 0
    q, k, v, u4, gates = _in_proj(x2, norm_mix[i][None], w_in[i].astype(bf16), b_gate[i][None],
                                  d_attn, kv_cols, d_ssm)
    attn = _attention(q, k, v, attn_sinks[i], b, l)
    tables = _ssm_tables(ssm_a_re[i], ssm_a_im[i], ssm_b_re[i], ssm_b_im[i], ssm_c_re[i], ssm_c_im[i],
                         ssm_d[i], ssm_log_dt[i], SSM_SEG)
    y4 = _ssm(u4, tables, b, l)
    pad = LANES - N_EXPERTS - N_EXPERT_GROUPS
    w_router = jnp.concatenate([w_router_expert[i], w_router_group[i], jnp.zeros((d, pad), jnp.float32)], axis=1)
    w_r_hi = w_router.astype(bf16)
    w_router = jnp.concatenate([w_r_hi, (w_router - w_r_hi.astype(jnp.float32)).astype(bf16)], axis=1)
    b_router = jnp.concatenate([b_router_expert[i], b_router_group[i], jnp.zeros((pad,), jnp.float32)])[None]
    x1, h_fat, ids, rw, rank, cnt = _merge(x2, attn, y4, gates, w_glu[i].astype(bf16), b_glu[i][None],
                                w_attn_branch[i].astype(bf16), w_ssm_branch[i].astype(bf16),
                                w_out[i].astype(bf16), norm_moe[i][None], w_router, b_router)
    out = _moe(x1, h_fat, ids, rw, rank, cnt, w_expert_gate[i], w_expert_up[i], w_expert_down[i], norm_final[None])
    return out.reshape(b, l, d)
```

```python
import functools
import math

import jax
import jax.numpy as jnp
from jax import lax
from jax.experimental import pallas as pl
from jax.experimental.pallas import tpu as pltpu

EPS = 1e-6
HEAD_DIM = 64
N_HEADS = 8
N_KV_HEADS = 2
Q_PER_KV = N_HEADS // N_KV_HEADS
ATTN_BLOCK = 128
ATTN_QB = 8
SSM_GROUP = 16
SSM_STATE = 64
N_EXPERT_GROUPS = 4
EXPERTS_PER_GROUP = 8
N_EXPERTS = N_EXPERT_GROUPS * EXPERTS_PER_GROUP
TOP_K = 2

LANES = 128
SUBLANES = 8
SSM_CHUNK_GROUPS = LANES // SSM_GROUP
SSM_CHUNK_STATES = SSM_CHUNK_GROUPS * SSM_STATE

TM_PROJ = 512
TM_SSM = 512
SSM_SEG = TM_SSM // SUBLANES
TM_MERGE = 512
TM_MOVE = 512
TM_EXPERT = 256
ROW_SUB = 4
ROW_BUFS = 4
VMEM_LIMIT = 56 * 1024 * 1024


def _cparams(*sem):
    return pltpu.CompilerParams(dimension_semantics=sem, vmem_limit_bytes=VMEM_LIMIT)


def _pack_bf16_pairs(x):
    half = x.shape[1] // 2
    bits = lax.bitcast_convert_type(x, jnp.uint32)
    return (bits[:, :half] & jnp.uint32(0xFFFF0000)) | (bits[:, half:] >> 16)


def _unpack_bf16_pairs(p):
    hi = lax.bitcast_convert_type(p & jnp.uint32(0xFFFF0000), jnp.float32)
    lo = lax.bitcast_convert_type(p << 16, jnp.float32)
    return jnp.concatenate([hi, lo], axis=1)


def _load_rows(ref, first, n):
    return jnp.concatenate(
        [ref[pl.ds(first * ROW_SUB + c, n, stride=ROW_SUB), :] for c in range(ROW_SUB)], axis=1)


def _store_rows(ref, first, val, mask=None):
    n = val.shape[0]
    for c in range(ROW_SUB):
        idx = pl.ds(first * ROW_SUB + c, n, stride=ROW_SUB)
        v = val[:, c * LANES:(c + 1) * LANES]
        ref[idx, :] = v if mask is None else jnp.where(mask, v, ref[idx, :])


def _row_gather(src_hbm, idx_ref, idx0, dst_ref, dst0, n, sem, start):
    rs = ROW_SUB
    if start:
        for r in range(n):
            pltpu.make_async_copy(src_hbm.at[pl.ds(pl.multiple_of(idx_ref[idx0 + r], rs), rs)],
                                  dst_ref.at[pl.ds(pl.multiple_of((dst0 + r) * rs, rs), rs)],
                                  sem).start(priority=r % 2)
    else:
        pltpu.make_async_copy(src_hbm.at[pl.ds(0, n * rs)],
                              dst_ref.at[pl.ds(pl.multiple_of(dst0 * rs, rs), n * rs)], sem).wait()


def _proj_kernel(x_ref, g_ref, w32_ref, bg_ref, q_ref, k_ref, v_ref, u_ref, gate_ref, w_ref, *, cols):
    q_c, kv_c, d_ssm = cols

    @pl.when(pl.program_id(0) == 0)
    def _():
        w_ref[...] = w32_ref[...].astype(jnp.bfloat16)

    xf = x_ref[...]
    inv = lax.rsqrt(jnp.mean(xf * xf, axis=-1, keepdims=True) + EPS)
    h = (xf * inv * g_ref[...]).astype(jnp.bfloat16)
    o_gate = q_c + 2 * kv_c + d_ssm
    gl = jnp.dot(h, w_ref[:, o_gate:], preferred_element_type=jnp.float32) + bg_ref[...]
    gate_ref[...] = jax.nn.sigmoid(gl).astype(gate_ref.dtype)
    o = 0
    q_ref[...] = (jnp.dot(h, w_ref[:, o:o + q_c], preferred_element_type=jnp.float32)
                  * (1.0 / math.sqrt(HEAD_DIM))).astype(q_ref.dtype)
    o += q_c
    k_ref[...] = jnp.dot(h, w_ref[:, o:o + kv_c], preferred_element_type=jnp.float32).astype(k_ref.dtype)
    o += kv_c
    v_ref[...] = jnp.dot(h, w_ref[:, o:o + kv_c], preferred_element_type=jnp.float32).astype(v_ref.dtype)
    o += kv_c
    seg = x_ref.shape[0] // SUBLANES
    for s in range(0, d_ssm // LANES, 2):
        uu = jnp.dot(h, w_ref[:, o:o + 2 * LANES], preferred_element_type=jnp.float32)
        for half in range(2):
            for j in range(SUBLANES):
                u_ref[s + half, pl.ds(j, seg, stride=SUBLANES), :] = (
                    uu[j * seg:(j + 1) * seg, half * LANES:(half + 1) * LANES])
        o += 2 * LANES


def _in_proj(x2, norm_w, w_in, b_gate, d_attn, kv_cols, d_ssm):
    t, d = x2.shape
    gate_cols = b_gate.shape[-1]
    n_slab = d_ssm // LANES
    tm = TM_PROJ
    kern = functools.partial(_proj_kernel, cols=(d_attn, kv_cols, d_ssm))
    return pl.pallas_call(
        kern,
        grid=(t // tm,),
        in_specs=[
            pl.BlockSpec((tm, d), lambda i: (i, 0)),
            pl.BlockSpec((1, d), lambda i: (0, 0)),
            pl.BlockSpec(w_in.shape, lambda i: (0, 0), pipeline_mode=pl.Buffered(1)),
            pl.BlockSpec((1, gate_cols), lambda i: (0, 0)),
        ],
        out_specs=[
            pl.BlockSpec((tm, d_attn), lambda i: (i, 0)),
            pl.BlockSpec((tm, kv_cols), lambda i: (i, 0)),
            pl.BlockSpec((tm, kv_cols), lambda i: (i, 0)),
            pl.BlockSpec((n_slab, tm, LANES), lambda i: (0, i, 0)),
            pl.BlockSpec((tm, gate_cols), lambda i: (i, 0)),
        ],
        out_shape=[
            jax.ShapeDtypeStruct((t, d_attn), jnp.bfloat16),
            jax.ShapeDtypeStruct((t, kv_cols), jnp.bfloat16),
            jax.ShapeDtypeStruct((t, kv_cols), jnp.bfloat16),
            jax.ShapeDtypeStruct((n_slab, t, LANES), jnp.float32),
            jax.ShapeDtypeStruct((t, gate_cols), jnp.bfloat16),
        ],
        scratch_shapes=[pltpu.VMEM(w_in.shape, jnp.bfloat16)],
        compiler_params=_cparams("arbitrary"),
        name="in_proj",
    )(x2, norm_w, w_in, b_gate)


def _attn_kernel(sink_ref, q_ref, kp_ref, kc_ref, vp_ref, vc_ref, rep_ref, mask_ref, o_ref):
    i = pl.program_id(1)
    blk = ATTN_BLOCK
    hw = Q_PER_KV * HEAD_DIM
    rows = Q_PER_KV * blk
    head_of_row = lax.broadcasted_iota(jnp.int32, (rows, 1), 0) >> 7
    lane_head_q = lax.broadcasted_iota(jnp.int32, (blk, hw), 1) >> 6
    lane_head_v = lax.broadcasted_iota(jnp.int32, (2 * blk, hw), 1) >> 6
    sinks = []
    for kh in range(N_KV_HEADS):
        sink = jnp.zeros((rows, 1), jnp.float32)
        for g in range(Q_PER_KV):
            sink = jnp.where(head_of_row == g, sink_ref[kh * Q_PER_KV + g], sink)
        sinks.append(sink)
    for qb in range(ATTN_QB):
        if qb == 0:
            k2 = jnp.concatenate([kp_ref[...], kc_ref[0:blk, :]], axis=0)
            v2 = jnp.concatenate([vp_ref[...], vc_ref[0:blk, :]], axis=0)
            bias = mask_ref[jnp.where(i == 0, 1, 0)]
        else:
            k2 = kc_ref[(qb - 1) * blk:(qb + 1) * blk, :]
            v2 = vc_ref[(qb - 1) * blk:(qb + 1) * blk, :]
            bias = mask_ref[0]
        for kh in range(N_KV_HEADS):
            rep = rep_ref[kh]
            k4 = jnp.dot(k2, rep, preferred_element_type=jnp.float32).astype(jnp.bfloat16)
            v4 = jnp.dot(v2, rep, preferred_element_type=jnp.float32).astype(jnp.bfloat16)
            qh = q_ref[qb * blk:(qb + 1) * blk, kh * hw:(kh + 1) * hw]
            qm = jnp.concatenate(
                [jnp.where(lane_head_q == g, qh, jnp.zeros_like(qh)) for g in range(Q_PER_KV)], axis=0)
            s = lax.dot_general(qm, k4, (((1,), (1,)), ((), ())), preferred_element_type=jnp.float32)
            s = s + bias
            sink = sinks[kh]
            m = jnp.maximum(jnp.max(s, axis=-1, keepdims=True), sink)
            p = jnp.exp(s - m)
            rinv = 1.0 / (jnp.sum(p, axis=-1, keepdims=True) + jnp.exp(sink - m))
            p = p.astype(jnp.bfloat16)
            p_cat = jnp.concatenate([p[g * blk:(g + 1) * blk, :] for g in range(Q_PER_KV)], axis=1)
            vm = jnp.concatenate(
                [jnp.where(lane_head_v == g, v4, jnp.zeros_like(v4)) for g in range(Q_PER_KV)], axis=0)
            o = jnp.dot(p_cat, vm, preferred_element_type=jnp.float32)
            scale = jnp.zeros((blk, hw), jnp.float32)
            for g in range(Q_PER_KV):
                scale = jnp.where(lane_head_q == g, rinv[g * blk:(g + 1) * blk, :], scale)
            o_ref[qb * blk:(qb + 1) * blk, kh * hw:(kh + 1) * hw] = (o * scale).astype(o_ref.dtype)


def _attention(q, k, v, sinks, b, l):
    d_attn = q.shape[-1]
    kv_cols = k.shape[-1]
    blk = ATTN_BLOCK
    tq = ATTN_QB * blk
    hw = Q_PER_KV * HEAD_DIM
    lane = jnp.arange(hw)[None, :]
    src = jnp.arange(kv_cols)[:, None]
    rep = jnp.stack([(src == kh * HEAD_DIM + (lane % HEAD_DIM)) for kh in range(N_KV_HEADS)]
                    ).astype(jnp.bfloat16)
    r = (jnp.arange(Q_PER_KV * blk) % blk)[:, None]
    c = jnp.arange(2 * blk)[None, :]
    band = (c > r) & (c <= r + blk)
    mask = jnp.where(jnp.stack([band, band & (c >= blk)]), 0.0, jnp.finfo(jnp.float32).min).astype(jnp.float32)
    q3 = q.reshape(b, l, d_attn)
    k3 = k.reshape(b, l, kv_cols)
    v3 = v.reshape(b, l, kv_cols)
    cur = lambda bi, i: (bi, i, 0)
    prev = lambda bi, i: (bi, jnp.maximum(ATTN_QB * i - 1, 0), 0)
    out = pl.pallas_call(
        _attn_kernel,
        grid=(b, l // tq),
        in_specs=[
            pl.BlockSpec(memory_space=pltpu.SMEM),
            pl.BlockSpec((None, tq, d_attn), cur),
            pl.BlockSpec((None, blk, kv_cols), prev),
            pl.BlockSpec((None, tq, kv_cols), cur),
            pl.BlockSpec((None, blk, kv_cols), prev),
            pl.BlockSpec((None, tq, kv_cols), cur),
            pl.BlockSpec(rep.shape, lambda bi, i: (0, 0, 0)),
            pl.BlockSpec(mask.shape, lambda bi, i: (0, 0, 0)),
        ],
        out_specs=pl.BlockSpec((None, tq, d_attn), cur),
        out_shape=jax.ShapeDtypeStruct((b, l, d_attn), jnp.bfloat16),
        compiler_params=_cparams("arbitrary", "arbitrary"),
        name="swa",
    )(sinks, q3, k3, k3, v3, v3, rep, mask)
    return out.reshape(b * l, d_attn)


def _ssm_kernel(u_ref, bmat_ref, cmat_ref, lam_ref, pw_ref, lamseg_ref, d_ref, y_ref, bu_ref, carry_ref):
    i = pl.program_id(1)
    n_slab = u_ref.shape[0]
    tm = u_ref.shape[1]
    seg = tm // SUBLANES
    ns = SSM_CHUNK_STATES
    npair = 2

    @pl.when(i == 0)
    def _():
        carry_ref[...] = jnp.zeros_like(carry_ref)

    sub = lax.broadcasted_iota(jnp.int32, (SUBLANES, ns), 0)
    for s0 in range(0, n_slab, npair):
        lam = []
        for q in range(npair):
            s = s0 + q
            bu_ref[q] = jnp.dot(u_ref[s].astype(jnp.bfloat16), bmat_ref[s], preferred_element_type=jnp.float32)
            lam.append((jnp.broadcast_to(lam_ref[s, 0:1, :], (SUBLANES, ns)),
                        jnp.broadcast_to(lam_ref[s, 1:2, :], (SUBLANES, ns))))

        def step(r, st):
            rows = pl.ds(r * SUBLANES, SUBLANES)
            out = []
            for q in range(npair):
                lr, li = lam[q]
                sr, si = st[q]
                nr = lr * sr - li * si + bu_ref[q, rows, 0:ns]
                ni = lr * si + li * sr + bu_ref[q, rows, ns:2 * ns]
                bu_ref[q, rows, 0:ns] = nr
                bu_ref[q, rows, ns:2 * ns] = ni
                out.append((nr, ni))
            return tuple(out)

        zero = jnp.zeros((SUBLANES, ns), jnp.float32)
        ends = ((zero, zero),) * npair
        for r in range(seg):
            ends = step(r, ends)

        for q in range(npair):
            s = s0 + q
            er, ei = ends[q]
            ar = lamseg_ref[s, 0:1, :]
            ai = lamseg_ref[s, 1:2, :]
            cr = carry_ref[s, 0:1, :]
            ci = carry_ref[s, 1:2, :]
            car = jnp.zeros((SUBLANES, ns), jnp.float32)
            cai = jnp.zeros((SUBLANES, ns), jnp.float32)
            for j in range(SUBLANES):
                car = jnp.where(sub == j, jnp.broadcast_to(cr, (SUBLANES, ns)), car)
                cai = jnp.where(sub == j, jnp.broadcast_to(ci, (SUBLANES, ns)), cai)
                ejr = jnp.sum(jnp.where(sub == j, er, 0.0), axis=0, keepdims=True)
                eji = jnp.sum(jnp.where(sub == j, ei, 0.0), axis=0, keepdims=True)
                cr, ci = ar * cr - ai * ci + ejr, ar * ci + ai * cr + eji
            carry_ref[s, 0:1, :] = cr
            carry_ref[s, 1:2, :] = ci

            ctr = jnp.broadcast_to(car[None], (seg, SUBLANES, ns)).reshape(tm, ns)
            cti = jnp.broadcast_to(cai[None], (seg, SUBLANES, ns)).reshape(tm, ns)
            pr = pw_ref[s, 0]
            pi = pw_ref[s, 1]
            st_r = bu_ref[q, :, 0:ns] + pr * ctr - pi * cti
            st_i = bu_ref[q, :, ns:2 * ns] + pr * cti + pi * ctr
            st = jnp.concatenate([st_r, st_i], axis=1).astype(jnp.bfloat16)
            y_ref[s] = jnp.dot(st, cmat_ref[s], preferred_element_type=jnp.float32) + d_ref[s] * u_ref[s]


def _ssm_tables(a_re, a_im, b_re, b_im, c_re, c_im, d_skip, log_dt, seg):
    f32 = jnp.float32
    g, p = a_re.shape
    c = b_re.shape[-1]
    ng = SSM_CHUNK_GROUPS
    n_slab = g // ng
    lam = lax.complex(a_re.astype(f32), a_im.astype(f32))
    dt = jnp.exp(log_dt.astype(f32))[:, None]
    lam_bar = jnp.exp(lam * dt)
    b_bar = ((lam_bar - 1.0) / lam)[:, :, None] * lax.complex(b_re.astype(f32), b_im.astype(f32))
    c_mat = lax.complex(c_re.astype(f32), c_im.astype(f32))
    eye = jnp.eye(ng, dtype=f32)

    def bdiag_b(m):
        m = m.reshape(n_slab, ng, p, c)
        return jnp.einsum('ab,kbpc->kacbp', eye, m).reshape(n_slab, ng * c, ng * p)

    def bdiag_c(m):
        m = m.reshape(n_slab, ng, c, p)
        return jnp.einsum('ab,kbcp->kbpac', eye, m).reshape(n_slab, ng * p, ng * c)

    bmat = jnp.concatenate([bdiag_b(jnp.real(b_bar)), bdiag_b(jnp.imag(b_bar))], axis=2).astype(jnp.bfloat16)
    cmat = jnp.concatenate([bdiag_c(jnp.real(c_mat)), -bdiag_c(jnp.imag(c_mat))], axis=1).astype(jnp.bfloat16)

    def slab_rows(z):
        z = z.reshape(n_slab, 1, ng * p)
        return jnp.concatenate([jnp.real(z), jnp.imag(z)], axis=1)

    lam_t = slab_rows(lam_bar)
    steps = jnp.arange(1, seg + 1, dtype=f32)
    pw = jnp.exp((lam * dt)[None] * steps[:, None, None])
    pw = pw.reshape(seg, n_slab, ng * p).transpose(1, 0, 2)
    pw = jnp.repeat(pw, SUBLANES, axis=1)
    pw_t = jnp.stack([jnp.real(pw), jnp.imag(pw)], axis=1)
    lamseg_t = slab_rows(jnp.exp(lam * dt * float(seg)))
    d_t = d_skip.astype(f32).reshape(n_slab, 1, ng * c)
    return bmat, cmat, lam_t, pw_t, lamseg_t, d_t


def _ssm(u4, tables, b, l):
    bmat, cmat, lam_t, pw_t, lamseg_t, d_t = tables
    n_slab, t, _ = u4.shape
    tm = TM_SSM
    nt = l // tm
    ns = SSM_CHUNK_STATES
    const = lambda nd: (lambda bi, i: (0,) * nd)
    return pl.pallas_call(
        _ssm_kernel,
        grid=(b, nt),
        in_specs=[
            pl.BlockSpec((n_slab, tm, LANES), lambda bi, i: (0, bi * nt + i, 0)),
            pl.BlockSpec(bmat.shape, const(3)),
            pl.BlockSpec(cmat.shape, const(3)),
            pl.BlockSpec(lam_t.shape, const(3)),
            pl.BlockSpec(pw_t.shape, const(4)),
            pl.BlockSpec(lamseg_t.shape, const(3)),
            pl.BlockSpec(d_t.shape, const(3)),
        ],
        out_specs=pl.BlockSpec((n_slab, tm, LANES), lambda bi, i: (0, bi * nt + i, 0)),
        out_shape=jax.ShapeDtypeStruct((n_slab, t, LANES), jnp.float32),
        scratch_shapes=[
            pltpu.VMEM((2, tm, 2 * ns), jnp.float32),
            pltpu.VMEM((n_slab, 2, ns), jnp.float32),
        ],
        compiler_params=_cparams("arbitrary", "arbitrary"),
        name="s5_scan",
    )(u4, bmat, cmat, lam_t, pw_t, lamseg_t, d_t)


def _merge_kernel(x_ref, attn_ref, y_ref, gate_ref, wglu32_ref, bglu_ref, wa32_ref, ws32_ref, wo32_ref,
                  nm_ref, wr_ref, br_ref, tri_ref, x1_ref, h2_ref, ids_ref, rw_ref, rank_ref, cnt_ref,
                  lg_ref, carry_ref, wglu_ref, wa_ref, ws_ref, wo_ref):
    i = pl.program_id(0)
    d = x_ref.shape[1]
    tm = x_ref.shape[0]

    @pl.when(i == 0)
    def _():
        lg_ref[...] = jnp.zeros_like(lg_ref)
        carry_ref[...] = jnp.zeros_like(carry_ref)
        for w32, wb in ((wglu32_ref, wglu_ref), (wa32_ref, wa_ref), (ws32_ref, ws_ref), (wo32_ref, wo_ref)):
            wb[...] = w32[...].astype(jnp.bfloat16)

    logits = lg_ref[...]

    seg = tm // SUBLANES
    y = jnp.concatenate(
        [jnp.concatenate([y_ref[s, pl.ds(j, seg, stride=SUBLANES), :] for j in range(SUBLANES)], axis=0)
         for s in range(y_ref.shape[0])], axis=1)
    z = jax.nn.gelu(y)
    zg = jnp.dot(z.astype(jnp.bfloat16), wglu_ref[...], preferred_element_type=jnp.float32) + bglu_ref[...]
    z = z * jax.nn.sigmoid(zg)
    a = jnp.dot(attn_ref[...], wa_ref[...], preferred_element_type=jnp.float32)
    sb = jnp.dot(z.astype(jnp.bfloat16), ws_ref[...], preferred_element_type=jnp.float32)
    merged = gate_ref[:, 0:d].astype(jnp.float32) * a + gate_ref[:, d:2 * d].astype(jnp.float32) * sb
    x1 = x_ref[...] + jnp.dot(merged.astype(jnp.bfloat16), wo_ref[...], preferred_element_type=jnp.float32)
    x1_ref[...] = x1
    inv = lax.rsqrt(jnp.mean(x1 * x1, axis=-1, keepdims=True) + EPS)
    h2 = x1 * inv * nm_ref[...]
    h_hi = h2.astype(jnp.bfloat16)
    h_pk = _pack_bf16_pairs(h_hi.astype(jnp.float32))
    _store_rows(h2_ref, 0, h_pk)

    h_lo = (h2 - h_hi.astype(jnp.float32)).astype(jnp.bfloat16)
    hh = jnp.dot(h_hi, wr_ref[...], preferred_element_type=jnp.float32)
    lh = jnp.dot(h_lo, wr_ref[:, 0:LANES], preferred_element_type=jnp.float32)
    lg_ref[...] = hh[:, 0:LANES] + hh[:, LANES:2 * LANES] + lh + br_ref[...]

    lane = lax.broadcasted_iota(jnp.int32, (tm, LANES), 1)
    ninf = -jnp.inf
    gl = jnp.where((lane >= N_EXPERTS) & (lane < N_EXPERTS + N_EXPERT_GROUPS), logits, ninf)
    gmax = jnp.max(gl, axis=-1, keepdims=True)
    gidx = jnp.min(jnp.where(gl == gmax, lane - N_EXPERTS, LANES), axis=-1, keepdims=True)
    group_p = 1.0 / jnp.sum(jnp.exp(gl - gmax), axis=-1, keepdims=True)
    el = jnp.where((lane < N_EXPERTS) & ((lane >> 3) == gidx), logits, ninf)
    m1 = jnp.max(el, axis=-1, keepdims=True)
    i1 = jnp.min(jnp.where(el == m1, lane, LANES), axis=-1, keepdims=True)
    el2 = jnp.where(lane == i1, ninf, el)
    m2 = jnp.max(el2, axis=-1, keepdims=True)
    i2 = jnp.min(jnp.where(el2 == m2, lane, LANES), axis=-1, keepdims=True)
    e2 = jnp.exp(m2 - m1)
    w1 = group_p / (1.0 + e2)
    w2 = group_p * e2 / (1.0 + e2)
    ids_ref[...] = jnp.transpose(jnp.where(lane == 0, i1, jnp.where(lane == 1, i2, 0)))[0:SUBLANES, :]
    rw_ref[...] = jnp.where(lane == 0, w1, jnp.where(lane == 1, w2, 0.0))

    oh0 = lane == i1
    oh1 = lane == i2
    live = jnp.where(i > 0, 1.0, 0.0)
    oh = (oh0.astype(jnp.float32) + oh1.astype(jnp.float32)) * live
    cum = jnp.dot(tri_ref[...], oh.astype(jnp.bfloat16), preferred_element_type=jnp.float32) + carry_ref[...]
    r0 = jnp.sum(jnp.where(oh0, cum, 0.0), axis=-1, keepdims=True)
    r1 = jnp.sum(jnp.where(oh1, cum, 0.0), axis=-1, keepdims=True)
    rank_ref[...] = jnp.transpose(
        jnp.where(lane == 0, r0, jnp.where(lane == 1, r1, 0.0)).astype(jnp.int32))[0:SUBLANES, :]
    carry_ref[...] = carry_ref[...] + jnp.sum(oh, axis=0, keepdims=True)
    cnt_ref[...] = carry_ref[...].astype(jnp.int32)


def _merge(x2, attn, y4, gates, w_glu, b_glu, w_a, w_s, w_o, norm_moe, w_router, b_router):
    t, d = x2.shape
    tm = TM_MERGE
    nt = t // tm
    n_slab = y4.shape[0]
    tri = (jnp.arange(tm)[None, :] < jnp.arange(tm)[:, None]).astype(jnp.bfloat16)
    full = lambda a: pl.BlockSpec(a.shape, lambda i: (0,) * a.ndim, pipeline_mode=pl.Buffered(1))
    cur = lambda i: (jnp.minimum(i, nt - 1), 0)
    prv = lambda i: (jnp.maximum(i - 1, 0), 0)
    row = lambda c, m=cur: pl.BlockSpec((tm, c), m)
    slots = pl.BlockSpec((SUBLANES, tm), lambda i: (0, jnp.maximum(i - 1, 0)))
    return pl.pallas_call(
        _merge_kernel,
        grid=(nt + 1,),
        in_specs=[
            row(d), row(attn.shape[1]),
            pl.BlockSpec((n_slab, tm, LANES), lambda i: (0, jnp.minimum(i, nt - 1), 0)),
            row(gates.shape[1]),
            full(w_glu), full(b_glu), full(w_a), full(w_s), full(w_o), full(norm_moe),
            full(w_router), full(b_router), full(tri),
        ],
        out_specs=[row(d), pl.BlockSpec((tm * ROW_SUB, LANES), cur), slots, row(LANES, prv),
                   slots, pl.BlockSpec((1, LANES), lambda i: (0, 0))],
        out_shape=[
            jax.ShapeDtypeStruct((t, d), jnp.float32),
            jax.ShapeDtypeStruct((t * ROW_SUB, LANES), jnp.uint32),
            jax.ShapeDtypeStruct((SUBLANES, t), jnp.int32),
            jax.ShapeDtypeStruct((t, LANES), jnp.float32),
            jax.ShapeDtypeStruct((SUBLANES, t), jnp.int32),
            jax.ShapeDtypeStruct((1, LANES), jnp.int32),
        ],
        scratch_shapes=[pltpu.VMEM((tm, LANES), jnp.float32), pltpu.VMEM((1, LANES), jnp.float32)]
        + [pltpu.VMEM(w.shape, jnp.bfloat16) for w in (w_glu, w_a, w_s, w_o)],
        compiler_params=_cparams("arbitrary"),
        name="merge_router",
    )(x2, attn, y4, gates, w_glu, b_glu, w_a, w_s, w_o, norm_moe, w_router, b_router, tri)


def _inverse_kernel(pos_ref, src_ref):
    n_slots = pos_ref.shape[0]
    t = n_slots // TOP_K

    def body(tok, c):
        for k in range(TOP_K):
            src_ref[pos_ref[k * t + tok]] = tok * ROW_SUB
        return c

    lax.fori_loop(0, t, body, 0, unroll=8)


def _inverse(pos1d):
    n_slots = pos1d.shape[0]
    smem = pl.BlockSpec(memory_space=pltpu.SMEM)
    return pl.pallas_call(
        _inverse_kernel,
        in_specs=[smem],
        out_specs=smem,
        out_shape=jax.ShapeDtypeStruct((n_slots,), jnp.int32),
        name="route_inverse",
    )(pos1d)


def _expert_kernel(ta_ref, tn_ref, tnew_ref, sexp_ref, slo_ref, shi_ref, meta_ref, src_ref,
                   h_ref, wg_hbm, wu_hbm, wd_hbm, y_ref,
                   xbuf_ref, wgs_ref, wus_ref, wds_ref, wgb_ref, wub_ref, wdb_ref, gsem, wsem):
    j = pl.program_id(0)
    n_tiles = pl.num_programs(0)
    last = n_tiles - 1
    tm = TM_EXPERT
    rs = ROW_SUB
    nb = ROW_BUFS
    n_used = meta_ref[0]
    slot = j % nb
    a = ta_ref[j]

    def gather(tile, sl, start):
        _row_gather(h_ref, src_ref, tile * tm, xbuf_ref, sl * tm, tm, gsem.at[sl], start)

    def weights(m, start):
        sl = m % 2
        e = sexp_ref[m]
        for hbm, stage in ((wg_hbm, wgs_ref), (wu_hbm, wus_ref), (wd_hbm, wds_ref)):
            cp = pltpu.make_async_copy(hbm.at[e], stage.at[sl], wsem.at[sl])
            cp.start() if start else cp.wait()

    def new_expert(m):
        weights(m, False)
        sl = m % 2
        wgb_ref[...] = wgs_ref[sl].astype(jnp.bfloat16)
        wub_ref[...] = wus_ref[sl].astype(jnp.bfloat16)
        wdb_ref[...] = wds_ref[sl].astype(jnp.bfloat16)

        @pl.when(m + 1 < n_used)
        def _():
            weights(m + 1, True)

    def expert_out(xp):
        x = _unpack_bf16_pairs(xp).astype(jnp.bfloat16)
        hg = jnp.dot(x, wgb_ref[...], preferred_element_type=jnp.float32)
        hu = jnp.dot(x, wub_ref[...], preferred_element_type=jnp.float32)
        act = (jax.nn.silu(hg) * hu).astype(jnp.bfloat16)
        y = jnp.dot(act, wdb_ref[...], preferred_element_type=jnp.float32)
        return _pack_bf16_pairs(y.astype(jnp.bfloat16).astype(jnp.float32))

    @pl.when(j == 0)
    def _():
        weights(0, True)
        for b in range(nb - 1):
            gather(b, b, True)

    @pl.when(tnew_ref[j] == 1)
    def _():
        new_expert(a)

    gather(j, slot, False)
    xp = _load_rows(xbuf_ref, slot * tm, tm)
    gather(jnp.minimum(j + nb - 1, last), (j + nb - 1) % nb, True)
    _store_rows(y_ref, 0, expert_out(xp))

    def extra(i, carry):
        m = a + i
        new_expert(m)
        yp = expert_out(_load_rows(xbuf_ref, slot * tm, tm))
        row = lax.broadcasted_iota(jnp.int32, (tm, LANES), 0) + j * tm
        mine = (row >= slo_ref[m]) & (row < shi_ref[m])
        _store_rows(y_ref, 0, yp, mine)
        return carry

    lax.fori_loop(1, tn_ref[j], extra, 0)

    @pl.when(j == last)
    def _():
        for b in range(1, nb):
            gather(j, (j + b) % nb, False)


def _experts(meta, src_tok, h_fat, w_gate, w_up, w_down):
    n_slots = src_tok.shape[0]
    ne, d, dff = w_gate.shape
    tm = TM_EXPERT
    rs = ROW_SUB
    assert n_slots // tm >= ROW_BUFS
    any_spec = pl.BlockSpec(memory_space=pl.ANY)
    grid_spec = pltpu.PrefetchScalarGridSpec(
        num_scalar_prefetch=8,
        grid=(n_slots // tm,),
        in_specs=[any_spec, any_spec, any_spec, any_spec],
        out_specs=pl.BlockSpec((tm * rs, LANES), lambda j, *_: (j, 0)),
        scratch_shapes=[
            pltpu.VMEM((ROW_BUFS * tm * rs, LANES), jnp.uint32),
            pltpu.VMEM((2, d, dff), jnp.float32),
            pltpu.VMEM((2, d, dff), jnp.float32),
            pltpu.VMEM((2, dff, d), jnp.float32),
            pltpu.VMEM((d, dff), jnp.bfloat16),
            pltpu.VMEM((d, dff), jnp.bfloat16),
            pltpu.VMEM((dff, d), jnp.bfloat16),
            pltpu.SemaphoreType.DMA((ROW_BUFS,)),
            pltpu.SemaphoreType.DMA((2,)),
        ],
    )
    return pl.pallas_call(
        _expert_kernel,
        grid_spec=grid_spec,
        out_shape=jax.ShapeDtypeStruct((n_slots * rs, LANES), jnp.uint32),
        compiler_params=_cparams("arbitrary"),
        name="moe_experts",
    )(*meta, src_tok, h_fat, w_gate, w_up, w_down)


def _combine_kernel(pos_ref, x1_ref, rw_ref, nf_ref, ys_ref, o_ref, buf_ref, sem):
    i = pl.program_id(0)
    n = pl.num_programs(0)
    last = n - 1
    tm = x1_ref.shape[0]
    t = n * tm
    nb = ROW_BUFS
    slot = i % nb

    def gather(tile, sl, start):
        for k in range(TOP_K):
            _row_gather(ys_ref, pos_ref, k * t + tile * tm, buf_ref, (sl * TOP_K + k) * tm, tm, sem.at[sl], start)

    @pl.when(i == 0)
    def _():
        for b in range(nb - 1):
            gather(b, b, True)

    gather(i, slot, False)
    y0 = _unpack_bf16_pairs(_load_rows(buf_ref, (slot * TOP_K) * tm, tm))
    y1 = _unpack_bf16_pairs(_load_rows(buf_ref, (slot * TOP_K + 1) * tm, tm))
    gather(jnp.minimum(i + nb - 1, last), (i + nb - 1) % nb, True)
    rw = rw_ref[...]
    x = x1_ref[...] + rw[:, 0:1] * y0 + rw[:, 1:2] * y1
    inv = lax.rsqrt(jnp.mean(x * x, axis=-1, keepdims=True) + EPS)
    o_ref[...] = x * inv * nf_ref[...]

    @pl.when(i == last)
    def _():
        for b in range(1, nb):
            gather(i, (i + b) % nb, False)


def _combine(pos_rows, x1, rw, norm_final, ys):
    t, d = x1.shape
    tm = TM_MOVE
    assert t // tm >= ROW_BUFS
    return pl.pallas_call(
        _combine_kernel,
        grid=(t // tm,),
        in_specs=[
            pl.BlockSpec(memory_space=pltpu.SMEM),
            pl.BlockSpec((tm, d), lambda i: (i, 0)),
            pl.BlockSpec((tm, LANES), lambda i: (i, 0)),
            pl.BlockSpec((1, d), lambda i: (0, 0)),
            pl.BlockSpec(memory_space=pl.ANY),
        ],
        out_specs=pl.BlockSpec((tm, d), lambda i: (i, 0)),
        out_shape=jax.ShapeDtypeStruct((t, d), jnp.float32),
        scratch_shapes=[pltpu.VMEM((ROW_BUFS * TOP_K * tm * ROW_SUB, LANES), jnp.uint32),
                        pltpu.SemaphoreType.DMA((ROW_BUFS,))],
        compiler_params=_cparams("arbitrary"),
        name="moe_combine",
    )(pos_rows, x1, rw, norm_final, ys)


def _expert_meta(counts, n_slots):
    tm = TM_EXPERT
    n_tiles = n_slots // tm
    i32 = jnp.int32
    ends = jnp.cumsum(counts).astype(i32)
    starts = ends - counts
    used = counts > 0
    n_used = jnp.sum(used).astype(i32)
    seq_of = jnp.cumsum(used).astype(i32) - 1
    m_idx = jnp.arange(N_EXPERTS, dtype=i32)
    pick = used[None, :] & (seq_of[None, :] == m_idx[:, None])
    s_exp = jnp.sum(jnp.where(pick, m_idx[None, :], 0), axis=1).astype(i32)
    s_lo = jnp.sum(jnp.where(pick, starts[None, :], 0), axis=1).astype(i32)
    s_hi = jnp.sum(jnp.where(pick, ends[None, :], 0), axis=1).astype(i32)
    row0 = jnp.arange(n_tiles, dtype=i32)[:, None] * tm
    t_a = jnp.sum(used[None, :] & (ends[None, :] <= row0), axis=1).astype(i32)
    t_b = jnp.sum(used[None, :] & (starts[None, :] < row0 + tm), axis=1).astype(i32) - 1
    t_new = jnp.any(used[None, :] & (starts[None, :] == row0), axis=1).astype(i32)
    return t_a, (t_b - t_a + 1).astype(i32), t_new, s_exp, s_lo, s_hi, n_used.reshape(1)


def _moe(x1, h_fat, ids, rw, rank, cnt, w_gate, w_up, w_down, norm_final):
    t, d = x1.shape
    counts = cnt[0, :N_EXPERTS]
    offs = (jnp.cumsum(counts) - counts).astype(jnp.int32)
    onehot = ids[:TOP_K, :, None] == jnp.arange(N_EXPERTS, dtype=jnp.int32)
    pos1d = (jnp.sum(jnp.where(onehot, offs, 0), axis=-1) + rank[:TOP_K]).reshape(-1)
    src_tok = _inverse(pos1d)
    meta = _expert_meta(counts, t * TOP_K)
    ys = _experts(meta, src_tok, h_fat, w_gate, w_up, w_down)
    return _combine(pos1d * ROW_SUB, x1, rw, norm_final, ys)


def kernel(x, norm_mix, w_in, b_gate, attn_sinks, ssm_a_re, ssm_a_im, ssm_b_re, ssm_b_im, ssm_c_re, ssm_c_im, ssm_d, ssm_log_dt, w_glu, b_glu, w_attn_branch, w_ssm_branch, w_out, norm_moe, w_router_group, b_router_group, w_router_expert, b_router_expert, w_expert_gate, w_expert_up, w_expert_down, norm_final):
    b, l, d = x.shape
    depth = w_in.shape[0]
    assert depth == 1, "the final norm is fused into the last layer's combine kernel"
    d_attn = N_HEADS * HEAD_DIM
    kv_cols = N_KV_HEADS * HEAD_DIM
    d_ssm = ssm_d.shape[-1]
    bf16 = jnp.bfloat16
    x2 = x.reshape(b * l, d)
    assert TM_PROJ == TM_SSM == TM_MERGE and l % TM_SSM == 0
    i = 0
    q, k, v, u4, gates = _in_proj(x2, norm_mix[i][None], w_in[i], b_gate[i][None],
                                  d_attn, kv_cols, d_ssm)
    attn = _attention(q, k, v, attn_sinks[i], b, l)
    tables = _ssm_tables(ssm_a_re[i], ssm_a_im[i], ssm_b_re[i], ssm_b_im[i], ssm_c_re[i], ssm_c_im[i],
                         ssm_d[i], ssm_log_dt[i], SSM_SEG)
    y4 = _ssm(u4, tables, b, l)
    pad = LANES - N_EXPERTS - N_EXPERT_GROUPS
    w_router = jnp.concatenate([w_router_expert[i], w_router_group[i], jnp.zeros((d, pad), jnp.float32)], axis=1)
    w_r_hi = w_router.astype(bf16)
    w_router = jnp.concatenate([w_r_hi, (w_router - w_r_hi.astype(jnp.float32)).astype(bf16)], axis=1)
    b_router = jnp.concatenate([b_router_expert[i], b_router_group[i], jnp.zeros((pad,), jnp.float32)])[None]
    x1, h_fat, ids, rw, rank, cnt = _merge(x2, attn, y4, gates, w_glu[i], b_glu[i][None],
                                w_attn_branch[i], w_ssm_branch[i],
                                w_out[i], norm_moe[i][None], w_router, b_router)
    out = _moe(x1, h_fat, ids, rw, rank, cnt, w_expert_gate[i], w_expert_up[i], w_expert_down[i], norm_final[None])
    return out.reshape(b, l, d)
```

```python
import functools
import math

import jax
import jax.numpy as jnp
from jax import lax
from jax.experimental import pallas as pl
from jax.experimental.pallas import tpu as pltpu

EPS = 1e-6
HEAD_DIM = 64
N_HEADS = 8
N_KV_HEADS = 2
Q_PER_KV = N_HEADS // N_KV_HEADS
ATTN_BLOCK = 128
ATTN_QB = 8
SSM_GROUP = 16
SSM_STATE = 64
N_EXPERT_GROUPS = 4
EXPERTS_PER_GROUP = 8
N_EXPERTS = N_EXPERT_GROUPS * EXPERTS_PER_GROUP
TOP_K = 2

LANES = 128
SUBLANES = 8
SSM_CHUNK_GROUPS = LANES // SSM_GROUP
SSM_CHUNK_STATES = SSM_CHUNK_GROUPS * SSM_STATE

TM_PROJ = 512
TM_SSM = 512
SSM_SEG = TM_SSM // SUBLANES
TM_MERGE = 512
TM_MOVE = 512
TM_EXPERT = 256
ROW_SUB = 4
ROW_BUFS = 4
VMEM_LIMIT = 56 * 1024 * 1024


def _cparams(*sem):
    return pltpu.CompilerParams(dimension_semantics=sem, vmem_limit_bytes=VMEM_LIMIT)


def _log2(n):
    assert n > 0 and n & (n - 1) == 0, n
    return n.bit_length() - 1


def _pack_bf16_pairs(x):
    half = x.shape[1] // 2
    bits = lax.bitcast_convert_type(x, jnp.uint32)
    return (bits[:, :half] & jnp.uint32(0xFFFF0000)) | (bits[:, half:] >> 16)


def _unpack_bf16_pairs(p):
    hi = lax.bitcast_convert_type(p & jnp.uint32(0xFFFF0000), jnp.float32)
    lo = lax.bitcast_convert_type(p << 16, jnp.float32)
    return jnp.concatenate([hi, lo], axis=1)


def _load_rows(ref, first, n):
    return jnp.concatenate(
        [ref[pl.ds(first * ROW_SUB + c, n, stride=ROW_SUB), :] for c in range(ROW_SUB)], axis=1)


def _store_rows(ref, first, val, mask=None):
    n = val.shape[0]
    for c in range(ROW_SUB):
        idx = pl.ds(first * ROW_SUB + c, n, stride=ROW_SUB)
        v = val[:, c * LANES:(c + 1) * LANES]
        ref[idx, :] = v if mask is None else jnp.where(mask, v, ref[idx, :])


def _row_gather(src_hbm, idx_ref, idx0, dst_ref, dst0, n, sem, start):
    rs = ROW_SUB
    if start:
        for r in range(n):
            pltpu.make_async_copy(src_hbm.at[pl.ds(pl.multiple_of(idx_ref[idx0 + r], rs), rs)],
                                  dst_ref.at[pl.ds(pl.multiple_of((dst0 + r) * rs, rs), rs)],
                                  sem).start(priority=r % 2)
    else:
        pltpu.make_async_copy(src_hbm.at[pl.ds(0, n * rs)],
                              dst_ref.at[pl.ds(pl.multiple_of(dst0 * rs, rs), n * rs)], sem).wait()


def _proj_kernel(x_ref, g_ref, w32_ref, bg_ref, q_ref, k_ref, v_ref, u_ref, gate_ref, w_ref, *, cols):
    q_c, kv_c, d_ssm = cols

    @pl.when(pl.program_id(0) == 0)
    def _():
        w_ref[...] = w32_ref[...].astype(jnp.bfloat16)

    xf = x_ref[...]
    inv = lax.rsqrt(jnp.mean(xf * xf, axis=-1, keepdims=True) + EPS)
    h = (xf * inv * g_ref[...]).astype(jnp.bfloat16)
    o_gate = q_c + 2 * kv_c + d_ssm
    gl = jnp.dot(h, w_ref[:, o_gate:], preferred_element_type=jnp.float32) + bg_ref[...]
    gate_ref[...] = jax.nn.sigmoid(gl).astype(gate_ref.dtype)
    o = 0
    q_ref[...] = (jnp.dot(h, w_ref[:, o:o + q_c], preferred_element_type=jnp.float32)
                  * (1.0 / math.sqrt(HEAD_DIM))).astype(q_ref.dtype)
    o += q_c
    k_ref[...] = jnp.dot(h, w_ref[:, o:o + kv_c], preferred_element_type=jnp.float32).astype(k_ref.dtype)
    o += kv_c
    v_ref[...] = jnp.dot(h, w_ref[:, o:o + kv_c], preferred_element_type=jnp.float32).astype(v_ref.dtype)
    o += kv_c
    seg = x_ref.shape[0] // SUBLANES
    for s in range(0, d_ssm // LANES, 2):
        uu = jnp.dot(h, w_ref[:, o:o + 2 * LANES], preferred_element_type=jnp.float32)
        for half in range(2):
            for j in range(SUBLANES):
                u_ref[s + half, pl.ds(j, seg, stride=SUBLANES), :] = (
                    uu[j * seg:(j + 1) * seg, half * LANES:(half + 1) * LANES])
        o += 2 * LANES


def _in_proj(x2, norm_w, w_in, b_gate, d_attn, kv_cols, d_ssm):
    t, d = x2.shape
    gate_cols = b_gate.shape[-1]
    n_slab = d_ssm // LANES
    tm = TM_PROJ
    kern = functools.partial(_proj_kernel, cols=(d_attn, kv_cols, d_ssm))
    return pl.pallas_call(
        kern,
        grid=(t // tm,),
        in_specs=[
            pl.BlockSpec((tm, d), lambda i: (i, 0)),
            pl.BlockSpec((1, d), lambda i: (0, 0)),
            pl.BlockSpec(w_in.shape, lambda i: (0, 0), pipeline_mode=pl.Buffered(1)),
            pl.BlockSpec((1, gate_cols), lambda i: (0, 0)),
        ],
        out_specs=[
            pl.BlockSpec((tm, d_attn), lambda i: (i, 0)),
            pl.BlockSpec((tm, kv_cols), lambda i: (i, 0)),
            pl.BlockSpec((tm, kv_cols), lambda i: (i, 0)),
            pl.BlockSpec((n_slab, tm, LANES), lambda i: (0, i, 0)),
            pl.BlockSpec((tm, gate_cols), lambda i: (i, 0)),
        ],
        out_shape=[
            jax.ShapeDtypeStruct((t, d_attn), jnp.bfloat16),
            jax.ShapeDtypeStruct((t, kv_cols), jnp.bfloat16),
            jax.ShapeDtypeStruct((t, kv_cols), jnp.bfloat16),
            jax.ShapeDtypeStruct((n_slab, t, LANES), jnp.float32),
            jax.ShapeDtypeStruct((t, gate_cols), jnp.bfloat16),
        ],
        scratch_shapes=[pltpu.VMEM(w_in.shape, jnp.bfloat16)],
        compiler_params=_cparams("arbitrary"),
        name="in_proj",
    )(x2, norm_w, w_in, b_gate)


def _attn_kernel(sink_ref, q_ref, kp_ref, kc_ref, vp_ref, vc_ref, rep_ref, mask_ref, o_ref):
    i = pl.program_id(1)
    blk = ATTN_BLOCK
    hw = Q_PER_KV * HEAD_DIM
    rows = Q_PER_KV * blk
    head_of_row = lax.broadcasted_iota(jnp.int32, (rows, 1), 0) >> _log2(blk)
    lane_head_q = lax.broadcasted_iota(jnp.int32, (blk, hw), 1) >> _log2(HEAD_DIM)
    lane_head_v = lax.broadcasted_iota(jnp.int32, (2 * blk, hw), 1) >> _log2(HEAD_DIM)
    sinks = []
    for kh in range(N_KV_HEADS):
        sink = jnp.zeros((rows, 1), jnp.float32)
        for g in range(Q_PER_KV):
            sink = jnp.where(head_of_row == g, sink_ref[kh * Q_PER_KV + g], sink)
        sinks.append(sink)
    for qb in range(ATTN_QB):
        if qb == 0:
            k2 = jnp.concatenate([kp_ref[...], kc_ref[0:blk, :]], axis=0)
            v2 = jnp.concatenate([vp_ref[...], vc_ref[0:blk, :]], axis=0)
            bias = mask_ref[jnp.where(i == 0, 1, 0)]
        else:
            k2 = kc_ref[(qb - 1) * blk:(qb + 1) * blk, :]
            v2 = vc_ref[(qb - 1) * blk:(qb + 1) * blk, :]
            bias = mask_ref[0]
        for kh in range(N_KV_HEADS):
            rep = rep_ref[kh]
            k4 = jnp.dot(k2, rep, preferred_element_type=jnp.float32).astype(jnp.bfloat16)
            v4 = jnp.dot(v2, rep, preferred_element_type=jnp.float32).astype(jnp.bfloat16)
            qh = q_ref[qb * blk:(qb + 1) * blk, kh * hw:(kh + 1) * hw]
            qm = jnp.concatenate(
                [jnp.where(lane_head_q == g, qh, jnp.zeros_like(qh)) for g in range(Q_PER_KV)], axis=0)
            s = lax.dot_general(qm, k4, (((1,), (1,)), ((), ())), preferred_element_type=jnp.float32)
            s = s + bias
            sink = sinks[kh]
            m = jnp.maximum(jnp.max(s, axis=-1, keepdims=True), sink)
            p = jnp.exp(s - m)
            rinv = 1.0 / (jnp.sum(p, axis=-1, keepdims=True) + jnp.exp(sink - m))
            p = p.astype(jnp.bfloat16)
            p_cat = jnp.concatenate([p[g * blk:(g + 1) * blk, :] for g in range(Q_PER_KV)], axis=1)
            vm = jnp.concatenate(
                [jnp.where(lane_head_v == g, v4, jnp.zeros_like(v4)) for g in range(Q_PER_KV)], axis=0)
            o = jnp.dot(p_cat, vm, preferred_element_type=jnp.float32)
            scale = jnp.zeros((blk, hw), jnp.float32)
            for g in range(Q_PER_KV):
                scale = jnp.where(lane_head_q == g, rinv[g * blk:(g + 1) * blk, :], scale)
            o_ref[qb * blk:(qb + 1) * blk, kh * hw:(kh + 1) * hw] = (o * scale).astype(o_ref.dtype)


def _attention(q, k, v, sinks, b, l):
    d_attn = q.shape[-1]
    kv_cols = k.shape[-1]
    blk = ATTN_BLOCK
    tq = ATTN_QB * blk
    hw = Q_PER_KV * HEAD_DIM
    lane = jnp.arange(hw)[None, :]
    src = jnp.arange(kv_cols)[:, None]
    rep = jnp.stack([(src == kh * HEAD_DIM + (lane % HEAD_DIM)) for kh in range(N_KV_HEADS)]
                    ).astype(jnp.bfloat16)
    r = (jnp.arange(Q_PER_KV * blk) % blk)[:, None]
    c = jnp.arange(2 * blk)[None, :]
    band = (c > r) & (c <= r + blk)
    mask = jnp.where(jnp.stack([band, band & (c >= blk)]), 0.0, jnp.finfo(jnp.float32).min).astype(jnp.float32)
    q3 = q.reshape(b, l, d_attn)
    k3 = k.reshape(b, l, kv_cols)
    v3 = v.reshape(b, l, kv_cols)
    cur = lambda bi, i: (bi, i, 0)
    prev = lambda bi, i: (bi, jnp.maximum(ATTN_QB * i - 1, 0), 0)
    out = pl.pallas_call(
        _attn_kernel,
        grid=(b, l // tq),
        in_specs=[
            pl.BlockSpec(memory_space=pltpu.SMEM),
            pl.BlockSpec((None, tq, d_attn), cur),
            pl.BlockSpec((None, blk, kv_cols), prev),
            pl.BlockSpec((None, tq, kv_cols), cur),
            pl.BlockSpec((None, blk, kv_cols), prev),
            pl.BlockSpec((None, tq, kv_cols), cur),
            pl.BlockSpec(rep.shape, lambda bi, i: (0, 0, 0)),
            pl.BlockSpec(mask.shape, lambda bi, i: (0, 0, 0)),
        ],
        out_specs=pl.BlockSpec((None, tq, d_attn), cur),
        out_shape=jax.ShapeDtypeStruct((b, l, d_attn), jnp.bfloat16),
        compiler_params=_cparams("arbitrary", "arbitrary"),
        name="swa",
    )(sinks, q3, k3, k3, v3, v3, rep, mask)
    return out.reshape(b * l, d_attn)


def _ssm_kernel(u_ref, bmat_ref, cmat_ref, lam_ref, pw_ref, lamseg_ref, d_ref, y_ref, bu_ref, carry_ref):
    i = pl.program_id(1)
    n_slab = u_ref.shape[0]
    tm = u_ref.shape[1]
    seg = tm // SUBLANES
    ns = SSM_CHUNK_STATES
    npair = 2

    @pl.when(i == 0)
    def _():
        carry_ref[...] = jnp.zeros_like(carry_ref)

    sub = lax.broadcasted_iota(jnp.int32, (SUBLANES, ns), 0)
    for s0 in range(0, n_slab, npair):
        lam = []
        for q in range(npair):
            s = s0 + q
            bu_ref[q] = jnp.dot(u_ref[s].astype(jnp.bfloat16), bmat_ref[s], preferred_element_type=jnp.float32)
            lam.append((jnp.broadcast_to(lam_ref[s, 0:1, :], (SUBLANES, ns)),
                        jnp.broadcast_to(lam_ref[s, 1:2, :], (SUBLANES, ns))))

        def step(r, st):
            rows = pl.ds(r * SUBLANES, SUBLANES)
            out = []
            for q in range(npair):
                lr, li = lam[q]
                sr, si = st[q]
                nr = lr * sr - li * si + bu_ref[q, rows, 0:ns]
                ni = lr * si + li * sr + bu_ref[q, rows, ns:2 * ns]
                bu_ref[q, rows, 0:ns] = nr
                bu_ref[q, rows, ns:2 * ns] = ni
                out.append((nr, ni))
            return tuple(out)

        zero = jnp.zeros((SUBLANES, ns), jnp.float32)
        ends = ((zero, zero),) * npair
        for r in range(seg):
            ends = step(r, ends)

        for q in range(npair):
            s = s0 + q
            er, ei = ends[q]
            ar = lamseg_ref[s, 0:1, :]
            ai = lamseg_ref[s, 1:2, :]
            cr = carry_ref[s, 0:1, :]
            ci = carry_ref[s, 1:2, :]
            car = jnp.zeros((SUBLANES, ns), jnp.float32)
            cai = jnp.zeros((SUBLANES, ns), jnp.float32)
            for j in range(SUBLANES):
                car = jnp.where(sub == j, jnp.broadcast_to(cr, (SUBLANES, ns)), car)
                cai = jnp.where(sub == j, jnp.broadcast_to(ci, (SUBLANES, ns)), cai)
                ejr = jnp.sum(jnp.where(sub == j, er, 0.0), axis=0, keepdims=True)
                eji = jnp.sum(jnp.where(sub == j, ei, 0.0), axis=0, keepdims=True)
                cr, ci = ar * cr - ai * ci + ejr, ar * ci + ai * cr + eji
            carry_ref[s, 0:1, :] = cr
            carry_ref[s, 1:2, :] = ci

            ctr = jnp.broadcast_to(car[None], (seg, SUBLANES, ns)).reshape(tm, ns)
            cti = jnp.broadcast_to(cai[None], (seg, SUBLANES, ns)).reshape(tm, ns)
            pr = pw_ref[s, 0]
            pi = pw_ref[s, 1]
            st_r = bu_ref[q, :, 0:ns] + pr * ctr - pi * cti
            st_i = bu_ref[q, :, ns:2 * ns] + pr * cti + pi * ctr
            st = jnp.concatenate([st_r, st_i], axis=1).astype(jnp.bfloat16)
            y_ref[s] = jnp.dot(st, cmat_ref[s], preferred_element_type=jnp.float32) + d_ref[s] * u_ref[s]


def _ssm_tables(a_re, a_im, b_re, b_im, c_re, c_im, d_skip, log_dt, seg):
    f32 = jnp.float32
    g, p = a_re.shape
    c = b_re.shape[-1]
    ng = SSM_CHUNK_GROUPS
    n_slab = g // ng
    lam = lax.complex(a_re.astype(f32), a_im.astype(f32))
    dt = jnp.exp(log_dt.astype(f32))[:, None]
    lam_bar = jnp.exp(lam * dt)
    b_bar = ((lam_bar - 1.0) / lam)[:, :, None] * lax.complex(b_re.astype(f32), b_im.astype(f32))
    c_mat = lax.complex(c_re.astype(f32), c_im.astype(f32))
    eye = jnp.eye(ng, dtype=f32)

    def bdiag_b(m):
        m = m.reshape(n_slab, ng, p, c)
        return jnp.einsum('ab,kbpc->kacbp', eye, m).reshape(n_slab, ng * c, ng * p)

    def bdiag_c(m):
        m = m.reshape(n_slab, ng, c, p)
        return jnp.einsum('ab,kbcp->kbpac', eye, m).reshape(n_slab, ng * p, ng * c)

    bmat = jnp.concatenate([bdiag_b(jnp.real(b_bar)), bdiag_b(jnp.imag(b_bar))], axis=2).astype(jnp.bfloat16)
    cmat = jnp.concatenate([bdiag_c(jnp.real(c_mat)), -bdiag_c(jnp.imag(c_mat))], axis=1).astype(jnp.bfloat16)

    def slab_rows(z):
        z = z.reshape(n_slab, 1, ng * p)
        return jnp.concatenate([jnp.real(z), jnp.imag(z)], axis=1)

    lam_t = slab_rows(lam_bar)
    steps = jnp.arange(1, seg + 1, dtype=f32)
    pw = jnp.exp((lam * dt)[None] * steps[:, None, None])
    pw = pw.reshape(seg, n_slab, ng * p).transpose(1, 0, 2)
    pw = jnp.repeat(pw, SUBLANES, axis=1)
    pw_t = jnp.stack([jnp.real(pw), jnp.imag(pw)], axis=1)
    lamseg_t = slab_rows(jnp.exp(lam * dt * float(seg)))
    d_t = d_skip.astype(f32).reshape(n_slab, 1, ng * c)
    return bmat, cmat, lam_t, pw_t, lamseg_t, d_t


def _ssm(u4, tables, b, l):
    bmat, cmat, lam_t, pw_t, lamseg_t, d_t = tables
    n_slab, t, _ = u4.shape
    tm = TM_SSM
    nt = l // tm
    ns = SSM_CHUNK_STATES
    const = lambda nd: (lambda bi, i: (0,) * nd)
    return pl.pallas_call(
        _ssm_kernel,
        grid=(b, nt),
        in_specs=[
            pl.BlockSpec((n_slab, tm, LANES), lambda bi, i: (0, bi * nt + i, 0)),
            pl.BlockSpec(bmat.shape, const(3)),
            pl.BlockSpec(cmat.shape, const(3)),
            pl.BlockSpec(lam_t.shape, const(3)),
            pl.BlockSpec(pw_t.shape, const(4)),
            pl.BlockSpec(lamseg_t.shape, const(3)),
            pl.BlockSpec(d_t.shape, const(3)),
        ],
        out_specs=pl.BlockSpec((n_slab, tm, LANES), lambda bi, i: (0, bi * nt + i, 0)),
        out_shape=jax.ShapeDtypeStruct((n_slab, t, LANES), jnp.float32),
        scratch_shapes=[
            pltpu.VMEM((2, tm, 2 * ns), jnp.float32),
            pltpu.VMEM((n_slab, 2, ns), jnp.float32),
        ],
        compiler_params=_cparams("arbitrary", "arbitrary"),
        name="s5_scan",
    )(u4, bmat, cmat, lam_t, pw_t, lamseg_t, d_t)


def _merge_kernel(x_ref, attn_ref, y_ref, gate_ref, wglu32_ref, bglu_ref, wa32_ref, ws32_ref, wo32_ref,
                  nm_ref, wr_ref, br_ref, tri_ref, x1_ref, h2_ref, ids_ref, rw_ref, rank_ref, cnt_ref,
                  lg_ref, carry_ref, wglu_ref, wa_ref, ws_ref, wo_ref):
    i = pl.program_id(0)
    d = x_ref.shape[1]
    tm = x_ref.shape[0]

    @pl.when(i == 0)
    def _():
        lg_ref[...] = jnp.zeros_like(lg_ref)
        carry_ref[...] = jnp.zeros_like(carry_ref)
        for w32, wb in ((wglu32_ref, wglu_ref), (wa32_ref, wa_ref), (ws32_ref, ws_ref), (wo32_ref, wo_ref)):
            wb[...] = w32[...].astype(jnp.bfloat16)

    logits = lg_ref[...]

    seg = tm // SUBLANES
    y = jnp.concatenate(
        [jnp.concatenate([y_ref[s, pl.ds(j, seg, stride=SUBLANES), :] for j in range(SUBLANES)], axis=0)
         for s in range(y_ref.shape[0])], axis=1)
    z = jax.nn.gelu(y)
    zg = jnp.dot(z.astype(jnp.bfloat16), wglu_ref[...], preferred_element_type=jnp.float32) + bglu_ref[...]
    z = z * jax.nn.sigmoid(zg)
    a = jnp.dot(attn_ref[...], wa_ref[...], preferred_element_type=jnp.float32)
    sb = jnp.dot(z.astype(jnp.bfloat16), ws_ref[...], preferred_element_type=jnp.float32)
    merged = gate_ref[:, 0:d].astype(jnp.float32) * a + gate_ref[:, d:2 * d].astype(jnp.float32) * sb
    x1 = x_ref[...] + jnp.dot(merged.astype(jnp.bfloat16), wo_ref[...], preferred_element_type=jnp.float32)
    x1_ref[...] = x1
    inv = lax.rsqrt(jnp.mean(x1 * x1, axis=-1, keepdims=True) + EPS)
    h2 = x1 * inv * nm_ref[...]
    h_hi = h2.astype(jnp.bfloat16)
    h_pk = _pack_bf16_pairs(h_hi.astype(jnp.float32))
    _store_rows(h2_ref, 0, h_pk)

    h_lo = (h2 - h_hi.astype(jnp.float32)).astype(jnp.bfloat16)
    hh = jnp.dot(h_hi, wr_ref[...], preferred_element_type=jnp.float32)
    lh = jnp.dot(h_lo, wr_ref[:, 0:LANES], preferred_element_type=jnp.float32)
    lg_ref[...] = hh[:, 0:LANES] + hh[:, LANES:2 * LANES] + lh + br_ref[...]

    lane = lax.broadcasted_iota(jnp.int32, (tm, LANES), 1)
    ninf = -jnp.inf
    gl = jnp.where((lane >= N_EXPERTS) & (lane < N_EXPERTS + N_EXPERT_GROUPS), logits, ninf)
    gmax = jnp.max(gl, axis=-1, keepdims=True)
    gidx = jnp.min(jnp.where(gl == gmax, lane - N_EXPERTS, LANES), axis=-1, keepdims=True)
    group_p = 1.0 / jnp.sum(jnp.exp(gl - gmax), axis=-1, keepdims=True)
    el = jnp.where((lane < N_EXPERTS) & ((lane >> _log2(EXPERTS_PER_GROUP)) == gidx), logits, ninf)
    m1 = jnp.max(el, axis=-1, keepdims=True)
    i1 = jnp.min(jnp.where(el == m1, lane, LANES), axis=-1, keepdims=True)
    el2 = jnp.where(lane == i1, ninf, el)
    m2 = jnp.max(el2, axis=-1, keepdims=True)
    i2 = jnp.min(jnp.where(el2 == m2, lane, LANES), axis=-1, keepdims=True)
    e2 = jnp.exp(m2 - m1)
    w1 = group_p / (1.0 + e2)
    w2 = group_p * e2 / (1.0 + e2)
    ids_ref[...] = jnp.transpose(jnp.where(lane == 0, i1, jnp.where(lane == 1, i2, 0)))[0:SUBLANES, :]
    rw_ref[...] = jnp.where(lane == 0, w1, jnp.where(lane == 1, w2, 0.0))

    oh0 = lane == i1
    oh1 = lane == i2
    live = jnp.where(i > 0, 1.0, 0.0)
    oh = (oh0.astype(jnp.float32) + oh1.astype(jnp.float32)) * live
    cum = jnp.dot(tri_ref[...], oh.astype(jnp.bfloat16), preferred_element_type=jnp.float32) + carry_ref[...]
    r0 = jnp.sum(jnp.where(oh0, cum, 0.0), axis=-1, keepdims=True)
    r1 = jnp.sum(jnp.where(oh1, cum, 0.0), axis=-1, keepdims=True)
    rank_ref[...] = jnp.transpose(
        jnp.where(lane == 0, r0, jnp.where(lane == 1, r1, 0.0)).astype(jnp.int32))[0:SUBLANES, :]
    carry_ref[...] = carry_ref[...] + jnp.sum(oh, axis=0, keepdims=True)
    cnt_ref[...] = carry_ref[...].astype(jnp.int32)


def _merge(x2, attn, y4, gates, w_glu, b_glu, w_a, w_s, w_o, norm_moe, w_router, b_router):
    t, d = x2.shape
    tm = TM_MERGE
    nt = t // tm
    n_slab = y4.shape[0]
    tri = (jnp.arange(tm)[None, :] < jnp.arange(tm)[:, None]).astype(jnp.bfloat16)
    full = lambda a: pl.BlockSpec(a.shape, lambda i: (0,) * a.ndim, pipeline_mode=pl.Buffered(1))
    cur = lambda i: (jnp.minimum(i, nt - 1), 0)
    prv = lambda i: (jnp.maximum(i - 1, 0), 0)
    row = lambda c, m=cur: pl.BlockSpec((tm, c), m)
    slots = pl.BlockSpec((SUBLANES, tm), lambda i: (0, jnp.maximum(i - 1, 0)))
    return pl.pallas_call(
        _merge_kernel,
        grid=(nt + 1,),
        in_specs=[
            row(d), row(attn.shape[1]),
            pl.BlockSpec((n_slab, tm, LANES), lambda i: (0, jnp.minimum(i, nt - 1), 0)),
            row(gates.shape[1]),
            full(w_glu), full(b_glu), full(w_a), full(w_s), full(w_o), full(norm_moe),
            full(w_router), full(b_router), full(tri),
        ],
        out_specs=[row(d), pl.BlockSpec((tm * ROW_SUB, LANES), cur), slots, row(LANES, prv),
                   slots, pl.BlockSpec((1, LANES), lambda i: (0, 0))],
        out_shape=[
            jax.ShapeDtypeStruct((t, d), jnp.float32),
            jax.ShapeDtypeStruct((t * ROW_SUB, LANES), jnp.uint32),
            jax.ShapeDtypeStruct((SUBLANES, t), jnp.int32),
            jax.ShapeDtypeStruct((t, LANES), jnp.float32),
            jax.ShapeDtypeStruct((SUBLANES, t), jnp.int32),
            jax.ShapeDtypeStruct((1, LANES), jnp.int32),
        ],
        scratch_shapes=[pltpu.VMEM((tm, LANES), jnp.float32), pltpu.VMEM((1, LANES), jnp.float32)]
        + [pltpu.VMEM(w.shape, jnp.bfloat16) for w in (w_glu, w_a, w_s, w_o)],
        compiler_params=_cparams("arbitrary"),
        name="merge_router",
    )(x2, attn, y4, gates, w_glu, b_glu, w_a, w_s, w_o, norm_moe, w_router, b_router, tri)


def _inverse_kernel(pos_ref, src_ref):
    n_slots = pos_ref.shape[0]
    t = n_slots // TOP_K

    def body(tok, c):
        for k in range(TOP_K):
            src_ref[pos_ref[k * t + tok]] = tok * ROW_SUB
        return c

    lax.fori_loop(0, t, body, 0, unroll=16)


def _inverse(pos1d):
    n_slots = pos1d.shape[0]
    smem = pl.BlockSpec(memory_space=pltpu.SMEM)
    return pl.pallas_call(
        _inverse_kernel,
        in_specs=[smem],
        out_specs=smem,
        out_shape=jax.ShapeDtypeStruct((n_slots,), jnp.int32),
        name="route_inverse",
    )(pos1d)


def _expert_kernel(ta_ref, tn_ref, tnew_ref, sexp_ref, slo_ref, shi_ref, meta_ref, src_ref,
                   h_ref, wg_hbm, wu_hbm, wd_hbm, y_ref,
                   xbuf_ref, wgs_ref, wus_ref, wds_ref, wgb_ref, wub_ref, wdb_ref, gsem, wsem):
    j = pl.program_id(0)
    n_tiles = pl.num_programs(0)
    last = n_tiles - 1
    tm = TM_EXPERT
    rs = ROW_SUB
    nb = ROW_BUFS
    n_used = meta_ref[0]
    slot = j % nb
    a = ta_ref[j]

    def gather(tile, sl, start):
        _row_gather(h_ref, src_ref, tile * tm, xbuf_ref, sl * tm, tm, gsem.at[sl], start)

    def weights(m, start):
        sl = m % 2
        e = sexp_ref[m]
        for hbm, stage in ((wg_hbm, wgs_ref), (wu_hbm, wus_ref), (wd_hbm, wds_ref)):
            cp = pltpu.make_async_copy(hbm.at[e], stage.at[sl], wsem.at[sl])
            cp.start() if start else cp.wait()

    def new_expert(m):
        weights(m, False)
        sl = m % 2
        wgb_ref[...] = wgs_ref[sl].astype(jnp.bfloat16)
        wub_ref[...] = wus_ref[sl].astype(jnp.bfloat16)
        wdb_ref[...] = wds_ref[sl].astype(jnp.bfloat16)

        @pl.when(m + 1 < n_used)
        def _():
            weights(m + 1, True)

    def expert_out(xp):
        x = _unpack_bf16_pairs(xp).astype(jnp.bfloat16)
        hg = jnp.dot(x, wgb_ref[...], preferred_element_type=jnp.float32)
        hu = jnp.dot(x, wub_ref[...], preferred_element_type=jnp.float32)
        act = (jax.nn.silu(hg) * hu).astype(jnp.bfloat16)
        y = jnp.dot(act, wdb_ref[...], preferred_element_type=jnp.float32)
        return _pack_bf16_pairs(y.astype(jnp.bfloat16).astype(jnp.float32))

    @pl.when(j == 0)
    def _():
        weights(0, True)
        for b in range(nb - 1):
            gather(b, b, True)

    @pl.when(tnew_ref[j] == 1)
    def _():
        new_expert(a)

    gather(j, slot, False)
    xp = _load_rows(xbuf_ref, slot * tm, tm)
    gather(jnp.minimum(j + nb - 1, last), (j + nb - 1) % nb, True)
    _store_rows(y_ref, 0, expert_out(xp))

    def extra(i, carry):
        m = a + i
        new_expert(m)
        yp = expert_out(_load_rows(xbuf_ref, slot * tm, tm))
        row = lax.broadcasted_iota(jnp.int32, (tm, LANES), 0) + j * tm
        mine = (row >= slo_ref[m]) & (row < shi_ref[m])
        _store_rows(y_ref, 0, yp, mine)
        return carry

    lax.fori_loop(1, tn_ref[j], extra, 0)

    @pl.when(j == last)
    def _():
        for b in range(1, nb):
            gather(j, (j + b) % nb, False)


def _experts(meta, src_tok, h_fat, w_gate, w_up, w_down):
    n_slots = src_tok.shape[0]
    ne, d, dff = w_gate.shape
    tm = TM_EXPERT
    rs = ROW_SUB
    assert n_slots // tm >= ROW_BUFS
    any_spec = pl.BlockSpec(memory_space=pl.ANY)
    grid_spec = pltpu.PrefetchScalarGridSpec(
        num_scalar_prefetch=8,
        grid=(n_slots // tm,),
        in_specs=[any_spec, any_spec, any_spec, any_spec],
        out_specs=pl.BlockSpec((tm * rs, LANES), lambda j, *_: (j, 0)),
        scratch_shapes=[
            pltpu.VMEM((ROW_BUFS * tm * rs, LANES), jnp.uint32),
            pltpu.VMEM((2, d, dff), jnp.float32),
            pltpu.VMEM((2, d, dff), jnp.float32),
            pltpu.VMEM((2, dff, d), jnp.float32),
            pltpu.VMEM((d, dff), jnp.bfloat16),
            pltpu.VMEM((d, dff), jnp.bfloat16),
            pltpu.VMEM((dff, d), jnp.bfloat16),
            pltpu.SemaphoreType.DMA((ROW_BUFS,)),
            pltpu.SemaphoreType.DMA((2,)),
        ],
    )
    return pl.pallas_call(
        _expert_kernel,
        grid_spec=grid_spec,
        out_shape=jax.ShapeDtypeStruct((n_slots * rs, LANES), jnp.uint32),
        compiler_params=_cparams("arbitrary"),
        name="moe_experts",
    )(*meta, src_tok, h_fat, w_gate, w_up, w_down)


def _combine_kernel(pos_ref, x1_ref, rw_ref, nf_ref, ys_ref, o_ref, buf_ref, sem):
    i = pl.program_id(0)
    n = pl.num_programs(0)
    last = n - 1
    tm = x1_ref.shape[0]
    t = n * tm
    nb = ROW_BUFS
    slot = i % nb

    def gather(tile, sl, start):
        for k in range(TOP_K):
            _row_gather(ys_ref, pos_ref, k * t + tile * tm, buf_ref, (sl * TOP_K + k) * tm, tm, sem.at[sl], start)

    @pl.when(i == 0)
    def _():
        for b in range(nb - 1):
            gather(b, b, True)

    gather(i, slot, False)
    y0 = _unpack_bf16_pairs(_load_rows(buf_ref, (slot * TOP_K) * tm, tm))
    y1 = _unpack_bf16_pairs(_load_rows(buf_ref, (slot * TOP_K + 1) * tm, tm))
    gather(jnp.minimum(i + nb - 1, last), (i + nb - 1) % nb, True)
    rw = rw_ref[...]
    x = x1_ref[...] + rw[:, 0:1] * y0 + rw[:, 1:2] * y1
    inv = lax.rsqrt(jnp.mean(x * x, axis=-1, keepdims=True) + EPS)
    o_ref[...] = x * inv * nf_ref[...]

    @pl.when(i == last)
    def _():
        for b in range(1, nb):
            gather(i, (i + b) % nb, False)


def _combine(pos_rows, x1, rw, norm_final, ys):
    t, d = x1.shape
    tm = TM_MOVE
    assert t // tm >= ROW_BUFS
    return pl.pallas_call(
        _combine_kernel,
        grid=(t // tm,),
        in_specs=[
            pl.BlockSpec(memory_space=pltpu.SMEM),
            pl.BlockSpec((tm, d), lambda i: (i, 0)),
            pl.BlockSpec((tm, LANES), lambda i: (i, 0)),
            pl.BlockSpec((1, d), lambda i: (0, 0)),
            pl.BlockSpec(memory_space=pl.ANY),
        ],
        out_specs=pl.BlockSpec((tm, d), lambda i: (i, 0)),
        out_shape=jax.ShapeDtypeStruct((t, d), jnp.float32),
        scratch_shapes=[pltpu.VMEM((ROW_BUFS * TOP_K * tm * ROW_SUB, LANES), jnp.uint32),
                        pltpu.SemaphoreType.DMA((ROW_BUFS,))],
        compiler_params=_cparams("arbitrary"),
        name="moe_combine",
    )(pos_rows, x1, rw, norm_final, ys)


def _expert_meta(counts, n_slots):
    tm = TM_EXPERT
    n_tiles = n_slots // tm
    i32 = jnp.int32
    ends = jnp.cumsum(counts).astype(i32)
    starts = ends - counts
    used = counts > 0
    n_used = jnp.sum(used).astype(i32)
    seq_of = jnp.cumsum(used).astype(i32) - 1
    m_idx = jnp.arange(N_EXPERTS, dtype=i32)
    pick = used[None, :] & (seq_of[None, :] == m_idx[:, None])
    s_exp = jnp.sum(jnp.where(pick, m_idx[None, :], 0), axis=1).astype(i32)
    s_lo = jnp.sum(jnp.where(pick, starts[None, :], 0), axis=1).astype(i32)
    s_hi = jnp.sum(jnp.where(pick, ends[None, :], 0), axis=1).astype(i32)
    row0 = jnp.arange(n_tiles, dtype=i32)[:, None] * tm
    t_a = jnp.sum(used[None, :] & (ends[None, :] <= row0), axis=1).astype(i32)
    t_b = jnp.sum(used[None, :] & (starts[None, :] < row0 + tm), axis=1).astype(i32) - 1
    t_new = jnp.any(used[None, :] & (starts[None, :] == row0), axis=1).astype(i32)
    return t_a, (t_b - t_a + 1).astype(i32), t_new, s_exp, s_lo, s_hi, n_used.reshape(1)


def _moe(x1, h_fat, ids, rw, rank, cnt, w_gate, w_up, w_down, norm_final):
    t, d = x1.shape
    counts = cnt[0, :N_EXPERTS]
    offs = (jnp.cumsum(counts) - counts).astype(jnp.int32)
    onehot = ids[:TOP_K, :, None] == jnp.arange(N_EXPERTS, dtype=jnp.int32)
    pos1d = (jnp.sum(jnp.where(onehot, offs, 0), axis=-1) + rank[:TOP_K]).reshape(-1)
    src_tok = _inverse(pos1d)
    meta = _expert_meta(counts, t * TOP_K)
    ys = _experts(meta, src_tok, h_fat, w_gate, w_up, w_down)
    return _combine(pos1d * ROW_SUB, x1, rw, norm_final, ys)


def kernel(x, norm_mix, w_in, b_gate, attn_sinks, ssm_a_re, ssm_a_im, ssm_b_re, ssm_b_im, ssm_c_re, ssm_c_im, ssm_d, ssm_log_dt, w_glu, b_glu, w_attn_branch, w_ssm_branch, w_out, norm_moe, w_router_group, b_router_group, w_router_expert, b_router_expert, w_expert_gate, w_expert_up, w_expert_down, norm_final):
    b, l, d = x.shape
    depth = w_in.shape[0]
    assert depth == 1, "the final norm is fused into the last layer's combine kernel"
    d_attn = N_HEADS * HEAD_DIM
    kv_cols = N_KV_HEADS * HEAD_DIM
    d_ssm = ssm_d.shape[-1]
    bf16 = jnp.bfloat16
    x2 = x.reshape(b * l, d)
    assert TM_PROJ == TM_SSM == TM_MERGE and l % TM_SSM == 0
    i = 0
    q, k, v, u4, gates = _in_proj(x2, norm_mix[i][None], w_in[i], b_gate[i][None],
                                  d_attn, kv_cols, d_ssm)
    attn = _attention(q, k, v, attn_sinks[i], b, l)
    tables = _ssm_tables(ssm_a_re[i], ssm_a_im[i], ssm_b_re[i], ssm_b_im[i], ssm_c_re[i], ssm_c_im[i],
                         ssm_d[i], ssm_log_dt[i], SSM_SEG)
    y4 = _ssm(u4, tables, b, l)
    pad = LANES - N_EXPERTS - N_EXPERT_GROUPS
    w_router = jnp.concatenate([w_router_expert[i], w_router_group[i], jnp.zeros((d, pad), jnp.float32)], axis=1)
    w_r_hi = w_router.astype(bf16)
    w_router = jnp.concatenate([w_r_hi, (w_router - w_r_hi.astype(jnp.float32)).astype(bf16)], axis=1)
    b_router = jnp.concatenate([b_router_expert[i], b_router_group[i], jnp.zeros((pad,), jnp.float32)])[None]
    x1, h_fat, ids, rw, rank, cnt = _merge(x2, attn, y4, gates, w_glu[i], b_glu[i][None],
                                w_attn_branch[i], w_ssm_branch[i],
                                w_out[i], norm_moe[i][None], w_router, b_router)
    out = _moe(x1, h_fat, ids, rw, rank, cnt, w_expert_gate[i], w_expert_up[i], w_expert_down[i], norm_final[None])
    return out.reshape(b, l, d)
```

```python
import functools
import math

import jax
import jax.numpy as jnp
from jax import lax
from jax.experimental import pallas as pl
from jax.experimental.pallas import tpu as pltpu

EPS = 1e-6
HEAD_DIM = 64
N_HEADS = 8
N_KV_HEADS = 2
Q_PER_KV = N_HEADS // N_KV_HEADS
ATTN_BLOCK = 128
ATTN_QB = 8
SSM_GROUP = 16
SSM_STATE = 64
N_EXPERT_GROUPS = 4
EXPERTS_PER_GROUP = 8
N_EXPERTS = N_EXPERT_GROUPS * EXPERTS_PER_GROUP
TOP_K = 2

LANES = 128
SUBLANES = 8
SSM_CHUNK_GROUPS = LANES // SSM_GROUP
SSM_CHUNK_STATES = SSM_CHUNK_GROUPS * SSM_STATE

TM_PROJ = 512
TM_SSM = 512
SSM_SEG = TM_SSM // SUBLANES
TM_MERGE = 512
MERGE_SPLIT = 2
TM_MOVE = 512
TM_EXPERT = 256
ROW_SUB = 4
ROW_BUFS = 4
VMEM_LIMIT = 56 * 1024 * 1024


def _cparams(*sem):
    return pltpu.CompilerParams(dimension_semantics=sem, vmem_limit_bytes=VMEM_LIMIT)


def _log2(n):
    assert n > 0 and n & (n - 1) == 0, n
    return n.bit_length() - 1


def _pack_bf16_pairs(x):
    half = x.shape[1] // 2
    bits = lax.bitcast_convert_type(x, jnp.uint32)
    return (bits[:, :half] & jnp.uint32(0xFFFF0000)) | (bits[:, half:] >> 16)


def _unpack_bf16_pairs(p):
    hi = lax.bitcast_convert_type(p & jnp.uint32(0xFFFF0000), jnp.float32)
    lo = lax.bitcast_convert_type(p << 16, jnp.float32)
    return jnp.concatenate([hi, lo], axis=1)


def _load_rows(ref, first, n):
    return jnp.concatenate(
        [ref[pl.ds(first * ROW_SUB + c, n, stride=ROW_SUB), :] for c in range(ROW_SUB)], axis=1)


def _store_rows(ref, first, val, mask=None):
    n = val.shape[0]
    for c in range(ROW_SUB):
        idx = pl.ds(first * ROW_SUB + c, n, stride=ROW_SUB)
        v = val[:, c * LANES:(c + 1) * LANES]
        ref[idx, :] = v if mask is None else jnp.where(mask, v, ref[idx, :])


def _row_gather(src_hbm, idx_ref, idx0, dst_ref, dst0, n, sem, start):
    rs = ROW_SUB
    if start:
        for r in range(n):
            pltpu.make_async_copy(src_hbm.at[pl.ds(pl.multiple_of(idx_ref[idx0 + r], rs), rs)],
                                  dst_ref.at[pl.ds(pl.multiple_of((dst0 + r) * rs, rs), rs)],
                                  sem).start(priority=r % 2)
    else:
        pltpu.make_async_copy(src_hbm.at[pl.ds(0, n * rs)],
                              dst_ref.at[pl.ds(pl.multiple_of(dst0 * rs, rs), n * rs)], sem).wait()


def _proj_kernel(x_ref, g_ref, w32_ref, bg_ref, q_ref, k_ref, v_ref, u_ref, gate_ref, w_ref, *, cols):
    q_c, kv_c, d_ssm = cols

    @pl.when(pl.program_id(0) == 0)
    def _():
        w_ref[...] = w32_ref[...].astype(jnp.bfloat16)

    xf = x_ref[...]
    inv = lax.rsqrt(jnp.mean(xf * xf, axis=-1, keepdims=True) + EPS)
    h = (xf * inv * g_ref[...]).astype(jnp.bfloat16)
    o_gate = q_c + 2 * kv_c + d_ssm
    gl = jnp.dot(h, w_ref[:, o_gate:], preferred_element_type=jnp.float32) + bg_ref[...]
    gate_ref[...] = jax.nn.sigmoid(gl).astype(gate_ref.dtype)
    o = 0
    q_ref[...] = (jnp.dot(h, w_ref[:, o:o + q_c], preferred_element_type=jnp.float32)
                  * (1.0 / math.sqrt(HEAD_DIM))).astype(q_ref.dtype)
    o += q_c
    k_ref[...] = jnp.dot(h, w_ref[:, o:o + kv_c], preferred_element_type=jnp.float32).astype(k_ref.dtype)
    o += kv_c
    v_ref[...] = jnp.dot(h, w_ref[:, o:o + kv_c], preferred_element_type=jnp.float32).astype(v_ref.dtype)
    o += kv_c
    seg = x_ref.shape[0] // SUBLANES
    for s in range(0, d_ssm // LANES, 2):
        uu = jnp.dot(h, w_ref[:, o:o + 2 * LANES], preferred_element_type=jnp.float32)
        for half in range(2):
            for j in range(SUBLANES):
                u_ref[s + half, pl.ds(j, seg, stride=SUBLANES), :] = (
                    uu[j * seg:(j + 1) * seg, half * LANES:(half + 1) * LANES])
        o += 2 * LANES


def _in_proj(x2, norm_w, w_in, b_gate, d_attn, kv_cols, d_ssm):
    t, d = x2.shape
    gate_cols = b_gate.shape[-1]
    n_slab = d_ssm // LANES
    tm = TM_PROJ
    kern = functools.partial(_proj_kernel, cols=(d_attn, kv_cols, d_ssm))
    return pl.pallas_call(
        kern,
        grid=(t // tm,),
        in_specs=[
            pl.BlockSpec((tm, d), lambda i: (i, 0)),
            pl.BlockSpec((1, d), lambda i: (0, 0)),
            pl.BlockSpec(w_in.shape, lambda i: (0, 0), pipeline_mode=pl.Buffered(1)),
            pl.BlockSpec((1, gate_cols), lambda i: (0, 0)),
        ],
        out_specs=[
            pl.BlockSpec((tm, d_attn), lambda i: (i, 0)),
            pl.BlockSpec((tm, kv_cols), lambda i: (i, 0)),
            pl.BlockSpec((tm, kv_cols), lambda i: (i, 0)),
            pl.BlockSpec((n_slab, tm, LANES), lambda i: (0, i, 0)),
            pl.BlockSpec((tm, gate_cols), lambda i: (i, 0)),
        ],
        out_shape=[
            jax.ShapeDtypeStruct((t, d_attn), jnp.bfloat16),
            jax.ShapeDtypeStruct((t, kv_cols), jnp.bfloat16),
            jax.ShapeDtypeStruct((t, kv_cols), jnp.bfloat16),
            jax.ShapeDtypeStruct((n_slab, t, LANES), jnp.float32),
            jax.ShapeDtypeStruct((t, gate_cols), jnp.bfloat16),
        ],
        scratch_shapes=[pltpu.VMEM(w_in.shape, jnp.bfloat16)],
        compiler_params=_cparams("arbitrary"),
        name="in_proj",
    )(x2, norm_w, w_in, b_gate)


def _attn_kernel(sink_ref, q_ref, kp_ref, kc_ref, vp_ref, vc_ref, rep_ref, mask_ref, o_ref):
    i = pl.program_id(1)
    blk = ATTN_BLOCK
    hw = Q_PER_KV * HEAD_DIM
    rows = Q_PER_KV * blk
    head_of_row = lax.broadcasted_iota(jnp.int32, (rows, 1), 0) >> _log2(blk)
    lane_head_q = lax.broadcasted_iota(jnp.int32, (blk, hw), 1) >> _log2(HEAD_DIM)
    lane_head_v = lax.broadcasted_iota(jnp.int32, (2 * blk, hw), 1) >> _log2(HEAD_DIM)
    sinks = []
    for kh in range(N_KV_HEADS):
        sink = jnp.zeros((rows, 1), jnp.float32)
        for g in range(Q_PER_KV):
            sink = jnp.where(head_of_row == g, sink_ref[kh * Q_PER_KV + g], sink)
        sinks.append(sink)
    for qb in range(ATTN_QB):
        if qb == 0:
            k2 = jnp.concatenate([kp_ref[...], kc_ref[0:blk, :]], axis=0)
            v2 = jnp.concatenate([vp_ref[...], vc_ref[0:blk, :]], axis=0)
            bias = mask_ref[jnp.where(i == 0, 1, 0)]
        else:
            k2 = kc_ref[(qb - 1) * blk:(qb + 1) * blk, :]
            v2 = vc_ref[(qb - 1) * blk:(qb + 1) * blk, :]
            bias = mask_ref[0]
        for kh in range(N_KV_HEADS):
            rep = rep_ref[kh]
            k4 = jnp.dot(k2, rep, preferred_element_type=jnp.float32).astype(jnp.bfloat16)
            v4 = jnp.dot(v2, rep, preferred_element_type=jnp.float32).astype(jnp.bfloat16)
            qh = q_ref[qb * blk:(qb + 1) * blk, kh * hw:(kh + 1) * hw]
            qm = jnp.concatenate(
                [jnp.where(lane_head_q == g, qh, jnp.zeros_like(qh)) for g in range(Q_PER_KV)], axis=0)
            s = lax.dot_general(qm, k4, (((1,), (1,)), ((), ())), preferred_element_type=jnp.float32)
            s = s + bias
            sink = sinks[kh]
            m = jnp.maximum(jnp.max(s, axis=-1, keepdims=True), sink)
            p = jnp.exp(s - m)
            rinv = 1.0 / (jnp.sum(p, axis=-1, keepdims=True) + jnp.exp(sink - m))
            p = p.astype(jnp.bfloat16)
            p_cat = jnp.concatenate([p[g * blk:(g + 1) * blk, :] for g in range(Q_PER_KV)], axis=1)
            vm = jnp.concatenate(
                [jnp.where(lane_head_v == g, v4, jnp.zeros_like(v4)) for g in range(Q_PER_KV)], axis=0)
            o = jnp.dot(p_cat, vm, preferred_element_type=jnp.float32)
            scale = jnp.zeros((blk, hw), jnp.float32)
            for g in range(Q_PER_KV):
                scale = jnp.where(lane_head_q == g, rinv[g * blk:(g + 1) * blk, :], scale)
            o_ref[qb * blk:(qb + 1) * blk, kh * hw:(kh + 1) * hw] = (o * scale).astype(o_ref.dtype)


def _attention(q, k, v, sinks, b, l):
    d_attn = q.shape[-1]
    kv_cols = k.shape[-1]
    blk = ATTN_BLOCK
    tq = ATTN_QB * blk
    hw = Q_PER_KV * HEAD_DIM
    lane = jnp.arange(hw)[None, :]
    src = jnp.arange(kv_cols)[:, None]
    rep = jnp.stack([(src == kh * HEAD_DIM + (lane % HEAD_DIM)) for kh in range(N_KV_HEADS)]
                    ).astype(jnp.bfloat16)
    r = (jnp.arange(Q_PER_KV * blk) % blk)[:, None]
    c = jnp.arange(2 * blk)[None, :]
    band = (c > r) & (c <= r + blk)
    mask = jnp.where(jnp.stack([band, band & (c >= blk)]), 0.0, jnp.finfo(jnp.float32).min).astype(jnp.float32)
    q3 = q.reshape(b, l, d_attn)
    k3 = k.reshape(b, l, kv_cols)
    v3 = v.reshape(b, l, kv_cols)
    cur = lambda bi, i: (bi, i, 0)
    prev = lambda bi, i: (bi, jnp.maximum(ATTN_QB * i - 1, 0), 0)
    out = pl.pallas_call(
        _attn_kernel,
        grid=(b, l // tq),
        in_specs=[
            pl.BlockSpec(memory_space=pltpu.SMEM),
            pl.BlockSpec((None, tq, d_attn), cur),
            pl.BlockSpec((None, blk, kv_cols), prev),
            pl.BlockSpec((None, tq, kv_cols), cur),
            pl.BlockSpec((None, blk, kv_cols), prev),
            pl.BlockSpec((None, tq, kv_cols), cur),
            pl.BlockSpec(rep.shape, lambda bi, i: (0, 0, 0)),
            pl.BlockSpec(mask.shape, lambda bi, i: (0, 0, 0)),
        ],
        out_specs=pl.BlockSpec((None, tq, d_attn), cur),
        out_shape=jax.ShapeDtypeStruct((b, l, d_attn), jnp.bfloat16),
        compiler_params=_cparams("arbitrary", "arbitrary"),
        name="swa",
    )(sinks, q3, k3, k3, v3, v3, rep, mask)
    return out.reshape(b * l, d_attn)


def _ssm_kernel(u_ref, bmat_ref, cmat_ref, lam_ref, pw_ref, lamseg_ref, d_ref, y_ref, bu_ref, carry_ref):
    i = pl.program_id(1)
    n_slab = u_ref.shape[0]
    tm = u_ref.shape[1]
    seg = tm // SUBLANES
    ns = SSM_CHUNK_STATES
    npair = 2

    @pl.when(i == 0)
    def _():
        carry_ref[...] = jnp.zeros_like(carry_ref)

    sub = lax.broadcasted_iota(jnp.int32, (SUBLANES, ns), 0)
    for s0 in range(0, n_slab, npair):
        lam = []
        for q in range(npair):
            s = s0 + q
            bu_ref[q] = jnp.dot(u_ref[s].astype(jnp.bfloat16), bmat_ref[s], preferred_element_type=jnp.float32)
            lam.append((jnp.broadcast_to(lam_ref[s, 0:1, :], (SUBLANES, ns)),
                        jnp.broadcast_to(lam_ref[s, 1:2, :], (SUBLANES, ns))))

        def step(r, st):
            rows = pl.ds(r * SUBLANES, SUBLANES)
            out = []
            for q in range(npair):
                lr, li = lam[q]
                sr, si = st[q]
                nr = lr * sr - li * si + bu_ref[q, rows, 0:ns]
                ni = lr * si + li * sr + bu_ref[q, rows, ns:2 * ns]
                bu_ref[q, rows, 0:ns] = nr
                bu_ref[q, rows, ns:2 * ns] = ni
                out.append((nr, ni))
            return tuple(out)

        zero = jnp.zeros((SUBLANES, ns), jnp.float32)
        ends = ((zero, zero),) * npair
        for r in range(seg):
            ends = step(r, ends)

        for q in range(npair):
            s = s0 + q
            er, ei = ends[q]
            ar = lamseg_ref[s, 0:1, :]
            ai = lamseg_ref[s, 1:2, :]
            cr = carry_ref[s, 0:1, :]
            ci = carry_ref[s, 1:2, :]
            car = jnp.zeros((SUBLANES, ns), jnp.float32)
            cai = jnp.zeros((SUBLANES, ns), jnp.float32)
            for j in range(SUBLANES):
                car = jnp.where(sub == j, jnp.broadcast_to(cr, (SUBLANES, ns)), car)
                cai = jnp.where(sub == j, jnp.broadcast_to(ci, (SUBLANES, ns)), cai)
                ejr = jnp.sum(jnp.where(sub == j, er, 0.0), axis=0, keepdims=True)
                eji = jnp.sum(jnp.where(sub == j, ei, 0.0), axis=0, keepdims=True)
                cr, ci = ar * cr - ai * ci + ejr, ar * ci + ai * cr + eji
            carry_ref[s, 0:1, :] = cr
            carry_ref[s, 1:2, :] = ci

            ctr = jnp.broadcast_to(car[None], (seg, SUBLANES, ns)).reshape(tm, ns)
            cti = jnp.broadcast_to(cai[None], (seg, SUBLANES, ns)).reshape(tm, ns)
            pr = pw_ref[s, 0]
            pi = pw_ref[s, 1]
            st_r = bu_ref[q, :, 0:ns] + pr * ctr - pi * cti
            st_i = bu_ref[q, :, ns:2 * ns] + pr * cti + pi * ctr
            st = jnp.concatenate([st_r, st_i], axis=1).astype(jnp.bfloat16)
            y_ref[s] = jnp.dot(st, cmat_ref[s], preferred_element_type=jnp.float32) + d_ref[s] * u_ref[s]


def _ssm_tables(a_re, a_im, b_re, b_im, c_re, c_im, d_skip, log_dt, seg):
    f32 = jnp.float32
    g, p = a_re.shape
    c = b_re.shape[-1]
    ng = SSM_CHUNK_GROUPS
    n_slab = g // ng
    lam = lax.complex(a_re.astype(f32), a_im.astype(f32))
    dt = jnp.exp(log_dt.astype(f32))[:, None]
    lam_bar = jnp.exp(lam * dt)
    b_bar = ((lam_bar - 1.0) / lam)[:, :, None] * lax.complex(b_re.astype(f32), b_im.astype(f32))
    c_mat = lax.complex(c_re.astype(f32), c_im.astype(f32))
    eye = jnp.eye(ng, dtype=f32)

    def bdiag_b(m):
        m = m.reshape(n_slab, ng, p, c)
        return jnp.einsum('ab,kbpc->kacbp', eye, m).reshape(n_slab, ng * c, ng * p)

    def bdiag_c(m):
        m = m.reshape(n_slab, ng, c, p)
        return jnp.einsum('ab,kbcp->kbpac', eye, m).reshape(n_slab, ng * p, ng * c)

    bmat = jnp.concatenate([bdiag_b(jnp.real(b_bar)), bdiag_b(jnp.imag(b_bar))], axis=2).astype(jnp.bfloat16)
    cmat = jnp.concatenate([bdiag_c(jnp.real(c_mat)), -bdiag_c(jnp.imag(c_mat))], axis=1).astype(jnp.bfloat16)

    def slab_rows(z):
        z = z.reshape(n_slab, 1, ng * p)
        return jnp.concatenate([jnp.real(z), jnp.imag(z)], axis=1)

    lam_t = slab_rows(lam_bar)
    steps = jnp.arange(1, seg + 1, dtype=f32)
    pw = jnp.exp((lam * dt)[None] * steps[:, None, None])
    pw = pw.reshape(seg, n_slab, ng * p).transpose(1, 0, 2)
    pw = jnp.repeat(pw, SUBLANES, axis=1)
    pw_t = jnp.stack([jnp.real(pw), jnp.imag(pw)], axis=1)
    lamseg_t = slab_rows(jnp.exp(lam * dt * float(seg)))
    d_t = d_skip.astype(f32).reshape(n_slab, 1, ng * c)
    return bmat, cmat, lam_t, pw_t, lamseg_t, d_t


def _ssm(u4, tables, b, l):
    bmat, cmat, lam_t, pw_t, lamseg_t, d_t = tables
    n_slab, t, _ = u4.shape
    tm = TM_SSM
    nt = l // tm
    ns = SSM_CHUNK_STATES
    const = lambda nd: (lambda bi, i: (0,) * nd)
    return pl.pallas_call(
        _ssm_kernel,
        grid=(b, nt),
        in_specs=[
            pl.BlockSpec((n_slab, tm, LANES), lambda bi, i: (0, bi * nt + i, 0)),
            pl.BlockSpec(bmat.shape, const(3)),
            pl.BlockSpec(cmat.shape, const(3)),
            pl.BlockSpec(lam_t.shape, const(3)),
            pl.BlockSpec(pw_t.shape, const(4)),
            pl.BlockSpec(lamseg_t.shape, const(3)),
            pl.BlockSpec(d_t.shape, const(3)),
        ],
        out_specs=pl.BlockSpec((n_slab, tm, LANES), lambda bi, i: (0, bi * nt + i, 0)),
        out_shape=jax.ShapeDtypeStruct((n_slab, t, LANES), jnp.float32),
        scratch_shapes=[
            pltpu.VMEM((2, tm, 2 * ns), jnp.float32),
            pltpu.VMEM((n_slab, 2, ns), jnp.float32),
        ],
        compiler_params=_cparams("arbitrary", "arbitrary"),
        name="s5_scan",
    )(u4, bmat, cmat, lam_t, pw_t, lamseg_t, d_t)


def _merge_kernel(x_ref, attn_ref, y_ref, gate_ref, wglu32_ref, bglu_ref, wa32_ref, ws32_ref, wo32_ref,
                  nm_ref, wr_ref, br_ref, tri_ref, x1_ref, h2_ref, ids_ref, rw_ref, rank_ref, cnt_ref,
                  lg_ref, carry_ref, wglu_ref, wa_ref, ws_ref, wo_ref):
    i = pl.program_id(0)
    d = x_ref.shape[1]
    tm = x_ref.shape[0]

    @pl.when(i == 0)
    def _():
        lg_ref[...] = jnp.zeros_like(lg_ref)
        carry_ref[...] = jnp.zeros_like(carry_ref)
        for w32, wb in ((wglu32_ref, wglu_ref), (wa32_ref, wa_ref), (ws32_ref, ws_ref), (wo32_ref, wo_ref)):
            wb[...] = w32[...].astype(jnp.bfloat16)

    logits = lg_ref[...]

    seg = tm // SUBLANES
    nh = MERGE_SPLIT
    hm = tm // nh
    js = SUBLANES // nh
    rows = [slice(g * hm, (g + 1) * hm) for g in range(nh)]
    y = [jnp.concatenate(
        [jnp.concatenate([y_ref[s, pl.ds(j, seg, stride=SUBLANES), :] for j in range(g * js, (g + 1) * js)], axis=0)
         for s in range(y_ref.shape[0])], axis=1) for g in range(nh)]
    z = [jax.nn.gelu(v) for v in y]
    zg = [jnp.dot(v.astype(jnp.bfloat16), wglu_ref[...], preferred_element_type=jnp.float32) + bglu_ref[...]
          for v in z]
    z = [v * jax.nn.sigmoid(w) for v, w in zip(z, zg)]
    a = [jnp.dot(attn_ref[r, :], wa_ref[...], preferred_element_type=jnp.float32) for r in rows]
    sb = [jnp.dot(v.astype(jnp.bfloat16), ws_ref[...], preferred_element_type=jnp.float32) for v in z]
    merged = [gate_ref[r, 0:d].astype(jnp.float32) * p + gate_ref[r, d:2 * d].astype(jnp.float32) * q
              for r, p, q in zip(rows, a, sb)]
    x1 = [x_ref[r, :] + jnp.dot(m.astype(jnp.bfloat16), wo_ref[...], preferred_element_type=jnp.float32)
          for r, m in zip(rows, merged)]
    for g in range(nh):
        x1_ref[rows[g], :] = x1[g]
        inv = lax.rsqrt(jnp.mean(x1[g] * x1[g], axis=-1, keepdims=True) + EPS)
        h2 = x1[g] * inv * nm_ref[...]
        h_hi = h2.astype(jnp.bfloat16)
        _store_rows(h2_ref, g * hm, _pack_bf16_pairs(h_hi.astype(jnp.float32)))
        h_lo = (h2 - h_hi.astype(jnp.float32)).astype(jnp.bfloat16)
        hh = jnp.dot(h_hi, wr_ref[...], preferred_element_type=jnp.float32)
        lh = jnp.dot(h_lo, wr_ref[:, 0:LANES], preferred_element_type=jnp.float32)
        lg_ref[rows[g], :] = hh[:, 0:LANES] + hh[:, LANES:2 * LANES] + lh + br_ref[...]

    lane = lax.broadcasted_iota(jnp.int32, (tm, LANES), 1)
    ninf = -jnp.inf
    gl = jnp.where((lane >= N_EXPERTS) & (lane < N_EXPERTS + N_EXPERT_GROUPS), logits, ninf)
    gmax = jnp.max(gl, axis=-1, keepdims=True)
    gidx = jnp.min(jnp.where(gl == gmax, lane - N_EXPERTS, LANES), axis=-1, keepdims=True)
    group_p = 1.0 / jnp.sum(jnp.exp(gl - gmax), axis=-1, keepdims=True)
    el = jnp.where((lane < N_EXPERTS) & ((lane >> _log2(EXPERTS_PER_GROUP)) == gidx), logits, ninf)
    m1 = jnp.max(el, axis=-1, keepdims=True)
    i1 = jnp.min(jnp.where(el == m1, lane, LANES), axis=-1, keepdims=True)
    el2 = jnp.where(lane == i1, ninf, el)
    m2 = jnp.max(el2, axis=-1, keepdims=True)
    i2 = jnp.min(jnp.where(el2 == m2, lane, LANES), axis=-1, keepdims=True)
    e2 = jnp.exp(m2 - m1)
    w1 = group_p / (1.0 + e2)
    w2 = group_p * e2 / (1.0 + e2)
    ids_ref[...] = jnp.transpose(jnp.where(lane == 0, i1, jnp.where(lane == 1, i2, 0)))[0:SUBLANES, :]
    rw_ref[...] = jnp.where(lane == 0, w1, jnp.where(lane == 1, w2, 0.0))

    oh0 = lane == i1
    oh1 = lane == i2
    live = jnp.where(i > 0, 1.0, 0.0)
    oh = (oh0.astype(jnp.float32) + oh1.astype(jnp.float32)) * live
    cum = jnp.dot(tri_ref[...], oh.astype(jnp.bfloat16), preferred_element_type=jnp.float32) + carry_ref[...]
    r0 = jnp.sum(jnp.where(oh0, cum, 0.0), axis=-1, keepdims=True)
    r1 = jnp.sum(jnp.where(oh1, cum, 0.0), axis=-1, keepdims=True)
    rank_ref[...] = jnp.transpose(
        jnp.where(lane == 0, r0, jnp.where(lane == 1, r1, 0.0)).astype(jnp.int32))[0:SUBLANES, :]
    carry_ref[...] = carry_ref[...] + jnp.sum(oh, axis=0, keepdims=True)
    cnt_ref[...] = carry_ref[...].astype(jnp.int32)


def _merge(x2, attn, y4, gates, w_glu, b_glu, w_a, w_s, w_o, norm_moe, w_router, b_router):
    t, d = x2.shape
    tm = TM_MERGE
    nt = t // tm
    n_slab = y4.shape[0]
    tri = (jnp.arange(tm)[None, :] < jnp.arange(tm)[:, None]).astype(jnp.bfloat16)
    full = lambda a: pl.BlockSpec(a.shape, lambda i: (0,) * a.ndim, pipeline_mode=pl.Buffered(1))
    cur = lambda i: (jnp.minimum(i, nt - 1), 0)
    prv = lambda i: (jnp.maximum(i - 1, 0), 0)
    row = lambda c, m=cur: pl.BlockSpec((tm, c), m)
    slots = pl.BlockSpec((SUBLANES, tm), lambda i: (0, jnp.maximum(i - 1, 0)))
    return pl.pallas_call(
        _merge_kernel,
        grid=(nt + 1,),
        in_specs=[
            row(d), row(attn.shape[1]),
            pl.BlockSpec((n_slab, tm, LANES), lambda i: (0, jnp.minimum(i, nt - 1), 0)),
            row(gates.shape[1]),
            full(w_glu), full(b_glu), full(w_a), full(w_s), full(w_o), full(norm_moe),
            full(w_router), full(b_router), full(tri),
        ],
        out_specs=[row(d), pl.BlockSpec((tm * ROW_SUB, LANES), cur), slots, row(LANES, prv),
                   slots, pl.BlockSpec((1, LANES), lambda i: (0, 0))],
        out_shape=[
            jax.ShapeDtypeStruct((t, d), jnp.float32),
            jax.ShapeDtypeStruct((t * ROW_SUB, LANES), jnp.uint32),
            jax.ShapeDtypeStruct((SUBLANES, t), jnp.int32),
            jax.ShapeDtypeStruct((t, LANES), jnp.float32),
            jax.ShapeDtypeStruct((SUBLANES, t), jnp.int32),
            jax.ShapeDtypeStruct((1, LANES), jnp.int32),
        ],
        scratch_shapes=[pltpu.VMEM((tm, LANES), jnp.float32), pltpu.VMEM((1, LANES), jnp.float32)]
        + [pltpu.VMEM(w.shape, jnp.bfloat16) for w in (w_glu, w_a, w_s, w_o)],
        compiler_params=_cparams("arbitrary"),
        name="merge_router",
    )(x2, attn, y4, gates, w_glu, b_glu, w_a, w_s, w_o, norm_moe, w_router, b_router, tri)


def _inverse_kernel(pos_ref, src_ref):
    n_slots = pos_ref.shape[0]
    t = n_slots // TOP_K

    def body(tok, c):
        for k in range(TOP_K):
            src_ref[pos_ref[k * t + tok]] = tok * ROW_SUB
        return c

    lax.fori_loop(0, t, body, 0, unroll=16)


def _inverse(pos1d):
    n_slots = pos1d.shape[0]
    smem = pl.BlockSpec(memory_space=pltpu.SMEM)
    return pl.pallas_call(
        _inverse_kernel,
        in_specs=[smem],
        out_specs=smem,
        out_shape=jax.ShapeDtypeStruct((n_slots,), jnp.int32),
        name="route_inverse",
    )(pos1d)


def _expert_kernel(ta_ref, tn_ref, tnew_ref, sexp_ref, slo_ref, shi_ref, meta_ref, src_ref,
                   h_ref, wg_hbm, wu_hbm, wd_hbm, y_ref,
                   xbuf_ref, wgs_ref, wus_ref, wds_ref, wgb_ref, wub_ref, wdb_ref, gsem, wsem):
    j = pl.program_id(0)
    n_tiles = pl.num_programs(0)
    last = n_tiles - 1
    tm = TM_EXPERT
    rs = ROW_SUB
    nb = ROW_BUFS
    n_used = meta_ref[0]
    slot = j % nb
    a = ta_ref[j]

    def gather(tile, sl, start):
        _row_gather(h_ref, src_ref, tile * tm, xbuf_ref, sl * tm, tm, gsem.at[sl], start)

    def weights(m, start):
        sl = m % 2
        e = sexp_ref[m]
        for hbm, stage in ((wg_hbm, wgs_ref), (wu_hbm, wus_ref), (wd_hbm, wds_ref)):
            cp = pltpu.make_async_copy(hbm.at[e], stage.at[sl], wsem.at[sl])
            cp.start() if start else cp.wait()

    def new_expert(m):
        weights(m, False)
        sl = m % 2
        wgb_ref[...] = wgs_ref[sl].astype(jnp.bfloat16)
        wub_ref[...] = wus_ref[sl].astype(jnp.bfloat16)
        wdb_ref[...] = wds_ref[sl].astype(jnp.bfloat16)

        @pl.when(m + 1 < n_used)
        def _():
            weights(m + 1, True)

    def expert_out(xp):
        x = _unpack_bf16_pairs(xp).astype(jnp.bfloat16)
        hg = jnp.dot(x, wgb_ref[...], preferred_element_type=jnp.float32)
        hu = jnp.dot(x, wub_ref[...], preferred_element_type=jnp.float32)
        act = (jax.nn.silu(hg) * hu).astype(jnp.bfloat16)
        y = jnp.dot(act, wdb_ref[...], preferred_element_type=jnp.float32)
        return _pack_bf16_pairs(y.astype(jnp.bfloat16).astype(jnp.float32))

    @pl.when(j == 0)
    def _():
        weights(0, True)
        for b in range(nb - 1):
            gather(b, b, True)

    @pl.when(tnew_ref[j] == 1)
    def _():
        new_expert(a)

    gather(j, slot, False)
    xp = _load_rows(xbuf_ref, slot * tm, tm)
    gather(jnp.minimum(j + nb - 1, last), (j + nb - 1) % nb, True)
    _store_rows(y_ref, 0, expert_out(xp))

    def extra(i, carry):
        m = a + i
        new_expert(m)
        yp = expert_out(_load_rows(xbuf_ref, slot * tm, tm))
        row = lax.broadcasted_iota(jnp.int32, (tm, LANES), 0) + j * tm
        mine = (row >= slo_ref[m]) & (row < shi_ref[m])
        _store_rows(y_ref, 0, yp, mine)
        return carry

    lax.fori_loop(1, tn_ref[j], extra, 0)

    @pl.when(j == last)
    def _():
        for b in range(1, nb):
            gather(j, (j + b) % nb, False)


def _experts(meta, src_tok, h_fat, w_gate, w_up, w_down):
    n_slots = src_tok.shape[0]
    ne, d, dff = w_gate.shape
    tm = TM_EXPERT
    rs = ROW_SUB
    assert n_slots // tm >= ROW_BUFS
    any_spec = pl.BlockSpec(memory_space=pl.ANY)
    grid_spec = pltpu.PrefetchScalarGridSpec(
        num_scalar_prefetch=8,
        grid=(n_slots // tm,),
        in_specs=[any_spec, any_spec, any_spec, any_spec],
        out_specs=pl.BlockSpec((tm * rs, LANES), lambda j, *_: (j, 0)),
        scratch_shapes=[
            pltpu.VMEM((ROW_BUFS * tm * rs, LANES), jnp.uint32),
            pltpu.VMEM((2, d, dff), jnp.float32),
            pltpu.VMEM((2, d, dff), jnp.float32),
            pltpu.VMEM((2, dff, d), jnp.float32),
            pltpu.VMEM((d, dff), jnp.bfloat16),
            pltpu.VMEM((d, dff), jnp.bfloat16),
            pltpu.VMEM((dff, d), jnp.bfloat16),
            pltpu.SemaphoreType.DMA((ROW_BUFS,)),
            pltpu.SemaphoreType.DMA((2,)),
        ],
    )
    return pl.pallas_call(
        _expert_kernel,
        grid_spec=grid_spec,
        out_shape=jax.ShapeDtypeStruct((n_slots * rs, LANES), jnp.uint32),
        compiler_params=_cparams("arbitrary"),
        name="moe_experts",
    )(*meta, src_tok, h_fat, w_gate, w_up, w_down)


def _combine_kernel(pos_ref, x1_ref, rw_ref, nf_ref, ys_ref, o_ref, buf_ref, sem):
    i = pl.program_id(0)
    n = pl.num_programs(0)
    last = n - 1
    tm = x1_ref.shape[0]
    t = n * tm
    nb = ROW_BUFS
    slot = i % nb

    def gather(tile, sl, start):
        for k in range(TOP_K):
            _row_gather(ys_ref, pos_ref, k * t + tile * tm, buf_ref, (sl * TOP_K + k) * tm, tm, sem.at[sl], start)

    @pl.when(i == 0)
    def _():
        for b in range(nb - 1):
            gather(b, b, True)

    gather(i, slot, False)
    y0 = _unpack_bf16_pairs(_load_rows(buf_ref, (slot * TOP_K) * tm, tm))
    y1 = _unpack_bf16_pairs(_load_rows(buf_ref, (slot * TOP_K + 1) * tm, tm))
    gather(jnp.minimum(i + nb - 1, last), (i + nb - 1) % nb, True)
    rw = rw_ref[...]
    x = x1_ref[...] + rw[:, 0:1] * y0 + rw[:, 1:2] * y1
    inv = lax.rsqrt(jnp.mean(x * x, axis=-1, keepdims=True) + EPS)
    o_ref[...] = x * inv * nf_ref[...]

    @pl.when(i == last)
    def _():
        for b in range(1, nb):
            gather(i, (i + b) % nb, False)


def _combine(pos_rows, x1, rw, norm_final, ys):
    t, d = x1.shape
    tm = TM_MOVE
    assert t // tm >= ROW_BUFS
    return pl.pallas_call(
        _combine_kernel,
        grid=(t // tm,),
        in_specs=[
            pl.BlockSpec(memory_space=pltpu.SMEM),
            pl.BlockSpec((tm, d), lambda i: (i, 0)),
            pl.BlockSpec((tm, LANES), lambda i: (i, 0)),
            pl.BlockSpec((1, d), lambda i: (0, 0)),
            pl.BlockSpec(memory_space=pl.ANY),
        ],
        out_specs=pl.BlockSpec((tm, d), lambda i: (i, 0)),
        out_shape=jax.ShapeDtypeStruct((t, d), jnp.float32),
        scratch_shapes=[pltpu.VMEM((ROW_BUFS * TOP_K * tm * ROW_SUB, LANES), jnp.uint32),
                        pltpu.SemaphoreType.DMA((ROW_BUFS,))],
        compiler_params=_cparams("arbitrary"),
        name="moe_combine",
    )(pos_rows, x1, rw, norm_final, ys)


def _expert_meta(counts, n_slots):
    tm = TM_EXPERT
    n_tiles = n_slots // tm
    i32 = jnp.int32
    ends = jnp.cumsum(counts).astype(i32)
    starts = ends - counts
    used = counts > 0
    n_used = jnp.sum(used).astype(i32)
    seq_of = jnp.cumsum(used).astype(i32) - 1
    m_idx = jnp.arange(N_EXPERTS, dtype=i32)
    pick = used[None, :] & (seq_of[None, :] == m_idx[:, None])
    s_exp = jnp.sum(jnp.where(pick, m_idx[None, :], 0), axis=1).astype(i32)
    s_lo = jnp.sum(jnp.where(pick, starts[None, :], 0), axis=1).astype(i32)
    s_hi = jnp.sum(jnp.where(pick, ends[None, :], 0), axis=1).astype(i32)
    row0 = jnp.arange(n_tiles, dtype=i32)[:, None] * tm
    t_a = jnp.sum(used[None, :] & (ends[None, :] <= row0), axis=1).astype(i32)
    t_b = jnp.sum(used[None, :] & (starts[None, :] < row0 + tm), axis=1).astype(i32) - 1
    t_new = jnp.any(used[None, :] & (starts[None, :] == row0), axis=1).astype(i32)
    return t_a, (t_b - t_a + 1).astype(i32), t_new, s_exp, s_lo, s_hi, n_used.reshape(1)


def _moe(x1, h_fat, ids, rw, rank, cnt, w_gate, w_up, w_down, norm_final):
    t, d = x1.shape
    counts = cnt[0, :N_EXPERTS]
    offs = (jnp.cumsum(counts) - counts).astype(jnp.int32)
    onehot = ids[:TOP_K, :, None] == jnp.arange(N_EXPERTS, dtype=jnp.int32)
    pos1d = (jnp.sum(jnp.where(onehot, offs, 0), axis=-1) + rank[:TOP_K]).reshape(-1)
    src_tok = _inverse(pos1d)
    meta = _expert_meta(counts, t * TOP_K)
    ys = _experts(meta, src_tok, h_fat, w_gate, w_up, w_down)
    return _combine(pos1d * ROW_SUB, x1, rw, norm_final, ys)


def kernel(x, norm_mix, w_in, b_gate, attn_sinks, ssm_a_re, ssm_a_im, ssm_b_re, ssm_b_im, ssm_c_re, ssm_c_im, ssm_d, ssm_log_dt, w_glu, b_glu, w_attn_branch, w_ssm_branch, w_out, norm_moe, w_router_group, b_router_group, w_router_expert, b_router_expert, w_expert_gate, w_expert_up, w_expert_down, norm_final):
    b, l, d = x.shape
    depth = w_in.shape[0]
    assert depth == 1, "the final norm is fused into the last layer's combine kernel"
    d_attn = N_HEADS * HEAD_DIM
    kv_cols = N_KV_HEADS * HEAD_DIM
    d_ssm = ssm_d.shape[-1]
    bf16 = jnp.bfloat16
    x2 = x.reshape(b * l, d)
    assert TM_PROJ == TM_SSM == TM_MERGE and l % TM_SSM == 0
    i = 0
    q, k, v, u4, gates = _in_proj(x2, norm_mix[i][None], w_in[i], b_gate[i][None],
                                  d_attn, kv_cols, d_ssm)
    attn = _attention(q, k, v, attn_sinks[i], b, l)
    tables = _ssm_tables(ssm_a_re[i], ssm_a_im[i], ssm_b_re[i], ssm_b_im[i], ssm_c_re[i], ssm_c_im[i],
                         ssm_d[i], ssm_log_dt[i], SSM_SEG)
    y4 = _ssm(u4, tables, b, l)
    pad = LANES - N_EXPERTS - N_EXPERT_GROUPS
    w_router = jnp.concatenate([w_router_expert[i], w_router_group[i], jnp.zeros((d, pad), jnp.float32)], axis=1)
    w_r_hi = w_router.astype(bf16)
    w_router = jnp.concatenate([w_r_hi, (w_router - w_r_hi.astype(jnp.float32)).astype(bf16)], axis=1)
    b_router = jnp.concatenate([b_router_expert[i], b_router_group[i], jnp.zeros((pad,), jnp.float32)])[None]
    x1, h_fat, ids, rw, rank, cnt = _merge(x2, attn, y4, gates, w_glu[i], b_glu[i][None],
                                w_attn_branch[i], w_ssm_branch[i],
                                w_out[i], norm_moe[i][None], w_router, b_router)
    out = _moe(x1, h_fat, ids, rw, rank, cnt, w_expert_gate[i], w_expert_up[i], w_expert_down[i], norm_final[None])
    return out.reshape(b, l, d)
```

```python
import functools
import math

import jax
import jax.numpy as jnp
from jax import lax
from jax.experimental import pallas as pl
from jax.experimental.pallas import tpu as pltpu

EPS = 1e-6
HEAD_DIM = 64
N_HEADS = 8
N_KV_HEADS = 2
Q_PER_KV = N_HEADS // N_KV_HEADS
ATTN_BLOCK = 128
ATTN_QB = 16
SSM_GROUP = 16
SSM_STATE = 64
N_EXPERT_GROUPS = 4
EXPERTS_PER_GROUP = 8
N_EXPERTS = N_EXPERT_GROUPS * EXPERTS_PER_GROUP
TOP_K = 2

LANES = 128
SUBLANES = 8
SSM_CHUNK_GROUPS = LANES // SSM_GROUP
SSM_CHUNK_STATES = SSM_CHUNK_GROUPS * SSM_STATE

TM_PROJ = 512
TM_SSM = 512
SSM_SEG = TM_SSM // SUBLANES
TM_MERGE = 512
MERGE_SPLIT = 2
TM_MOVE = 512
TM_EXPERT = 256
ROW_SUB = 4
ROW_BUFS = 4
VMEM_LIMIT = 56 * 1024 * 1024


def _cparams(*sem):
    return pltpu.CompilerParams(dimension_semantics=sem, vmem_limit_bytes=VMEM_LIMIT)


def _log2(n):
    assert n > 0 and n & (n - 1) == 0, n
    return n.bit_length() - 1


def _pack_bf16_pairs(x):
    half = x.shape[1] // 2
    bits = lax.bitcast_convert_type(x, jnp.uint32)
    return (bits[:, :half] & jnp.uint32(0xFFFF0000)) | (bits[:, half:] >> 16)


def _unpack_bf16_pairs(p):
    hi = lax.bitcast_convert_type(p & jnp.uint32(0xFFFF0000), jnp.float32)
    lo = lax.bitcast_convert_type(p << 16, jnp.float32)
    return jnp.concatenate([hi, lo], axis=1)


def _load_rows(ref, first, n):
    return jnp.concatenate(
        [ref[pl.ds(first * ROW_SUB + c, n, stride=ROW_SUB), :] for c in range(ROW_SUB)], axis=1)


def _store_rows(ref, first, val, mask=None):
    n = val.shape[0]
    for c in range(ROW_SUB):
        idx = pl.ds(first * ROW_SUB + c, n, stride=ROW_SUB)
        v = val[:, c * LANES:(c + 1) * LANES]
        ref[idx, :] = v if mask is None else jnp.where(mask, v, ref[idx, :])


def _row_gather(src_hbm, idx_ref, idx0, dst_ref, dst0, n, sem, start):
    rs = ROW_SUB
    if start:
        for r in range(n):
            pltpu.make_async_copy(src_hbm.at[pl.ds(pl.multiple_of(idx_ref[idx0 + r], rs), rs)],
                                  dst_ref.at[pl.ds(pl.multiple_of((dst0 + r) * rs, rs), rs)],
                                  sem).start(priority=r % 2)
    else:
        pltpu.make_async_copy(src_hbm.at[pl.ds(0, n * rs)],
                              dst_ref.at[pl.ds(pl.multiple_of(dst0 * rs, rs), n * rs)], sem).wait()


def _proj_kernel(x_ref, g_ref, w32_ref, bg_ref, q_ref, k_ref, v_ref, u_ref, gate_ref, w_ref, *, cols):
    q_c, kv_c, d_ssm = cols

    @pl.when(pl.program_id(0) == 0)
    def _():
        w_ref[...] = w32_ref[...].astype(jnp.bfloat16)

    xf = x_ref[...]
    inv = lax.rsqrt(jnp.mean(xf * xf, axis=-1, keepdims=True) + EPS)
    h = (xf * inv * g_ref[...]).astype(jnp.bfloat16)
    o_gate = q_c + 2 * kv_c + d_ssm
    gl = jnp.dot(h, w_ref[:, o_gate:], preferred_element_type=jnp.float32) + bg_ref[...]
    gate_ref[...] = jax.nn.sigmoid(gl).astype(gate_ref.dtype)
    o = 0
    q_ref[...] = (jnp.dot(h, w_ref[:, o:o + q_c], preferred_element_type=jnp.float32)
                  * (1.0 / math.sqrt(HEAD_DIM))).astype(q_ref.dtype)
    o += q_c
    k_ref[...] = jnp.dot(h, w_ref[:, o:o + kv_c], preferred_element_type=jnp.float32).astype(k_ref.dtype)
    o += kv_c
    v_ref[...] = jnp.dot(h, w_ref[:, o:o + kv_c], preferred_element_type=jnp.float32).astype(v_ref.dtype)
    o += kv_c
    seg = x_ref.shape[0] // SUBLANES
    for s in range(0, d_ssm // LANES, 2):
        uu = jnp.dot(h, w_ref[:, o:o + 2 * LANES], preferred_element_type=jnp.float32)
        for half in range(2):
            for j in range(SUBLANES):
                u_ref[s + half, pl.ds(j, seg, stride=SUBLANES), :] = (
                    uu[j * seg:(j + 1) * seg, half * LANES:(half + 1) * LANES])
        o += 2 * LANES


def _in_proj(x2, norm_w, w_in, b_gate, d_attn, kv_cols, d_ssm):
    t, d = x2.shape
    gate_cols = b_gate.shape[-1]
    n_slab = d_ssm // LANES
    tm = TM_PROJ
    kern = functools.partial(_proj_kernel, cols=(d_attn, kv_cols, d_ssm))
    return pl.pallas_call(
        kern,
        grid=(t // tm,),
        in_specs=[
            pl.BlockSpec((tm, d), lambda i: (i, 0)),
            pl.BlockSpec((1, d), lambda i: (0, 0)),
            pl.BlockSpec(w_in.shape, lambda i: (0, 0), pipeline_mode=pl.Buffered(1)),
            pl.BlockSpec((1, gate_cols), lambda i: (0, 0)),
        ],
        out_specs=[
            pl.BlockSpec((tm, d_attn), lambda i: (i, 0)),
            pl.BlockSpec((tm, kv_cols), lambda i: (i, 0)),
            pl.BlockSpec((tm, kv_cols), lambda i: (i, 0)),
            pl.BlockSpec((n_slab, tm, LANES), lambda i: (0, i, 0)),
            pl.BlockSpec((tm, gate_cols), lambda i: (i, 0)),
        ],
        out_shape=[
            jax.ShapeDtypeStruct((t, d_attn), jnp.bfloat16),
            jax.ShapeDtypeStruct((t, kv_cols), jnp.bfloat16),
            jax.ShapeDtypeStruct((t, kv_cols), jnp.bfloat16),
            jax.ShapeDtypeStruct((n_slab, t, LANES), jnp.float32),
            jax.ShapeDtypeStruct((t, gate_cols), jnp.bfloat16),
        ],
        scratch_shapes=[pltpu.VMEM(w_in.shape, jnp.bfloat16)],
        compiler_params=_cparams("arbitrary"),
        name="in_proj",
    )(x2, norm_w, w_in, b_gate)


def _attn_kernel(sink_ref, q_ref, kp_ref, kc_ref, vp_ref, vc_ref, rep_ref, mask_ref, o_ref):
    i = pl.program_id(1)
    blk = ATTN_BLOCK
    hw = Q_PER_KV * HEAD_DIM
    rows = Q_PER_KV * blk
    head_of_row = lax.broadcasted_iota(jnp.int32, (rows, 1), 0) >> _log2(blk)
    lane_head_q = lax.broadcasted_iota(jnp.int32, (blk, hw), 1) >> _log2(HEAD_DIM)
    lane_head_v = lax.broadcasted_iota(jnp.int32, (2 * blk, hw), 1) >> _log2(HEAD_DIM)
    sinks = []
    for kh in range(N_KV_HEADS):
        sink = jnp.zeros((rows, 1), jnp.float32)
        for g in range(Q_PER_KV):
            sink = jnp.where(head_of_row == g, sink_ref[kh * Q_PER_KV + g], sink)
        sinks.append(sink)
    for qb in range(ATTN_QB):
        if qb == 0:
            k2 = jnp.concatenate([kp_ref[...], kc_ref[0:blk, :]], axis=0)
            v2 = jnp.concatenate([vp_ref[...], vc_ref[0:blk, :]], axis=0)
            bias = mask_ref[jnp.where(i == 0, 1, 0)]
        else:
            k2 = kc_ref[(qb - 1) * blk:(qb + 1) * blk, :]
            v2 = vc_ref[(qb - 1) * blk:(qb + 1) * blk, :]
            bias = mask_ref[0]
        for kh in range(N_KV_HEADS):
            rep = rep_ref[kh]
            k4 = jnp.dot(k2, rep, preferred_element_type=jnp.float32).astype(jnp.bfloat16)
            v4 = jnp.dot(v2, rep, preferred_element_type=jnp.float32).astype(jnp.bfloat16)
            qh = q_ref[qb * blk:(qb + 1) * blk, kh * hw:(kh + 1) * hw]
            qm = jnp.concatenate(
                [jnp.where(lane_head_q == g, qh, jnp.zeros_like(qh)) for g in range(Q_PER_KV)], axis=0)
            s = lax.dot_general(qm, k4, (((1,), (1,)), ((), ())), preferred_element_type=jnp.float32)
            s = s + bias
            sink = sinks[kh]
            m = jnp.maximum(jnp.max(s, axis=-1, keepdims=True), sink)
            p = jnp.exp(s - m)
            rinv = 1.0 / (jnp.sum(p, axis=-1, keepdims=True) + jnp.exp(sink - m))
            p = p.astype(jnp.bfloat16)
            p_cat = jnp.concatenate([p[g * blk:(g + 1) * blk, :] for g in range(Q_PER_KV)], axis=1)
            vm = jnp.concatenate(
                [jnp.where(lane_head_v == g, v4, jnp.zeros_like(v4)) for g in range(Q_PER_KV)], axis=0)
            o = jnp.dot(p_cat, vm, preferred_element_type=jnp.float32)
            scale = jnp.zeros((blk, hw), jnp.float32)
            for g in range(Q_PER_KV):
                scale = jnp.where(lane_head_q == g, rinv[g * blk:(g + 1) * blk, :], scale)
            o_ref[qb * blk:(qb + 1) * blk, kh * hw:(kh + 1) * hw] = (o * scale).astype(o_ref.dtype)


def _attention(q, k, v, sinks, b, l):
    d_attn = q.shape[-1]
    kv_cols = k.shape[-1]
    blk = ATTN_BLOCK
    tq = ATTN_QB * blk
    hw = Q_PER_KV * HEAD_DIM
    lane = jnp.arange(hw)[None, :]
    src = jnp.arange(kv_cols)[:, None]
    rep = jnp.stack([(src == kh * HEAD_DIM + (lane % HEAD_DIM)) for kh in range(N_KV_HEADS)]
                    ).astype(jnp.bfloat16)
    r = (jnp.arange(Q_PER_KV * blk) % blk)[:, None]
    c = jnp.arange(2 * blk)[None, :]
    band = (c > r) & (c <= r + blk)
    mask = jnp.where(jnp.stack([band, band & (c >= blk)]), 0.0, jnp.finfo(jnp.float32).min).astype(jnp.float32)
    q3 = q.reshape(b, l, d_attn)
    k3 = k.reshape(b, l, kv_cols)
    v3 = v.reshape(b, l, kv_cols)
    cur = lambda bi, i: (bi, i, 0)
    prev = lambda bi, i: (bi, jnp.maximum(ATTN_QB * i - 1, 0), 0)
    out = pl.pallas_call(
        _attn_kernel,
        grid=(b, l // tq),
        in_specs=[
            pl.BlockSpec(memory_space=pltpu.SMEM),
            pl.BlockSpec((None, tq, d_attn), cur),
            pl.BlockSpec((None, blk, kv_cols), prev),
            pl.BlockSpec((None, tq, kv_cols), cur),
            pl.BlockSpec((None, blk, kv_cols), prev),
            pl.BlockSpec((None, tq, kv_cols), cur),
            pl.BlockSpec(rep.shape, lambda bi, i: (0, 0, 0)),
            pl.BlockSpec(mask.shape, lambda bi, i: (0, 0, 0)),
        ],
        out_specs=pl.BlockSpec((None, tq, d_attn), cur),
        out_shape=jax.ShapeDtypeStruct((b, l, d_attn), jnp.bfloat16),
        compiler_params=_cparams("arbitrary", "arbitrary"),
        name="swa",
    )(sinks, q3, k3, k3, v3, v3, rep, mask)
    return out.reshape(b * l, d_attn)


def _ssm_kernel(u_ref, bmat_ref, cmat_ref, lam_ref, pw_ref, lamseg_ref, d_ref, y_ref, bu_ref, carry_ref):
    i = pl.program_id(1)
    n_slab = u_ref.shape[0]
    tm = u_ref.shape[1]
    seg = tm // SUBLANES
    ns = SSM_CHUNK_STATES
    npair = 2

    @pl.when(i == 0)
    def _():
        carry_ref[...] = jnp.zeros_like(carry_ref)

    sub = lax.broadcasted_iota(jnp.int32, (SUBLANES, ns), 0)
    for s0 in range(0, n_slab, npair):
        lam = []
        for q in range(npair):
            s = s0 + q
            bu_ref[q] = jnp.dot(u_ref[s].astype(jnp.bfloat16), bmat_ref[s], preferred_element_type=jnp.float32)
            lam.append((jnp.broadcast_to(lam_ref[s, 0:1, :], (SUBLANES, ns)),
                        jnp.broadcast_to(lam_ref[s, 1:2, :], (SUBLANES, ns))))

        def step(r, st):
            rows = pl.ds(r * SUBLANES, SUBLANES)
            out = []
            for q in range(npair):
                lr, li = lam[q]
                sr, si = st[q]
                nr = lr * sr - li * si + bu_ref[q, rows, 0:ns]
                ni = lr * si + li * sr + bu_ref[q, rows, ns:2 * ns]
                bu_ref[q, rows, 0:ns] = nr
                bu_ref[q, rows, ns:2 * ns] = ni
                out.append((nr, ni))
            return tuple(out)

        zero = jnp.zeros((SUBLANES, ns), jnp.float32)
        ends = ((zero, zero),) * npair
        for r in range(seg):
            ends = step(r, ends)

        for q in range(npair):
            s = s0 + q
            er, ei = ends[q]
            ar = lamseg_ref[s, 0:1, :]
            ai = lamseg_ref[s, 1:2, :]
            cr = carry_ref[s, 0:1, :]
            ci = carry_ref[s, 1:2, :]
            car = jnp.zeros((SUBLANES, ns), jnp.float32)
            cai = jnp.zeros((SUBLANES, ns), jnp.float32)
            for j in range(SUBLANES):
                car = jnp.where(sub == j, jnp.broadcast_to(cr, (SUBLANES, ns)), car)
                cai = jnp.where(sub == j, jnp.broadcast_to(ci, (SUBLANES, ns)), cai)
                ejr = jnp.sum(jnp.where(sub == j, er, 0.0), axis=0, keepdims=True)
                eji = jnp.sum(jnp.where(sub == j, ei, 0.0), axis=0, keepdims=True)
                cr, ci = ar * cr - ai * ci + ejr, ar * ci + ai * cr + eji
            carry_ref[s, 0:1, :] = cr
            carry_ref[s, 1:2, :] = ci

            ctr = jnp.broadcast_to(car[None], (seg, SUBLANES, ns)).reshape(tm, ns)
            cti = jnp.broadcast_to(cai[None], (seg, SUBLANES, ns)).reshape(tm, ns)
            pr = pw_ref[s, 0]
            pi = pw_ref[s, 1]
            st_r = bu_ref[q, :, 0:ns] + pr * ctr - pi * cti
            st_i = bu_ref[q, :, ns:2 * ns] + pr * cti + pi * ctr
            st = jnp.concatenate([st_r, st_i], axis=1).astype(jnp.bfloat16)
            y_ref[s] = jnp.dot(st, cmat_ref[s], preferred_element_type=jnp.float32) + d_ref[s] * u_ref[s]


def _ssm_tables(a_re, a_im, b_re, b_im, c_re, c_im, d_skip, log_dt, seg):
    f32 = jnp.float32
    g, p = a_re.shape
    c = b_re.shape[-1]
    ng = SSM_CHUNK_GROUPS
    n_slab = g // ng
    lam = lax.complex(a_re.astype(f32), a_im.astype(f32))
    dt = jnp.exp(log_dt.astype(f32))[:, None]
    lam_bar = jnp.exp(lam * dt)
    b_bar = ((lam_bar - 1.0) / lam)[:, :, None] * lax.complex(b_re.astype(f32), b_im.astype(f32))
    c_mat = lax.complex(c_re.astype(f32), c_im.astype(f32))
    eye = jnp.eye(ng, dtype=f32)

    def bdiag_b(m):
        m = m.reshape(n_slab, ng, p, c)
        return jnp.einsum('ab,kbpc->kacbp', eye, m).reshape(n_slab, ng * c, ng * p)

    def bdiag_c(m):
        m = m.reshape(n_slab, ng, c, p)
        return jnp.einsum('ab,kbcp->kbpac', eye, m).reshape(n_slab, ng * p, ng * c)

    bmat = jnp.concatenate([bdiag_b(jnp.real(b_bar)), bdiag_b(jnp.imag(b_bar))], axis=2).astype(jnp.bfloat16)
    cmat = jnp.concatenate([bdiag_c(jnp.real(c_mat)), -bdiag_c(jnp.imag(c_mat))], axis=1).astype(jnp.bfloat16)

    def slab_rows(z):
        z = z.reshape(n_slab, 1, ng * p)
        return jnp.concatenate([jnp.real(z), jnp.imag(z)], axis=1)

    lam_t = slab_rows(lam_bar)
    steps = jnp.arange(1, seg + 1, dtype=f32)
    pw = jnp.exp((lam * dt)[None] * steps[:, None, None])
    pw = pw.reshape(seg, n_slab, ng * p).transpose(1, 0, 2)
    pw = jnp.repeat(pw, SUBLANES, axis=1)
    pw_t = jnp.stack([jnp.real(pw), jnp.imag(pw)], axis=1)
    lamseg_t = slab_rows(jnp.exp(lam * dt * float(seg)))
    d_t = d_skip.astype(f32).reshape(n_slab, 1, ng * c)
    return bmat, cmat, lam_t, pw_t, lamseg_t, d_t


def _ssm(u4, tables, b, l):
    bmat, cmat, lam_t, pw_t, lamseg_t, d_t = tables
    n_slab, t, _ = u4.shape
    tm = TM_SSM
    nt = l // tm
    ns = SSM_CHUNK_STATES
    const = lambda nd: (lambda bi, i: (0,) * nd)
    return pl.pallas_call(
        _ssm_kernel,
        grid=(b, nt),
        in_specs=[
            pl.BlockSpec((n_slab, tm, LANES), lambda bi, i: (0, bi * nt + i, 0)),
            pl.BlockSpec(bmat.shape, const(3)),
            pl.BlockSpec(cmat.shape, const(3)),
            pl.BlockSpec(lam_t.shape, const(3)),
            pl.BlockSpec(pw_t.shape, const(4)),
            pl.BlockSpec(lamseg_t.shape, const(3)),
            pl.BlockSpec(d_t.shape, const(3)),
        ],
        out_specs=pl.BlockSpec((n_slab, tm, LANES), lambda bi, i: (0, bi * nt + i, 0)),
        out_shape=jax.ShapeDtypeStruct((n_slab, t, LANES), jnp.float32),
        scratch_shapes=[
            pltpu.VMEM((2, tm, 2 * ns), jnp.float32),
            pltpu.VMEM((n_slab, 2, ns), jnp.float32),
        ],
        compiler_params=_cparams("arbitrary", "arbitrary"),
        name="s5_scan",
    )(u4, bmat, cmat, lam_t, pw_t, lamseg_t, d_t)


def _merge_kernel(x_ref, attn_ref, y_ref, gate_ref, wglu32_ref, bglu_ref, wa32_ref, ws32_ref, wo32_ref,
                  nm_ref, wr_ref, br_ref, tri_ref, x1_ref, h2_ref, ids_ref, rw_ref, rank_ref, cnt_ref,
                  lg_ref, carry_ref, wglu_ref, wa_ref, ws_ref, wo_ref):
    i = pl.program_id(0)
    d = x_ref.shape[1]
    tm = x_ref.shape[0]

    @pl.when(i == 0)
    def _():
        lg_ref[...] = jnp.zeros_like(lg_ref)
        carry_ref[...] = jnp.zeros_like(carry_ref)
        for w32, wb in ((wglu32_ref, wglu_ref), (wa32_ref, wa_ref), (ws32_ref, ws_ref), (wo32_ref, wo_ref)):
            wb[...] = w32[...].astype(jnp.bfloat16)

    logits = lg_ref[...]

    seg = tm // SUBLANES
    nh = MERGE_SPLIT
    hm = tm // nh
    js = SUBLANES // nh
    rows = [slice(g * hm, (g + 1) * hm) for g in range(nh)]
    y = [jnp.concatenate(
        [jnp.concatenate([y_ref[s, pl.ds(j, seg, stride=SUBLANES), :] for j in range(g * js, (g + 1) * js)], axis=0)
         for s in range(y_ref.shape[0])], axis=1) for g in range(nh)]
    z = [jax.nn.gelu(v) for v in y]
    zg = [jnp.dot(v.astype(jnp.bfloat16), wglu_ref[...], preferred_element_type=jnp.float32) + bglu_ref[...]
          for v in z]
    z = [v * jax.nn.sigmoid(w) for v, w in zip(z, zg)]
    a = [jnp.dot(attn_ref[r, :], wa_ref[...], preferred_element_type=jnp.float32) for r in rows]
    sb = [jnp.dot(v.astype(jnp.bfloat16), ws_ref[...], preferred_element_type=jnp.float32) for v in z]
    merged = [gate_ref[r, 0:d].astype(jnp.float32) * p + gate_ref[r, d:2 * d].astype(jnp.float32) * q
              for r, p, q in zip(rows, a, sb)]
    x1 = [x_ref[r, :] + jnp.dot(m.astype(jnp.bfloat16), wo_ref[...], preferred_element_type=jnp.float32)
          for r, m in zip(rows, merged)]
    for g in range(nh):
        x1_ref[rows[g], :] = x1[g]
        inv = lax.rsqrt(jnp.mean(x1[g] * x1[g], axis=-1, keepdims=True) + EPS)
        h2 = x1[g] * inv * nm_ref[...]
        h_hi = h2.astype(jnp.bfloat16)
        _store_rows(h2_ref, g * hm, _pack_bf16_pairs(h_hi.astype(jnp.float32)))
        h_lo = (h2 - h_hi.astype(jnp.float32)).astype(jnp.bfloat16)
        hh = jnp.dot(h_hi, wr_ref[...], preferred_element_type=jnp.float32)
        lh = jnp.dot(h_lo, wr_ref[:, 0:LANES], preferred_element_type=jnp.float32)
        lg_ref[rows[g], :] = hh[:, 0:LANES] + hh[:, LANES:2 * LANES] + lh + br_ref[...]

    lane = lax.broadcasted_iota(jnp.int32, (tm, LANES), 1)
    ninf = -jnp.inf
    gl = jnp.where((lane >= N_EXPERTS) & (lane < N_EXPERTS + N_EXPERT_GROUPS), logits, ninf)
    gmax = jnp.max(gl, axis=-1, keepdims=True)
    gidx = jnp.min(jnp.where(gl == gmax, lane - N_EXPERTS, LANES), axis=-1, keepdims=True)
    group_p = 1.0 / jnp.sum(jnp.exp(gl - gmax), axis=-1, keepdims=True)
    el = jnp.where((lane < N_EXPERTS) & ((lane >> _log2(EXPERTS_PER_GROUP)) == gidx), logits, ninf)
    m1 = jnp.max(el, axis=-1, keepdims=True)
    i1 = jnp.min(jnp.where(el == m1, lane, LANES), axis=-1, keepdims=True)
    el2 = jnp.where(lane == i1, ninf, el)
    m2 = jnp.max(el2, axis=-1, keepdims=True)
    i2 = jnp.min(jnp.where(el2 == m2, lane, LANES), axis=-1, keepdims=True)
    e2 = jnp.exp(m2 - m1)
    w1 = group_p / (1.0 + e2)
    w2 = group_p * e2 / (1.0 + e2)
    ids_ref[...] = jnp.transpose(jnp.where(lane == 0, i1, jnp.where(lane == 1, i2, 0)))[0:SUBLANES, :]
    rw_ref[...] = jnp.where(lane == 0, w1, jnp.where(lane == 1, w2, 0.0))

    oh0 = lane == i1
    oh1 = lane == i2
    live = jnp.where(i > 0, 1.0, 0.0)
    oh = (oh0.astype(jnp.float32) + oh1.astype(jnp.float32)) * live
    cum = jnp.dot(tri_ref[...], oh.astype(jnp.bfloat16), preferred_element_type=jnp.float32) + carry_ref[...]
    r0 = jnp.sum(jnp.where(oh0, cum, 0.0), axis=-1, keepdims=True)
    r1 = jnp.sum(jnp.where(oh1, cum, 0.0), axis=-1, keepdims=True)
    rank_ref[...] = jnp.transpose(
        jnp.where(lane == 0, r0, jnp.where(lane == 1, r1, 0.0)).astype(jnp.int32))[0:SUBLANES, :]
    carry_ref[...] = carry_ref[...] + jnp.sum(oh, axis=0, keepdims=True)
    cnt_ref[...] = carry_ref[...].astype(jnp.int32)


def _merge(x2, attn, y4, gates, w_glu, b_glu, w_a, w_s, w_o, norm_moe, w_router, b_router):
    t, d = x2.shape
    tm = TM_MERGE
    nt = t // tm
    n_slab = y4.shape[0]
    tri = (jnp.arange(tm)[None, :] < jnp.arange(tm)[:, None]).astype(jnp.bfloat16)
    full = lambda a: pl.BlockSpec(a.shape, lambda i: (0,) * a.ndim, pipeline_mode=pl.Buffered(1))
    cur = lambda i: (jnp.minimum(i, nt - 1), 0)
    prv = lambda i: (jnp.maximum(i - 1, 0), 0)
    row = lambda c, m=cur: pl.BlockSpec((tm, c), m)
    slots = pl.BlockSpec((SUBLANES, tm), lambda i: (0, jnp.maximum(i - 1, 0)))
    return pl.pallas_call(
        _merge_kernel,
        grid=(nt + 1,),
        in_specs=[
            row(d), row(attn.shape[1]),
            pl.BlockSpec((n_slab, tm, LANES), lambda i: (0, jnp.minimum(i, nt - 1), 0)),
            row(gates.shape[1]),
            full(w_glu), full(b_glu), full(w_a), full(w_s), full(w_o), full(norm_moe),
            full(w_router), full(b_router), full(tri),
        ],
        out_specs=[row(d), pl.BlockSpec((tm * ROW_SUB, LANES), cur), slots, row(LANES, prv),
                   slots, pl.BlockSpec((1, LANES), lambda i: (0, 0))],
        out_shape=[
            jax.ShapeDtypeStruct((t, d), jnp.float32),
            jax.ShapeDtypeStruct((t * ROW_SUB, LANES), jnp.uint32),
            jax.ShapeDtypeStruct((SUBLANES, t), jnp.int32),
            jax.ShapeDtypeStruct((t, LANES), jnp.float32),
            jax.ShapeDtypeStruct((SUBLANES, t), jnp.int32),
            jax.ShapeDtypeStruct((1, LANES), jnp.int32),
        ],
        scratch_shapes=[pltpu.VMEM((tm, LANES), jnp.float32), pltpu.VMEM((1, LANES), jnp.float32)]
        + [pltpu.VMEM(w.shape, jnp.bfloat16) for w in (w_glu, w_a, w_s, w_o)],
        compiler_params=_cparams("arbitrary"),
        name="merge_router",
    )(x2, attn, y4, gates, w_glu, b_glu, w_a, w_s, w_o, norm_moe, w_router, b_router, tri)


def _inverse_kernel(pos_ref, src_ref):
    n_slots = pos_ref.shape[0]
    t = n_slots // TOP_K

    def body(tok, c):
        for k in range(TOP_K):
            src_ref[pos_ref[k * t + tok]] = tok * ROW_SUB
        return c

    lax.fori_loop(0, t, body, 0, unroll=16)


def _inverse(pos1d):
    n_slots = pos1d.shape[0]
    smem = pl.BlockSpec(memory_space=pltpu.SMEM)
    return pl.pallas_call(
        _inverse_kernel,
        in_specs=[smem],
        out_specs=smem,
        out_shape=jax.ShapeDtypeStruct((n_slots,), jnp.int32),
        name="route_inverse",
    )(pos1d)


def _expert_kernel(ta_ref, tn_ref, tnew_ref, sexp_ref, slo_ref, shi_ref, meta_ref, src_ref,
                   h_ref, wg_hbm, wu_hbm, wd_hbm, y_ref,
                   xbuf_ref, wgs_ref, wus_ref, wds_ref, wgb_ref, wub_ref, wdb_ref, gsem, wsem):
    j = pl.program_id(0)
    n_tiles = pl.num_programs(0)
    last = n_tiles - 1
    tm = TM_EXPERT
    rs = ROW_SUB
    nb = ROW_BUFS
    n_used = meta_ref[0]
    slot = j % nb
    a = ta_ref[j]

    def gather(tile, sl, start):
        _row_gather(h_ref, src_ref, tile * tm, xbuf_ref, sl * tm, tm, gsem.at[sl], start)

    def weights(m, start):
        sl = m % 2
        e = sexp_ref[m]
        for hbm, stage in ((wg_hbm, wgs_ref), (wu_hbm, wus_ref), (wd_hbm, wds_ref)):
            cp = pltpu.make_async_copy(hbm.at[e], stage.at[sl], wsem.at[sl])
            cp.start() if start else cp.wait()

    def new_expert(m):
        weights(m, False)
        sl = m % 2
        wgb_ref[...] = wgs_ref[sl].astype(jnp.bfloat16)
        wub_ref[...] = wus_ref[sl].astype(jnp.bfloat16)
        wdb_ref[...] = wds_ref[sl].astype(jnp.bfloat16)

        @pl.when(m + 1 < n_used)
        def _():
            weights(m + 1, True)

    def expert_out(xp):
        x = _unpack_bf16_pairs(xp).astype(jnp.bfloat16)
        hg = jnp.dot(x, wgb_ref[...], preferred_element_type=jnp.float32)
        hu = jnp.dot(x, wub_ref[...], preferred_element_type=jnp.float32)
        act = (jax.nn.silu(hg) * hu).astype(jnp.bfloat16)
        y = jnp.dot(act, wdb_ref[...], preferred_element_type=jnp.float32)
        return _pack_bf16_pairs(y.astype(jnp.bfloat16).astype(jnp.float32))

    @pl.when(j == 0)
    def _():
        weights(0, True)
        for b in range(nb - 1):
            gather(b, b, True)

    @pl.when(tnew_ref[j] == 1)
    def _():
        new_expert(a)

    gather(j, slot, False)
    xp = _load_rows(xbuf_ref, slot * tm, tm)
    gather(jnp.minimum(j + nb - 1, last), (j + nb - 1) % nb, True)
    _store_rows(y_ref, 0, expert_out(xp))

    def extra(i, carry):
        m = a + i
        new_expert(m)
        yp = expert_out(_load_rows(xbuf_ref, slot * tm, tm))
        row = lax.broadcasted_iota(jnp.int32, (tm, LANES), 0) + j * tm
        mine = (row >= slo_ref[m]) & (row < shi_ref[m])
        _store_rows(y_ref, 0, yp, mine)
        return carry

    lax.fori_loop(1, tn_ref[j], extra, 0)

    @pl.when(j == last)
    def _():
        for b in range(1, nb):
            gather(j, (j + b) % nb, False)


def _experts(meta, src_tok, h_fat, w_gate, w_up, w_down):
    n_slots = src_tok.shape[0]
    ne, d, dff = w_gate.shape
    tm = TM_EXPERT
    rs = ROW_SUB
    assert n_slots // tm >= ROW_BUFS
    any_spec = pl.BlockSpec(memory_space=pl.ANY)
    grid_spec = pltpu.PrefetchScalarGridSpec(
        num_scalar_prefetch=8,
        grid=(n_slots // tm,),
        in_specs=[any_spec, any_spec, any_spec, any_spec],
        out_specs=pl.BlockSpec((tm * rs, LANES), lambda j, *_: (j, 0)),
        scratch_shapes=[
            pltpu.VMEM((ROW_BUFS * tm * rs, LANES), jnp.uint32),
            pltpu.VMEM((2, d, dff), jnp.float32),
            pltpu.VMEM((2, d, dff), jnp.float32),
            pltpu.VMEM((2, dff, d), jnp.float32),
            pltpu.VMEM((d, dff), jnp.bfloat16),
            pltpu.VMEM((d, dff), jnp.bfloat16),
            pltpu.VMEM((dff, d), jnp.bfloat16),
            pltpu.SemaphoreType.DMA((ROW_BUFS,)),
            pltpu.SemaphoreType.DMA((2,)),
        ],
    )
    return pl.pallas_call(
        _expert_kernel,
        grid_spec=grid_spec,
        out_shape=jax.ShapeDtypeStruct((n_slots * rs, LANES), jnp.uint32),
        compiler_params=_cparams("arbitrary"),
        name="moe_experts",
    )(*meta, src_tok, h_fat, w_gate, w_up, w_down)


def _combine_kernel(pos_ref, x1_ref, rw_ref, nf_ref, ys_ref, o_ref, buf_ref, sem):
    i = pl.program_id(0)
    n = pl.num_programs(0)
    last = n - 1
    tm = x1_ref.shape[0]
    t = n * tm
    nb = ROW_BUFS
    slot = i % nb

    def gather(tile, sl, start):
        for k in range(TOP_K):
            _row_gather(ys_ref, pos_ref, k * t + tile * tm, buf_ref, (sl * TOP_K + k) * tm, tm, sem.at[sl], start)

    @pl.when(i == 0)
    def _():
        for b in range(nb - 1):
            gather(b, b, True)

    gather(i, slot, False)
    y0 = _unpack_bf16_pairs(_load_rows(buf_ref, (slot * TOP_K) * tm, tm))
    y1 = _unpack_bf16_pairs(_load_rows(buf_ref, (slot * TOP_K + 1) * tm, tm))
    gather(jnp.minimum(i + nb - 1, last), (i + nb - 1) % nb, True)
    rw = rw_ref[...]
    x = x1_ref[...] + rw[:, 0:1] * y0 + rw[:, 1:2] * y1
    inv = lax.rsqrt(jnp.mean(x * x, axis=-1, keepdims=True) + EPS)
    o_ref[...] = x * inv * nf_ref[...]

    @pl.when(i == last)
    def _():
        for b in range(1, nb):
            gather(i, (i + b) % nb, False)


def _combine(pos_rows, x1, rw, norm_final, ys):
    t, d = x1.shape
    tm = TM_MOVE
    assert t // tm >= ROW_BUFS
    return pl.pallas_call(
        _combine_kernel,
        grid=(t // tm,),
        in_specs=[
            pl.BlockSpec(memory_space=pltpu.SMEM),
            pl.BlockSpec((tm, d), lambda i: (i, 0)),
            pl.BlockSpec((tm, LANES), lambda i: (i, 0)),
            pl.BlockSpec((1, d), lambda i: (0, 0)),
            pl.BlockSpec(memory_space=pl.ANY),
        ],
        out_specs=pl.BlockSpec((tm, d), lambda i: (i, 0)),
        out_shape=jax.ShapeDtypeStruct((t, d), jnp.float32),
        scratch_shapes=[pltpu.VMEM((ROW_BUFS * TOP_K * tm * ROW_SUB, LANES), jnp.uint32),
                        pltpu.SemaphoreType.DMA((ROW_BUFS,))],
        compiler_params=_cparams("arbitrary"),
        name="moe_combine",
    )(pos_rows, x1, rw, norm_final, ys)


def _expert_meta(counts, n_slots):
    tm = TM_EXPERT
    n_tiles = n_slots // tm
    i32 = jnp.int32
    ends = jnp.cumsum(counts).astype(i32)
    starts = ends - counts
    used = counts > 0
    n_used = jnp.sum(used).astype(i32)
    seq_of = jnp.cumsum(used).astype(i32) - 1
    m_idx = jnp.arange(N_EXPERTS, dtype=i32)
    pick = used[None, :] & (seq_of[None, :] == m_idx[:, None])
    s_exp = jnp.sum(jnp.where(pick, m_idx[None, :], 0), axis=1).astype(i32)
    s_lo = jnp.sum(jnp.where(pick, starts[None, :], 0), axis=1).astype(i32)
    s_hi = jnp.sum(jnp.where(pick, ends[None, :], 0), axis=1).astype(i32)
    row0 = jnp.arange(n_tiles, dtype=i32)[:, None] * tm
    t_a = jnp.sum(used[None, :] & (ends[None, :] <= row0), axis=1).astype(i32)
    t_b = jnp.sum(used[None, :] & (starts[None, :] < row0 + tm), axis=1).astype(i32) - 1
    t_new = jnp.any(used[None, :] & (starts[None, :] == row0), axis=1).astype(i32)
    return t_a, (t_b - t_a + 1).astype(i32), t_new, s_exp, s_lo, s_hi, n_used.reshape(1)


def _moe(x1, h_fat, ids, rw, rank, cnt, w_gate, w_up, w_down, norm_final):
    t, d = x1.shape
    counts = cnt[0, :N_EXPERTS]
    offs = (jnp.cumsum(counts) - counts).astype(jnp.int32)
    onehot = ids[:TOP_K, :, None] == jnp.arange(N_EXPERTS, dtype=jnp.int32)
    pos1d = (jnp.sum(jnp.where(onehot, offs, 0), axis=-1) + rank[:TOP_K]).reshape(-1)
    src_tok = _inverse(pos1d)
    meta = _expert_meta(counts, t * TOP_K)
    ys = _experts(meta, src_tok, h_fat, w_gate, w_up, w_down)
    return _combine(pos1d * ROW_SUB, x1, rw, norm_final, ys)


def kernel(x, norm_mix, w_in, b_gate, attn_sinks, ssm_a_re, ssm_a_im, ssm_b_re, ssm_b_im, ssm_c_re, ssm_c_im, ssm_d, ssm_log_dt, w_glu, b_glu, w_attn_branch, w_ssm_branch, w_out, norm_moe, w_router_group, b_router_group, w_router_expert, b_router_expert, w_expert_gate, w_expert_up, w_expert_down, norm_final):
    b, l, d = x.shape
    depth = w_in.shape[0]
    assert depth == 1, "the final norm is fused into the last layer's combine kernel"
    d_attn = N_HEADS * HEAD_DIM
    kv_cols = N_KV_HEADS * HEAD_DIM
    d_ssm = ssm_d.shape[-1]
    bf16 = jnp.bfloat16
    x2 = x.reshape(b * l, d)
    assert TM_PROJ == TM_SSM == TM_MERGE and l % TM_SSM == 0
    i = 0
    q, k, v, u4, gates = _in_proj(x2, norm_mix[i][None], w_in[i], b_gate[i][None],
                                  d_attn, kv_cols, d_ssm)
    attn = _attention(q, k, v, attn_sinks[i], b, l)
    tables = _ssm_tables(ssm_a_re[i], ssm_a_im[i], ssm_b_re[i], ssm_b_im[i], ssm_c_re[i], ssm_c_im[i],
                         ssm_d[i], ssm_log_dt[i], SSM_SEG)
    y4 = _ssm(u4, tables, b, l)
    pad = LANES - N_EXPERTS - N_EXPERT_GROUPS
    w_router = jnp.concatenate([w_router_expert[i], w_router_group[i], jnp.zeros((d, pad), jnp.float32)], axis=1)
    w_r_hi = w_router.astype(bf16)
    w_router = jnp.concatenate([w_r_hi, (w_router - w_r_hi.astype(jnp.float32)).astype(bf16)], axis=1)
    b_router = jnp.concatenate([b_router_expert[i], b_router_group[i], jnp.zeros((pad,), jnp.float32)])[None]
    x1, h_fat, ids, rw, rank, cnt = _merge(x2, attn, y4, gates, w_glu[i], b_glu[i][None],
                                w_attn_branch[i], w_ssm_branch[i],
                                w_out[i], norm_moe[i][None], w_router, b_router)
    out = _moe(x1, h_fat, ids, rw, rank, cnt, w_expert_gate[i], w_expert_up[i], w_expert_down[i], norm_final[None])
    return out.reshape(b, l, d)
```

```python
import functools
import math

import jax
import jax.numpy as jnp
from jax import lax
from jax.experimental import pallas as pl
from jax.experimental.pallas import tpu as pltpu

EPS = 1e-6
HEAD_DIM = 64
N_HEADS = 8
N_KV_HEADS = 2
Q_PER_KV = N_HEADS // N_KV_HEADS
ATTN_BLOCK = 128
ATTN_QB = 8
SSM_GROUP = 16
SSM_STATE = 64
N_EXPERT_GROUPS = 4
EXPERTS_PER_GROUP = 8
N_EXPERTS = N_EXPERT_GROUPS * EXPERTS_PER_GROUP
TOP_K = 2

LANES = 128
SUBLANES = 8
SSM_CHUNK_GROUPS = LANES // SSM_GROUP
SSM_CHUNK_STATES = SSM_CHUNK_GROUPS * SSM_STATE

TM_PROJ = 512
PROJ_SPLIT = 2
TM_SSM = 512
SSM_SEG = TM_SSM // SUBLANES
TM_MERGE = 512
MERGE_SPLIT = 2
TM_MOVE = 512
TM_EXPERT = 256
ROW_SUB = 4
ROW_BUFS = 4
VMEM_LIMIT = 56 * 1024 * 1024


def _cparams(*sem):
    return pltpu.CompilerParams(dimension_semantics=sem, vmem_limit_bytes=VMEM_LIMIT)


def _log2(n):
    assert n > 0 and n & (n - 1) == 0, n
    return n.bit_length() - 1


def _pack_bf16_pairs(x):
    half = x.shape[1] // 2
    bits = lax.bitcast_convert_type(x, jnp.uint32)
    return (bits[:, :half] & jnp.uint32(0xFFFF0000)) | (bits[:, half:] >> 16)


def _unpack_bf16_pairs(p):
    hi = lax.bitcast_convert_type(p & jnp.uint32(0xFFFF0000), jnp.float32)
    lo = lax.bitcast_convert_type(p << 16, jnp.float32)
    return jnp.concatenate([hi, lo], axis=1)


def _load_rows(ref, first, n):
    return jnp.concatenate(
        [ref[pl.ds(first * ROW_SUB + c, n, stride=ROW_SUB), :] for c in range(ROW_SUB)], axis=1)


def _store_rows(ref, first, val, mask=None):
    n = val.shape[0]
    for c in range(ROW_SUB):
        idx = pl.ds(first * ROW_SUB + c, n, stride=ROW_SUB)
        v = val[:, c * LANES:(c + 1) * LANES]
        ref[idx, :] = v if mask is None else jnp.where(mask, v, ref[idx, :])


def _row_gather(src_hbm, idx_ref, idx0, dst_ref, dst0, n, sem, start):
    rs = ROW_SUB
    if start:
        for r in range(n):
            pltpu.make_async_copy(src_hbm.at[pl.ds(pl.multiple_of(idx_ref[idx0 + r], rs), rs)],
                                  dst_ref.at[pl.ds(pl.multiple_of((dst0 + r) * rs, rs), rs)],
                                  sem).start(priority=r % 2)
    else:
        pltpu.make_async_copy(src_hbm.at[pl.ds(0, n * rs)],
                              dst_ref.at[pl.ds(pl.multiple_of(dst0 * rs, rs), n * rs)], sem).wait()


def _proj_kernel(x_ref, g_ref, w32_ref, bg_ref, q_ref, k_ref, v_ref, u_ref, gate_ref, w_ref, *, cols):
    q_c, kv_c, d_ssm = cols

    @pl.when(pl.program_id(0) == 0)
    def _():
        w_ref[...] = w32_ref[...].astype(jnp.bfloat16)

    tm = x_ref.shape[0]
    nh = PROJ_SPLIT
    hm = tm // nh
    seg = tm // SUBLANES
    js = SUBLANES // nh
    rows = [slice(g * hm, (g + 1) * hm) for g in range(nh)]
    h = []
    for r in rows:
        xf = x_ref[r, :]
        inv = lax.rsqrt(jnp.mean(xf * xf, axis=-1, keepdims=True) + EPS)
        h.append((xf * inv * g_ref[...]).astype(jnp.bfloat16))
    o_gate = q_c + 2 * kv_c + d_ssm
    for g in range(nh):
        gl = jnp.dot(h[g], w_ref[:, o_gate:], preferred_element_type=jnp.float32) + bg_ref[...]
        gate_ref[rows[g], :] = jax.nn.sigmoid(gl).astype(gate_ref.dtype)
    o = 0
    for g in range(nh):
        q_ref[rows[g], :] = (jnp.dot(h[g], w_ref[:, o:o + q_c], preferred_element_type=jnp.float32)
                             * (1.0 / math.sqrt(HEAD_DIM))).astype(q_ref.dtype)
    o += q_c
    for g in range(nh):
        k_ref[rows[g], :] = jnp.dot(h[g], w_ref[:, o:o + kv_c], preferred_element_type=jnp.float32).astype(k_ref.dtype)
    o += kv_c
    for g in range(nh):
        v_ref[rows[g], :] = jnp.dot(h[g], w_ref[:, o:o + kv_c], preferred_element_type=jnp.float32).astype(v_ref.dtype)
    o += kv_c
    for s in range(0, d_ssm // LANES, 2):
        for g in range(nh):
            uu = jnp.dot(h[g], w_ref[:, o:o + 2 * LANES], preferred_element_type=jnp.float32)
            for half in range(2):
                for j in range(js):
                    u_ref[s + half, pl.ds(g * js + j, seg, stride=SUBLANES), :] = (
                        uu[j * seg:(j + 1) * seg, half * LANES:(half + 1) * LANES])
        o += 2 * LANES


def _in_proj(x2, norm_w, w_in, b_gate, d_attn, kv_cols, d_ssm):
    t, d = x2.shape
    gate_cols = b_gate.shape[-1]
    n_slab = d_ssm // LANES
    tm = TM_PROJ
    kern = functools.partial(_proj_kernel, cols=(d_attn, kv_cols, d_ssm))
    return pl.pallas_call(
        kern,
        grid=(t // tm,),
        in_specs=[
            pl.BlockSpec((tm, d), lambda i: (i, 0)),
            pl.BlockSpec((1, d), lambda i: (0, 0)),
            pl.BlockSpec(w_in.shape, lambda i: (0, 0), pipeline_mode=pl.Buffered(1)),
            pl.BlockSpec((1, gate_cols), lambda i: (0, 0)),
        ],
        out_specs=[
            pl.BlockSpec((tm, d_attn), lambda i: (i, 0)),
            pl.BlockSpec((tm, kv_cols), lambda i: (i, 0)),
            pl.BlockSpec((tm, kv_cols), lambda i: (i, 0)),
            pl.BlockSpec((n_slab, tm, LANES), lambda i: (0, i, 0)),
            pl.BlockSpec((tm, gate_cols), lambda i: (i, 0)),
        ],
        out_shape=[
            jax.ShapeDtypeStruct((t, d_attn), jnp.bfloat16),
            jax.ShapeDtypeStruct((t, kv_cols), jnp.bfloat16),
            jax.ShapeDtypeStruct((t, kv_cols), jnp.bfloat16),
            jax.ShapeDtypeStruct((n_slab, t, LANES), jnp.float32),
            jax.ShapeDtypeStruct((t, gate_cols), jnp.bfloat16),
        ],
        scratch_shapes=[pltpu.VMEM(w_in.shape, jnp.bfloat16)],
        compiler_params=_cparams("arbitrary"),
        name="in_proj",
    )(x2, norm_w, w_in, b_gate)


def _attn_kernel(sink_ref, q_ref, kp_ref, kc_ref, vp_ref, vc_ref, rep_ref, mask_ref, o_ref):
    i = pl.program_id(1)
    blk = ATTN_BLOCK
    hw = Q_PER_KV * HEAD_DIM
    rows = Q_PER_KV * blk
    head_of_row = lax.broadcasted_iota(jnp.int32, (rows, 1), 0) >> _log2(blk)
    lane_head_q = lax.broadcasted_iota(jnp.int32, (blk, hw), 1) >> _log2(HEAD_DIM)
    lane_head_v = lax.broadcasted_iota(jnp.int32, (2 * blk, hw), 1) >> _log2(HEAD_DIM)
    sinks = []
    for kh in range(N_KV_HEADS):
        sink = jnp.zeros((rows, 1), jnp.float32)
        for g in range(Q_PER_KV):
            sink = jnp.where(head_of_row == g, sink_ref[kh * Q_PER_KV + g], sink)
        sinks.append(sink)
    for qb in range(ATTN_QB):
        if qb == 0:
            k2 = jnp.concatenate([kp_ref[...], kc_ref[0:blk, :]], axis=0)
            v2 = jnp.concatenate([vp_ref[...], vc_ref[0:blk, :]], axis=0)
            bias = mask_ref[jnp.where(i == 0, 1, 0)]
        else:
            k2 = kc_ref[(qb - 1) * blk:(qb + 1) * blk, :]
            v2 = vc_ref[(qb - 1) * blk:(qb + 1) * blk, :]
            bias = mask_ref[0]
        for kh in range(N_KV_HEADS):
            rep = rep_ref[kh]
            k4 = jnp.dot(k2, rep, preferred_element_type=jnp.float32).astype(jnp.bfloat16)
            v4 = jnp.dot(v2, rep, preferred_element_type=jnp.float32).astype(jnp.bfloat16)
            qh = q_ref[qb * blk:(qb + 1) * blk, kh * hw:(kh + 1) * hw]
            qm = jnp.concatenate(
                [jnp.where(lane_head_q == g, qh, jnp.zeros_like(qh)) for g in range(Q_PER_KV)], axis=0)
            s = lax.dot_general(qm, k4, (((1,), (1,)), ((), ())), preferred_element_type=jnp.float32)
            s = s + bias
            sink = sinks[kh]
            m = jnp.maximum(jnp.max(s, axis=-1, keepdims=True), sink)
            p = jnp.exp(s - m)
            rinv = 1.0 / (jnp.sum(p, axis=-1, keepdims=True) + jnp.exp(sink - m))
            p = p.astype(jnp.bfloat16)
            p_cat = jnp.concatenate([p[g * blk:(g + 1) * blk, :] for g in range(Q_PER_KV)], axis=1)
            vm = jnp.concatenate(
                [jnp.where(lane_head_v == g, v4, jnp.zeros_like(v4)) for g in range(Q_PER_KV)], axis=0)
            o = jnp.dot(p_cat, vm, preferred_element_type=jnp.float32)
            scale = jnp.zeros((blk, hw), jnp.float32)
            for g in range(Q_PER_KV):
                scale = jnp.where(lane_head_q == g, rinv[g * blk:(g + 1) * blk, :], scale)
            o_ref[qb * blk:(qb + 1) * blk, kh * hw:(kh + 1) * hw] = (o * scale).astype(o_ref.dtype)


def _attention(q, k, v, sinks, b, l):
    d_attn = q.shape[-1]
    kv_cols = k.shape[-1]
    blk = ATTN_BLOCK
    tq = ATTN_QB * blk
    hw = Q_PER_KV * HEAD_DIM
    lane = jnp.arange(hw)[None, :]
    src = jnp.arange(kv_cols)[:, None]
    rep = jnp.stack([(src == kh * HEAD_DIM + (lane % HEAD_DIM)) for kh in range(N_KV_HEADS)]
                    ).astype(jnp.bfloat16)
    r = (jnp.arange(Q_PER_KV * blk) % blk)[:, None]
    c = jnp.arange(2 * blk)[None, :]
    band = (c > r) & (c <= r + blk)
    mask = jnp.where(jnp.stack([band, band & (c >= blk)]), 0.0, jnp.finfo(jnp.float32).min).astype(jnp.float32)
    q3 = q.reshape(b, l, d_attn)
    k3 = k.reshape(b, l, kv_cols)
    v3 = v.reshape(b, l, kv_cols)
    cur = lambda bi, i: (bi, i, 0)
    prev = lambda bi, i: (bi, jnp.maximum(ATTN_QB * i - 1, 0), 0)
    out = pl.pallas_call(
        _attn_kernel,
        grid=(b, l // tq),
        in_specs=[
            pl.BlockSpec(memory_space=pltpu.SMEM),
            pl.BlockSpec((None, tq, d_attn), cur),
            pl.BlockSpec((None, blk, kv_cols), prev),
            pl.BlockSpec((None, tq, kv_cols), cur),
            pl.BlockSpec((None, blk, kv_cols), prev),
            pl.BlockSpec((None, tq, kv_cols), cur),
            pl.BlockSpec(rep.shape, lambda bi, i: (0, 0, 0)),
            pl.BlockSpec(mask.shape, lambda bi, i: (0, 0, 0)),
        ],
        out_specs=pl.BlockSpec((None, tq, d_attn), cur),
        out_shape=jax.ShapeDtypeStruct((b, l, d_attn), jnp.bfloat16),
        compiler_params=_cparams("arbitrary", "arbitrary"),
        name="swa",
    )(sinks, q3, k3, k3, v3, v3, rep, mask)
    return out.reshape(b * l, d_attn)


def _ssm_kernel(u_ref, bmat_ref, cmat_ref, lam_ref, pw_ref, lamseg_ref, d_ref, y_ref, bu_ref, carry_ref):
    i = pl.program_id(1)
    n_slab = u_ref.shape[0]
    tm = u_ref.shape[1]
    seg = tm // SUBLANES
    ns = SSM_CHUNK_STATES
    npair = 2

    @pl.when(i == 0)
    def _():
        carry_ref[...] = jnp.zeros_like(carry_ref)

    sub = lax.broadcasted_iota(jnp.int32, (SUBLANES, ns), 0)
    for s0 in range(0, n_slab, npair):
        lam = []
        for q in range(npair):
            s = s0 + q
            bu_ref[q] = jnp.dot(u_ref[s].astype(jnp.bfloat16), bmat_ref[s], preferred_element_type=jnp.float32)
            lam.append((jnp.broadcast_to(lam_ref[s, 0:1, :], (SUBLANES, ns)),
                        jnp.broadcast_to(lam_ref[s, 1:2, :], (SUBLANES, ns))))

        def step(r, st):
            rows = pl.ds(r * SUBLANES, SUBLANES)
            out = []
            for q in range(npair):
                lr, li = lam[q]
                sr, si = st[q]
                nr = lr * sr - li * si + bu_ref[q, rows, 0:ns]
                ni = lr * si + li * sr + bu_ref[q, rows, ns:2 * ns]
                bu_ref[q, rows, 0:ns] = nr
                bu_ref[q, rows, ns:2 * ns] = ni
                out.append((nr, ni))
            return tuple(out)

        zero = jnp.zeros((SUBLANES, ns), jnp.float32)
        ends = ((zero, zero),) * npair
        for r in range(seg):
            ends = step(r, ends)

        for q in range(npair):
            s = s0 + q
            er, ei = ends[q]
            ar = lamseg_ref[s, 0:1, :]
            ai = lamseg_ref[s, 1:2, :]
            cr = carry_ref[s, 0:1, :]
            ci = carry_ref[s, 1:2, :]
            car = jnp.zeros((SUBLANES, ns), jnp.float32)
            cai = jnp.zeros((SUBLANES, ns), jnp.float32)
            for j in range(SUBLANES):
                car = jnp.where(sub == j, jnp.broadcast_to(cr, (SUBLANES, ns)), car)
                cai = jnp.where(sub == j, jnp.broadcast_to(ci, (SUBLANES, ns)), cai)
                ejr = jnp.sum(jnp.where(sub == j, er, 0.0), axis=0, keepdims=True)
                eji = jnp.sum(jnp.where(sub == j, ei, 0.0), axis=0, keepdims=True)
                cr, ci = ar * cr - ai * ci + ejr, ar * ci + ai * cr + eji
            carry_ref[s, 0:1, :] = cr
            carry_ref[s, 1:2, :] = ci

            ctr = jnp.broadcast_to(car[None], (seg, SUBLANES, ns)).reshape(tm, ns)
            cti = jnp.broadcast_to(cai[None], (seg, SUBLANES, ns)).reshape(tm, ns)
            pr = pw_ref[s, 0]
            pi = pw_ref[s, 1]
            st_r = bu_ref[q, :, 0:ns] + pr * ctr - pi * cti
            st_i = bu_ref[q, :, ns:2 * ns] + pr * cti + pi * ctr
            st = jnp.concatenate([st_r, st_i], axis=1).astype(jnp.bfloat16)
            y_ref[s] = jnp.dot(st, cmat_ref[s], preferred_element_type=jnp.float32) + d_ref[s] * u_ref[s]


def _ssm_tables(a_re, a_im, b_re, b_im, c_re, c_im, d_skip, log_dt, seg):
    f32 = jnp.float32
    g, p = a_re.shape
    c = b_re.shape[-1]
    ng = SSM_CHUNK_GROUPS
    n_slab = g // ng
    lam = lax.complex(a_re.astype(f32), a_im.astype(f32))
    dt = jnp.exp(log_dt.astype(f32))[:, None]
    lam_bar = jnp.exp(lam * dt)
    b_bar = ((lam_bar - 1.0) / lam)[:, :, None] * lax.complex(b_re.astype(f32), b_im.astype(f32))
    c_mat = lax.complex(c_re.astype(f32), c_im.astype(f32))
    eye = jnp.eye(ng, dtype=f32)

    def bdiag_b(m):
        m = m.reshape(n_slab, ng, p, c)
        return jnp.einsum('ab,kbpc->kacbp', eye, m).reshape(n_slab, ng * c, ng * p)

    def bdiag_c(m):
        m = m.reshape(n_slab, ng, c, p)
        return jnp.einsum('ab,kbcp->kbpac', eye, m).reshape(n_slab, ng * p, ng * c)

    bmat = jnp.concatenate([bdiag_b(jnp.real(b_bar)), bdiag_b(jnp.imag(b_bar))], axis=2).astype(jnp.bfloat16)
    cmat = jnp.concatenate([bdiag_c(jnp.real(c_mat)), -bdiag_c(jnp.imag(c_mat))], axis=1).astype(jnp.bfloat16)

    def slab_rows(z):
        z = z.reshape(n_slab, 1, ng * p)
        return jnp.concatenate([jnp.real(z), jnp.imag(z)], axis=1)

    lam_t = slab_rows(lam_bar)
    steps = jnp.arange(1, seg + 1, dtype=f32)
    pw = jnp.exp((lam * dt)[None] * steps[:, None, None])
    pw = pw.reshape(seg, n_slab, ng * p).transpose(1, 0, 2)
    pw = jnp.repeat(pw, SUBLANES, axis=1)
    pw_t = jnp.stack([jnp.real(pw), jnp.imag(pw)], axis=1)
    lamseg_t = slab_rows(jnp.exp(lam * dt * float(seg)))
    d_t = d_skip.astype(f32).reshape(n_slab, 1, ng * c)
    return bmat, cmat, lam_t, pw_t, lamseg_t, d_t


def _ssm(u4, tables, b, l):
    bmat, cmat, lam_t, pw_t, lamseg_t, d_t = tables
    n_slab, t, _ = u4.shape
    tm = TM_SSM
    nt = l // tm
    ns = SSM_CHUNK_STATES
    const = lambda nd: (lambda bi, i: (0,) * nd)
    return pl.pallas_call(
        _ssm_kernel,
        grid=(b, nt),
        in_specs=[
            pl.BlockSpec((n_slab, tm, LANES), lambda bi, i: (0, bi * nt + i, 0)),
            pl.BlockSpec(bmat.shape, const(3)),
            pl.BlockSpec(cmat.shape, const(3)),
            pl.BlockSpec(lam_t.shape, const(3)),
            pl.BlockSpec(pw_t.shape, const(4)),
            pl.BlockSpec(lamseg_t.shape, const(3)),
            pl.BlockSpec(d_t.shape, const(3)),
        ],
        out_specs=pl.BlockSpec((n_slab, tm, LANES), lambda bi, i: (0, bi * nt + i, 0)),
        out_shape=jax.ShapeDtypeStruct((n_slab, t, LANES), jnp.float32),
        scratch_shapes=[
            pltpu.VMEM((2, tm, 2 * ns), jnp.float32),
            pltpu.VMEM((n_slab, 2, ns), jnp.float32),
        ],
        compiler_params=_cparams("arbitrary", "arbitrary"),
        name="s5_scan",
    )(u4, bmat, cmat, lam_t, pw_t, lamseg_t, d_t)


def _merge_kernel(x_ref, attn_ref, y_ref, gate_ref, wglu32_ref, bglu_ref, wa32_ref, ws32_ref, wo32_ref,
                  nm_ref, wr_ref, br_ref, tri_ref, x1_ref, h2_ref, ids_ref, rw_ref, rank_ref, cnt_ref,
                  lg_ref, carry_ref, wglu_ref, wa_ref, ws_ref, wo_ref):
    i = pl.program_id(0)
    d = x_ref.shape[1]
    tm = x_ref.shape[0]

    @pl.when(i == 0)
    def _():
        lg_ref[...] = jnp.zeros_like(lg_ref)
        carry_ref[...] = jnp.zeros_like(carry_ref)
        for w32, wb in ((wglu32_ref, wglu_ref), (wa32_ref, wa_ref), (ws32_ref, ws_ref), (wo32_ref, wo_ref)):
            wb[...] = w32[...].astype(jnp.bfloat16)

    logits = lg_ref[...]

    seg = tm // SUBLANES
    nh = MERGE_SPLIT
    hm = tm // nh
    js = SUBLANES // nh
    rows = [slice(g * hm, (g + 1) * hm) for g in range(nh)]
    y = [jnp.concatenate(
        [jnp.concatenate([y_ref[s, pl.ds(j, seg, stride=SUBLANES), :] for j in range(g * js, (g + 1) * js)], axis=0)
         for s in range(y_ref.shape[0])], axis=1) for g in range(nh)]
    z = [jax.nn.gelu(v) for v in y]
    zg = [jnp.dot(v.astype(jnp.bfloat16), wglu_ref[...], preferred_element_type=jnp.float32) + bglu_ref[...]
          for v in z]
    z = [v * jax.nn.sigmoid(w) for v, w in zip(z, zg)]
    a = [jnp.dot(attn_ref[r, :], wa_ref[...], preferred_element_type=jnp.float32) for r in rows]
    sb = [jnp.dot(v.astype(jnp.bfloat16), ws_ref[...], preferred_element_type=jnp.float32) for v in z]
    merged = [gate_ref[r, 0:d].astype(jnp.float32) * p + gate_ref[r, d:2 * d].astype(jnp.float32) * q
              for r, p, q in zip(rows, a, sb)]
    x1 = [x_ref[r, :] + jnp.dot(m.astype(jnp.bfloat16), wo_ref[...], preferred_element_type=jnp.float32)
          for r, m in zip(rows, merged)]
    for g in range(nh):
        x1_ref[rows[g], :] = x1[g]
        inv = lax.rsqrt(jnp.mean(x1[g] * x1[g], axis=-1, keepdims=True) + EPS)
        h2 = x1[g] * inv * nm_ref[...]
        h_hi = h2.astype(jnp.bfloat16)
        _store_rows(h2_ref, g * hm, _pack_bf16_pairs(h_hi.astype(jnp.float32)))
        h_lo = (h2 - h_hi.astype(jnp.float32)).astype(jnp.bfloat16)
        hh = jnp.dot(h_hi, wr_ref[...], preferred_element_type=jnp.float32)
        lh = jnp.dot(h_lo, wr_ref[:, 0:LANES], preferred_element_type=jnp.float32)
        lg_ref[rows[g], :] = hh[:, 0:LANES] + hh[:, LANES:2 * LANES] + lh + br_ref[...]

    lane = lax.broadcasted_iota(jnp.int32, (tm, LANES), 1)
    ninf = -jnp.inf
    gl = jnp.where((lane >= N_EXPERTS) & (lane < N_EXPERTS + N_EXPERT_GROUPS), logits, ninf)
    gmax = jnp.max(gl, axis=-1, keepdims=True)
    gidx = jnp.min(jnp.where(gl == gmax, lane - N_EXPERTS, LANES), axis=-1, keepdims=True)
    group_p = 1.0 / jnp.sum(jnp.exp(gl - gmax), axis=-1, keepdims=True)
    el = jnp.where((lane < N_EXPERTS) & ((lane >> _log2(EXPERTS_PER_GROUP)) == gidx), logits, ninf)
    m1 = jnp.max(el, axis=-1, keepdims=True)
    i1 = jnp.min(jnp.where(el == m1, lane, LANES), axis=-1, keepdims=True)
    el2 = jnp.where(lane == i1, ninf, el)
    m2 = jnp.max(el2, axis=-1, keepdims=True)
    i2 = jnp.min(jnp.where(el2 == m2, lane, LANES), axis=-1, keepdims=True)
    e2 = jnp.exp(m2 - m1)
    w1 = group_p / (1.0 + e2)
    w2 = group_p * e2 / (1.0 + e2)
    ids_ref[...] = jnp.transpose(jnp.where(lane == 0, i1, jnp.where(lane == 1, i2, 0)))[0:SUBLANES, :]
    rw_ref[...] = jnp.where(lane == 0, w1, jnp.where(lane == 1, w2, 0.0))

    oh0 = lane == i1
    oh1 = lane == i2
    live = jnp.where(i > 0, 1.0, 0.0)
    oh = (oh0.astype(jnp.float32) + oh1.astype(jnp.float32)) * live
    cum = jnp.dot(tri_ref[...], oh.astype(jnp.bfloat16), preferred_element_type=jnp.float32) + carry_ref[...]
    r0 = jnp.sum(jnp.where(oh0, cum, 0.0), axis=-1, keepdims=True)
    r1 = jnp.sum(jnp.where(oh1, cum, 0.0), axis=-1, keepdims=True)
    rank_ref[...] = jnp.transpose(
        jnp.where(lane == 0, r0, jnp.where(lane == 1, r1, 0.0)).astype(jnp.int32))[0:SUBLANES, :]
    carry_ref[...] = carry_ref[...] + jnp.sum(oh, axis=0, keepdims=True)
    cnt_ref[...] = carry_ref[...].astype(jnp.int32)


def _merge(x2, attn, y4, gates, w_glu, b_glu, w_a, w_s, w_o, norm_moe, w_router, b_router):
    t, d = x2.shape
    tm = TM_MERGE
    nt = t // tm
    n_slab = y4.shape[0]
    tri = (jnp.arange(tm)[None, :] < jnp.arange(tm)[:, None]).astype(jnp.bfloat16)
    full = lambda a: pl.BlockSpec(a.shape, lambda i: (0,) * a.ndim, pipeline_mode=pl.Buffered(1))
    cur = lambda i: (jnp.minimum(i, nt - 1), 0)
    prv = lambda i: (jnp.maximum(i - 1, 0), 0)
    row = lambda c, m=cur: pl.BlockSpec((tm, c), m)
    slots = pl.BlockSpec((SUBLANES, tm), lambda i: (0, jnp.maximum(i - 1, 0)))
    return pl.pallas_call(
        _merge_kernel,
        grid=(nt + 1,),
        in_specs=[
            row(d), row(attn.shape[1]),
            pl.BlockSpec((n_slab, tm, LANES), lambda i: (0, jnp.minimum(i, nt - 1), 0)),
            row(gates.shape[1]),
            full(w_glu), full(b_glu), full(w_a), full(w_s), full(w_o), full(norm_moe),
            full(w_router), full(b_router), full(tri),
        ],
        out_specs=[row(d), pl.BlockSpec((tm * ROW_SUB, LANES), cur), slots, row(LANES, prv),
                   slots, pl.BlockSpec((1, LANES), lambda i: (0, 0))],
        out_shape=[
            jax.ShapeDtypeStruct((t, d), jnp.float32),
            jax.ShapeDtypeStruct((t * ROW_SUB, LANES), jnp.uint32),
            jax.ShapeDtypeStruct((SUBLANES, t), jnp.int32),
            jax.ShapeDtypeStruct((t, LANES), jnp.float32),
            jax.ShapeDtypeStruct((SUBLANES, t), jnp.int32),
            jax.ShapeDtypeStruct((1, LANES), jnp.int32),
        ],
        scratch_shapes=[pltpu.VMEM((tm, LANES), jnp.float32), pltpu.VMEM((1, LANES), jnp.float32)]
        + [pltpu.VMEM(w.shape, jnp.bfloat16) for w in (w_glu, w_a, w_s, w_o)],
        compiler_params=_cparams("arbitrary"),
        name="merge_router",
    )(x2, attn, y4, gates, w_glu, b_glu, w_a, w_s, w_o, norm_moe, w_router, b_router, tri)


def _inverse_kernel(pos_ref, src_ref):
    n_slots = pos_ref.shape[0]
    t = n_slots // TOP_K

    def body(tok, c):
        for k in range(TOP_K):
            src_ref[pos_ref[k * t + tok]] = tok * ROW_SUB
        return c

    lax.fori_loop(0, t, body, 0, unroll=16)


def _inverse(pos1d):
    n_slots = pos1d.shape[0]
    smem = pl.BlockSpec(memory_space=pltpu.SMEM)
    return pl.pallas_call(
        _inverse_kernel,
        in_specs=[smem],
        out_specs=smem,
        out_shape=jax.ShapeDtypeStruct((n_slots,), jnp.int32),
        name="route_inverse",
    )(pos1d)


def _expert_kernel(ta_ref, tn_ref, tnew_ref, sexp_ref, slo_ref, shi_ref, meta_ref, src_ref,
                   h_ref, wg_hbm, wu_hbm, wd_hbm, y_ref,
                   xbuf_ref, wgs_ref, wus_ref, wds_ref, wgb_ref, wub_ref, wdb_ref, gsem, wsem):
    j = pl.program_id(0)
    n_tiles = pl.num_programs(0)
    last = n_tiles - 1
    tm = TM_EXPERT
    rs = ROW_SUB
    nb = ROW_BUFS
    n_used = meta_ref[0]
    slot = j % nb
    a = ta_ref[j]

    def gather(tile, sl, start):
        _row_gather(h_ref, src_ref, tile * tm, xbuf_ref, sl * tm, tm, gsem.at[sl], start)

    def weights(m, start):
        sl = m % 2
        e = sexp_ref[m]
        for hbm, stage in ((wg_hbm, wgs_ref), (wu_hbm, wus_ref), (wd_hbm, wds_ref)):
            cp = pltpu.make_async_copy(hbm.at[e], stage.at[sl], wsem.at[sl])
            cp.start() if start else cp.wait()

    def new_expert(m):
        weights(m, False)
        sl = m % 2
        wgb_ref[...] = wgs_ref[sl].astype(jnp.bfloat16)
        wub_ref[...] = wus_ref[sl].astype(jnp.bfloat16)
        wdb_ref[...] = wds_ref[sl].astype(jnp.bfloat16)

        @pl.when(m + 1 < n_used)
        def _():
            weights(m + 1, True)

    def expert_out(xp):
        x = _unpack_bf16_pairs(xp).astype(jnp.bfloat16)
        hg = jnp.dot(x, wgb_ref[...], preferred_element_type=jnp.float32)
        hu = jnp.dot(x, wub_ref[...], preferred_element_type=jnp.float32)
        act = (jax.nn.silu(hg) * hu).astype(jnp.bfloat16)
        y = jnp.dot(act, wdb_ref[...], preferred_element_type=jnp.float32)
        return _pack_bf16_pairs(y.astype(jnp.bfloat16).astype(jnp.float32))

    @pl.when(j == 0)
    def _():
        weights(0, True)
        for b in range(nb - 1):
            gather(b, b, True)

    @pl.when(tnew_ref[j] == 1)
    def _():
        new_expert(a)

    gather(j, slot, False)
    xp = _load_rows(xbuf_ref, slot * tm, tm)
    gather(jnp.minimum(j + nb - 1, last), (j + nb - 1) % nb, True)
    _store_rows(y_ref, 0, expert_out(xp))

    def extra(i, carry):
        m = a + i
        new_expert(m)
        yp = expert_out(_load_rows(xbuf_ref, slot * tm, tm))
        row = lax.broadcasted_iota(jnp.int32, (tm, LANES), 0) + j * tm
        mine = (row >= slo_ref[m]) & (row < shi_ref[m])
        _store_rows(y_ref, 0, yp, mine)
        return carry

    lax.fori_loop(1, tn_ref[j], extra, 0)

    @pl.when(j == last)
    def _():
        for b in range(1, nb):
            gather(j, (j + b) % nb, False)


def _experts(meta, src_tok, h_fat, w_gate, w_up, w_down):
    n_slots = src_tok.shape[0]
    ne, d, dff = w_gate.shape
    tm = TM_EXPERT
    rs = ROW_SUB
    assert n_slots // tm >= ROW_BUFS
    any_spec = pl.BlockSpec(memory_space=pl.ANY)
    grid_spec = pltpu.PrefetchScalarGridSpec(
        num_scalar_prefetch=8,
        grid=(n_slots // tm,),
        in_specs=[any_spec, any_spec, any_spec, any_spec],
        out_specs=pl.BlockSpec((tm * rs, LANES), lambda j, *_: (j, 0)),
        scratch_shapes=[
            pltpu.VMEM((ROW_BUFS * tm * rs, LANES), jnp.uint32),
            pltpu.VMEM((2, d, dff), jnp.float32),
            pltpu.VMEM((2, d, dff), jnp.float32),
            pltpu.VMEM((2, dff, d), jnp.float32),
            pltpu.VMEM((d, dff), jnp.bfloat16),
            pltpu.VMEM((d, dff), jnp.bfloat16),
            pltpu.VMEM((dff, d), jnp.bfloat16),
            pltpu.SemaphoreType.DMA((ROW_BUFS,)),
            pltpu.SemaphoreType.DMA((2,)),
        ],
    )
    return pl.pallas_call(
        _expert_kernel,
        grid_spec=grid_spec,
        out_shape=jax.ShapeDtypeStruct((n_slots * rs, LANES), jnp.uint32),
        compiler_params=_cparams("arbitrary"),
        name="moe_experts",
    )(*meta, src_tok, h_fat, w_gate, w_up, w_down)


def _combine_kernel(pos_ref, x1_ref, rw_ref, nf_ref, ys_ref, o_ref, buf_ref, sem):
    i = pl.program_id(0)
    n = pl.num_programs(0)
    last = n - 1
    tm = x1_ref.shape[0]
    t = n * tm
    nb = ROW_BUFS
    slot = i % nb

    def gather(tile, sl, start):
        for k in range(TOP_K):
            _row_gather(ys_ref, pos_ref, k * t + tile * tm, buf_ref, (sl * TOP_K + k) * tm, tm, sem.at[sl], start)

    @pl.when(i == 0)
    def _():
        for b in range(nb - 1):
            gather(b, b, True)

    gather(i, slot, False)
    y0 = _unpack_bf16_pairs(_load_rows(buf_ref, (slot * TOP_K) * tm, tm))
    y1 = _unpack_bf16_pairs(_load_rows(buf_ref, (slot * TOP_K + 1) * tm, tm))
    gather(jnp.minimum(i + nb - 1, last), (i + nb - 1) % nb, True)
    rw = rw_ref[...]
    x = x1_ref[...] + rw[:, 0:1] * y0 + rw[:, 1:2] * y1
    inv = lax.rsqrt(jnp.mean(x * x, axis=-1, keepdims=True) + EPS)
    o_ref[...] = x * inv * nf_ref[...]

    @pl.when(i == last)
    def _():
        for b in range(1, nb):
            gather(i, (i + b) % nb, False)


def _combine(pos_rows, x1, rw, norm_final, ys):
    t, d = x1.shape
    tm = TM_MOVE
    assert t // tm >= ROW_BUFS
    return pl.pallas_call(
        _combine_kernel,
        grid=(t // tm,),
        in_specs=[
            pl.BlockSpec(memory_space=pltpu.SMEM),
            pl.BlockSpec((tm, d), lambda i: (i, 0)),
            pl.BlockSpec((tm, LANES), lambda i: (i, 0)),
            pl.BlockSpec((1, d), lambda i: (0, 0)),
            pl.BlockSpec(memory_space=pl.ANY),
        ],
        out_specs=pl.BlockSpec((tm, d), lambda i: (i, 0)),
        out_shape=jax.ShapeDtypeStruct((t, d), jnp.float32),
        scratch_shapes=[pltpu.VMEM((ROW_BUFS * TOP_K * tm * ROW_SUB, LANES), jnp.uint32),
                        pltpu.SemaphoreType.DMA((ROW_BUFS,))],
        compiler_params=_cparams("arbitrary"),
        name="moe_combine",
    )(pos_rows, x1, rw, norm_final, ys)


def _expert_meta(counts, n_slots):
    tm = TM_EXPERT
    n_tiles = n_slots // tm
    i32 = jnp.int32
    ends = jnp.cumsum(counts).astype(i32)
    starts = ends - counts
    used = counts > 0
    n_used = jnp.sum(used).astype(i32)
    seq_of = jnp.cumsum(used).astype(i32) - 1
    m_idx = jnp.arange(N_EXPERTS, dtype=i32)
    pick = used[None, :] & (seq_of[None, :] == m_idx[:, None])
    s_exp = jnp.sum(jnp.where(pick, m_idx[None, :], 0), axis=1).astype(i32)
    s_lo = jnp.sum(jnp.where(pick, starts[None, :], 0), axis=1).astype(i32)
    s_hi = jnp.sum(jnp.where(pick, ends[None, :], 0), axis=1).astype(i32)
    row0 = jnp.arange(n_tiles, dtype=i32)[:, None] * tm
    t_a = jnp.sum(used[None, :] & (ends[None, :] <= row0), axis=1).astype(i32)
    t_b = jnp.sum(used[None, :] & (starts[None, :] < row0 + tm), axis=1).astype(i32) - 1
    t_new = jnp.any(used[None, :] & (starts[None, :] == row0), axis=1).astype(i32)
    return t_a, (t_b - t_a + 1).astype(i32), t_new, s_exp, s_lo, s_hi, n_used.reshape(1)


def _moe(x1, h_fat, ids, rw, rank, cnt, w_gate, w_up, w_down, norm_final):
    t, d = x1.shape
    counts = cnt[0, :N_EXPERTS]
    offs = (jnp.cumsum(counts) - counts).astype(jnp.int32)
    onehot = ids[:TOP_K, :, None] == jnp.arange(N_EXPERTS, dtype=jnp.int32)
    pos1d = (jnp.sum(jnp.where(onehot, offs, 0), axis=-1) + rank[:TOP_K]).reshape(-1)
    src_tok = _inverse(pos1d)
    meta = _expert_meta(counts, t * TOP_K)
    ys = _experts(meta, src_tok, h_fat, w_gate, w_up, w_down)
    return _combine(pos1d * ROW_SUB, x1, rw, norm_final, ys)


def kernel(x, norm_mix, w_in, b_gate, attn_sinks, ssm_a_re, ssm_a_im, ssm_b_re, ssm_b_im, ssm_c_re, ssm_c_im, ssm_d, ssm_log_dt, w_glu, b_glu, w_attn_branch, w_ssm_branch, w_out, norm_moe, w_router_group, b_router_group, w_router_expert, b_router_expert, w_expert_gate, w_expert_up, w_expert_down, norm_final):
    b, l, d = x.shape
    depth = w_in.shape[0]
    assert depth == 1, "the final norm is fused into the last layer's combine kernel"
    d_attn = N_HEADS * HEAD_DIM
    kv_cols = N_KV_HEADS * HEAD_DIM
    d_ssm = ssm_d.shape[-1]
    bf16 = jnp.bfloat16
    x2 = x.reshape(b * l, d)
    assert TM_PROJ == TM_SSM == TM_MERGE and l % TM_SSM == 0
    i = 0
    q, k, v, u4, gates = _in_proj(x2, norm_mix[i][None], w_in[i], b_gate[i][None],
                                  d_attn, kv_cols, d_ssm)
    attn = _attention(q, k, v, attn_sinks[i], b, l)
    tables = _ssm_tables(ssm_a_re[i], ssm_a_im[i], ssm_b_re[i], ssm_b_im[i], ssm_c_re[i], ssm_c_im[i],
                         ssm_d[i], ssm_log_dt[i], SSM_SEG)
    y4 = _ssm(u4, tables, b, l)
    pad = LANES - N_EXPERTS - N_EXPERT_GROUPS
    w_router = jnp.concatenate([w_router_expert[i], w_router_group[i], jnp.zeros((d, pad), jnp.float32)], axis=1)
    w_r_hi = w_router.astype(bf16)
    w_router = jnp.concatenate([w_r_hi, (w_router - w_r_hi.astype(jnp.float32)).astype(bf16)], axis=1)
    b_router = jnp.concatenate([b_router_expert[i], b_router_group[i], jnp.zeros((pad,), jnp.float32)])[None]
    x1, h_fat, ids, rw, rank, cnt = _merge(x2, attn, y4, gates, w_glu[i], b_glu[i][None],
                                w_attn_branch[i], w_ssm_branch[i],
                                w_out[i], norm_moe[i][None], w_router, b_router)
    out = _moe(x1, h_fat, ids, rw, rank, cnt, w_expert_gate[i], w_expert_up[i], w_expert_down[i], norm_final[None])
    return out.reshape(b, l, d)
```
